```python
import jax, jax.numpy as jnp
from jax import lax
import numpy as np

D_MODEL = 1024
BATCH = 32
SEQ = 256
DEPTH = 4
DEC_BATCH = 8
DEC_SEQ = 1024
PAST_LEN = 256

GRID_W = 64
N_MIXERS = 2
N_RET_LAYERS = (DEPTH + 1) // 2
N_CONV_LAYERS = DEPTH // 2
RET_HEADS = 4
RET_DK = D_MODEL // RET_HEADS
RET_DV = 2 * D_MODEL // RET_HEADS
RET_CHUNK = 64
ROPE_BASE = 10000.0
CONV_WIDTH = 3
N_EXPERTS = 64
TOP_K = 6
N_GROUPS = 8
TOPK_GROUPS = 4
EXPERT_FF = 256
SHARED_FF = 256
ROUTED_SCALE = 2.5
MOE_BLOCK = 128
LN_EPS = 1e-5
DEEPNORM_ALPHA = (2.0 * DEPTH) ** 0.25
DEEPNORM_BETA = (8.0 * DEPTH) ** -0.25

kernel_name = 'hybrid_retention_shortconv_moe_dit_step'

F32 = jnp.float32


def layer_norm(x, g, b):
    xf = x.astype(F32)
    mu = xf.mean(-1, keepdims=True)
    var = jnp.square(xf - mu).mean(-1, keepdims=True)
    return ((xf - mu) * lax.rsqrt(var + LN_EPS) * g.astype(F32) + b.astype(F32)).astype(x.dtype)


def head_norm(o):
    mu = o.mean(-1, keepdims=True)
    var = jnp.square(o - mu).mean(-1, keepdims=True)
    return (o - mu) * lax.rsqrt(var + LN_EPS)


def rope_half(x, pos):
    half = x.shape[-1] // 2
    freqs = ROPE_BASE ** (-jnp.arange(half, dtype=F32) / half)
    ang = pos[:, None] * freqs[None, :]
    cos, sin = jnp.cos(ang), jnp.sin(ang)
    x1, x2 = x[..., :half], x[..., half:]
    return jnp.concatenate([x1 * cos - x2 * sin, x1 * sin + x2 * cos], axis=-1)


def rope_2d(x, row, col):
    h = x.shape[-1] // 2
    return jnp.concatenate([rope_half(x[..., :h], row), rope_half(x[..., h:], col)], axis=-1)


def grid_positions(rows):
    row = jnp.repeat(jnp.arange(rows, dtype=F32), GRID_W)
    col = jnp.tile(jnp.arange(GRID_W, dtype=F32), rows)
    return row, col


def retention_scan(q, k, v, log_gamma, s0):
    b, h, n, dk = q.shape
    dv = v.shape[-1]
    C = RET_CHUNK
    nc = n // C
    qc = q.reshape(b, h, nc, C, dk)
    kc = k.reshape(b, h, nc, C, dk)
    vc = v.reshape(b, h, nc, C, dv)
    pos = jnp.arange(C, dtype=F32)
    diff = pos[:, None] - pos[None, :]
    lg = log_gamma[:, None, None]
    intra_decay = jnp.where(diff >= 0, jnp.exp(lg * jnp.maximum(diff, 0.0)), 0.0)
    scores = jnp.einsum('bhncd,bhnmd->bhncm', qc, kc) * intra_decay[None, :, None]
    o_intra = jnp.einsum('bhncm,bhnme->bhnce', scores, vc)
    q_dec = jnp.exp(log_gamma[:, None] * (pos + 1.0))[None, :, None, :, None]
    k_dec = jnp.exp(log_gamma[:, None] * (C - 1.0 - pos))[None, :, None, :, None]
    chunk_dec = jnp.exp(log_gamma * C)[None, :, None, None]
    xs = (jnp.moveaxis(qc * q_dec, 2, 0), jnp.moveaxis(kc * k_dec, 2, 0), jnp.moveaxis(vc, 2, 0))

    def step(s, inp):
        qi, ki, vi = inp
        out = jnp.einsum('bhcd,bhde->bhce', qi, s)
        s = chunk_dec * s + jnp.einsum('bhcd,bhce->bhde', ki, vi)
        return s, out

    s_fin, o_inter = lax.scan(step, s0, xs)
    o = o_intra + jnp.moveaxis(o_inter, 0, 2)
    return o.reshape(b, h, n, dv), s_fin


def retention_mixer(h, w_in, w_out, decay_logit, s0_f, s0_b, pos):
    b, n, _ = h.shape
    proj = h @ w_in
    q, k, v, g = jnp.split(proj, [D_MODEL, 2 * D_MODEL, 4 * D_MODEL], axis=-1)

    def heads(t, d):
        return t.reshape(b, n, RET_HEADS, d).transpose(0, 2, 1, 3).astype(F32)

    q = heads(q, RET_DK)
    k = heads(k, RET_DK) * (RET_DK ** -0.5)
    v = heads(v, RET_DV)
    if pos is not None:
        row, col = pos
        q = rope_2d(q, row, col)
        k = rope_2d(k, row, col)
    log_gamma = jax.nn.log_sigmoid(decay_logit.astype(F32))
    o_f, s_f = retention_scan(q, k, v, log_gamma[0], s0_f)
    o_b, s_b = retention_scan(q[:, :, ::-1], k[:, :, ::-1], v[:, :, ::-1], log_gamma[1], s0_b)
    o = head_norm(o_f + o_b[:, :, ::-1])
    o = o.transpose(0, 2, 1, 3).reshape(b, n, RET_HEADS * RET_DV).astype(h.dtype)
    y = (jax.nn.silu(g) * o) @ w_out
    return y, s_f, s_b


def conv3(u, w):
    pad = [(0, 0)] * (u.ndim - 2) + [(1, 1), (0, 0)]
    up = jnp.pad(u, pad)
    return up[..., :-2, :] * w[0] + up[..., 1:-1, :] * w[1] + up[..., 2:, :] * w[2]


def short_conv_mixer(h, w_in, w_conv, w_out, rows):
    bg, cg, xt = jnp.split(h @ w_in, 3, axis=-1)
    u = cg * xt
    if rows is not None:
        b, n, d = u.shape
        cu = conv3(u.reshape(b, rows, GRID_W, d), w_conv).reshape(b, n, d)
    else:
        cu = conv3(u, w_conv)
    return (bg * cu) @ w_out


def moe_ffn(h, router_w, router_bias, w_gate, w_up, w_down, s_gate, s_up, s_down):
    b, n, d = h.shape
    t = h.reshape(b * n, d)
    scores = jax.nn.sigmoid((t @ router_w).astype(F32))
    sel = scores + router_bias.astype(F32)
    per_group = N_EXPERTS // N_GROUPS
    grp_score = lax.top_k(sel.reshape(-1, N_GROUPS, per_group), 2)[0].sum(-1)
    _, gidx = lax.top_k(grp_score, TOPK_GROUPS)
    gmask = jax.nn.one_hot(gidx, N_GROUPS, dtype=F32).sum(1) > 0
    emask = jnp.repeat(gmask, per_group, axis=-1)
    _, eidx = lax.top_k(jnp.where(emask, sel, -jnp.inf), TOP_K)
    wk = jnp.take_along_axis(scores, eidx, axis=-1)
    wk = wk / wk.sum(-1, keepdims=True) * ROUTED_SCALE
    combine = jnp.einsum('tk,tke->te', wk, jax.nn.one_hot(eidx, N_EXPERTS, dtype=F32)).astype(h.dtype)

    def expert_block(args):
        tb, cb = args
        hid = jax.nn.silu(jnp.einsum('td,edf->tef', tb, w_gate)) * jnp.einsum('td,edf->tef', tb, w_up)
        return jnp.einsum('tef,efd->td', hid * cb[:, :, None], w_down)

    nb = t.shape[0] // MOE_BLOCK
    routed = lax.map(expert_block, (t.reshape(nb, MOE_BLOCK, d),
                                    combine.reshape(nb, MOE_BLOCK, N_EXPERTS))).reshape(-1, d)
    shared = (jax.nn.silu(t @ s_gate) * (t @ s_up)) @ s_down
    return (routed + shared).reshape(b, n, d)


def run_trunk(x, cond, state_in, rows, ada_w, ada_b, ln_g, ln_b, ret_w_in, ret_w_out, ret_decay,
              conv_w_in, conv_w, conv_w_out, moe_router, moe_bias, moe_w_gate, moe_w_up, moe_w_down,
              shared_w_gate, shared_w_up, shared_w_down):
    b = x.shape[0]
    pos = None if rows is None else grid_positions(rows)
    new_states = []
    for i in range(DEPTH):
        j = i // N_MIXERS
        sh1, sc1, g1, sh2, sc2, g2 = jnp.split(jax.nn.silu(cond) @ ada_w[i] + ada_b[i], 6, axis=-1)
        h = x * (1 + sc1) + sh1
        if i % N_MIXERS == 0:
            if state_in is None:
                s0f = jnp.zeros((b, RET_HEADS, RET_DK, RET_DV), F32)
                s0b = s0f
            else:
                s0f = state_in[:, j, 0].astype(F32)
                s0b = state_in[:, j, 1].astype(F32)
            y, s_f, s_b = retention_mixer(h, ret_w_in[j], ret_w_out[j], ret_decay[j], s0f, s0b, pos)
            if state_in is None:
                new_states.append(jnp.stack([s_f, s_b], axis=1))
        else:
            y = short_conv_mixer(h, conv_w_in[j], conv_w[j], conv_w_out[j], rows)
        x = layer_norm(DEEPNORM_ALPHA * x + g1 * y, ln_g[i, 0], ln_b[i, 0])
        h = x * (1 + sc2) + sh2
        y = moe_ffn(h, moe_router[i], moe_bias[i], moe_w_gate[i], moe_w_up[i], moe_w_down[i],
                    shared_w_gate[i], shared_w_up[i], shared_w_down[i])
        x = layer_norm(DEEPNORM_ALPHA * x + g2 * y, ln_g[i, 1], ln_b[i, 1])
    return x, new_states


def setup_inputs(seed: int = 0) -> dict:
    key = jax.random.key(seed)
    ks = jax.random.split(key, 24)
    D = D_MODEL
    nrm = jax.random.normal
    gamma0 = 1.0 - 2.0 ** (-5.0 - np.arange(RET_HEADS, dtype=np.float32))
    logit0 = jnp.asarray(np.log(gamma0 / (1.0 - gamma0)), dtype=F32)
    return {
        'x_prompt': nrm(ks[0], (BATCH, SEQ, D), F32),
        'x_sample': nrm(ks[1], (DEC_BATCH, DEC_SEQ, D), F32),
        'state_ret': 0.5 * nrm(ks[2], (DEC_BATCH, N_RET_LAYERS, 2, RET_HEADS, RET_DK, RET_DV), F32),
        'c': nrm(ks[3], (DEC_BATCH, D), F32),
        'c_ctx': nrm(ks[4], (D,), F32),
        'ada_w': 0.5 * D ** -0.5 * nrm(ks[5], (DEPTH, D, 6 * D), F32),
        'ada_b': 0.02 * nrm(ks[6], (DEPTH, 6 * D), F32),
        'ln_g': 1.0 + 0.02 * nrm(ks[7], (DEPTH, 2, D), F32),
        'ln_b': 0.02 * nrm(ks[8], (DEPTH, 2, D), F32),
        'ret_w_in': D ** -0.5 * nrm(ks[9], (N_RET_LAYERS, D, 6 * D), F32),
        'ret_w_out': DEEPNORM_BETA * (2 * D) ** -0.5 * nrm(ks[10], (N_RET_LAYERS, 2 * D, D), F32),
        'ret_decay': logit0[None, None, :] + 0.1 * nrm(ks[11], (N_RET_LAYERS, 2, RET_HEADS), F32),
        'conv_w_in': D ** -0.5 * nrm(ks[12], (N_CONV_LAYERS, D, 3 * D), F32),
        'conv_w': CONV_WIDTH ** -0.5 * nrm(ks[13], (N_CONV_LAYERS, CONV_WIDTH, D), F32),
        'conv_w_out': DEEPNORM_BETA * D ** -0.5 * nrm(ks[14], (N_CONV_LAYERS, D, D), F32),
        'moe_router': D ** -0.5 * nrm(ks[15], (DEPTH, D, N_EXPERTS), F32),
        'moe_bias': 0.01 * nrm(ks[16], (DEPTH, N_EXPERTS), F32),
        'moe_w_gate': D ** -0.5 * nrm(ks[17], (DEPTH, N_EXPERTS, D, EXPERT_FF), F32),
        'moe_w_up': D ** -0.5 * nrm(ks[18], (DEPTH, N_EXPERTS, D, EXPERT_FF), F32),
        'moe_w_down': DEEPNORM_BETA * EXPERT_FF ** -0.5 * nrm(ks[19], (DEPTH, N_EXPERTS, EXPERT_FF, D), F32),
        'shared_w_gate': D ** -0.5 * nrm(ks[20], (DEPTH, D, SHARED_FF), F32),
        'shared_w_up': D ** -0.5 * nrm(ks[21], (DEPTH, D, SHARED_FF), F32),
        'shared_w_down': DEEPNORM_BETA * SHARED_FF ** -0.5 * nrm(ks[22], (DEPTH, SHARED_FF, D), F32),
    }


def reference(x_prompt, x_sample, state_ret, c, c_ctx, ada_w, ada_b, ln_g, ln_b, ret_w_in, ret_w_out,
              ret_decay, conv_w_in, conv_w, conv_w_out, moe_router, moe_bias, moe_w_gate, moe_w_up,
              moe_w_down, shared_w_gate, shared_w_up, shared_w_down):
    y_prompt, ctx_states = run_trunk(
        x_prompt, c_ctx[None, None, :], None, None, ada_w, ada_b, ln_g, ln_b, ret_w_in, ret_w_out,
        ret_decay, conv_w_in, conv_w, conv_w_out, moe_router, moe_bias, moe_w_gate, moe_w_up,
        moe_w_down, shared_w_gate, shared_w_up, shared_w_down)
    state_ret_new = jnp.stack(ctx_states, axis=1)
    rows = x_sample.shape[1] // GRID_W
    y_sample, _ = run_trunk(
        x_sample, c[:, None, :], state_ret, rows, ada_w, ada_b, ln_g, ln_b, ret_w_in, ret_w_out,
        ret_decay, conv_w_in, conv_w, conv_w_out, moe_router, moe_bias, moe_w_gate, moe_w_up,
        moe_w_down, shared_w_gate, shared_w_up, shared_w_down)
    return (y_prompt, y_sample, state_ret_new)
```

```python
import functools

import jax
import jax.numpy as jnp
from jax import lax
from jax.experimental import pallas as pl
from jax.experimental.pallas import tpu as pltpu

F32 = jnp.float32
BF16 = jnp.bfloat16

D_MODEL = 1024
BATCH = 32
SEQ = 256
DEPTH = 4
DEC_BATCH = 8
DEC_SEQ = 1024
GRID_W = 64
RET_HEADS = 4
RET_DK = D_MODEL // RET_HEADS
RET_DV = 2 * D_MODEL // RET_HEADS
ROPE_BASE = 10000.0
N_EXPERTS = 64
TOP_K = 6
N_GROUPS = 8
TOPK_GROUPS = 4
EXPERT_FF = 256
ROUTED_SCALE = 2.5
LN_EPS = 1e-5
DEEPNORM_ALPHA = (2.0 * DEPTH) ** 0.25

N_CTX = BATCH * SEQ
N_LAT = DEC_BATCH * DEC_SEQ
N_TOK = N_CTX + N_LAT
N_COND = 16
RET_CHUNK = 256
TOK_BLOCK = 256
MOE_TOK_BLOCK = 1024
VMEM_LIMIT = 56 * 1024 * 1024


def _cond_row(tok_block_idx, tok_block):
    t0 = tok_block_idx * tok_block
    return jnp.where(t0 < N_CTX, 0, 1 + (t0 - N_CTX) // DEC_SEQ)


def _silu(x):
    return x * jax.nn.sigmoid(x)


def _layer_norm(z, g, b):
    mu = jnp.mean(z, axis=-1, keepdims=True)
    zc = z - mu
    var = jnp.mean(zc * zc, axis=-1, keepdims=True)
    return zc * lax.rsqrt(var + LN_EPS) * g + b


def _ada_body(cond_ref, w_ref, b_ref, o_ref):
    s = _silu(cond_ref[...]).astype(BF16)
    o_ref[...] = jnp.dot(s, w_ref[...].astype(BF16), preferred_element_type=F32) + b_ref[...]


def _ada_table(cond, ada_w, ada_b):
    tn = 2048
    return pl.pallas_call(
        _ada_body,
        grid=(DEPTH, 6 * D_MODEL // tn),
        in_specs=[
            pl.BlockSpec((N_COND, D_MODEL), lambda l, j: (0, 0)),
            pl.BlockSpec((None, D_MODEL, tn), lambda l, j: (l, 0, j)),
            pl.BlockSpec((None, 1, tn), lambda l, j: (l, 0, j)),
        ],
        out_specs=pl.BlockSpec((None, N_COND, tn), lambda l, j: (l, 0, j)),
        out_shape=jax.ShapeDtypeStruct((DEPTH, N_COND, 6 * D_MODEL), F32),
        compiler_params=pltpu.CompilerParams(
            dimension_semantics=("arbitrary", "arbitrary"), vmem_limit_bytes=VMEM_LIMIT),
        name="ada_table",
    )(cond, ada_w, ada_b.reshape(DEPTH, 1, 6 * D_MODEL))


def _mod_spec(tok_block, col, grid_rank, tok_axis):
    def index_map(*idx):
        return (_cond_row(idx[tok_axis], tok_block), 0, col)
    del grid_rank
    return pl.BlockSpec((None, 1, D_MODEL), index_map)


def _modproj_body(x_ref, sh_ref, sc_ref, w_ref, o_ref, wbf_ref):
    @pl.when(pl.program_id(1) == 0)
    def _():
        wbf_ref[...] = w_ref[...].astype(BF16)

    h = (x_ref[...] * (1.0 + sc_ref[...]) + sh_ref[...]).astype(BF16)
    o_ref[...] = jnp.dot(h, wbf_ref[...], preferred_element_type=F32).astype(o_ref.dtype)


def _modproj(x, mod, w_all, w_idx, tn):
    n_out = w_all.shape[-1]
    tm = 512
    return pl.pallas_call(
        _modproj_body,
        grid=(n_out // tn, N_TOK // tm),
        in_specs=[
            pl.BlockSpec((tm, D_MODEL), lambda j, i: (i, 0)),
            _mod_spec(tm, 0, 2, 1),
            _mod_spec(tm, 1, 2, 1),
            pl.BlockSpec((None, D_MODEL, tn), lambda j, i: (w_idx, 0, j)),
        ],
        out_specs=pl.BlockSpec((tm, tn), lambda j, i: (i, j)),
        out_shape=jax.ShapeDtypeStruct((N_TOK, n_out), BF16),
        scratch_shapes=[pltpu.VMEM((D_MODEL, tn), BF16)],
        compiler_params=pltpu.CompilerParams(
            dimension_semantics=("arbitrary", "arbitrary"), vmem_limit_bytes=VMEM_LIMIT),
        name="modproj",
    )(x, mod, mod, w_all)


def _log_sigmoid(v):
    return jnp.minimum(v, 0.0) - jnp.log1p(jnp.exp(-jnp.abs(v)))


def _decay_tables(dec_ref, head):
    c = RET_CHUNK
    lgf = _log_sigmoid(jnp.full((c, c), dec_ref[0, head], F32))
    lgb = _log_sigmoid(jnp.full((c, c), dec_ref[1, head], F32))
    row = lax.broadcasted_iota(jnp.int32, (c, c), 0).astype(F32)
    col = lax.broadcasted_iota(jnp.int32, (c, c), 1).astype(F32)
    diff = row - col
    kscale = RET_DK ** -0.5
    intra = jnp.where(diff > 0, jnp.exp(lgf * diff),
                      jnp.where(diff < 0, jnp.exp(-lgb * diff), 2.0)) * kscale
    qdec_f = jnp.exp(lgf * (row + 1.0))
    qdec_b = jnp.exp(lgb * (c - row))
    kdec_f = jnp.exp(lgf * (c - 1.0 - row)) * kscale
    kdec_b = jnp.exp(lgb * row) * kscale
    cdec_f = jnp.exp(lgf * c)
    cdec_b = jnp.exp(lgb * c)
    return intra, qdec_f, qdec_b, kdec_f, kdec_b, cdec_f, cdec_b


def _head_norm_gate(o, g):
    mu = jnp.mean(o, axis=-1, keepdims=True)
    oc = o - mu
    var = jnp.mean(oc * oc, axis=-1, keepdims=True)
    on = oc * lax.rsqrt(var + LN_EPS)
    return (_silu(g.astype(F32)) * on).astype(BF16)


_NT = (((1,), (1,)), ((), ()))
_TN = (((0,), (0,)), ((), ()))


def _ret_ctx_body(dec_ref, q_ref, k_ref, v_ref, g_ref, a_ref, sf_ref, sb_ref):
    head = pl.program_id(1)
    intra, _, _, kdec_f, kdec_b, _, _ = _decay_tables(dec_ref, head)
    q = q_ref[...]
    k = k_ref[...]
    v = v_ref[...]
    scores = lax.dot_general(q, k, _NT, preferred_element_type=F32)
    p = (scores * intra).astype(BF16)
    o = jnp.dot(p, v, preferred_element_type=F32)
    a_ref[...] = _head_norm_gate(o, g_ref[...])
    kf = k.astype(F32)
    sf_ref[...] = lax.dot_general((kf * kdec_f).astype(BF16), v, _TN, preferred_element_type=F32)
    sb_ref[...] = lax.dot_general((kf * kdec_b).astype(BF16), v, _TN, preferred_element_type=F32)


def _retention_ctx(proj, decay):
    n = SEQ
    assert n == RET_CHUNK
    st = jax.ShapeDtypeStruct((BATCH, RET_HEADS, RET_DK, RET_DV), F32)
    st_spec = pl.BlockSpec((None, None, RET_DK, RET_DV), lambda b, h: (b, h, 0, 0))
    return pl.pallas_call(
        _ret_ctx_body,
        grid=(BATCH, RET_HEADS),
        in_specs=[
            pl.BlockSpec(memory_space=pltpu.SMEM),
            pl.BlockSpec((n, RET_DK), lambda b, h: (b, h)),
            pl.BlockSpec((n, RET_DK), lambda b, h: (b, RET_HEADS + h)),
            pl.BlockSpec((n, RET_DV), lambda b, h: (b, RET_HEADS + h)),
            pl.BlockSpec((n, RET_DV), lambda b, h: (b, 2 * RET_HEADS + h)),
        ],
        out_specs=[pl.BlockSpec((n, RET_DV), lambda b, h: (b, h)), st_spec, st_spec],
        out_shape=[jax.ShapeDtypeStruct((N_CTX, RET_HEADS * RET_DV), BF16), st, st],
        compiler_params=pltpu.CompilerParams(
            dimension_semantics=("arbitrary", "arbitrary"), vmem_limit_bytes=VMEM_LIMIT),
        name="retention_ctx",
    )(decay, proj, proj, proj, proj)


def _rope(x, cos, sin):
    halves = [pltpu.roll(x[:, s:s + 128], 64, axis=1) for s in (0, 128)]
    return x * cos + jnp.concatenate(halves, axis=1) * sin


def _ret_lat_body(dec_ref, q_ref, k_ref, v_ref, g_ref, s0f_ref, s0b_ref, cos_ref, sin_ref,
                  a_ref, qr_ref, kr_ref, o_ref, st_ref):
    head = pl.program_id(1)
    c = RET_CHUNK
    nc = DEC_SEQ // c
    intra, qdec_f, qdec_b, kdec_f, kdec_b, cdec_f, cdec_b = _decay_tables(dec_ref, head)
    cdec_f = jnp.concatenate([cdec_f, cdec_f], axis=1)
    cdec_b = jnp.concatenate([cdec_b, cdec_b], axis=1)

    qr_ref[...] = _rope(q_ref[...].astype(F32), cos_ref[...], sin_ref[...]).astype(BF16)
    kr_ref[...] = _rope(k_ref[...].astype(F32), cos_ref[...], sin_ref[...]).astype(BF16)

    st_ref[...] = s0f_ref[...]
    for ci in range(nc):
        rows = pl.ds(ci * c, c)
        q = qr_ref[rows, :]
        k = kr_ref[rows, :]
        v = v_ref[rows, :]
        scores = lax.dot_general(q, k, _NT, preferred_element_type=F32)
        p = (scores * intra).astype(BF16)
        o = jnp.dot(p, v, preferred_element_type=F32)
        qd = (q.astype(F32) * qdec_f).astype(BF16)
        o = o + jnp.dot(qd, st_ref[...].astype(BF16), preferred_element_type=F32)
        o_ref[rows, :] = o
        if ci + 1 < nc:
            kd = (k.astype(F32) * kdec_f).astype(BF16)
            st_ref[...] = cdec_f * st_ref[...] + lax.dot_general(kd, v, _TN, preferred_element_type=F32)

    st_ref[...] = s0b_ref[...]
    for ci in reversed(range(nc)):
        rows = pl.ds(ci * c, c)
        q = qr_ref[rows, :]
        qd = (q.astype(F32) * qdec_b).astype(BF16)
        o = o_ref[rows, :] + jnp.dot(qd, st_ref[...].astype(BF16), preferred_element_type=F32)
        a_ref[rows, :] = _head_norm_gate(o, g_ref[rows, :])
        if ci > 0:
            k = kr_ref[rows, :]
            v = v_ref[rows, :]
            kd = (k.astype(F32) * kdec_b).astype(BF16)
            st_ref[...] = cdec_b * st_ref[...] + lax.dot_general(kd, v, _TN, preferred_element_type=F32)


def _rope_tables():
    half = RET_DK // 4
    freqs = ROPE_BASE ** (-jnp.arange(half, dtype=F32) / half)
    t = jnp.arange(DEC_SEQ)
    row = (t // GRID_W).astype(F32)
    col = (t % GRID_W).astype(F32)
    ang_r = row[:, None] * freqs[None, :]
    ang_c = col[:, None] * freqs[None, :]
    cos = jnp.concatenate([jnp.cos(ang_r)] * 2 + [jnp.cos(ang_c)] * 2, axis=1)
    sin = jnp.concatenate([-jnp.sin(ang_r), jnp.sin(ang_r), -jnp.sin(ang_c), jnp.sin(ang_c)], axis=1)
    return cos, sin


def _retention_lat(proj, decay, state_ret, ret_idx, cos, sin):
    n = DEC_SEQ
    off = N_CTX // n
    s0_spec = lambda d: pl.BlockSpec((None, None, None, None, RET_DK, RET_DV),
                                     lambda b, h: (b, ret_idx, d, h, 0, 0))
    tab_spec = pl.BlockSpec((n, RET_DK), lambda b, h: (0, 0))
    return pl.pallas_call(
        _ret_lat_body,
        grid=(DEC_BATCH, RET_HEADS),
        in_specs=[
            pl.BlockSpec(memory_space=pltpu.SMEM),
            pl.BlockSpec((n, RET_DK), lambda b, h: (off + b, h)),
            pl.BlockSpec((n, RET_DK), lambda b, h: (off + b, RET_HEADS + h)),
            pl.BlockSpec((n, RET_DV), lambda b, h: (off + b, RET_HEADS + h)),
            pl.BlockSpec((n, RET_DV), lambda b, h: (off + b, 2 * RET_HEADS + h)),
            s0_spec(0), s0_spec(1), tab_spec, tab_spec,
        ],
        out_specs=pl.BlockSpec((n, RET_DV), lambda b, h: (b, h)),
        out_shape=jax.ShapeDtypeStruct((N_LAT, RET_HEADS * RET_DV), BF16),
        scratch_shapes=[
            pltpu.VMEM((n, RET_DK), BF16), pltpu.VMEM((n, RET_DK), BF16),
            pltpu.VMEM((n, RET_DV), F32), pltpu.VMEM((RET_DK, RET_DV), F32),
        ],
        compiler_params=pltpu.CompilerParams(
            dimension_semantics=("arbitrary", "arbitrary"), vmem_limit_bytes=VMEM_LIMIT),
        name="retention_lat",
    )(decay, proj, proj, proj, proj, state_ret, state_ret, cos, sin)


def _mixer_out_tail(a, x_ref, g1_ref, w_ref, lng_ref, lnb_ref, o_ref, wbf_ref):
    @pl.when(pl.program_id(0) == 0)
    def _():
        wbf_ref[...] = w_ref[...].astype(BF16)

    y = jnp.dot(a, wbf_ref[...], preferred_element_type=F32)
    z = DEEPNORM_ALPHA * x_ref[...] + g1_ref[...] * y
    o_ref[...] = _layer_norm(z, lng_ref[...], lnb_ref[...])


def _ret_out_body(a_ref, x_ref, g1_ref, w_ref, lng_ref, lnb_ref, o_ref, wbf_ref):
    _mixer_out_tail(a_ref[...], x_ref, g1_ref, w_ref, lng_ref, lnb_ref, o_ref, wbf_ref)


def _conv_out_body(bg_ref, cg_ref, xt_ref, cw_ref, x_ref, g1_ref, w_ref, lng_ref, lnb_ref, o_ref, wbf_ref):
    tm = TOK_BLOCK
    seg = jnp.where(pl.program_id(0) * tm < N_CTX, SEQ, GRID_W)
    u = cg_ref[...].astype(F32) * xt_ref[...].astype(F32)
    pos = lax.broadcasted_iota(jnp.int32, (tm, D_MODEL), 0) & (seg - 1)
    u_prev = jnp.where(pos == 0, 0.0, pltpu.roll(u, 1, axis=0))
    u_next = jnp.where(pos == seg - 1, 0.0, pltpu.roll(u, tm - 1, axis=0))
    cu = u_prev * cw_ref[0:1, :] + u * cw_ref[1:2, :] + u_next * cw_ref[2:3, :]
    a = (bg_ref[...].astype(F32) * cu).astype(BF16)
    _mixer_out_tail(a, x_ref, g1_ref, w_ref, lng_ref, lnb_ref, o_ref, wbf_ref)


def _mixer_out(kind, act, x, mod, w_all, w_idx, ln_g, ln_b, conv_w=None):
    tm = TOK_BLOCK
    k_dim = w_all.shape[1]
    row = lambda i: (i, 0)
    vec = pl.BlockSpec((1, D_MODEL), lambda i: (0, 0))
    tail_specs = [
        pl.BlockSpec((tm, D_MODEL), row),
        _mod_spec(tm, 2, 1, 0),
        pl.BlockSpec((None, k_dim, D_MODEL), lambda i: (w_idx, 0, 0)),
        vec, vec,
    ]
    if kind == "ret":
        body = _ret_out_body
        in_specs = [pl.BlockSpec((tm, k_dim), row)] + tail_specs
        args = (act, x, mod, w_all, ln_g, ln_b)
    else:
        body = _conv_out_body
        in_specs = [pl.BlockSpec((tm, D_MODEL), lambda i, c=c: (i, c)) for c in range(3)]
        in_specs += [pl.BlockSpec((3, D_MODEL), lambda i: (0, 0))] + tail_specs
        args = (act, act, act, conv_w, x, mod, w_all, ln_g, ln_b)
    return pl.pallas_call(
        body,
        grid=(N_TOK // tm,),
        in_specs=in_specs,
        out_specs=pl.BlockSpec((tm, D_MODEL), row),
        out_shape=jax.ShapeDtypeStruct((N_TOK, D_MODEL), F32),
        scratch_shapes=[pltpu.VMEM((k_dim, D_MODEL), BF16)],
        compiler_params=pltpu.CompilerParams(
            dimension_semantics=("arbitrary",), vmem_limit_bytes=VMEM_LIMIT),
        name=f"{kind}_out_ln",
    )(*args)


def _split_bf16(v):
    hi = v.astype(BF16)
    lo = (v - hi.astype(F32)).astype(BF16)
    return hi, lo


def _router_body(x_ref, sh_ref, sc_ref, rwt_ref, bias_ref, comb_ref):
    tm = x_ref.shape[0]
    e = N_EXPERTS
    per = e // N_GROUPS
    neg = -jnp.inf
    h = x_ref[...] * (1.0 + sc_ref[...]) + sh_ref[...]
    h_hi, h_lo = _split_bf16(h)
    w_hi, w_lo = _split_bf16(rwt_ref[...])
    dot = lambda a, b: lax.dot_general(a, b, _NT, preferred_element_type=F32)
    logits = dot(w_hi, h_hi) + (dot(w_hi, h_lo) + dot(w_lo, h_hi))
    s = jax.nn.sigmoid(logits)
    sel = s + bias_ref[...]

    g3 = sel.reshape(N_GROUPS, per, tm)
    sub = lax.broadcasted_iota(jnp.int32, (N_GROUPS, per, tm), 1)
    m1 = jnp.max(g3, axis=1, keepdims=True)
    i1 = jnp.min(jnp.where(g3 == m1, sub, per), axis=1, keepdims=True)
    m2 = jnp.max(jnp.where(sub == i1, neg, g3), axis=1, keepdims=True)
    gs = (m1 + m2).reshape(N_GROUPS, tm)

    gi = lax.broadcasted_iota(jnp.int32, (N_GROUPS, tm), 0)
    gmask = jnp.zeros((N_GROUPS, tm), jnp.bool_)
    cur = gs
    for _ in range(TOPK_GROUPS):
        m = jnp.max(cur, axis=0, keepdims=True)
        idx = jnp.min(jnp.where(cur == m, gi, N_GROUPS), axis=0, keepdims=True)
        pick = gi == idx
        gmask = jnp.logical_or(gmask, pick)
        cur = jnp.where(pick, neg, cur)
    emask = jnp.broadcast_to(gmask.reshape(N_GROUPS, 1, tm), (N_GROUPS, per, tm)).reshape(e, tm)

    ei = lax.broadcasted_iota(jnp.int32, (e, tm), 0)
    chosen = jnp.zeros((e, tm), jnp.bool_)
    cur = jnp.where(emask, sel, neg)
    for _ in range(TOP_K):
        m = jnp.max(cur, axis=0, keepdims=True)
        idx = jnp.min(jnp.where(cur == m, ei, e), axis=0, keepdims=True)
        pick = ei == idx
        chosen = jnp.logical_or(chosen, pick)
        cur = jnp.where(pick, neg, cur)

    w = jnp.where(chosen, s, 0.0)
    comb = w / jnp.sum(w, axis=0, keepdims=True) * ROUTED_SCALE
    comb = jnp.concatenate([comb, jnp.zeros((128 - e, tm), F32)], axis=0)
    comb_ref[...] = comb.T


def _router(x, mod, router_wt, bias_col):
    tm = TOK_BLOCK
    return pl.pallas_call(
        _router_body,
        grid=(N_TOK // tm,),
        in_specs=[
            pl.BlockSpec((tm, D_MODEL), lambda i: (i, 0)),
            _mod_spec(tm, 3, 1, 0),
            _mod_spec(tm, 4, 1, 0),
            pl.BlockSpec((N_EXPERTS, D_MODEL), lambda i: (0, 0)),
            pl.BlockSpec((N_EXPERTS, 1), lambda i: (0, 0)),
        ],
        out_specs=pl.BlockSpec((tm, 128), lambda i: (i, 0)),
        out_shape=jax.ShapeDtypeStruct((N_TOK, 128), F32),
        compiler_params=pltpu.CompilerParams(
            dimension_semantics=("arbitrary",), vmem_limit_bytes=VMEM_LIMIT),
        name="router",
    )(x, mod, mod, router_wt, bias_col)


def _ffn(h, wg, wu, wd, scale=None):
    hid = _silu(jnp.dot(h, wg, preferred_element_type=F32)) * jnp.dot(h, wu, preferred_element_type=F32)
    if scale is not None:
        hid = hid * scale
    return jnp.dot(hid.astype(BF16), wd, preferred_element_type=F32)


def _moe_body(x_ref, sh_ref, sc_ref, g2_ref, comb_ref, wg_ref, wu_ref, wd_ref, sg_ref, su_ref, sd_ref,
              lng_ref, lnb_ref, o_ref, h_ref, acc_ref):
    e = pl.program_id(1)

    @pl.when(e == 0)
    def _():
        h = (x_ref[...] * (1.0 + sc_ref[...]) + sh_ref[...]).astype(BF16)
        h_ref[...] = h
        acc_ref[...] = _ffn(h, sg_ref[...].astype(BF16), su_ref[...].astype(BF16), sd_ref[...].astype(BF16))

    lane = lax.broadcasted_iota(jnp.int32, comb_ref.shape, 1)
    scale = jnp.sum(jnp.where(lane == e, comb_ref[...], 0.0), axis=1, keepdims=True)
    acc_ref[...] += _ffn(h_ref[...], wg_ref[...].astype(BF16), wu_ref[...].astype(BF16),
                         wd_ref[...].astype(BF16), scale)

    @pl.when(e == N_EXPERTS - 1)
    def _():
        z = DEEPNORM_ALPHA * x_ref[...] + g2_ref[...] * acc_ref[...]
        o_ref[...] = _layer_norm(z, lng_ref[...], lnb_ref[...])


def _moe(x, mod, comb, layer, w_gate, w_up, w_down, s_gate, s_up, s_down, ln_g, ln_b):
    tm = MOE_TOK_BLOCK
    row = lambda i, e: (i, 0)
    vec = pl.BlockSpec((1, D_MODEL), lambda i, e: (0, 0))
    ew = lambda shape: pl.BlockSpec((None, None) + shape, lambda i, e: (layer, e, 0, 0))
    sw = lambda shape: pl.BlockSpec((None,) + shape, lambda i, e: (layer, 0, 0))
    return pl.pallas_call(
        _moe_body,
        grid=(N_TOK // tm, N_EXPERTS),
        in_specs=[
            pl.BlockSpec((tm, D_MODEL), row),
            _mod_spec(tm, 3, 2, 0), _mod_spec(tm, 4, 2, 0), _mod_spec(tm, 5, 2, 0),
            pl.BlockSpec((tm, 128), row),
            ew((D_MODEL, EXPERT_FF)), ew((D_MODEL, EXPERT_FF)), ew((EXPERT_FF, D_MODEL)),
            sw((D_MODEL, EXPERT_FF)), sw((D_MODEL, EXPERT_FF)), sw((EXPERT_FF, D_MODEL)),
            vec, vec,
        ],
        out_specs=pl.BlockSpec((tm, D_MODEL), row),
        out_shape=jax.ShapeDtypeStruct((N_TOK, D_MODEL), F32),
        scratch_shapes=[pltpu.VMEM((tm, D_MODEL), BF16), pltpu.VMEM((tm, D_MODEL), F32)],
        compiler_params=pltpu.CompilerParams(
            dimension_semantics=("arbitrary", "arbitrary"), vmem_limit_bytes=VMEM_LIMIT),
        name="moe_dense",
    )(x, mod, mod, mod, comb, w_gate, w_up, w_down, s_gate, s_up, s_down, ln_g, ln_b)


def kernel(x_prompt, x_sample, state_ret, c, c_ctx, ada_w, ada_b, ln_g, ln_b, ret_w_in, ret_w_out, ret_decay, conv_w_in, conv_w, conv_w_out, moe_router, moe_bias, moe_w_gate, moe_w_up, moe_w_down, shared_w_gate, shared_w_up, shared_w_down):
    x = jnp.concatenate([x_prompt.reshape(N_CTX, D_MODEL), x_sample.reshape(N_LAT, D_MODEL)], axis=0)
    cond = jnp.concatenate(
        [c_ctx[None, :], c, jnp.zeros((N_COND - 1 - DEC_BATCH, D_MODEL), F32)], axis=0)
    mods = _ada_table(cond, ada_w, ada_b).reshape(DEPTH, N_COND, 1, 6 * D_MODEL)
    cos, sin = _rope_tables()
    router_wt = jnp.swapaxes(moe_router, 1, 2)

    states = []
    for i in range(DEPTH):
        j = i // 2
        mod = mods[i]
        lng = ln_g[i].reshape(2, 1, D_MODEL)
        lnb = ln_b[i].reshape(2, 1, D_MODEL)
        if i % 2 == 0:
            proj = _modproj(x, mod, ret_w_in, j, 2048)
            a_ctx, s_f, s_b = _retention_ctx(proj, ret_decay[j])
            a_lat = _retention_lat(proj, ret_decay[j], state_ret, j, cos, sin)
            states.append(jnp.stack([s_f, s_b], axis=1))
            act = jnp.concatenate([a_ctx, a_lat], axis=0)
            x = _mixer_out("ret", act, x, mod, ret_w_out, j, lng[0], lnb[0])
        else:
            proj = _modproj(x, mod, conv_w_in, j, 1024)
            x = _mixer_out("conv", proj, x, mod, conv_w_out, j, lng[0], lnb[0], conv_w[j])
        comb = _router(x, mod, router_wt[i], moe_bias[i].reshape(N_EXPERTS, 1))
        x = _moe(x, mod, comb, i, moe_w_gate, moe_w_up, moe_w_down,
                 shared_w_gate, shared_w_up, shared_w_down, lng[1], lnb[1])

    y_prompt = x[:N_CTX].reshape(BATCH, SEQ, D_MODEL)
    y_sample = x[N_CTX:].reshape(DEC_BATCH, DEC_SEQ, D_MODEL)
    return y_prompt, y_sample, jnp.stack(states, axis=1)
```

```python
import functools

import jax
import jax.numpy as jnp
from jax import lax
from jax.experimental import pallas as pl
from jax.experimental.pallas import tpu as pltpu

F32 = jnp.float32
BF16 = jnp.bfloat16

D_MODEL = 1024
BATCH = 32
SEQ = 256
DEPTH = 4
DEC_BATCH = 8
DEC_SEQ = 1024
GRID_W = 64
RET_HEADS = 4
RET_DK = D_MODEL // RET_HEADS
RET_DV = 2 * D_MODEL // RET_HEADS
ROPE_BASE = 10000.0
N_EXPERTS = 64
TOP_K = 6
N_GROUPS = 8
TOPK_GROUPS = 4
EXPERT_FF = 256
ROUTED_SCALE = 2.5
LN_EPS = 1e-5
DEEPNORM_ALPHA = (2.0 * DEPTH) ** 0.25

N_CTX = BATCH * SEQ
N_LAT = DEC_BATCH * DEC_SEQ
N_TOK = N_CTX + N_LAT
N_COND = 16
RET_CHUNK = 256
TOK_BLOCK = 256
VMEM_LIMIT = 56 * 1024 * 1024

HALF = D_MODEL // 2
MOE_TILE = 512
MOE_TILE_LOG2 = 9
N_PAIR = N_TOK * TOP_K
N_SLOT = N_PAIR + N_EXPERTS * MOE_TILE
N_TILE = N_SLOT // MOE_TILE
K_PAD = 8

SC_CORES = 2
SC_SUBCORES = 16
SC_WORKERS = SC_CORES * SC_SUBCORES
SC_SCATTER_WIN = 64
SC_GATHER_WIN = 128


def _cond_row(tok_block_idx, tok_block):
    t0 = tok_block_idx * tok_block
    return jnp.where(t0 < N_CTX, 0, 1 + (t0 - N_CTX) // DEC_SEQ)


def _silu(x):
    return x * jax.nn.sigmoid(x)


def _layer_norm(z, g, b):
    mu = jnp.mean(z, axis=-1, keepdims=True)
    zc = z - mu
    var = jnp.mean(zc * zc, axis=-1, keepdims=True)
    return zc * lax.rsqrt(var + LN_EPS) * g + b


def _ada_body(cond_ref, w_ref, b_ref, o_ref):
    s = _silu(cond_ref[...]).astype(BF16)
    o_ref[...] = jnp.dot(s, w_ref[...].astype(BF16), preferred_element_type=F32) + b_ref[...]


def _ada_table(cond, ada_w, ada_b):
    tn = 2048
    return pl.pallas_call(
        _ada_body,
        grid=(DEPTH, 6 * D_MODEL // tn),
        in_specs=[
            pl.BlockSpec((N_COND, D_MODEL), lambda l, j: (0, 0)),
            pl.BlockSpec((None, D_MODEL, tn), lambda l, j: (l, 0, j)),
            pl.BlockSpec((None, 1, tn), lambda l, j: (l, 0, j)),
        ],
        out_specs=pl.BlockSpec((None, N_COND, tn), lambda l, j: (l, 0, j)),
        out_shape=jax.ShapeDtypeStruct((DEPTH, N_COND, 6 * D_MODEL), F32),
        compiler_params=pltpu.CompilerParams(
            dimension_semantics=("arbitrary", "arbitrary"), vmem_limit_bytes=VMEM_LIMIT),
        name="ada_table",
    )(cond, ada_w, ada_b.reshape(DEPTH, 1, 6 * D_MODEL))


def _mod_spec(tok_block, col, grid_rank, tok_axis):
    def index_map(*idx):
        return (_cond_row(idx[tok_axis], tok_block), 0, col)
    del grid_rank
    return pl.BlockSpec((None, 1, D_MODEL), index_map)


def _modproj_body(x_ref, sh_ref, sc_ref, w_ref, o_ref, wbf_ref):
    @pl.when(pl.program_id(1) == 0)
    def _():
        wbf_ref[...] = w_ref[...].astype(BF16)

    h = (x_ref[...] * (1.0 + sc_ref[...]) + sh_ref[...]).astype(BF16)
    o_ref[...] = jnp.dot(h, wbf_ref[...], preferred_element_type=F32).astype(o_ref.dtype)


def _modproj(x, mod, w_all, w_idx, tn):
    n_out = w_all.shape[-1]
    tm = 512
    return pl.pallas_call(
        _modproj_body,
        grid=(n_out // tn, N_TOK // tm),
        in_specs=[
            pl.BlockSpec((tm, D_MODEL), lambda j, i: (i, 0)),
            _mod_spec(tm, 0, 2, 1),
            _mod_spec(tm, 1, 2, 1),
            pl.BlockSpec((None, D_MODEL, tn), lambda j, i: (w_idx, 0, j)),
        ],
        out_specs=pl.BlockSpec((tm, tn), lambda j, i: (i, j)),
        out_shape=jax.ShapeDtypeStruct((N_TOK, n_out), BF16),
        scratch_shapes=[pltpu.VMEM((D_MODEL, tn), BF16)],
        compiler_params=pltpu.CompilerParams(
            dimension_semantics=("arbitrary", "arbitrary"), vmem_limit_bytes=VMEM_LIMIT),
        name="modproj",
    )(x, mod, mod, w_all)


def _log_sigmoid(v):
    return jnp.minimum(v, 0.0) - jnp.log1p(jnp.exp(-jnp.abs(v)))


def _decay_tables(dec_ref, head):
    c = RET_CHUNK
    lgf = _log_sigmoid(jnp.full((c, c), dec_ref[0, head], F32))
    lgb = _log_sigmoid(jnp.full((c, c), dec_ref[1, head], F32))
    row = lax.broadcasted_iota(jnp.int32, (c, c), 0).astype(F32)
    col = lax.broadcasted_iota(jnp.int32, (c, c), 1).astype(F32)
    diff = row - col
    kscale = RET_DK ** -0.5
    intra = jnp.where(diff > 0, jnp.exp(lgf * diff),
                      jnp.where(diff < 0, jnp.exp(-lgb * diff), 2.0)) * kscale
    qdec_f = jnp.exp(lgf * (row + 1.0))
    qdec_b = jnp.exp(lgb * (c - row))
    kdec_f = jnp.exp(lgf * (c - 1.0 - row)) * kscale
    kdec_b = jnp.exp(lgb * row) * kscale
    cdec_f = jnp.exp(lgf * c)
    cdec_b = jnp.exp(lgb * c)
    return intra, qdec_f, qdec_b, kdec_f, kdec_b, cdec_f, cdec_b


def _head_norm_gate(o, g):
    mu = jnp.mean(o, axis=-1, keepdims=True)
    oc = o - mu
    var = jnp.mean(oc * oc, axis=-1, keepdims=True)
    on = oc * lax.rsqrt(var + LN_EPS)
    return (_silu(g.astype(F32)) * on).astype(BF16)


_NT = (((1,), (1,)), ((), ()))
_TN = (((0,), (0,)), ((), ()))


def _ret_ctx_body(dec_ref, q_ref, k_ref, v_ref, g_ref, *rest):
    a_ref, st_ref = rest[-2:]
    head = pl.program_id(1)
    intra, _, _, kdec_f, kdec_b, _, _ = _decay_tables(dec_ref, head)
    q = q_ref[...]
    k = k_ref[...]
    v = v_ref[...]
    scores = lax.dot_general(q, k, _NT, preferred_element_type=F32)
    p = (scores * intra).astype(BF16)
    o = jnp.dot(p, v, preferred_element_type=F32)
    a_ref[...] = _head_norm_gate(o, g_ref[...])
    kf = k.astype(F32)
    st_ref[0] = lax.dot_general((kf * kdec_f).astype(BF16), v, _TN, preferred_element_type=F32)
    st_ref[1] = lax.dot_general((kf * kdec_b).astype(BF16), v, _TN, preferred_element_type=F32)


def _retention_ctx(proj, decay, ret_idx, states):
    n = SEQ
    assert n == RET_CHUNK
    n_ret = (DEPTH + 1) // 2
    st = jax.ShapeDtypeStruct((BATCH, n_ret, 2, RET_HEADS, RET_DK, RET_DV), F32)
    st_spec = pl.BlockSpec((None, None, 2, None, RET_DK, RET_DV), lambda b, h: (b, ret_idx, 0, h, 0, 0))
    in_specs = [
        pl.BlockSpec(memory_space=pltpu.SMEM),
        pl.BlockSpec((n, RET_DK), lambda b, h: (b, h)),
        pl.BlockSpec((n, RET_DK), lambda b, h: (b, RET_HEADS + h)),
        pl.BlockSpec((n, RET_DV), lambda b, h: (b, RET_HEADS + h)),
        pl.BlockSpec((n, RET_DV), lambda b, h: (b, 2 * RET_HEADS + h)),
    ]
    args = (decay, proj, proj, proj, proj)
    aliases = {}
    if states is not None:
        in_specs.append(pl.BlockSpec(memory_space=pl.ANY))
        args += (states,)
        aliases = {5: 1}
    return pl.pallas_call(
        _ret_ctx_body,
        grid=(BATCH, RET_HEADS),
        in_specs=in_specs,
        out_specs=[pl.BlockSpec((n, RET_DV), lambda b, h: (b, h)), st_spec],
        out_shape=[jax.ShapeDtypeStruct((N_CTX, RET_HEADS * RET_DV), BF16), st],
        input_output_aliases=aliases,
        compiler_params=pltpu.CompilerParams(
            dimension_semantics=("arbitrary", "arbitrary"), vmem_limit_bytes=VMEM_LIMIT),
        name="retention_ctx",
    )(*args)


def _rope(x, cos, sin):
    halves = [pltpu.roll(x[:, s:s + 128], 64, axis=1) for s in (0, 128)]
    return x * cos + jnp.concatenate(halves, axis=1) * sin


def _ret_lat_body(dec_ref, q_ref, k_ref, v_ref, g_ref, s0f_ref, s0b_ref, cos_ref, sin_ref,
                  a_ref, qr_ref, kr_ref, o_ref, st_ref):
    head = pl.program_id(1)
    c = RET_CHUNK
    nc = DEC_SEQ // c
    intra, qdec_f, qdec_b, kdec_f, kdec_b, cdec_f, cdec_b = _decay_tables(dec_ref, head)
    cdec_f = jnp.concatenate([cdec_f, cdec_f], axis=1)
    cdec_b = jnp.concatenate([cdec_b, cdec_b], axis=1)

    qr_ref[...] = _rope(q_ref[...].astype(F32), cos_ref[...], sin_ref[...]).astype(BF16)
    kr_ref[...] = _rope(k_ref[...].astype(F32), cos_ref[...], sin_ref[...]).astype(BF16)

    st_ref[...] = s0f_ref[...]
    for ci in range(nc):
        rows = pl.ds(ci * c, c)
        q = qr_ref[rows, :]
        k = kr_ref[rows, :]
        v = v_ref[rows, :]
        scores = lax.dot_general(q, k, _NT, preferred_element_type=F32)
        p = (scores * intra).astype(BF16)
        o = jnp.dot(p, v, preferred_element_type=F32)
        qd = (q.astype(F32) * qdec_f).astype(BF16)
        o = o + jnp.dot(qd, st_ref[...].astype(BF16), preferred_element_type=F32)
        o_ref[rows, :] = o
        if ci + 1 < nc:
            kd = (k.astype(F32) * kdec_f).astype(BF16)
            st_ref[...] = cdec_f * st_ref[...] + lax.dot_general(kd, v, _TN, preferred_element_type=F32)

    st_ref[...] = s0b_ref[...]
    for ci in reversed(range(nc)):
        rows = pl.ds(ci * c, c)
        q = qr_ref[rows, :]
        qd = (q.astype(F32) * qdec_b).astype(BF16)
        o = o_ref[rows, :] + jnp.dot(qd, st_ref[...].astype(BF16), preferred_element_type=F32)
        a_ref[rows, :] = _head_norm_gate(o, g_ref[rows, :])
        if ci > 0:
            k = kr_ref[rows, :]
            v = v_ref[rows, :]
            kd = (k.astype(F32) * kdec_b).astype(BF16)
            st_ref[...] = cdec_b * st_ref[...] + lax.dot_general(kd, v, _TN, preferred_element_type=F32)


def _rope_tables():
    half = RET_DK // 4
    freqs = ROPE_BASE ** (-jnp.arange(half, dtype=F32) / half)
    t = jnp.arange(DEC_SEQ)
    row = (t // GRID_W).astype(F32)
    col = (t % GRID_W).astype(F32)
    ang_r = row[:, None] * freqs[None, :]
    ang_c = col[:, None] * freqs[None, :]
    cos = jnp.concatenate([jnp.cos(ang_r)] * 2 + [jnp.cos(ang_c)] * 2, axis=1)
    sin = jnp.concatenate([-jnp.sin(ang_r), jnp.sin(ang_r), -jnp.sin(ang_c), jnp.sin(ang_c)], axis=1)
    return cos, sin


def _retention_lat(proj, decay, state_ret, ret_idx, cos, sin):
    n = DEC_SEQ
    off = N_CTX // n
    s0_spec = lambda d: pl.BlockSpec((None, None, None, None, RET_DK, RET_DV),
                                     lambda b, h: (b, ret_idx, d, h, 0, 0))
    tab_spec = pl.BlockSpec((n, RET_DK), lambda b, h: (0, 0))
    return pl.pallas_call(
        _ret_lat_body,
        grid=(DEC_BATCH, RET_HEADS),
        in_specs=[
            pl.BlockSpec(memory_space=pltpu.SMEM),
            pl.BlockSpec((n, RET_DK), lambda b, h: (off + b, h)),
            pl.BlockSpec((n, RET_DK), lambda b, h: (off + b, RET_HEADS + h)),
            pl.BlockSpec((n, RET_DV), lambda b, h: (off + b, RET_HEADS + h)),
            pl.BlockSpec((n, RET_DV), lambda b, h: (off + b, 2 * RET_HEADS + h)),
            s0_spec(0), s0_spec(1), tab_spec, tab_spec,
        ],
        out_specs=pl.BlockSpec((n, RET_DV), lambda b, h: (b, h)),
        out_shape=jax.ShapeDtypeStruct((N_LAT, RET_HEADS * RET_DV), BF16),
        scratch_shapes=[
            pltpu.VMEM((n, RET_DK), BF16), pltpu.VMEM((n, RET_DK), BF16),
            pltpu.VMEM((n, RET_DV), F32), pltpu.VMEM((RET_DK, RET_DV), F32),
        ],
        compiler_params=pltpu.CompilerParams(
            dimension_semantics=("arbitrary", "arbitrary"), vmem_limit_bytes=VMEM_LIMIT),
        name="retention_lat",
    )(decay, proj, proj, proj, proj, state_ret, state_ret, cos, sin)


def _mixer_out_tail(a, x_ref, g1_ref, w_ref, lng_ref, lnb_ref, o_ref, wbf_ref):
    @pl.when(pl.program_id(0) == 0)
    def _():
        wbf_ref[...] = w_ref[...].astype(BF16)

    y = jnp.dot(a, wbf_ref[...], preferred_element_type=F32)
    z = DEEPNORM_ALPHA * x_ref[...] + g1_ref[...] * y
    o_ref[...] = _layer_norm(z, lng_ref[...], lnb_ref[...])


def _ret_out_body(actx_ref, alat_ref, x_ref, g1_ref, w_ref, lng_ref, lnb_ref, o_ref, wbf_ref):
    is_ctx = pl.program_id(0) * TOK_BLOCK < N_CTX
    a = jnp.where(is_ctx, actx_ref[...], alat_ref[...])
    _mixer_out_tail(a, x_ref, g1_ref, w_ref, lng_ref, lnb_ref, o_ref, wbf_ref)


def _conv_out_body(bg_ref, cg_ref, xt_ref, cw_ref, x_ref, g1_ref, w_ref, lng_ref, lnb_ref, o_ref, wbf_ref):
    tm = TOK_BLOCK
    seg = jnp.where(pl.program_id(0) * tm < N_CTX, SEQ, GRID_W)
    u = cg_ref[...].astype(F32) * xt_ref[...].astype(F32)
    pos = lax.broadcasted_iota(jnp.int32, (tm, D_MODEL), 0) & (seg - 1)
    u_prev = jnp.where(pos == 0, 0.0, pltpu.roll(u, 1, axis=0))
    u_next = jnp.where(pos == seg - 1, 0.0, pltpu.roll(u, tm - 1, axis=0))
    cu = u_prev * cw_ref[0:1, :] + u * cw_ref[1:2, :] + u_next * cw_ref[2:3, :]
    a = (bg_ref[...].astype(F32) * cu).astype(BF16)
    _mixer_out_tail(a, x_ref, g1_ref, w_ref, lng_ref, lnb_ref, o_ref, wbf_ref)


def _mixer_out(kind, act, x, mod, w_all, w_idx, ln_g, ln_b, conv_w=None):
    tm = TOK_BLOCK
    k_dim = w_all.shape[1]
    row = lambda i: (i, 0)
    vec = pl.BlockSpec((1, D_MODEL), lambda i: (0, 0))
    tail_specs = [
        pl.BlockSpec((tm, D_MODEL), row),
        _mod_spec(tm, 2, 1, 0),
        pl.BlockSpec((None, k_dim, D_MODEL), lambda i: (w_idx, 0, 0)),
        vec, vec,
    ]
    if kind == "ret":
        body = _ret_out_body
        n_ctx_blk = N_CTX // tm
        in_specs = [
            pl.BlockSpec((tm, k_dim), lambda i: (jnp.minimum(i, n_ctx_blk - 1), 0)),
            pl.BlockSpec((tm, k_dim), lambda i: (jnp.maximum(i - n_ctx_blk, 0), 0)),
        ] + tail_specs
        args = (act[0], act[1], x, mod, w_all, ln_g, ln_b)
    else:
        body = _conv_out_body
        in_specs = [pl.BlockSpec((tm, D_MODEL), lambda i, c=c: (i, c)) for c in range(3)]
        in_specs += [pl.BlockSpec((3, D_MODEL), lambda i: (0, 0))] + tail_specs
        args = (act, act, act, conv_w, x, mod, w_all, ln_g, ln_b)
    return pl.pallas_call(
        body,
        grid=(N_TOK // tm,),
        in_specs=in_specs,
        out_specs=pl.BlockSpec((tm, D_MODEL), row),
        out_shape=jax.ShapeDtypeStruct((N_TOK, D_MODEL), F32),
        scratch_shapes=[pltpu.VMEM((k_dim, D_MODEL), BF16)],
        compiler_params=pltpu.CompilerParams(
            dimension_semantics=("arbitrary",), vmem_limit_bytes=VMEM_LIMIT),
        name=f"{kind}_out_ln",
    )(*args)


def _split_bf16(v):
    hi = v.astype(BF16)
    lo = (v - hi.astype(F32)).astype(BF16)
    return hi, lo


def _pack_bf16_pair(lo_f32, hi_f32):
    lo = lax.bitcast_convert_type(lo_f32.astype(BF16).astype(F32), jnp.uint32) >> 16
    hi = lax.bitcast_convert_type(hi_f32.astype(BF16).astype(F32), jnp.uint32) & jnp.uint32(0xFFFF0000)
    return hi | lo


def _unpack_bf16_pair(u):
    lo = lax.bitcast_convert_type(u << 16, F32)
    hi = lax.bitcast_convert_type(u & jnp.uint32(0xFFFF0000), F32)
    return lo, hi


def _rows_to_tile(rows, n_sub, dtype):
    tm = rows[0].shape[1]
    sub = lax.broadcasted_iota(jnp.int32, (n_sub, tm), 0)
    out = jnp.zeros((n_sub, tm), dtype)
    for k, r in enumerate(rows):
        out = jnp.where(sub == k, jnp.broadcast_to(r.astype(dtype), (n_sub, tm)), out)
    return out


def _router_body(x_ref, sh_ref, sc_ref, rwt_ref, bias_ref,
                 hp_ref, eidx_ref, rank_ref, wt_ref, cnt_ref, carry_ref):
    tm = x_ref.shape[0]
    e = N_EXPERTS
    per = e // N_GROUPS
    neg = -jnp.inf

    @pl.when(pl.program_id(0) == 0)
    def _():
        carry_ref[...] = jnp.zeros_like(carry_ref)

    h = x_ref[...] * (1.0 + sc_ref[...]) + sh_ref[...]
    hp_ref[...] = _pack_bf16_pair(h[:, :HALF], h[:, HALF:])
    h_hi, h_lo = _split_bf16(h)
    w_hi, w_lo = _split_bf16(rwt_ref[...])
    dot = lambda a, b: lax.dot_general(a, b, _NT, preferred_element_type=F32)
    logits = dot(w_hi, h_hi) + (dot(w_hi, h_lo) + dot(w_lo, h_hi))
    s = jax.nn.sigmoid(logits)
    sel = s + bias_ref[...]

    g3 = sel.reshape(N_GROUPS, per, tm)
    sub = lax.broadcasted_iota(jnp.int32, (N_GROUPS, per, tm), 1)
    m1 = jnp.max(g3, axis=1, keepdims=True)
    i1 = jnp.min(jnp.where(g3 == m1, sub, per), axis=1, keepdims=True)
    m2 = jnp.max(jnp.where(sub == i1, neg, g3), axis=1, keepdims=True)
    gs = (m1 + m2).reshape(N_GROUPS, tm)

    gi = lax.broadcasted_iota(jnp.int32, (N_GROUPS, tm), 0)
    gmask = jnp.zeros((N_GROUPS, tm), jnp.bool_)
    cur = gs
    for _ in range(TOPK_GROUPS):
        m = jnp.max(cur, axis=0, keepdims=True)
        idx = jnp.min(jnp.where(cur == m, gi, N_GROUPS), axis=0, keepdims=True)
        pick = gi == idx
        gmask = jnp.logical_or(gmask, pick)
        cur = jnp.where(pick, neg, cur)
    emask = jnp.broadcast_to(gmask.reshape(N_GROUPS, 1, tm), (N_GROUPS, per, tm)).reshape(e, tm)

    ei = lax.broadcasted_iota(jnp.int32, (e, tm), 0)
    picks, ids = [], []
    cur = jnp.where(emask, sel, neg)
    for _ in range(TOP_K):
        m = jnp.max(cur, axis=0, keepdims=True)
        idx = jnp.min(jnp.where(cur == m, ei, e), axis=0, keepdims=True)
        pick = ei == idx
        picks.append(pick)
        ids.append(idx)
        cur = jnp.where(pick, neg, cur)

    chosen = functools.reduce(jnp.logical_or, picks)
    cf = jnp.where(chosen, 1.0, 0.0)
    before = (lax.broadcasted_iota(jnp.int32, (tm, tm), 0)
              < lax.broadcasted_iota(jnp.int32, (tm, tm), 1)).astype(BF16)
    rank = carry_ref[:, 0:1] + jnp.dot(cf.astype(BF16), before, preferred_element_type=F32)
    carry_ref[...] = carry_ref[...] + jnp.sum(cf, axis=1, keepdims=True)
    cnt_ref[...] = carry_ref[...].astype(jnp.int32)

    w_rows = [jnp.sum(jnp.where(p, s, 0.0), axis=0, keepdims=True) for p in picks]
    r_rows = [jnp.sum(jnp.where(p, rank, 0.0), axis=0, keepdims=True) for p in picks]
    den = functools.reduce(lambda a, b: a + b, w_rows)
    w_rows = [w / den * ROUTED_SCALE for w in w_rows]
    eidx_ref[...] = _rows_to_tile(ids, K_PAD, jnp.int32)
    rank_ref[...] = _rows_to_tile(r_rows, K_PAD, F32).astype(jnp.int32)
    wt_ref[...] = _rows_to_tile(w_rows, 128, F32).T


def _router(x, mod, router_wt, bias_col):
    tm = TOK_BLOCK
    tok_major = lambda i: (i, 0)
    choice_major = lambda i: (0, i)
    return pl.pallas_call(
        _router_body,
        grid=(N_TOK // tm,),
        in_specs=[
            pl.BlockSpec((tm, D_MODEL), tok_major),
            _mod_spec(tm, 3, 1, 0),
            _mod_spec(tm, 4, 1, 0),
            pl.BlockSpec((N_EXPERTS, D_MODEL), lambda i: (0, 0)),
            pl.BlockSpec((N_EXPERTS, 1), lambda i: (0, 0)),
        ],
        out_specs=[
            pl.BlockSpec((tm, HALF), tok_major),
            pl.BlockSpec((K_PAD, tm), choice_major),
            pl.BlockSpec((K_PAD, tm), choice_major),
            pl.BlockSpec((tm, 128), tok_major),
            pl.BlockSpec((N_EXPERTS, 128), lambda i: (0, 0)),
        ],
        out_shape=[
            jax.ShapeDtypeStruct((N_TOK, HALF), jnp.uint32),
            jax.ShapeDtypeStruct((K_PAD, N_TOK), jnp.int32),
            jax.ShapeDtypeStruct((K_PAD, N_TOK), jnp.int32),
            jax.ShapeDtypeStruct((N_TOK, 128), F32),
            jax.ShapeDtypeStruct((N_EXPERTS, 128), jnp.int32),
        ],
        scratch_shapes=[pltpu.VMEM((N_EXPERTS, 128), F32)],
        compiler_params=pltpu.CompilerParams(
            dimension_semantics=("arbitrary",), vmem_limit_bytes=VMEM_LIMIT),
        name="router",
    )(x, mod, mod, router_wt, bias_col)


def _finalize_body(cnt_ref, eidx_ref, rank_ref, pos_ref, te_ref, nt_ref):
    pos_ref[...] = rank_ref[...]

    def per_expert(e, carry):
        off, t = carry
        n_tile = lax.shift_right_logical(cnt_ref[e] + (MOE_TILE - 1), MOE_TILE_LOG2)
        pos_ref[...] = pos_ref[...] + jnp.where(eidx_ref[...] == e, off, 0)

        def fill(j, c):
            te_ref[t + j] = e
            return c
        lax.fori_loop(0, n_tile, fill, 0)
        return off + n_tile * MOE_TILE, t + n_tile

    _, n_used = lax.fori_loop(0, N_EXPERTS, per_expert, (jnp.int32(0), jnp.int32(0)))
    nt_ref[0] = n_used
    last = te_ref[jnp.maximum(n_used - 1, 0)]

    def fill_tail(j, c):
        te_ref[j] = last
        return c
    lax.fori_loop(n_used, N_TILE, fill_tail, 0)


def _finalize(counts, eidx, rank):
    smem = pl.BlockSpec(memory_space=pltpu.SMEM)
    full = pl.BlockSpec((K_PAD, N_TOK), lambda: (0, 0))
    return pl.pallas_call(
        _finalize_body,
        in_specs=[smem, full, full],
        out_specs=[full, smem, smem],
        out_shape=[
            jax.ShapeDtypeStruct((K_PAD, N_TOK), jnp.int32),
            jax.ShapeDtypeStruct((N_TILE,), jnp.int32),
            jax.ShapeDtypeStruct((1,), jnp.int32),
        ],
        compiler_params=pltpu.CompilerParams(vmem_limit_bytes=VMEM_LIMIT),
        name="route_finalize",
    )(counts, eidx, rank)


def _ffn(h, wg, wu, wd):
    hid = _silu(jnp.dot(h, wg, preferred_element_type=F32)) * jnp.dot(h, wu, preferred_element_type=F32)
    return jnp.dot(hid.astype(BF16), wd, preferred_element_type=F32)


def _sc_mesh():
    from jax.experimental.pallas import tpu_sc as plsc
    return plsc.VectorSubcoreMesh(core_axis_name="c", subcore_axis_name="s",
                                  num_cores=SC_CORES, num_subcores=SC_SUBCORES)


def _sc_worker_id():
    return lax.axis_index("s") * SC_CORES + lax.axis_index("c")


def _sc_dispatch(hp, pos_flat):
    win = SC_SCATTER_WIN
    per_worker = N_TOK // SC_WORKERS

    def body(rows_hbm, idx_hbm, out_hbm, *scratch):
        idx_v, rows_v, sem = scratch[:TOP_K], scratch[TOP_K], scratch[TOP_K + 1]
        base = _sc_worker_id() * per_worker

        @pl.loop(0, per_worker // win)
        def _(j):
            off = base + j * win
            pltpu.sync_copy(rows_hbm.at[pl.ds(off, win)], rows_v)
            for k in range(TOP_K):
                pltpu.sync_copy(idx_hbm.at[pl.ds(k * N_TOK + off, win)], idx_v[k])
            copies = [pltpu.async_copy(rows_v, out_hbm.at[idx_v[k]], sem) for k in range(TOP_K)]
            for cp in copies:
                cp.wait()

    return pl.kernel(
        body, mesh=_sc_mesh(),
        out_type=jax.ShapeDtypeStruct((N_SLOT, HALF), jnp.uint32),
        scratch_types=[pltpu.VMEM((win,), jnp.int32)] * TOP_K
        + [pltpu.VMEM((win, HALF), jnp.uint32), pltpu.SemaphoreType.DMA],
        name="sc_dispatch",
    )(hp, pos_flat)


def _sc_return(ys, pos_flat):
    win = SC_GATHER_WIN
    per_worker = N_PAIR // SC_WORKERS

    def body(src_hbm, idx_hbm, out_hbm, idx_v, rows_v, sem):
        base = _sc_worker_id() * per_worker

        @pl.loop(0, per_worker // win)
        def _(j):
            off = base + j * win
            pltpu.sync_copy(idx_hbm.at[pl.ds(off, win)], idx_v)
            pltpu.async_copy(src_hbm.at[idx_v], rows_v, sem).wait()
            pltpu.sync_copy(rows_v, out_hbm.at[pl.ds(off, win)])

    return pl.kernel(
        body, mesh=_sc_mesh(),
        out_type=jax.ShapeDtypeStruct((N_PAIR, HALF), jnp.uint32),
        scratch_types=[pltpu.VMEM((win,), jnp.int32), pltpu.VMEM((win, HALF), jnp.uint32),
                       pltpu.SemaphoreType.DMA],
        name="sc_return",
    )(ys, pos_flat)


def _expert_body(te_ref, nt_ref, xs_ref, wg_ref, wu_ref, wd_ref, ys_ref, wgb_ref, wub_ref, wdb_ref):
    i = pl.program_id(0)

    @pl.when(jnp.logical_or(i == 0, te_ref[i] != te_ref[jnp.maximum(i - 1, 0)]))
    def _():
        wgb_ref[...] = wg_ref[...].astype(BF16)
        wub_ref[...] = wu_ref[...].astype(BF16)
        wdb_ref[...] = wd_ref[...].astype(BF16)

    @pl.when(i < nt_ref[0])
    def _():
        lo, hi = _unpack_bf16_pair(xs_ref[...])
        h = jnp.concatenate([lo.astype(BF16), hi.astype(BF16)], axis=1)
        y = _ffn(h, wgb_ref[...], wub_ref[...], wdb_ref[...])
        ys_ref[...] = _pack_bf16_pair(y[:, :HALF], y[:, HALF:])


def _expert_ffn(tile_expert, n_tiles, xs, layer, w_gate, w_up, w_down):
    tile = lambda i, te, nt: (jnp.minimum(i, nt[0] - 1), 0)
    ew = lambda shape: pl.BlockSpec((None, None) + shape, lambda i, te, nt: (layer, te[i], 0, 0))
    return pl.pallas_call(
        _expert_body,
        grid_spec=pltpu.PrefetchScalarGridSpec(
            num_scalar_prefetch=2,
            grid=(N_TILE,),
            in_specs=[
                pl.BlockSpec((MOE_TILE, HALF), tile),
                ew((D_MODEL, EXPERT_FF)), ew((D_MODEL, EXPERT_FF)), ew((EXPERT_FF, D_MODEL)),
            ],
            out_specs=pl.BlockSpec((MOE_TILE, HALF), tile),
            scratch_shapes=[pltpu.VMEM((D_MODEL, EXPERT_FF), BF16), pltpu.VMEM((D_MODEL, EXPERT_FF), BF16),
                            pltpu.VMEM((EXPERT_FF, D_MODEL), BF16)],
        ),
        out_shape=jax.ShapeDtypeStruct((N_SLOT, HALF), jnp.uint32),
        compiler_params=pltpu.CompilerParams(
            dimension_semantics=("arbitrary",), vmem_limit_bytes=VMEM_LIMIT),
        name="expert_ffn",
    )(tile_expert, n_tiles, xs, w_gate, w_up, w_down)


def _combine_body(x_ref, sh_ref, sc_ref, g2_ref, yk_ref, wt_ref, sg_ref, su_ref, sd_ref, lng_ref, lnb_ref,
                  o_ref, sgb_ref, sub_ref, sdb_ref):
    @pl.when(pl.program_id(0) == 0)
    def _():
        sgb_ref[...] = sg_ref[...].astype(BF16)
        sub_ref[...] = su_ref[...].astype(BF16)
        sdb_ref[...] = sd_ref[...].astype(BF16)

    x = x_ref[...]
    h = (x * (1.0 + sc_ref[...]) + sh_ref[...]).astype(BF16)
    y = _ffn(h, sgb_ref[...], sub_ref[...], sdb_ref[...])
    wt = wt_ref[...]
    lo_acc = jnp.zeros((x.shape[0], HALF), F32)
    hi_acc = jnp.zeros((x.shape[0], HALF), F32)
    for k in range(TOP_K):
        lo, hi = _unpack_bf16_pair(yk_ref[k])
        w = wt[:, k:k + 1]
        lo_acc = lo_acc + w * lo
        hi_acc = hi_acc + w * hi
    y = y + jnp.concatenate([lo_acc, hi_acc], axis=1)
    z = DEEPNORM_ALPHA * x + g2_ref[...] * y
    o_ref[...] = _layer_norm(z, lng_ref[...], lnb_ref[...])


def _combine(x, mod, yk, wt, layer, s_gate, s_up, s_down, ln_g, ln_b):
    tm = TOK_BLOCK
    row = lambda i: (i, 0)
    vec = pl.BlockSpec((1, D_MODEL), lambda i: (0, 0))
    sw = lambda shape: pl.BlockSpec((None,) + shape, lambda i: (layer, 0, 0))
    return pl.pallas_call(
        _combine_body,
        grid=(N_TOK // tm,),
        in_specs=[
            pl.BlockSpec((tm, D_MODEL), row),
            _mod_spec(tm, 3, 1, 0), _mod_spec(tm, 4, 1, 0), _mod_spec(tm, 5, 1, 0),
            pl.BlockSpec((TOP_K, tm, HALF), lambda i: (0, i, 0)),
            pl.BlockSpec((tm, 128), row),
            sw((D_MODEL, EXPERT_FF)), sw((D_MODEL, EXPERT_FF)), sw((EXPERT_FF, D_MODEL)),
            vec, vec,
        ],
        out_specs=pl.BlockSpec((tm, D_MODEL), row),
        out_shape=jax.ShapeDtypeStruct((N_TOK, D_MODEL), F32),
        scratch_shapes=[pltpu.VMEM((D_MODEL, EXPERT_FF), BF16), pltpu.VMEM((D_MODEL, EXPERT_FF), BF16),
                        pltpu.VMEM((EXPERT_FF, D_MODEL), BF16)],
        compiler_params=pltpu.CompilerParams(
            dimension_semantics=("arbitrary",), vmem_limit_bytes=VMEM_LIMIT),
        name="moe_combine_ln",
    )(x, mod, mod, mod, yk, wt, s_gate, s_up, s_down, ln_g, ln_b)


def _moe(x, mod, layer, router_wt, bias_col, w_gate, w_up, w_down, s_gate, s_up, s_down, ln_g, ln_b):
    hp, eidx, rank, wt, counts = _router(x, mod, router_wt, bias_col)
    pos, tile_expert, n_tiles = _finalize(counts[:, 0], eidx, rank)
    pos_flat = pos.reshape(K_PAD * N_TOK)
    xs = _sc_dispatch(hp, pos_flat)
    ys = _expert_ffn(tile_expert, n_tiles, xs, layer, w_gate, w_up, w_down)
    yk = _sc_return(ys, pos_flat).reshape(TOP_K, N_TOK, HALF)
    return _combine(x, mod, yk, wt, layer, s_gate, s_up, s_down, ln_g, ln_b)


def kernel(x_prompt, x_sample, state_ret, c, c_ctx, ada_w, ada_b, ln_g, ln_b, ret_w_in, ret_w_out, ret_decay, conv_w_in, conv_w, conv_w_out, moe_router, moe_bias, moe_w_gate, moe_w_up, moe_w_down, shared_w_gate, shared_w_up, shared_w_down):
    x = jnp.concatenate([x_prompt.reshape(N_CTX, D_MODEL), x_sample.reshape(N_LAT, D_MODEL)], axis=0)
    cond = jnp.concatenate(
        [c_ctx[None, :], c, jnp.zeros((N_COND - 1 - DEC_BATCH, D_MODEL), F32)], axis=0)
    mods = _ada_table(cond, ada_w, ada_b).reshape(DEPTH, N_COND, 1, 6 * D_MODEL)
    cos, sin = _rope_tables()
    router_wt = jnp.swapaxes(moe_router, 1, 2)

    states = None
    for i in range(DEPTH):
        j = i // 2
        mod = mods[i]
        lng = ln_g[i].reshape(2, 1, D_MODEL)
        lnb = ln_b[i].reshape(2, 1, D_MODEL)
        if i % 2 == 0:
            proj = _modproj(x, mod, ret_w_in, j, 2048)
            a_ctx, states = _retention_ctx(proj, ret_decay[j], j, states)
            a_lat = _retention_lat(proj, ret_decay[j], state_ret, j, cos, sin)
            x = _mixer_out("ret", (a_ctx, a_lat), x, mod, ret_w_out, j, lng[0], lnb[0])
        else:
            proj = _modproj(x, mod, conv_w_in, j, 1024)
            x = _mixer_out("conv", proj, x, mod, conv_w_out, j, lng[0], lnb[0], conv_w[j])
        x = _moe(x, mod, i, router_wt[i], moe_bias[i].reshape(N_EXPERTS, 1), moe_w_gate, moe_w_up, moe_w_down,
                 shared_w_gate, shared_w_up, shared_w_down, lng[1], lnb[1])

    y_prompt = x[:N_CTX].reshape(BATCH, SEQ, D_MODEL)
    y_sample = x[N_CTX:].reshape(DEC_BATCH, DEC_SEQ, D_MODEL)
    return y_prompt, y_sample, states
```

```python
import functools

import jax
import jax.numpy as jnp
from jax import lax
from jax.experimental import pallas as pl
from jax.experimental.pallas import tpu as pltpu

F32 = jnp.float32
BF16 = jnp.bfloat16

D_MODEL = 1024
BATCH = 32
SEQ = 256
DEPTH = 4
DEC_BATCH = 8
DEC_SEQ = 1024
GRID_W = 64
RET_HEADS = 4
RET_DK = D_MODEL // RET_HEADS
RET_DV = 2 * D_MODEL // RET_HEADS
ROPE_BASE = 10000.0
N_EXPERTS = 64
TOP_K = 6
N_GROUPS = 8
TOPK_GROUPS = 4
EXPERT_FF = 256
ROUTED_SCALE = 2.5
LN_EPS = 1e-5
DEEPNORM_ALPHA = (2.0 * DEPTH) ** 0.25

N_CTX = BATCH * SEQ
N_LAT = DEC_BATCH * DEC_SEQ
N_TOK = N_CTX + N_LAT
N_COND = 16
RET_CHUNK = 256
TOK_BLOCK = 256
VMEM_LIMIT = 56 * 1024 * 1024

HALF = D_MODEL // 2
MOE_TILE = 1024
MOE_TILE_LOG2 = 10
N_PAIR = N_TOK * TOP_K
N_SLOT = N_PAIR + N_EXPERTS * MOE_TILE
N_TILE = N_SLOT // MOE_TILE
K_PAD = 8

SC_CORES = 2
SC_SUBCORES = 16
SC_WORKERS = SC_CORES * SC_SUBCORES
SC_SCATTER_WIN = 64
SC_GATHER_WIN = 128


def _cond_row(tok_block_idx, tok_block):
    t0 = tok_block_idx * tok_block
    return jnp.where(t0 < N_CTX, 0, 1 + (t0 - N_CTX) // DEC_SEQ)


def _silu(x):
    return x * jax.nn.sigmoid(x)


def _layer_norm(z, g, b):
    mu = jnp.mean(z, axis=-1, keepdims=True)
    zc = z - mu
    var = jnp.mean(zc * zc, axis=-1, keepdims=True)
    return zc * lax.rsqrt(var + LN_EPS) * g + b


def _ada_body(cond_ref, w_ref, b_ref, o_ref):
    s = _silu(cond_ref[...]).astype(BF16)
    o_ref[...] = jnp.dot(s, w_ref[...].astype(BF16), preferred_element_type=F32) + b_ref[...]


def _ada_table(cond, ada_w, ada_b):
    tn = 2048
    return pl.pallas_call(
        _ada_body,
        grid=(DEPTH, 6 * D_MODEL // tn),
        in_specs=[
            pl.BlockSpec((N_COND, D_MODEL), lambda l, j: (0, 0)),
            pl.BlockSpec((None, D_MODEL, tn), lambda l, j: (l, 0, j)),
            pl.BlockSpec((None, 1, tn), lambda l, j: (l, 0, j)),
        ],
        out_specs=pl.BlockSpec((None, N_COND, tn), lambda l, j: (l, 0, j)),
        out_shape=jax.ShapeDtypeStruct((DEPTH, N_COND, 6 * D_MODEL), F32),
        compiler_params=pltpu.CompilerParams(
            dimension_semantics=("arbitrary", "arbitrary"), vmem_limit_bytes=VMEM_LIMIT),
        name="ada_table",
    )(cond, ada_w, ada_b.reshape(DEPTH, 1, 6 * D_MODEL))


def _mod_spec(tok_block, col, grid_rank, tok_axis):
    def index_map(*idx):
        return (_cond_row(idx[tok_axis], tok_block), 0, col)
    del grid_rank
    return pl.BlockSpec((None, 1, D_MODEL), index_map)


def _x_specs(x, tm, tok_axis):
    if not isinstance(x, tuple):
        return [pl.BlockSpec((tm, D_MODEL), lambda *idx: (idx[tok_axis], 0))], (x,)
    n_ctx_blk = N_CTX // tm
    return [
        pl.BlockSpec((tm, D_MODEL), lambda *idx: (jnp.minimum(idx[tok_axis], n_ctx_blk - 1), 0)),
        pl.BlockSpec((tm, D_MODEL), lambda *idx: (jnp.maximum(idx[tok_axis] - n_ctx_blk, 0), 0)),
    ], x


def _x_block(x_refs, tok_block_idx, tm):
    if len(x_refs) == 1:
        return x_refs[0][...]
    return jnp.where(tok_block_idx * tm < N_CTX, x_refs[0][...], x_refs[1][...])


def _modproj_body(*refs):
    x_refs, (sh_ref, sc_ref, w_ref, o_ref, wbf_ref) = refs[:-5], refs[-5:]

    @pl.when(pl.program_id(1) == 0)
    def _():
        wbf_ref[...] = w_ref[...].astype(BF16)

    x = _x_block(x_refs, pl.program_id(1), o_ref.shape[0])
    h = (x * (1.0 + sc_ref[...]) + sh_ref[...]).astype(BF16)
    o_ref[...] = jnp.dot(h, wbf_ref[...], preferred_element_type=F32).astype(o_ref.dtype)


def _modproj(x, mod, w_all, w_idx, tn):
    n_out = w_all.shape[-1]
    tm = 512
    x_specs, x_args = _x_specs(x, tm, 1)
    return pl.pallas_call(
        _modproj_body,
        grid=(n_out // tn, N_TOK // tm),
        in_specs=x_specs + [
            _mod_spec(tm, 0, 2, 1),
            _mod_spec(tm, 1, 2, 1),
            pl.BlockSpec((None, D_MODEL, tn), lambda j, i: (w_idx, 0, j)),
        ],
        out_specs=pl.BlockSpec((tm, tn), lambda j, i: (i, j)),
        out_shape=jax.ShapeDtypeStruct((N_TOK, n_out), BF16),
        scratch_shapes=[pltpu.VMEM((D_MODEL, tn), BF16)],
        compiler_params=pltpu.CompilerParams(
            dimension_semantics=("arbitrary", "arbitrary"), vmem_limit_bytes=VMEM_LIMIT),
        name="modproj",
    )(*x_args, mod, mod, w_all)


def _log_sigmoid(v):
    return jnp.minimum(v, 0.0) - jnp.log1p(jnp.exp(-jnp.abs(v)))


def _decay_tables(dec_ref, head):
    c = RET_CHUNK
    lgf = _log_sigmoid(jnp.full((c, c), dec_ref[0, head], F32))
    lgb = _log_sigmoid(jnp.full((c, c), dec_ref[1, head], F32))
    row = lax.broadcasted_iota(jnp.int32, (c, c), 0).astype(F32)
    col = lax.broadcasted_iota(jnp.int32, (c, c), 1).astype(F32)
    diff = row - col
    kscale = RET_DK ** -0.5
    intra = jnp.where(diff > 0, jnp.exp(lgf * diff),
                      jnp.where(diff < 0, jnp.exp(-lgb * diff), 2.0)) * kscale
    qdec_f = jnp.exp(lgf * (row + 1.0))
    qdec_b = jnp.exp(lgb * (c - row))
    kdec_f = jnp.exp(lgf * (c - 1.0 - row)) * kscale
    kdec_b = jnp.exp(lgb * row) * kscale
    cdec_f = jnp.exp(lgf * c)
    cdec_b = jnp.exp(lgb * c)
    return intra, qdec_f, qdec_b, kdec_f, kdec_b, cdec_f, cdec_b


def _head_norm_gate(o, g):
    mu = jnp.mean(o, axis=-1, keepdims=True)
    oc = o - mu
    var = jnp.mean(oc * oc, axis=-1, keepdims=True)
    on = oc * lax.rsqrt(var + LN_EPS)
    return (_silu(g.astype(F32)) * on).astype(BF16)


_NT = (((1,), (1,)), ((), ()))
_TN = (((0,), (0,)), ((), ()))


def _ret_ctx_body(dec_ref, proj_ref, *rest):
    a_ref, st_ref, tab_ref = rest[-3:]

    @pl.when(pl.program_id(0) == 0)
    def _():
        for head in range(RET_HEADS):
            intra, _, _, kdec_f, kdec_b, _, _ = _decay_tables(dec_ref, head)
            tab_ref[head, 0] = intra
            tab_ref[head, 1] = kdec_f
            tab_ref[head, 2] = kdec_b

    for head in range(RET_HEADS):
        q = proj_ref[:, head * RET_DK:(head + 1) * RET_DK]
        k = proj_ref[:, D_MODEL + head * RET_DK:D_MODEL + (head + 1) * RET_DK]
        v = proj_ref[:, 2 * D_MODEL + head * RET_DV:2 * D_MODEL + (head + 1) * RET_DV]
        g = proj_ref[:, 4 * D_MODEL + head * RET_DV:4 * D_MODEL + (head + 1) * RET_DV]
        scores = lax.dot_general(q, k, _NT, preferred_element_type=F32)
        p = (scores * tab_ref[head, 0]).astype(BF16)
        o = jnp.dot(p, v, preferred_element_type=F32)
        a_ref[:, head * RET_DV:(head + 1) * RET_DV] = _head_norm_gate(o, g)
        kf = k.astype(F32)
        st_ref[0, head] = lax.dot_general((kf * tab_ref[head, 1]).astype(BF16), v, _TN,
                                          preferred_element_type=F32)
        st_ref[1, head] = lax.dot_general((kf * tab_ref[head, 2]).astype(BF16), v, _TN,
                                          preferred_element_type=F32)


def _retention_ctx(proj, decay, ret_idx, states):
    n = SEQ
    assert n == RET_CHUNK
    n_ret = (DEPTH + 1) // 2
    st = jax.ShapeDtypeStruct((BATCH, n_ret, 2, RET_HEADS, RET_DK, RET_DV), F32)
    st_spec = pl.BlockSpec((None, None, 2, RET_HEADS, RET_DK, RET_DV), lambda b: (b, ret_idx, 0, 0, 0, 0))
    in_specs = [
        pl.BlockSpec(memory_space=pltpu.SMEM),
        pl.BlockSpec((n, 6 * D_MODEL), lambda b: (b, 0)),
    ]
    args = (decay, proj)
    aliases = {}
    if states is not None:
        in_specs.append(pl.BlockSpec(memory_space=pl.ANY))
        args += (states,)
        aliases = {2: 1}
    return pl.pallas_call(
        _ret_ctx_body,
        grid=(BATCH,),
        in_specs=in_specs,
        out_specs=[pl.BlockSpec((n, RET_HEADS * RET_DV), lambda b: (b, 0)), st_spec],
        out_shape=[jax.ShapeDtypeStruct((N_CTX, RET_HEADS * RET_DV), BF16), st],
        input_output_aliases=aliases,
        scratch_shapes=[pltpu.VMEM((RET_HEADS, 3, RET_CHUNK, RET_CHUNK), F32)],
        compiler_params=pltpu.CompilerParams(
            dimension_semantics=("arbitrary",), vmem_limit_bytes=VMEM_LIMIT),
        name="retention_ctx",
    )(*args)


def _rope(x, cos, sin):
    halves = [pltpu.roll(x[:, s:s + 128], 64, axis=1) for s in (0, 128)]
    return x * cos + jnp.concatenate(halves, axis=1) * sin


def _ret_lat_body(dec_ref, q_ref, k_ref, v_ref, g_ref, s0f_ref, s0b_ref, cos_ref, sin_ref,
                  a_ref, qr_ref, kr_ref, o_ref, st_ref):
    head = pl.program_id(1)
    c = RET_CHUNK
    nc = DEC_SEQ // c
    intra, qdec_f, qdec_b, kdec_f, kdec_b, cdec_f, cdec_b = _decay_tables(dec_ref, head)
    cdec_f = jnp.concatenate([cdec_f, cdec_f], axis=1)
    cdec_b = jnp.concatenate([cdec_b, cdec_b], axis=1)

    qr_ref[...] = _rope(q_ref[...].astype(F32), cos_ref[...], sin_ref[...]).astype(BF16)
    kr_ref[...] = _rope(k_ref[...].astype(F32), cos_ref[...], sin_ref[...]).astype(BF16)

    st_ref[...] = s0f_ref[...]
    for ci in range(nc):
        rows = pl.ds(ci * c, c)
        q = qr_ref[rows, :]
        k = kr_ref[rows, :]
        v = v_ref[rows, :]
        scores = lax.dot_general(q, k, _NT, preferred_element_type=F32)
        p = (scores * intra).astype(BF16)
        o = jnp.dot(p, v, preferred_element_type=F32)
        qd = (q.astype(F32) * qdec_f).astype(BF16)
        o = o + jnp.dot(qd, st_ref[...].astype(BF16), preferred_element_type=F32)
        o_ref[rows, :] = o
        if ci + 1 < nc:
            kd = (k.astype(F32) * kdec_f).astype(BF16)
            st_ref[...] = cdec_f * st_ref[...] + lax.dot_general(kd, v, _TN, preferred_element_type=F32)

    st_ref[...] = s0b_ref[...]
    for ci in reversed(range(nc)):
        rows = pl.ds(ci * c, c)
        q = qr_ref[rows, :]
        qd = (q.astype(F32) * qdec_b).astype(BF16)
        o = o_ref[rows, :] + jnp.dot(qd, st_ref[...].astype(BF16), preferred_element_type=F32)
        a_ref[rows, :] = _head_norm_gate(o, g_ref[rows, :])
        if ci > 0:
            k = kr_ref[rows, :]
            v = v_ref[rows, :]
            kd = (k.astype(F32) * kdec_b).astype(BF16)
            st_ref[...] = cdec_b * st_ref[...] + lax.dot_general(kd, v, _TN, preferred_element_type=F32)


def _rope_tables():
    half = RET_DK // 4
    freqs = ROPE_BASE ** (-jnp.arange(half, dtype=F32) / half)
    t = jnp.arange(DEC_SEQ)
    row = (t // GRID_W).astype(F32)
    col = (t % GRID_W).astype(F32)
    ang_r = row[:, None] * freqs[None, :]
    ang_c = col[:, None] * freqs[None, :]
    cos = jnp.concatenate([jnp.cos(ang_r)] * 2 + [jnp.cos(ang_c)] * 2, axis=1)
    sin = jnp.concatenate([-jnp.sin(ang_r), jnp.sin(ang_r), -jnp.sin(ang_c), jnp.sin(ang_c)], axis=1)
    return cos, sin


def _retention_lat(proj, decay, state_ret, ret_idx, cos, sin):
    n = DEC_SEQ
    off = N_CTX // n
    s0_spec = lambda d: pl.BlockSpec((None, None, None, None, RET_DK, RET_DV),
                                     lambda b, h: (b, ret_idx, d, h, 0, 0))
    tab_spec = pl.BlockSpec((n, RET_DK), lambda b, h: (0, 0))
    return pl.pallas_call(
        _ret_lat_body,
        grid=(DEC_BATCH, RET_HEADS),
        in_specs=[
            pl.BlockSpec(memory_space=pltpu.SMEM),
            pl.BlockSpec((n, RET_DK), lambda b, h: (off + b, h)),
            pl.BlockSpec((n, RET_DK), lambda b, h: (off + b, RET_HEADS + h)),
            pl.BlockSpec((n, RET_DV), lambda b, h: (off + b, RET_HEADS + h)),
            pl.BlockSpec((n, RET_DV), lambda b, h: (off + b, 2 * RET_HEADS + h)),
            s0_spec(0), s0_spec(1), tab_spec, tab_spec,
        ],
        out_specs=pl.BlockSpec((n, RET_DV), lambda b, h: (b, h)),
        out_shape=jax.ShapeDtypeStruct((N_LAT, RET_HEADS * RET_DV), BF16),
        scratch_shapes=[
            pltpu.VMEM((n, RET_DK), BF16), pltpu.VMEM((n, RET_DK), BF16),
            pltpu.VMEM((n, RET_DV), F32), pltpu.VMEM((RET_DK, RET_DV), F32),
        ],
        compiler_params=pltpu.CompilerParams(
            dimension_semantics=("arbitrary", "arbitrary"), vmem_limit_bytes=VMEM_LIMIT),
        name="retention_lat",
    )(decay, proj, proj, proj, proj, state_ret, state_ret, cos, sin)


def _ret_out_body(actx_ref, alat_ref, *refs):
    x_refs, (g1_ref, w_ref, lng_ref, lnb_ref, o_ref, wbf_ref) = refs[:-6], refs[-6:]
    i = pl.program_id(0)

    @pl.when(i == 0)
    def _():
        wbf_ref[...] = w_ref[...].astype(BF16)

    a = jnp.where(i * TOK_BLOCK < N_CTX, actx_ref[...], alat_ref[...])
    y = jnp.dot(a, wbf_ref[...], preferred_element_type=F32)
    z = DEEPNORM_ALPHA * _x_block(x_refs, i, TOK_BLOCK) + g1_ref[...] * y
    o_ref[...] = _layer_norm(z, lng_ref[...], lnb_ref[...])


def _ret_out(a_ctx, a_lat, x, mod, w_all, w_idx, ln_g, ln_b):
    tm = TOK_BLOCK
    k_dim = w_all.shape[1]
    n_ctx_blk = N_CTX // tm
    vec = pl.BlockSpec((1, D_MODEL), lambda i: (0, 0))
    x_specs, x_args = _x_specs(x, tm, 0)
    return pl.pallas_call(
        _ret_out_body,
        grid=(N_TOK // tm,),
        in_specs=[
            pl.BlockSpec((tm, k_dim), lambda i: (jnp.minimum(i, n_ctx_blk - 1), 0)),
            pl.BlockSpec((tm, k_dim), lambda i: (jnp.maximum(i - n_ctx_blk, 0), 0)),
        ] + x_specs + [
            _mod_spec(tm, 2, 1, 0),
            pl.BlockSpec((None, k_dim, D_MODEL), lambda i: (w_idx, 0, 0)),
            vec, vec,
        ],
        out_specs=pl.BlockSpec((tm, D_MODEL), lambda i: (i, 0)),
        out_shape=jax.ShapeDtypeStruct((N_TOK, D_MODEL), F32),
        scratch_shapes=[pltpu.VMEM((k_dim, D_MODEL), BF16)],
        compiler_params=pltpu.CompilerParams(
            dimension_semantics=("arbitrary",), vmem_limit_bytes=VMEM_LIMIT),
        name="ret_out_ln",
    )(a_ctx, a_lat, *x_args, mod, w_all, ln_g, ln_b)


def _conv_layer_body(x_ref, sh_ref, sc_ref, g1_ref, win_ref, wout_ref, cw_ref, lng_ref, lnb_ref,
                     o_ref, winb_ref, woutb_ref):
    tm = TOK_BLOCK

    @pl.when(pl.program_id(0) == 0)
    def _():
        winb_ref[...] = win_ref[...].astype(BF16)
        woutb_ref[...] = wout_ref[...].astype(BF16)

    x = x_ref[...]
    h = (x * (1.0 + sc_ref[...]) + sh_ref[...]).astype(BF16)
    proj = jnp.dot(h, winb_ref[...], preferred_element_type=F32)
    bg, cg, xt = (proj[:, c * D_MODEL:(c + 1) * D_MODEL] for c in range(3))
    seg = jnp.where(pl.program_id(0) * tm < N_CTX, SEQ, GRID_W)
    u = cg * xt
    pos = lax.broadcasted_iota(jnp.int32, (tm, D_MODEL), 0) & (seg - 1)
    u_prev = jnp.where(pos == 0, 0.0, pltpu.roll(u, 1, axis=0))
    u_next = jnp.where(pos == seg - 1, 0.0, pltpu.roll(u, tm - 1, axis=0))
    cu = u_prev * cw_ref[0:1, :] + u * cw_ref[1:2, :] + u_next * cw_ref[2:3, :]
    y = jnp.dot((bg * cu).astype(BF16), woutb_ref[...], preferred_element_type=F32)
    z = DEEPNORM_ALPHA * x + g1_ref[...] * y
    o_ref[...] = _layer_norm(z, lng_ref[...], lnb_ref[...])


def _conv_layer(x, mod, w_in_all, w_out_all, w_idx, conv_w, ln_g, ln_b):
    tm = TOK_BLOCK
    row = lambda i: (i, 0)
    vec = pl.BlockSpec((1, D_MODEL), lambda i: (0, 0))
    once = pl.Buffered(1)
    return pl.pallas_call(
        _conv_layer_body,
        grid=(N_TOK // tm,),
        in_specs=[
            pl.BlockSpec((tm, D_MODEL), row),
            _mod_spec(tm, 0, 1, 0), _mod_spec(tm, 1, 1, 0), _mod_spec(tm, 2, 1, 0),
            pl.BlockSpec((None, D_MODEL, 3 * D_MODEL), lambda i: (w_idx, 0, 0), pipeline_mode=once),
            pl.BlockSpec((None, D_MODEL, D_MODEL), lambda i: (w_idx, 0, 0), pipeline_mode=once),
            pl.BlockSpec((3, D_MODEL), lambda i: (0, 0)),
            vec, vec,
        ],
        out_specs=pl.BlockSpec((tm, D_MODEL), row),
        out_shape=jax.ShapeDtypeStruct((N_TOK, D_MODEL), F32),
        scratch_shapes=[pltpu.VMEM((D_MODEL, 3 * D_MODEL), BF16), pltpu.VMEM((D_MODEL, D_MODEL), BF16)],
        compiler_params=pltpu.CompilerParams(
            dimension_semantics=("arbitrary",), vmem_limit_bytes=VMEM_LIMIT),
        name="conv_layer",
    )(x, mod, mod, mod, w_in_all, w_out_all, conv_w, ln_g, ln_b)


def _split_bf16(v):
    hi = v.astype(BF16)
    lo = (v - hi.astype(F32)).astype(BF16)
    return hi, lo


def _pack_bf16_pair(lo_f32, hi_f32):
    lo = lax.bitcast_convert_type(lo_f32.astype(BF16).astype(F32), jnp.uint32) >> 16
    hi = lax.bitcast_convert_type(hi_f32.astype(BF16).astype(F32), jnp.uint32) & jnp.uint32(0xFFFF0000)
    return hi | lo


def _unpack_bf16_pair(u):
    lo = lax.bitcast_convert_type(u << 16, F32)
    hi = lax.bitcast_convert_type(u & jnp.uint32(0xFFFF0000), F32)
    return lo, hi


def _rows_to_tile(rows, n_sub, dtype):
    tm = rows[0].shape[1]
    sub = lax.broadcasted_iota(jnp.int32, (n_sub, tm), 0)
    out = jnp.zeros((n_sub, tm), dtype)
    for k, r in enumerate(rows):
        out = jnp.where(sub == k, jnp.broadcast_to(r.astype(dtype), (n_sub, tm)), out)
    return out


def _router_body(x_ref, sh_ref, sc_ref, rwt_ref, bias_ref,
                 hp_ref, eidx_ref, rank_ref, wt_ref, cnt_ref, carry_ref):
    tm = x_ref.shape[0]
    e = N_EXPERTS
    per = e // N_GROUPS
    neg = -jnp.inf

    @pl.when(pl.program_id(0) == 0)
    def _():
        carry_ref[...] = jnp.zeros_like(carry_ref)

    h = x_ref[...] * (1.0 + sc_ref[...]) + sh_ref[...]
    hp_ref[...] = _pack_bf16_pair(h[:, :HALF], h[:, HALF:])
    h_hi, h_lo = _split_bf16(h)
    w_hi, w_lo = _split_bf16(rwt_ref[...])
    dot = lambda a, b: lax.dot_general(a, b, _NT, preferred_element_type=F32)
    logits = dot(w_hi, h_hi) + (dot(w_hi, h_lo) + dot(w_lo, h_hi))
    s = jax.nn.sigmoid(logits)
    sel = s + bias_ref[...]

    g3 = sel.reshape(N_GROUPS, per, tm)
    sub = lax.broadcasted_iota(jnp.int32, (N_GROUPS, per, tm), 1)
    m1 = jnp.max(g3, axis=1, keepdims=True)
    i1 = jnp.min(jnp.where(g3 == m1, sub, per), axis=1, keepdims=True)
    m2 = jnp.max(jnp.where(sub == i1, neg, g3), axis=1, keepdims=True)
    gs = (m1 + m2).reshape(N_GROUPS, tm)

    gi = lax.broadcasted_iota(jnp.int32, (N_GROUPS, tm), 0)
    gmask = jnp.zeros((N_GROUPS, tm), jnp.bool_)
    cur = gs
    for _ in range(TOPK_GROUPS):
        m = jnp.max(cur, axis=0, keepdims=True)
        idx = jnp.min(jnp.where(cur == m, gi, N_GROUPS), axis=0, keepdims=True)
        pick = gi == idx
        gmask = jnp.logical_or(gmask, pick)
        cur = jnp.where(pick, neg, cur)
    emask = jnp.broadcast_to(gmask.reshape(N_GROUPS, 1, tm), (N_GROUPS, per, tm)).reshape(e, tm)

    ei = lax.broadcasted_iota(jnp.int32, (e, tm), 0)
    picks, ids = [], []
    cur = jnp.where(emask, sel, neg)
    for _ in range(TOP_K):
        m = jnp.max(cur, axis=0, keepdims=True)
        idx = jnp.min(jnp.where(cur == m, ei, e), axis=0, keepdims=True)
        pick = ei == idx
        picks.append(pick)
        ids.append(idx)
        cur = jnp.where(pick, neg, cur)

    chosen = functools.reduce(jnp.logical_or, picks)
    cf = jnp.where(chosen, 1.0, 0.0)
    before = (lax.broadcasted_iota(jnp.int32, (tm, tm), 0)
              < lax.broadcasted_iota(jnp.int32, (tm, tm), 1)).astype(BF16)
    rank = carry_ref[:, 0:1] + jnp.dot(cf.astype(BF16), before, preferred_element_type=F32)
    carry_ref[...] = carry_ref[...] + jnp.sum(cf, axis=1, keepdims=True)
    cnt_ref[...] = carry_ref[...].astype(jnp.int32)

    w_rows = [jnp.sum(jnp.where(p, s, 0.0), axis=0, keepdims=True) for p in picks]
    r_rows = [jnp.sum(jnp.where(p, rank, 0.0), axis=0, keepdims=True) for p in picks]
    den = functools.reduce(lambda a, b: a + b, w_rows)
    w_rows = [w / den * ROUTED_SCALE for w in w_rows]
    eidx_ref[...] = _rows_to_tile(ids, K_PAD, jnp.int32)
    rank_ref[...] = _rows_to_tile(r_rows, K_PAD, F32).astype(jnp.int32)
    wt_ref[...] = _rows_to_tile(w_rows, 128, F32).T


def _router(x, mod, router_wt, bias_col):
    tm = TOK_BLOCK
    tok_major = lambda i: (i, 0)
    choice_major = lambda i: (0, i)
    return pl.pallas_call(
        _router_body,
        grid=(N_TOK // tm,),
        in_specs=[
            pl.BlockSpec((tm, D_MODEL), tok_major),
            _mod_spec(tm, 3, 1, 0),
            _mod_spec(tm, 4, 1, 0),
            pl.BlockSpec((N_EXPERTS, D_MODEL), lambda i: (0, 0)),
            pl.BlockSpec((N_EXPERTS, 1), lambda i: (0, 0)),
        ],
        out_specs=[
            pl.BlockSpec((tm, HALF), tok_major),
            pl.BlockSpec((K_PAD, tm), choice_major),
            pl.BlockSpec((K_PAD, tm), choice_major),
            pl.BlockSpec((tm, 128), tok_major),
            pl.BlockSpec((N_EXPERTS, 128), lambda i: (0, 0)),
        ],
        out_shape=[
            jax.ShapeDtypeStruct((N_TOK, HALF), jnp.uint32),
            jax.ShapeDtypeStruct((K_PAD, N_TOK), jnp.int32),
            jax.ShapeDtypeStruct((K_PAD, N_TOK), jnp.int32),
            jax.ShapeDtypeStruct((N_TOK, 128), F32),
            jax.ShapeDtypeStruct((N_EXPERTS, 128), jnp.int32),
        ],
        scratch_shapes=[pltpu.VMEM((N_EXPERTS, 128), F32)],
        compiler_params=pltpu.CompilerParams(
            dimension_semantics=("arbitrary",), vmem_limit_bytes=VMEM_LIMIT),
        name="router",
    )(x, mod, mod, router_wt, bias_col)


def _finalize_body(cnt_ref, eidx_ref, rank_ref, pos_ref, te_ref, nt_ref):
    pos_ref[...] = rank_ref[...]

    def per_expert(e, carry):
        off, t = carry
        n_tile = lax.shift_right_logical(cnt_ref[e] + (MOE_TILE - 1), MOE_TILE_LOG2)
        pos_ref[...] = pos_ref[...] + jnp.where(eidx_ref[...] == e, off, 0)

        def fill(j, c):
            te_ref[t + j] = e
            return c
        lax.fori_loop(0, n_tile, fill, 0)
        return off + n_tile * MOE_TILE, t + n_tile

    _, n_used = lax.fori_loop(0, N_EXPERTS, per_expert, (jnp.int32(0), jnp.int32(0)))
    nt_ref[0] = n_used
    last = te_ref[jnp.maximum(n_used - 1, 0)]

    def fill_tail(j, c):
        te_ref[j] = last
        return c
    lax.fori_loop(n_used, N_TILE, fill_tail, 0)


def _finalize(counts, eidx, rank):
    smem = pl.BlockSpec(memory_space=pltpu.SMEM)
    full = pl.BlockSpec((K_PAD, N_TOK), lambda: (0, 0))
    return pl.pallas_call(
        _finalize_body,
        in_specs=[smem, full, full],
        out_specs=[full, smem, smem],
        out_shape=[
            jax.ShapeDtypeStruct((K_PAD, N_TOK), jnp.int32),
            jax.ShapeDtypeStruct((N_TILE,), jnp.int32),
            jax.ShapeDtypeStruct((1,), jnp.int32),
        ],
        compiler_params=pltpu.CompilerParams(vmem_limit_bytes=VMEM_LIMIT),
        name="route_finalize",
    )(counts, eidx, rank)


def _ffn(h, wg, wu, wd):
    hid = _silu(jnp.dot(h, wg, preferred_element_type=F32)) * jnp.dot(h, wu, preferred_element_type=F32)
    return jnp.dot(hid.astype(BF16), wd, preferred_element_type=F32)


def _sc_mesh():
    from jax.experimental.pallas import tpu_sc as plsc
    return plsc.VectorSubcoreMesh(core_axis_name="c", subcore_axis_name="s",
                                  num_cores=SC_CORES, num_subcores=SC_SUBCORES)


def _sc_worker_id():
    return lax.axis_index("s") * SC_CORES + lax.axis_index("c")


def _sc_dispatch(hp, pos_flat):
    win = SC_SCATTER_WIN
    per_worker = N_TOK // SC_WORKERS

    def body(rows_hbm, idx_hbm, out_hbm, *scratch):
        idx_v, rows_v, sem = scratch[:TOP_K], scratch[TOP_K], scratch[TOP_K + 1]
        base = _sc_worker_id() * per_worker

        @pl.loop(0, per_worker // win)
        def _(j):
            off = base + j * win
            pltpu.sync_copy(rows_hbm.at[pl.ds(off, win)], rows_v)
            for k in range(TOP_K):
                pltpu.sync_copy(idx_hbm.at[pl.ds(k * N_TOK + off, win)], idx_v[k])
            copies = [pltpu.async_copy(rows_v, out_hbm.at[idx_v[k]], sem) for k in range(TOP_K)]
            for cp in copies:
                cp.wait()

    return pl.kernel(
        body, mesh=_sc_mesh(),
        out_type=jax.ShapeDtypeStruct((N_SLOT, HALF), jnp.uint32),
        scratch_types=[pltpu.VMEM((win,), jnp.int32)] * TOP_K
        + [pltpu.VMEM((win, HALF), jnp.uint32), pltpu.SemaphoreType.DMA],
        name="sc_dispatch",
    )(hp, pos_flat)


def _sc_return(ys, pos_flat):
    win = SC_GATHER_WIN
    per_worker = N_PAIR // SC_WORKERS

    def body(src_hbm, idx_hbm, out_hbm, idx_v, rows_v, sem):
        base = _sc_worker_id() * per_worker

        @pl.loop(0, per_worker // win)
        def _(j):
            off = base + j * win
            pltpu.sync_copy(idx_hbm.at[pl.ds(off, win)], idx_v)
            pltpu.async_copy(src_hbm.at[idx_v], rows_v, sem).wait()
            pltpu.sync_copy(rows_v, out_hbm.at[pl.ds(off, win)])

    return pl.kernel(
        body, mesh=_sc_mesh(),
        out_type=jax.ShapeDtypeStruct((N_PAIR, HALF), jnp.uint32),
        scratch_types=[pltpu.VMEM((win,), jnp.int32), pltpu.VMEM((win, HALF), jnp.uint32),
                       pltpu.SemaphoreType.DMA],
        name="sc_return",
    )(ys, pos_flat)


def _expert_body(te_ref, nt_ref, xs_ref, wg_ref, wu_ref, wd_ref, ys_ref, wgb_ref, wub_ref, wdb_ref):
    i = pl.program_id(0)

    @pl.when(jnp.logical_or(i == 0, te_ref[i] != te_ref[jnp.maximum(i - 1, 0)]))
    def _():
        wgb_ref[...] = wg_ref[...].astype(BF16)
        wub_ref[...] = wu_ref[...].astype(BF16)
        wdb_ref[...] = wd_ref[...].astype(BF16)

    @pl.when(i < nt_ref[0])
    def _():
        lo, hi = _unpack_bf16_pair(xs_ref[...])
        h = jnp.concatenate([lo.astype(BF16), hi.astype(BF16)], axis=1)
        y = _ffn(h, wgb_ref[...], wub_ref[...], wdb_ref[...])
        ys_ref[...] = _pack_bf16_pair(y[:, :HALF], y[:, HALF:])


def _expert_ffn(tile_expert, n_tiles, xs, layer, w_gate, w_up, w_down):
    tile = lambda i, te, nt: (jnp.minimum(i, nt[0] - 1), 0)
    ew = lambda shape: pl.BlockSpec((None, None) + shape, lambda i, te, nt: (layer, te[i], 0, 0))
    return pl.pallas_call(
        _expert_body,
        grid_spec=pltpu.PrefetchScalarGridSpec(
            num_scalar_prefetch=2,
            grid=(N_TILE,),
            in_specs=[
                pl.BlockSpec((MOE_TILE, HALF), tile),
                ew((D_MODEL, EXPERT_FF)), ew((D_MODEL, EXPERT_FF)), ew((EXPERT_FF, D_MODEL)),
            ],
            out_specs=pl.BlockSpec((MOE_TILE, HALF), tile),
            scratch_shapes=[pltpu.VMEM((D_MODEL, EXPERT_FF), BF16), pltpu.VMEM((D_MODEL, EXPERT_FF), BF16),
                            pltpu.VMEM((EXPERT_FF, D_MODEL), BF16)],
        ),
        out_shape=jax.ShapeDtypeStruct((N_SLOT, HALF), jnp.uint32),
        compiler_params=pltpu.CompilerParams(
            dimension_semantics=("arbitrary",), vmem_limit_bytes=VMEM_LIMIT),
        name="expert_ffn",
    )(tile_expert, n_tiles, xs, w_gate, w_up, w_down)


def _combine_body(x_ref, sh_ref, sc_ref, g2_ref, yk_ref, wt_ref, sg_ref, su_ref, sd_ref, lng_ref, lnb_ref,
                  *rest):
    o_refs, (sgb_ref, sub_ref, sdb_ref) = rest[:-3], rest[-3:]

    @pl.when(pl.program_id(0) == 0)
    def _():
        sgb_ref[...] = sg_ref[...].astype(BF16)
        sub_ref[...] = su_ref[...].astype(BF16)
        sdb_ref[...] = sd_ref[...].astype(BF16)

    x = x_ref[...]
    h = (x * (1.0 + sc_ref[...]) + sh_ref[...]).astype(BF16)
    y = _ffn(h, sgb_ref[...], sub_ref[...], sdb_ref[...])
    wt = wt_ref[...]
    lo_acc = jnp.zeros((x.shape[0], HALF), F32)
    hi_acc = jnp.zeros((x.shape[0], HALF), F32)
    for k in range(TOP_K):
        lo, hi = _unpack_bf16_pair(yk_ref[k])
        w = wt[:, k:k + 1]
        lo_acc = lo_acc + w * lo
        hi_acc = hi_acc + w * hi
    y = y + jnp.concatenate([lo_acc, hi_acc], axis=1)
    z = DEEPNORM_ALPHA * x + g2_ref[...] * y
    out = _layer_norm(z, lng_ref[...], lnb_ref[...])
    if len(o_refs) == 1:
        o_refs[0][...] = out
    else:
        is_ctx = pl.program_id(0) * TOK_BLOCK < N_CTX

        @pl.when(is_ctx)
        def _():
            o_refs[0][...] = out

        @pl.when(jnp.logical_not(is_ctx))
        def _():
            o_refs[1][...] = out


def _combine(x, mod, yk, wt, layer, s_gate, s_up, s_down, ln_g, ln_b, split_trunks=False):
    tm = TOK_BLOCK
    row = lambda i: (i, 0)
    vec = pl.BlockSpec((1, D_MODEL), lambda i: (0, 0))
    sw = lambda shape: pl.BlockSpec((None,) + shape, lambda i: (layer, 0, 0))
    if split_trunks:
        n_ctx_blk = N_CTX // tm
        out_specs = [
            pl.BlockSpec((tm, D_MODEL), lambda i: (jnp.minimum(i, n_ctx_blk - 1), 0)),
            pl.BlockSpec((tm, D_MODEL), lambda i: (jnp.maximum(i - n_ctx_blk, 0), 0)),
        ]
        out_shape = [jax.ShapeDtypeStruct((N_CTX, D_MODEL), F32), jax.ShapeDtypeStruct((N_LAT, D_MODEL), F32)]
    else:
        out_specs = pl.BlockSpec((tm, D_MODEL), row)
        out_shape = jax.ShapeDtypeStruct((N_TOK, D_MODEL), F32)
    return pl.pallas_call(
        _combine_body,
        grid=(N_TOK // tm,),
        in_specs=[
            pl.BlockSpec((tm, D_MODEL), row),
            _mod_spec(tm, 3, 1, 0), _mod_spec(tm, 4, 1, 0), _mod_spec(tm, 5, 1, 0),
            pl.BlockSpec((TOP_K, tm, HALF), lambda i: (0, i, 0)),
            pl.BlockSpec((tm, 128), row),
            sw((D_MODEL, EXPERT_FF)), sw((D_MODEL, EXPERT_FF)), sw((EXPERT_FF, D_MODEL)),
            vec, vec,
        ],
        out_specs=out_specs,
        out_shape=out_shape,
        scratch_shapes=[pltpu.VMEM((D_MODEL, EXPERT_FF), BF16), pltpu.VMEM((D_MODEL, EXPERT_FF), BF16),
                        pltpu.VMEM((EXPERT_FF, D_MODEL), BF16)],
        compiler_params=pltpu.CompilerParams(
            dimension_semantics=("arbitrary",), vmem_limit_bytes=VMEM_LIMIT),
        name="moe_combine_ln",
    )(x, mod, mod, mod, yk, wt, s_gate, s_up, s_down, ln_g, ln_b)


def _moe(x, mod, layer, router_wt, bias_col, w_gate, w_up, w_down, s_gate, s_up, s_down, ln_g, ln_b,
         split_trunks=False):
    hp, eidx, rank, wt, counts = _router(x, mod, router_wt, bias_col)
    pos, tile_expert, n_tiles = _finalize(counts[:, 0], eidx, rank)
    pos_flat = pos.reshape(K_PAD * N_TOK)
    xs = _sc_dispatch(hp, pos_flat)
    ys = _expert_ffn(tile_expert, n_tiles, xs, layer, w_gate, w_up, w_down)
    yk = _sc_return(ys, pos_flat).reshape(TOP_K, N_TOK, HALF)
    return _combine(x, mod, yk, wt, layer, s_gate, s_up, s_down, ln_g, ln_b, split_trunks)


def kernel(x_prompt, x_sample, state_ret, c, c_ctx, ada_w, ada_b, ln_g, ln_b, ret_w_in, ret_w_out, ret_decay, conv_w_in, conv_w, conv_w_out, moe_router, moe_bias, moe_w_gate, moe_w_up, moe_w_down, shared_w_gate, shared_w_up, shared_w_down):
    x = (x_prompt.reshape(N_CTX, D_MODEL), x_sample.reshape(N_LAT, D_MODEL))
    cond = jnp.concatenate(
        [c_ctx[None, :], c, jnp.zeros((N_COND - 1 - DEC_BATCH, D_MODEL), F32)], axis=0)
    mods = _ada_table(cond, ada_w, ada_b).reshape(DEPTH, N_COND, 1, 6 * D_MODEL)
    cos, sin = _rope_tables()
    router_wt = jnp.swapaxes(moe_router, 1, 2)

    states = None
    for i in range(DEPTH):
        j = i // 2
        mod = mods[i]
        lng = ln_g[i].reshape(2, 1, D_MODEL)
        lnb = ln_b[i].reshape(2, 1, D_MODEL)
        if i % 2 == 0:
            proj = _modproj(x, mod, ret_w_in, j, 2048)
            a_ctx, states = _retention_ctx(proj, ret_decay[j], j, states)
            a_lat = _retention_lat(proj, ret_decay[j], state_ret, j, cos, sin)
            x = _ret_out(a_ctx, a_lat, x, mod, ret_w_out, j, lng[0], lnb[0])
        else:
            x = _conv_layer(x, mod, conv_w_in, conv_w_out, j, conv_w[j], lng[0], lnb[0])
        x = _moe(x, mod, i, router_wt[i], moe_bias[i].reshape(N_EXPERTS, 1), moe_w_gate, moe_w_up, moe_w_down,
                 shared_w_gate, shared_w_up, shared_w_down, lng[1], lnb[1], split_trunks=(i == DEPTH - 1))

    y_prompt = x[0].reshape(BATCH, SEQ, D_MODEL)
    y_sample = x[1].reshape(DEC_BATCH, DEC_SEQ, D_MODEL)
    return y_prompt, y_sample, states
```

```python
import functools

import jax
import jax.numpy as jnp
from jax import lax
from jax.experimental import pallas as pl
from jax.experimental.pallas import tpu as pltpu

F32 = jnp.float32
BF16 = jnp.bfloat16

D_MODEL = 1024
BATCH = 32
SEQ = 256
DEPTH = 4
DEC_BATCH = 8
DEC_SEQ = 1024
GRID_W = 64
RET_HEADS = 4
RET_DK = D_MODEL // RET_HEADS
RET_DV = 2 * D_MODEL // RET_HEADS
ROPE_BASE = 10000.0
N_EXPERTS = 64
TOP_K = 6
N_GROUPS = 8
TOPK_GROUPS = 4
EXPERT_FF = 256
ROUTED_SCALE = 2.5
LN_EPS = 1e-5
DEEPNORM_ALPHA = (2.0 * DEPTH) ** 0.25

N_CTX = BATCH * SEQ
N_LAT = DEC_BATCH * DEC_SEQ
N_TOK = N_CTX + N_LAT
N_COND = 16
RET_CHUNK = 256
assert N_CTX == N_LAT
N_TRUNK = N_CTX
TOK_BLOCK = 256
MIX_BLOCK = 512
VMEM_LIMIT = 56 * 1024 * 1024

HALF = D_MODEL // 2
MOE_TILE = 1024
MOE_TILE_LOG2 = 10
N_PAIR = N_TOK * TOP_K
N_SLOT = N_PAIR + N_EXPERTS * MOE_TILE
N_TILE = N_SLOT // MOE_TILE
K_PAD = 8

SC_CORES = 2
SC_SUBCORES = 16
SC_WORKERS = SC_CORES * SC_SUBCORES
SC_SCATTER_WIN = 64
SC_GATHER_WIN = 128


def _cond_row(tok_block_idx, tok_block):
    t0 = tok_block_idx * tok_block
    return jnp.where(t0 < N_CTX, 0, 1 + (t0 - N_CTX) // DEC_SEQ)


def _silu(x):
    return x * jax.nn.sigmoid(x)


def _layer_norm(z, g, b):
    mu = jnp.mean(z, axis=-1, keepdims=True)
    zc = z - mu
    var = jnp.mean(zc * zc, axis=-1, keepdims=True)
    return zc * lax.rsqrt(var + LN_EPS) * g + b


def _ada_body(cond_ref, w_ref, b_ref, o_ref):
    s = _silu(cond_ref[...]).astype(BF16)
    o_ref[...] = jnp.dot(s, w_ref[...].astype(BF16), preferred_element_type=F32) + b_ref[...]


def _ada_table(cond, ada_w, ada_b):
    tn = 2048
    return pl.pallas_call(
        _ada_body,
        grid=(DEPTH, 6 * D_MODEL // tn),
        in_specs=[
            pl.BlockSpec((N_COND, D_MODEL), lambda l, j: (0, 0)),
            pl.BlockSpec((None, D_MODEL, tn), lambda l, j: (l, 0, j)),
            pl.BlockSpec((None, 1, tn), lambda l, j: (l, 0, j)),
        ],
        out_specs=pl.BlockSpec((None, N_COND, tn), lambda l, j: (l, 0, j)),
        out_shape=jax.ShapeDtypeStruct((DEPTH, N_COND, 6 * D_MODEL), F32),
        compiler_params=pltpu.CompilerParams(
            dimension_semantics=("arbitrary", "arbitrary"), vmem_limit_bytes=VMEM_LIMIT),
        name="ada_table",
    )(cond, ada_w, ada_b.reshape(DEPTH, 1, 6 * D_MODEL))


def _mod_spec(tok_block, col, grid_rank, tok_axis, first_block=0):
    def index_map(*idx):
        return (_cond_row(first_block + idx[tok_axis], tok_block), 0, col)
    del grid_rank
    return pl.BlockSpec((None, 1, D_MODEL), index_map)


def _x_specs(x, tm, tok_axis):
    if not isinstance(x, tuple):
        return [pl.BlockSpec((tm, D_MODEL), lambda *idx: (idx[tok_axis], 0))], (x,)
    n_ctx_blk = N_CTX // tm
    return [
        pl.BlockSpec((tm, D_MODEL), lambda *idx: (jnp.minimum(idx[tok_axis], n_ctx_blk - 1), 0)),
        pl.BlockSpec((tm, D_MODEL), lambda *idx: (jnp.maximum(idx[tok_axis] - n_ctx_blk, 0), 0)),
    ], x


def _x_block(x_refs, tok_block_idx, tm):
    if len(x_refs) == 1:
        return x_refs[0][...]
    return jnp.where(tok_block_idx * tm < N_CTX, x_refs[0][...], x_refs[1][...])


def _modproj_body(*refs):
    x_refs, (sh_ref, sc_ref, w_ref, o_ref, wbf_ref) = refs[:-5], refs[-5:]

    @pl.when(pl.program_id(1) == 0)
    def _():
        wbf_ref[...] = w_ref[...].astype(BF16)

    x = _x_block(x_refs, pl.program_id(1), o_ref.shape[0])
    h = (x * (1.0 + sc_ref[...]) + sh_ref[...]).astype(BF16)
    o_ref[...] = jnp.dot(h, wbf_ref[...], preferred_element_type=F32).astype(o_ref.dtype)


def _modproj(x, mod, w_all, w_idx, tn):
    n_out = w_all.shape[-1]
    tm = 512
    x_specs, x_args = _x_specs(x, tm, 1)
    return pl.pallas_call(
        _modproj_body,
        grid=(n_out // tn, N_TOK // tm),
        in_specs=x_specs + [
            _mod_spec(tm, 0, 2, 1),
            _mod_spec(tm, 1, 2, 1),
            pl.BlockSpec((None, D_MODEL, tn), lambda j, i: (w_idx, 0, j)),
        ],
        out_specs=pl.BlockSpec((tm, tn), lambda j, i: (i, j)),
        out_shape=jax.ShapeDtypeStruct((N_TOK, n_out), BF16),
        scratch_shapes=[pltpu.VMEM((D_MODEL, tn), BF16)],
        compiler_params=pltpu.CompilerParams(
            dimension_semantics=("arbitrary", "arbitrary"), vmem_limit_bytes=VMEM_LIMIT),
        name="modproj",
    )(*x_args, mod, mod, w_all)


def _log_sigmoid(v):
    return jnp.minimum(v, 0.0) - jnp.log1p(jnp.exp(-jnp.abs(v)))


def _decay_tables(dec_ref, head):
    c = RET_CHUNK
    lgf = _log_sigmoid(jnp.full((c, c), dec_ref[0, head], F32))
    lgb = _log_sigmoid(jnp.full((c, c), dec_ref[1, head], F32))
    row = lax.broadcasted_iota(jnp.int32, (c, c), 0).astype(F32)
    col = lax.broadcasted_iota(jnp.int32, (c, c), 1).astype(F32)
    diff = row - col
    kscale = RET_DK ** -0.5
    intra = jnp.where(diff > 0, jnp.exp(lgf * diff),
                      jnp.where(diff < 0, jnp.exp(-lgb * diff), 2.0)) * kscale
    qdec_f = jnp.exp(lgf * (row + 1.0))
    qdec_b = jnp.exp(lgb * (c - row))
    kdec_f = jnp.exp(lgf * (c - 1.0 - row)) * kscale
    kdec_b = jnp.exp(lgb * row) * kscale
    cdec_f = jnp.exp(lgf * c)
    cdec_b = jnp.exp(lgb * c)
    return intra, qdec_f, qdec_b, kdec_f, kdec_b, cdec_f, cdec_b


def _head_norm_gate(o, g):
    mu = jnp.mean(o, axis=-1, keepdims=True)
    oc = o - mu
    var = jnp.mean(oc * oc, axis=-1, keepdims=True)
    on = oc * lax.rsqrt(var + LN_EPS)
    return (_silu(g.astype(F32)) * on).astype(BF16)


_NT = (((1,), (1,)), ((), ()))
_TN = (((0,), (0,)), ((), ()))


def _ret_ctx_body(dec_ref, proj_ref, *rest):
    a_ref, st_ref, tab_ref = rest[-3:]

    @pl.when(pl.program_id(0) == 0)
    def _():
        for head in range(RET_HEADS):
            intra, _, _, kdec_f, kdec_b, _, _ = _decay_tables(dec_ref, head)
            tab_ref[head, 0] = intra
            tab_ref[head, 1] = kdec_f
            tab_ref[head, 2] = kdec_b

    for head in range(RET_HEADS):
        q = proj_ref[:, head * RET_DK:(head + 1) * RET_DK]
        k = proj_ref[:, D_MODEL + head * RET_DK:D_MODEL + (head + 1) * RET_DK]
        v = proj_ref[:, 2 * D_MODEL + head * RET_DV:2 * D_MODEL + (head + 1) * RET_DV]
        g = proj_ref[:, 4 * D_MODEL + head * RET_DV:4 * D_MODEL + (head + 1) * RET_DV]
        scores = lax.dot_general(q, k, _NT, preferred_element_type=F32)
        p = (scores * tab_ref[head, 0]).astype(BF16)
        o = jnp.dot(p, v, preferred_element_type=F32)
        a_ref[:, head * RET_DV:(head + 1) * RET_DV] = _head_norm_gate(o, g)
        kf = k.astype(F32)
        st_ref[0, head] = lax.dot_general((kf * tab_ref[head, 1]).astype(BF16), v, _TN,
                                          preferred_element_type=F32)
        st_ref[1, head] = lax.dot_general((kf * tab_ref[head, 2]).astype(BF16), v, _TN,
                                          preferred_element_type=F32)


def _retention_ctx(proj, decay, ret_idx, states):
    n = SEQ
    assert n == RET_CHUNK
    n_ret = (DEPTH + 1) // 2
    st = jax.ShapeDtypeStruct((BATCH, n_ret, 2, RET_HEADS, RET_DK, RET_DV), F32)
    st_spec = pl.BlockSpec((None, None, 2, RET_HEADS, RET_DK, RET_DV), lambda b: (b, ret_idx, 0, 0, 0, 0))
    in_specs = [
        pl.BlockSpec(memory_space=pltpu.SMEM),
        pl.BlockSpec((n, 6 * D_MODEL), lambda b: (b, 0)),
    ]
    args = (decay, proj)
    aliases = {}
    if states is not None:
        in_specs.append(pl.BlockSpec(memory_space=pl.ANY))
        args += (states,)
        aliases = {2: 1}
    return pl.pallas_call(
        _ret_ctx_body,
        grid=(BATCH,),
        in_specs=in_specs,
        out_specs=[pl.BlockSpec((n, RET_HEADS * RET_DV), lambda b: (b, 0)), st_spec],
        out_shape=[jax.ShapeDtypeStruct((N_CTX, RET_HEADS * RET_DV), BF16), st],
        input_output_aliases=aliases,
        scratch_shapes=[pltpu.VMEM((RET_HEADS, 3, RET_CHUNK, RET_CHUNK), F32)],
        compiler_params=pltpu.CompilerParams(
            dimension_semantics=("arbitrary",), vmem_limit_bytes=VMEM_LIMIT),
        name="retention_ctx",
    )(*args)


def _rope(x, cos, sin):
    halves = [pltpu.roll(x[:, s:s + 128], 64, axis=1) for s in (0, 128)]
    return x * cos + jnp.concatenate(halves, axis=1) * sin


def _ret_lat_body(dec_ref, q_ref, k_ref, v_ref, g_ref, s0f_ref, s0b_ref, cos_ref, sin_ref,
                  a_ref, qr_ref, kr_ref, o_ref, st_ref):
    head = pl.program_id(1)
    c = RET_CHUNK
    nc = DEC_SEQ // c
    intra, qdec_f, qdec_b, kdec_f, kdec_b, cdec_f, cdec_b = _decay_tables(dec_ref, head)
    cdec_f = jnp.concatenate([cdec_f, cdec_f], axis=1)
    cdec_b = jnp.concatenate([cdec_b, cdec_b], axis=1)

    qr_ref[...] = _rope(q_ref[...].astype(F32), cos_ref[...], sin_ref[...]).astype(BF16)
    kr_ref[...] = _rope(k_ref[...].astype(F32), cos_ref[...], sin_ref[...]).astype(BF16)

    st_ref[...] = s0f_ref[...]
    for ci in range(nc):
        rows = pl.ds(ci * c, c)
        q = qr_ref[rows, :]
        k = kr_ref[rows, :]
        v = v_ref[rows, :]
        scores = lax.dot_general(q, k, _NT, preferred_element_type=F32)
        p = (scores * intra).astype(BF16)
        o = jnp.dot(p, v, preferred_element_type=F32)
        qd = (q.astype(F32) * qdec_f).astype(BF16)
        o = o + jnp.dot(qd, st_ref[...].astype(BF16), preferred_element_type=F32)
        o_ref[rows, :] = o
        if ci + 1 < nc:
            kd = (k.astype(F32) * kdec_f).astype(BF16)
            st_ref[...] = cdec_f * st_ref[...] + lax.dot_general(kd, v, _TN, preferred_element_type=F32)

    st_ref[...] = s0b_ref[...]
    for ci in reversed(range(nc)):
        rows = pl.ds(ci * c, c)
        q = qr_ref[rows, :]
        qd = (q.astype(F32) * qdec_b).astype(BF16)
        o = o_ref[rows, :] + jnp.dot(qd, st_ref[...].astype(BF16), preferred_element_type=F32)
        a_ref[rows, :] = _head_norm_gate(o, g_ref[rows, :])
        if ci > 0:
            k = kr_ref[rows, :]
            v = v_ref[rows, :]
            kd = (k.astype(F32) * kdec_b).astype(BF16)
            st_ref[...] = cdec_b * st_ref[...] + lax.dot_general(kd, v, _TN, preferred_element_type=F32)


def _rope_tables():
    half = RET_DK // 4
    freqs = ROPE_BASE ** (-jnp.arange(half, dtype=F32) / half)
    t = jnp.arange(DEC_SEQ)
    row = (t // GRID_W).astype(F32)
    col = (t % GRID_W).astype(F32)
    ang_r = row[:, None] * freqs[None, :]
    ang_c = col[:, None] * freqs[None, :]
    cos = jnp.concatenate([jnp.cos(ang_r)] * 2 + [jnp.cos(ang_c)] * 2, axis=1)
    sin = jnp.concatenate([-jnp.sin(ang_r), jnp.sin(ang_r), -jnp.sin(ang_c), jnp.sin(ang_c)], axis=1)
    return cos, sin


def _retention_lat(proj, decay, state_ret, ret_idx, cos, sin):
    n = DEC_SEQ
    off = N_CTX // n
    s0_spec = lambda d: pl.BlockSpec((None, None, None, None, RET_DK, RET_DV),
                                     lambda b, h: (b, ret_idx, d, h, 0, 0))
    tab_spec = pl.BlockSpec((n, RET_DK), lambda b, h: (0, 0))
    return pl.pallas_call(
        _ret_lat_body,
        grid=(DEC_BATCH, RET_HEADS),
        in_specs=[
            pl.BlockSpec(memory_space=pltpu.SMEM),
            pl.BlockSpec((n, RET_DK), lambda b, h: (off + b, h)),
            pl.BlockSpec((n, RET_DK), lambda b, h: (off + b, RET_HEADS + h)),
            pl.BlockSpec((n, RET_DV), lambda b, h: (off + b, RET_HEADS + h)),
            pl.BlockSpec((n, RET_DV), lambda b, h: (off + b, 2 * RET_HEADS + h)),
            s0_spec(0), s0_spec(1), tab_spec, tab_spec,
        ],
        out_specs=pl.BlockSpec((n, RET_DV), lambda b, h: (b, h)),
        out_shape=jax.ShapeDtypeStruct((N_LAT, RET_HEADS * RET_DV), BF16),
        scratch_shapes=[
            pltpu.VMEM((n, RET_DK), BF16), pltpu.VMEM((n, RET_DK), BF16),
            pltpu.VMEM((n, RET_DV), F32), pltpu.VMEM((RET_DK, RET_DV), F32),
        ],
        compiler_params=pltpu.CompilerParams(
            dimension_semantics=("arbitrary", "arbitrary"), vmem_limit_bytes=VMEM_LIMIT),
        name="retention_lat",
    )(decay, proj, proj, proj, proj, state_ret, state_ret, cos, sin)


N_ROUTE_IN = 4
N_ROUTE_OUT = 5


def _ret_out_body(actx_ref, alat_ref, *refs):
    n_x = len(refs) - (4 + N_ROUTE_IN + 1 + N_ROUTE_OUT + 2)
    x_refs = refs[:n_x]
    g1_ref, w_ref, lng_ref, lnb_ref = refs[n_x:n_x + 4]
    route_in = refs[n_x + 4:n_x + 4 + N_ROUTE_IN]
    o_ref = refs[n_x + 4 + N_ROUTE_IN]
    route_out = refs[n_x + 5 + N_ROUTE_IN:n_x + 5 + N_ROUTE_IN + N_ROUTE_OUT]
    wbf_ref, carry_ref = refs[-2:]
    i = pl.program_id(0)
    tm = o_ref.shape[0]

    @pl.when(i == 0)
    def _():
        wbf_ref[...] = w_ref[...].astype(BF16)

    a = jnp.where(i * tm < N_CTX, actx_ref[...], alat_ref[...])
    y = jnp.dot(a, wbf_ref[...], preferred_element_type=F32)
    z = DEEPNORM_ALPHA * _x_block(x_refs, i, tm) + g1_ref[...] * y
    xn = _layer_norm(z, lng_ref[...], lnb_ref[...])
    o_ref[...] = xn
    _route_block(xn, *route_in, *route_out, carry_ref)


def _ret_out(a_ctx, a_lat, x, mod, w_all, w_idx, ln_g, ln_b, router_wt, bias_col):
    tm = MIX_BLOCK
    k_dim = w_all.shape[1]
    n_ctx_blk = N_CTX // tm
    vec = pl.BlockSpec((1, D_MODEL), lambda i: (0, 0))
    x_specs, x_args = _x_specs(x, tm, 0)
    r_in, r_out, r_shapes, r_scratch = _route_io(tm)
    return pl.pallas_call(
        _ret_out_body,
        grid=(N_TOK // tm,),
        in_specs=[
            pl.BlockSpec((tm, k_dim), lambda i: (jnp.minimum(i, n_ctx_blk - 1), 0)),
            pl.BlockSpec((tm, k_dim), lambda i: (jnp.maximum(i - n_ctx_blk, 0), 0)),
        ] + x_specs + [
            _mod_spec(tm, 2, 1, 0),
            pl.BlockSpec((None, k_dim, D_MODEL), lambda i: (w_idx, 0, 0)),
            vec, vec,
        ] + r_in,
        out_specs=[pl.BlockSpec((tm, D_MODEL), lambda i: (i, 0))] + r_out,
        out_shape=[jax.ShapeDtypeStruct((N_TOK, D_MODEL), F32)] + r_shapes,
        scratch_shapes=[pltpu.VMEM((k_dim, D_MODEL), BF16)] + r_scratch,
        compiler_params=pltpu.CompilerParams(
            dimension_semantics=("arbitrary",), vmem_limit_bytes=VMEM_LIMIT),
        name="ret_out_ln_route",
    )(a_ctx, a_lat, *x_args, mod, w_all, ln_g, ln_b, mod, mod, router_wt, bias_col)


def _conv_layer_body(*refs):
    n_x = len(refs) - (8 + N_ROUTE_IN + 1 + N_ROUTE_OUT + 3)
    x_refs = refs[:n_x]
    sh_ref, sc_ref, g1_ref, win_ref, wout_ref, cw_ref, lng_ref, lnb_ref = refs[n_x:n_x + 8]
    route_in = refs[n_x + 8:n_x + 8 + N_ROUTE_IN]
    o_ref = refs[n_x + 8 + N_ROUTE_IN]
    route_out = refs[n_x + 9 + N_ROUTE_IN:n_x + 9 + N_ROUTE_IN + N_ROUTE_OUT]
    winb_ref, woutb_ref, carry_ref = refs[-3:]
    tm = o_ref.shape[0]

    @pl.when(pl.program_id(0) == 0)
    def _():
        winb_ref[...] = win_ref[...].astype(BF16)
        woutb_ref[...] = wout_ref[...].astype(BF16)

    x = _x_block(x_refs, pl.program_id(0), tm)
    h = (x * (1.0 + sc_ref[...]) + sh_ref[...]).astype(BF16)
    proj = jnp.dot(h, winb_ref[...], preferred_element_type=F32)
    bg, cg, xt = (proj[:, c * D_MODEL:(c + 1) * D_MODEL] for c in range(3))
    seg = jnp.where(pl.program_id(0) * tm < N_CTX, SEQ, GRID_W)
    u = cg * xt
    pos = lax.broadcasted_iota(jnp.int32, (tm, D_MODEL), 0) & (seg - 1)
    u_prev = jnp.where(pos == 0, 0.0, pltpu.roll(u, 1, axis=0))
    u_next = jnp.where(pos == seg - 1, 0.0, pltpu.roll(u, tm - 1, axis=0))
    cu = u_prev * cw_ref[0:1, :] + u * cw_ref[1:2, :] + u_next * cw_ref[2:3, :]
    y = jnp.dot((bg * cu).astype(BF16), woutb_ref[...], preferred_element_type=F32)
    z = DEEPNORM_ALPHA * x + g1_ref[...] * y
    xn = _layer_norm(z, lng_ref[...], lnb_ref[...])
    o_ref[...] = xn
    _route_block(xn, *route_in, *route_out, carry_ref)


def _conv_layer(x, mod, w_in_all, w_out_all, w_idx, conv_w, ln_g, ln_b, router_wt, bias_col):
    tm = MIX_BLOCK
    vec = pl.BlockSpec((1, D_MODEL), lambda i: (0, 0))
    once = pl.Buffered(1)
    x_specs, x_args = _x_specs(x, tm, 0)
    r_in, r_out, r_shapes, r_scratch = _route_io(tm)
    return pl.pallas_call(
        _conv_layer_body,
        grid=(N_TOK // tm,),
        in_specs=x_specs + [
            _mod_spec(tm, 0, 1, 0), _mod_spec(tm, 1, 1, 0), _mod_spec(tm, 2, 1, 0),
            pl.BlockSpec((None, D_MODEL, 3 * D_MODEL), lambda i: (w_idx, 0, 0), pipeline_mode=once),
            pl.BlockSpec((None, D_MODEL, D_MODEL), lambda i: (w_idx, 0, 0), pipeline_mode=once),
            pl.BlockSpec((3, D_MODEL), lambda i: (0, 0)),
            vec, vec,
        ] + r_in,
        out_specs=[pl.BlockSpec((tm, D_MODEL), lambda i: (i, 0))] + r_out,
        out_shape=[jax.ShapeDtypeStruct((N_TOK, D_MODEL), F32)] + r_shapes,
        scratch_shapes=[pltpu.VMEM((D_MODEL, 3 * D_MODEL), BF16), pltpu.VMEM((D_MODEL, D_MODEL), BF16)]
        + r_scratch,
        compiler_params=pltpu.CompilerParams(
            dimension_semantics=("arbitrary",), vmem_limit_bytes=VMEM_LIMIT),
        name="conv_layer_route",
    )(*x_args, mod, mod, mod, w_in_all, w_out_all, conv_w, ln_g, ln_b, mod, mod, router_wt, bias_col)


def _split_bf16(v):
    hi = v.astype(BF16)
    lo = (v - hi.astype(F32)).astype(BF16)
    return hi, lo


def _pack_bf16_pair(lo_f32, hi_f32):
    lo = lax.bitcast_convert_type(lo_f32.astype(BF16).astype(F32), jnp.uint32) >> 16
    hi = lax.bitcast_convert_type(hi_f32.astype(BF16).astype(F32), jnp.uint32) & jnp.uint32(0xFFFF0000)
    return hi | lo


def _unpack_bf16_pair(u):
    lo = lax.bitcast_convert_type(u << 16, F32)
    hi = lax.bitcast_convert_type(u & jnp.uint32(0xFFFF0000), F32)
    return lo, hi


def _rows_to_tile(rows, n_sub, dtype):
    tm = rows[0].shape[1]
    sub = lax.broadcasted_iota(jnp.int32, (n_sub, tm), 0)
    out = jnp.zeros((n_sub, tm), dtype)
    for k, r in enumerate(rows):
        out = jnp.where(sub == k, jnp.broadcast_to(r.astype(dtype), (n_sub, tm)), out)
    return out


def _route_block(xn, sh_ref, sc_ref, rwt_ref, bias_ref,
                 hp_ref, eidx_ref, rank_ref, wt_ref, cnt_ref, carry_ref):
    tm = xn.shape[0]
    e = N_EXPERTS
    per = e // N_GROUPS
    neg = -jnp.inf

    @pl.when(pl.program_id(0) == 0)
    def _():
        carry_ref[...] = jnp.zeros_like(carry_ref)

    h = xn * (1.0 + sc_ref[...]) + sh_ref[...]
    hp_ref[...] = _pack_bf16_pair(h[:, :HALF], h[:, HALF:])
    h_hi, h_lo = _split_bf16(h)
    w_hi, w_lo = _split_bf16(rwt_ref[...])
    dot = lambda a, b: lax.dot_general(a, b, _NT, preferred_element_type=F32)
    logits = dot(w_hi, h_hi) + (dot(w_hi, h_lo) + dot(w_lo, h_hi))
    s = jax.nn.sigmoid(logits)
    sel = s + bias_ref[...]

    g3 = sel.reshape(N_GROUPS, per, tm)
    sub = lax.broadcasted_iota(jnp.int32, (N_GROUPS, per, tm), 1)
    m1 = jnp.max(g3, axis=1, keepdims=True)
    i1 = jnp.min(jnp.where(g3 == m1, sub, per), axis=1, keepdims=True)
    m2 = jnp.max(jnp.where(sub == i1, neg, g3), axis=1, keepdims=True)
    gs = (m1 + m2).reshape(N_GROUPS, tm)

    gi = lax.broadcasted_iota(jnp.int32, (N_GROUPS, tm), 0)
    gmask = jnp.zeros((N_GROUPS, tm), jnp.bool_)
    cur = gs
    for _ in range(TOPK_GROUPS):
        m = jnp.max(cur, axis=0, keepdims=True)
        idx = jnp.min(jnp.where(cur == m, gi, N_GROUPS), axis=0, keepdims=True)
        pick = gi == idx
        gmask = jnp.logical_or(gmask, pick)
        cur = jnp.where(pick, neg, cur)
    emask = jnp.broadcast_to(gmask.reshape(N_GROUPS, 1, tm), (N_GROUPS, per, tm)).reshape(e, tm)

    ei = lax.broadcasted_iota(jnp.int32, (e, tm), 0)
    picks, ids = [], []
    cur = jnp.where(emask, sel, neg)
    for _ in range(TOP_K):
        m = jnp.max(cur, axis=0, keepdims=True)
        idx = jnp.min(jnp.where(cur == m, ei, e), axis=0, keepdims=True)
        pick = ei == idx
        picks.append(pick)
        ids.append(idx)
        cur = jnp.where(pick, neg, cur)

    chosen = functools.reduce(jnp.logical_or, picks)
    cf = jnp.where(chosen, 1.0, 0.0)
    before = (lax.broadcasted_iota(jnp.int32, (tm, tm), 0)
              < lax.broadcasted_iota(jnp.int32, (tm, tm), 1)).astype(BF16)
    rank = carry_ref[:, 0:1] + jnp.dot(cf.astype(BF16), before, preferred_element_type=F32)
    carry_ref[...] = carry_ref[...] + jnp.sum(cf, axis=1, keepdims=True)
    cnt_ref[...] = carry_ref[...].astype(jnp.int32)

    w_rows = [jnp.sum(jnp.where(p, s, 0.0), axis=0, keepdims=True) for p in picks]
    r_rows = [jnp.sum(jnp.where(p, rank, 0.0), axis=0, keepdims=True) for p in picks]
    den = functools.reduce(lambda a, b: a + b, w_rows)
    w_rows = [w / den * ROUTED_SCALE for w in w_rows]
    eidx_ref[...] = _rows_to_tile(ids, K_PAD, jnp.int32)
    rank_ref[...] = _rows_to_tile(r_rows, K_PAD, F32).astype(jnp.int32)
    wt_ref[...] = _rows_to_tile(w_rows, 128, F32).T


def _route_io(tm):
    tok_major = lambda i: (i, 0)
    choice_major = lambda i: (0, i)
    in_specs = [
        _mod_spec(tm, 3, 1, 0),
        _mod_spec(tm, 4, 1, 0),
        pl.BlockSpec((N_EXPERTS, D_MODEL), lambda i: (0, 0)),
        pl.BlockSpec((N_EXPERTS, 1), lambda i: (0, 0)),
    ]
    out_specs = [
        pl.BlockSpec((tm, HALF), tok_major),
        pl.BlockSpec((K_PAD, tm), choice_major),
        pl.BlockSpec((K_PAD, tm), choice_major),
        pl.BlockSpec((tm, 128), tok_major),
        pl.BlockSpec((N_EXPERTS, 128), lambda i: (0, 0)),
    ]
    out_shapes = [
        jax.ShapeDtypeStruct((N_TOK, HALF), jnp.uint32),
        jax.ShapeDtypeStruct((K_PAD, N_TOK), jnp.int32),
        jax.ShapeDtypeStruct((K_PAD, N_TOK), jnp.int32),
        jax.ShapeDtypeStruct((N_TOK, 128), F32),
        jax.ShapeDtypeStruct((N_EXPERTS, 128), jnp.int32),
    ]
    return in_specs, out_specs, out_shapes, [pltpu.VMEM((N_EXPERTS, 128), F32)]


def _finalize_body(cnt_ref, eidx_ref, rank_ref, pos_ref, te_ref, nt_ref):
    pos_ref[...] = rank_ref[...]

    def per_expert(e, carry):
        off, t = carry
        n_tile = lax.shift_right_logical(cnt_ref[e] + (MOE_TILE - 1), MOE_TILE_LOG2)
        pos_ref[...] = pos_ref[...] + jnp.where(eidx_ref[...] == e, off, 0)

        def fill(j, c):
            te_ref[t + j] = e
            return c
        lax.fori_loop(0, n_tile, fill, 0)
        return off + n_tile * MOE_TILE, t + n_tile

    _, n_used = lax.fori_loop(0, N_EXPERTS, per_expert, (jnp.int32(0), jnp.int32(0)))
    nt_ref[0] = n_used
    last = te_ref[jnp.maximum(n_used - 1, 0)]

    def fill_tail(j, c):
        te_ref[j] = last
        return c
    lax.fori_loop(n_used, N_TILE, fill_tail, 0)


def _finalize(counts, eidx, rank):
    smem = pl.BlockSpec(memory_space=pltpu.SMEM)
    full = pl.BlockSpec((K_PAD, N_TOK), lambda: (0, 0))
    return pl.pallas_call(
        _finalize_body,
        in_specs=[smem, full, full],
        out_specs=[full, smem, smem],
        out_shape=[
            jax.ShapeDtypeStruct((K_PAD, N_TOK), jnp.int32),
            jax.ShapeDtypeStruct((N_TILE,), jnp.int32),
            jax.ShapeDtypeStruct((1,), jnp.int32),
        ],
        compiler_params=pltpu.CompilerParams(vmem_limit_bytes=VMEM_LIMIT),
        name="route_finalize",
    )(counts, eidx, rank)


def _ffn(h, wg, wu, wd):
    hid = _silu(jnp.dot(h, wg, preferred_element_type=F32)) * jnp.dot(h, wu, preferred_element_type=F32)
    return jnp.dot(hid.astype(BF16), wd, preferred_element_type=F32)


def _sc_mesh():
    from jax.experimental.pallas import tpu_sc as plsc
    return plsc.VectorSubcoreMesh(core_axis_name="c", subcore_axis_name="s",
                                  num_cores=SC_CORES, num_subcores=SC_SUBCORES)


def _sc_worker_id():
    return lax.axis_index("s") * SC_CORES + lax.axis_index("c")


def _sc_dispatch(hp, pos_flat):
    win = SC_SCATTER_WIN
    per_worker = N_TOK // SC_WORKERS

    def body(rows_hbm, idx_hbm, out_hbm, *scratch):
        idx_v, rows_v, sem = scratch[:TOP_K], scratch[TOP_K], scratch[TOP_K + 1]
        base = _sc_worker_id() * per_worker

        @pl.loop(0, per_worker // win)
        def _(j):
            off = base + j * win
            pltpu.sync_copy(rows_hbm.at[pl.ds(off, win)], rows_v)
            for k in range(TOP_K):
                pltpu.sync_copy(idx_hbm.at[pl.ds(k * N_TOK + off, win)], idx_v[k])
            copies = [pltpu.async_copy(rows_v, out_hbm.at[idx_v[k]], sem) for k in range(TOP_K)]
            for cp in copies:
                cp.wait()

    return pl.kernel(
        body, mesh=_sc_mesh(),
        out_type=jax.ShapeDtypeStruct((N_SLOT, HALF), jnp.uint32),
        scratch_types=[pltpu.VMEM((win,), jnp.int32)] * TOP_K
        + [pltpu.VMEM((win, HALF), jnp.uint32), pltpu.SemaphoreType.DMA],
        name="sc_dispatch",
    )(hp, pos_flat)


def _sc_return(ys, pos_flat, trunk):
    win = SC_GATHER_WIN
    wins_per_choice = N_TRUNK // win
    log2_wins = wins_per_choice.bit_length() - 1
    assert wins_per_choice == 1 << log2_wins
    wins_per_worker = TOP_K * wins_per_choice // SC_WORKERS

    def body(src_hbm, idx_hbm, out_hbm, idx_v, rows_v, sem):
        first = _sc_worker_id() * wins_per_worker

        @pl.loop(0, wins_per_worker)
        def _(j):
            g = first + j
            k = lax.shift_right_logical(g, log2_wins)
            src_off = k * N_TOK + trunk * N_TRUNK + (g - k * wins_per_choice) * win
            pltpu.sync_copy(idx_hbm.at[pl.ds(src_off, win)], idx_v)
            pltpu.async_copy(src_hbm.at[idx_v], rows_v, sem).wait()
            pltpu.sync_copy(rows_v, out_hbm.at[pl.ds(g * win, win)])

    return pl.kernel(
        body, mesh=_sc_mesh(),
        out_type=jax.ShapeDtypeStruct((TOP_K * N_TRUNK, HALF), jnp.uint32),
        scratch_types=[pltpu.VMEM((win,), jnp.int32), pltpu.VMEM((win, HALF), jnp.uint32),
                       pltpu.SemaphoreType.DMA],
        name="sc_return",
    )(ys, pos_flat)


def _expert_body(te_ref, nt_ref, xs_ref, wg_ref, wu_ref, wd_ref, ys_ref, wgb_ref, wub_ref, wdb_ref):
    i = pl.program_id(0)

    @pl.when(jnp.logical_or(i == 0, te_ref[i] != te_ref[jnp.maximum(i - 1, 0)]))
    def _():
        wgb_ref[...] = wg_ref[...].astype(BF16)
        wub_ref[...] = wu_ref[...].astype(BF16)
        wdb_ref[...] = wd_ref[...].astype(BF16)

    @pl.when(i < nt_ref[0])
    def _():
        lo, hi = _unpack_bf16_pair(xs_ref[...])
        h = jnp.concatenate([lo.astype(BF16), hi.astype(BF16)], axis=1)
        y = _ffn(h, wgb_ref[...], wub_ref[...], wdb_ref[...])
        ys_ref[...] = _pack_bf16_pair(y[:, :HALF], y[:, HALF:])


def _expert_ffn(tile_expert, n_tiles, xs, layer, w_gate, w_up, w_down):
    tile = lambda i, te, nt: (jnp.minimum(i, nt[0] - 1), 0)
    ew = lambda shape: pl.BlockSpec((None, None) + shape, lambda i, te, nt: (layer, te[i], 0, 0))
    return pl.pallas_call(
        _expert_body,
        grid_spec=pltpu.PrefetchScalarGridSpec(
            num_scalar_prefetch=2,
            grid=(N_TILE,),
            in_specs=[
                pl.BlockSpec((MOE_TILE, HALF), tile),
                ew((D_MODEL, EXPERT_FF)), ew((D_MODEL, EXPERT_FF)), ew((EXPERT_FF, D_MODEL)),
            ],
            out_specs=pl.BlockSpec((MOE_TILE, HALF), tile),
            scratch_shapes=[pltpu.VMEM((D_MODEL, EXPERT_FF), BF16), pltpu.VMEM((D_MODEL, EXPERT_FF), BF16),
                            pltpu.VMEM((EXPERT_FF, D_MODEL), BF16)],
        ),
        out_shape=jax.ShapeDtypeStruct((N_SLOT, HALF), jnp.uint32),
        compiler_params=pltpu.CompilerParams(
            dimension_semantics=("arbitrary",), vmem_limit_bytes=VMEM_LIMIT),
        name="expert_ffn",
    )(tile_expert, n_tiles, xs, w_gate, w_up, w_down)


def _combine_body(x_ref, sh_ref, sc_ref, g2_ref, yk_ref, wt_ref, sg_ref, su_ref, sd_ref, lng_ref, lnb_ref,
                  o_ref, sgb_ref, sub_ref, sdb_ref):
    @pl.when(pl.program_id(0) == 0)
    def _():
        sgb_ref[...] = sg_ref[...].astype(BF16)
        sub_ref[...] = su_ref[...].astype(BF16)
        sdb_ref[...] = sd_ref[...].astype(BF16)

    x = x_ref[...]
    h = (x * (1.0 + sc_ref[...]) + sh_ref[...]).astype(BF16)
    y = _ffn(h, sgb_ref[...], sub_ref[...], sdb_ref[...])
    wt = wt_ref[...]
    lo_acc = jnp.zeros((x.shape[0], HALF), F32)
    hi_acc = jnp.zeros((x.shape[0], HALF), F32)
    for k in range(TOP_K):
        lo, hi = _unpack_bf16_pair(yk_ref[k])
        w = wt[:, k:k + 1]
        lo_acc = lo_acc + w * lo
        hi_acc = hi_acc + w * hi
    y = y + jnp.concatenate([lo_acc, hi_acc], axis=1)
    z = DEEPNORM_ALPHA * x + g2_ref[...] * y
    o_ref[...] = _layer_norm(z, lng_ref[...], lnb_ref[...])


def _combine(x, trunk, mod, yk, wt, layer, s_gate, s_up, s_down, ln_g, ln_b):
    tm = TOK_BLOCK
    blk0 = trunk * (N_TRUNK // tm)
    vec = pl.BlockSpec((1, D_MODEL), lambda i: (0, 0))
    sw = lambda shape: pl.BlockSpec((None,) + shape, lambda i: (layer, 0, 0))
    return pl.pallas_call(
        _combine_body,
        grid=(N_TRUNK // tm,),
        in_specs=[
            pl.BlockSpec((tm, D_MODEL), lambda i: (blk0 + i, 0)),
            _mod_spec(tm, 3, 1, 0, blk0), _mod_spec(tm, 4, 1, 0, blk0), _mod_spec(tm, 5, 1, 0, blk0),
            pl.BlockSpec((TOP_K, tm, HALF), lambda i: (0, i, 0)),
            pl.BlockSpec((tm, 128), lambda i: (blk0 + i, 0)),
            sw((D_MODEL, EXPERT_FF)), sw((D_MODEL, EXPERT_FF)), sw((EXPERT_FF, D_MODEL)),
            vec, vec,
        ],
        out_specs=pl.BlockSpec((tm, D_MODEL), lambda i: (i, 0)),
        out_shape=jax.ShapeDtypeStruct((N_TRUNK, D_MODEL), F32),
        scratch_shapes=[pltpu.VMEM((D_MODEL, EXPERT_FF), BF16), pltpu.VMEM((D_MODEL, EXPERT_FF), BF16),
                        pltpu.VMEM((EXPERT_FF, D_MODEL), BF16)],
        compiler_params=pltpu.CompilerParams(
            dimension_semantics=("arbitrary",), vmem_limit_bytes=VMEM_LIMIT),
        name="moe_combine_ln",
    )(x, mod, mod, mod, yk, wt, s_gate, s_up, s_down, ln_g, ln_b)


def _moe(x, routing, mod, layer, w_gate, w_up, w_down, s_gate, s_up, s_down, ln_g, ln_b):
    hp, eidx, rank, wt, counts = routing
    pos, tile_expert, n_tiles = _finalize(counts[:, 0], eidx, rank)
    pos_flat = pos.reshape(K_PAD * N_TOK)
    xs = _sc_dispatch(hp, pos_flat)
    ys = _expert_ffn(tile_expert, n_tiles, xs, layer, w_gate, w_up, w_down)
    yks = [_sc_return(ys, pos_flat, trunk).reshape(TOP_K, N_TRUNK, HALF) for trunk in range(2)]
    return tuple(_combine(x, trunk, mod, yks[trunk], wt, layer, s_gate, s_up, s_down, ln_g, ln_b)
                 for trunk in range(2))


def kernel(x_prompt, x_sample, state_ret, c, c_ctx, ada_w, ada_b, ln_g, ln_b, ret_w_in, ret_w_out, ret_decay, conv_w_in, conv_w, conv_w_out, moe_router, moe_bias, moe_w_gate, moe_w_up, moe_w_down, shared_w_gate, shared_w_up, shared_w_down):
    x = (x_prompt.reshape(N_CTX, D_MODEL), x_sample.reshape(N_LAT, D_MODEL))
    cond = jnp.concatenate(
        [c_ctx[None, :], c, jnp.zeros((N_COND - 1 - DEC_BATCH, D_MODEL), F32)], axis=0)
    mods = _ada_table(cond, ada_w, ada_b).reshape(DEPTH, N_COND, 1, 6 * D_MODEL)
    cos, sin = _rope_tables()
    router_wt = jnp.swapaxes(moe_router, 1, 2)

    states = None
    for i in range(DEPTH):
        j = i // 2
        mod = mods[i]
        lng = ln_g[i].reshape(2, 1, D_MODEL)
        lnb = ln_b[i].reshape(2, 1, D_MODEL)
        bias_col = moe_bias[i].reshape(N_EXPERTS, 1)
        if i % 2 == 0:
            proj = _modproj(x, mod, ret_w_in, j, 2048)
            a_ctx, states = _retention_ctx(proj, ret_decay[j], j, states)
            a_lat = _retention_lat(proj, ret_decay[j], state_ret, j, cos, sin)
            xm, *routing = _ret_out(a_ctx, a_lat, x, mod, ret_w_out, j, lng[0], lnb[0], router_wt[i], bias_col)
        else:
            xm, *routing = _conv_layer(x, mod, conv_w_in, conv_w_out, j, conv_w[j], lng[0], lnb[0],
                                       router_wt[i], bias_col)
        x = _moe(xm, routing, mod, i, moe_w_gate, moe_w_up, moe_w_down,
                 shared_w_gate, shared_w_up, shared_w_down, lng[1], lnb[1])

    y_prompt = x[0].reshape(BATCH, SEQ, D_MODEL)
    y_sample = x[1].reshape(DEC_BATCH, DEC_SEQ, D_MODEL)
    return y_prompt, y_sample, states
```

```python
import functools

import jax
import jax.numpy as jnp
from jax import lax
from jax.experimental import pallas as pl
from jax.experimental.pallas import tpu as pltpu

F32 = jnp.float32
BF16 = jnp.bfloat16

D_MODEL = 1024
BATCH = 32
SEQ = 256
DEPTH = 4
DEC_BATCH = 8
DEC_SEQ = 1024
GRID_W = 64
RET_HEADS = 4
RET_DK = D_MODEL // RET_HEADS
RET_DV = 2 * D_MODEL // RET_HEADS
ROPE_BASE = 10000.0
N_EXPERTS = 64
TOP_K = 6
N_GROUPS = 8
TOPK_GROUPS = 4
EXPERT_FF = 256
ROUTED_SCALE = 2.5
LN_EPS = 1e-5
DEEPNORM_ALPHA = (2.0 * DEPTH) ** 0.25

N_CTX = BATCH * SEQ
N_LAT = DEC_BATCH * DEC_SEQ
N_TOK = N_CTX + N_LAT
N_COND = 16
RET_CHUNK = 256
assert N_CTX == N_LAT
N_TRUNK = N_CTX
TOK_BLOCK = 256
MIX_BLOCK = 512
VMEM_LIMIT = 56 * 1024 * 1024

HALF = D_MODEL // 2
MOE_TILE = 896
N_PAIR = N_TOK * TOP_K
N_TILE = -(-(N_PAIR + N_EXPERTS * (MOE_TILE - 1)) // MOE_TILE)
N_SLOT = N_TILE * MOE_TILE
K_PAD = 8

SC_CORES = 2
SC_SUBCORES = 16
SC_WORKERS = SC_CORES * SC_SUBCORES
SC_SCATTER_WIN = 64
SC_GATHER_WIN = 128


def _cond_row(tok_block_idx, tok_block):
    t0 = tok_block_idx * tok_block
    return jnp.where(t0 < N_CTX, 0, 1 + (t0 - N_CTX) // DEC_SEQ)


def _silu(x):
    return x * jax.nn.sigmoid(x)


def _layer_norm(z, g, b):
    mu = jnp.mean(z, axis=-1, keepdims=True)
    zc = z - mu
    var = jnp.mean(zc * zc, axis=-1, keepdims=True)
    return zc * lax.rsqrt(var + LN_EPS) * g + b


def _ada_body(cond_ref, w_ref, b_ref, o_ref):
    s = _silu(cond_ref[...]).astype(BF16)
    o_ref[...] = jnp.dot(s, w_ref[...].astype(BF16), preferred_element_type=F32) + b_ref[...]


def _ada_table(cond, ada_w, ada_b):
    tn = 2048
    return pl.pallas_call(
        _ada_body,
        grid=(DEPTH, 6 * D_MODEL // tn),
        in_specs=[
            pl.BlockSpec((N_COND, D_MODEL), lambda l, j: (0, 0)),
            pl.BlockSpec((None, D_MODEL, tn), lambda l, j: (l, 0, j)),
            pl.BlockSpec((None, 1, tn), lambda l, j: (l, 0, j)),
        ],
        out_specs=pl.BlockSpec((None, N_COND, tn), lambda l, j: (l, 0, j)),
        out_shape=jax.ShapeDtypeStruct((DEPTH, N_COND, 6 * D_MODEL), F32),
        compiler_params=pltpu.CompilerParams(
            dimension_semantics=("arbitrary", "arbitrary"), vmem_limit_bytes=VMEM_LIMIT),
        name="ada_table",
    )(cond, ada_w, ada_b.reshape(DEPTH, 1, 6 * D_MODEL))


def _mod_spec(tok_block, col, grid_rank, tok_axis, first_block=0):
    def index_map(*idx):
        return (_cond_row(first_block + idx[tok_axis], tok_block), 0, col)
    del grid_rank
    return pl.BlockSpec((None, 1, D_MODEL), index_map)


def _x_specs(x, tm, tok_axis):
    if not isinstance(x, tuple):
        return [pl.BlockSpec((tm, D_MODEL), lambda *idx: (idx[tok_axis], 0))], (x,)
    n_ctx_blk = N_CTX // tm
    return [
        pl.BlockSpec((tm, D_MODEL), lambda *idx: (jnp.minimum(idx[tok_axis], n_ctx_blk - 1), 0)),
        pl.BlockSpec((tm, D_MODEL), lambda *idx: (jnp.maximum(idx[tok_axis] - n_ctx_blk, 0), 0)),
    ], x


def _x_block(x_refs, tok_block_idx, tm):
    if len(x_refs) == 1:
        return x_refs[0][...]
    return jnp.where(tok_block_idx * tm < N_CTX, x_refs[0][...], x_refs[1][...])


def _to_bf16_body(w_ref, o_ref):
    o_ref[...] = w_ref[...].astype(BF16)


def _to_bf16(w_all, w_idx):
    k_dim, n_out = w_all.shape[1:]
    tn = 1024
    return pl.pallas_call(
        _to_bf16_body,
        grid=(n_out // tn,),
        in_specs=[pl.BlockSpec((None, k_dim, tn), lambda j: (w_idx, 0, j))],
        out_specs=pl.BlockSpec((k_dim, tn), lambda j: (0, j)),
        out_shape=jax.ShapeDtypeStruct((k_dim, n_out), BF16),
        compiler_params=pltpu.CompilerParams(
            dimension_semantics=("arbitrary",), vmem_limit_bytes=VMEM_LIMIT),
        name="weight_to_bf16",
    )(w_all)


RET_GROUP = 1024


def _head_proj_specs():
    return [
        pl.BlockSpec((D_MODEL, RET_DK), lambda t, h: (0, h)),
        pl.BlockSpec((D_MODEL, RET_DK), lambda t, h: (0, RET_HEADS + h)),
        pl.BlockSpec((D_MODEL, RET_DV), lambda t, h: (0, RET_HEADS + h)),
        pl.BlockSpec((D_MODEL, RET_DV), lambda t, h: (0, 2 * RET_HEADS + h)),
    ]


def _head_proj(x_ref, sh_ref, sc_ref, wq_ref, wk_ref, wv_ref, wg_ref):
    h = (x_ref[...] * (1.0 + sc_ref[...]) + sh_ref[...]).astype(BF16)
    dot = lambda w_ref: jnp.dot(h, w_ref[...], preferred_element_type=F32)
    return dot(wq_ref), dot(wk_ref), dot(wv_ref).astype(BF16), dot(wg_ref)

def _log_sigmoid(v):
    return jnp.minimum(v, 0.0) - jnp.log1p(jnp.exp(-jnp.abs(v)))


def _decay_tables(dec_ref, head):
    c = RET_CHUNK
    lgf = _log_sigmoid(jnp.full((c, c), dec_ref[0, head], F32))
    lgb = _log_sigmoid(jnp.full((c, c), dec_ref[1, head], F32))
    row = lax.broadcasted_iota(jnp.int32, (c, c), 0).astype(F32)
    col = lax.broadcasted_iota(jnp.int32, (c, c), 1).astype(F32)
    diff = row - col
    kscale = RET_DK ** -0.5
    intra = jnp.where(diff > 0, jnp.exp(lgf * diff),
                      jnp.where(diff < 0, jnp.exp(-lgb * diff), 2.0)) * kscale
    qdec_f = jnp.exp(lgf * (row + 1.0))
    qdec_b = jnp.exp(lgb * (c - row))
    kdec_f = jnp.exp(lgf * (c - 1.0 - row)) * kscale
    kdec_b = jnp.exp(lgb * row) * kscale
    cdec_f = jnp.exp(lgf * c)
    cdec_b = jnp.exp(lgb * c)
    return intra, qdec_f, qdec_b, kdec_f, kdec_b, cdec_f, cdec_b


def _head_norm_gate(o, g):
    mu = jnp.mean(o, axis=-1, keepdims=True)
    oc = o - mu
    var = jnp.mean(oc * oc, axis=-1, keepdims=True)
    on = oc * lax.rsqrt(var + LN_EPS)
    return (_silu(g.astype(F32)) * on).astype(BF16)


_NT = (((1,), (1,)), ((), ()))
_TN = (((0,), (0,)), ((), ()))


def _ret_ctx_body(dec_ref, x_ref, sh_ref, sc_ref, wq_ref, wk_ref, wv_ref, wg_ref, *rest):
    a_ref, st_ref, tab_ref = rest[-3:]
    head = pl.program_id(1)

    @pl.when(pl.program_id(0) == 0)
    def _():
        intra, _, _, kdec_f, kdec_b, _, _ = _decay_tables(dec_ref, head)
        tab_ref[head, 0] = intra
        tab_ref[head, 1] = kdec_f
        tab_ref[head, 2] = kdec_b

    q, k, v, g = _head_proj(x_ref, sh_ref, sc_ref, wq_ref, wk_ref, wv_ref, wg_ref)
    for s in range(RET_GROUP // SEQ):
        rows = slice(s * SEQ, (s + 1) * SEQ)
        scores = lax.dot_general(q[rows].astype(BF16), k[rows].astype(BF16), _NT, preferred_element_type=F32)
        p = (scores * tab_ref[head, 0]).astype(BF16)
        o = jnp.dot(p, v[rows], preferred_element_type=F32)
        a_ref[rows, :] = _head_norm_gate(o, g[rows])
        st_ref[s, 0] = lax.dot_general((k[rows] * tab_ref[head, 1]).astype(BF16), v[rows], _TN,
                                       preferred_element_type=F32)
        st_ref[s, 1] = lax.dot_general((k[rows] * tab_ref[head, 2]).astype(BF16), v[rows], _TN,
                                       preferred_element_type=F32)


def _retention_ctx(x, mod, w_in_bf, decay, ret_idx, states):
    assert SEQ == RET_CHUNK
    seqs = RET_GROUP // SEQ
    n_ret = (DEPTH + 1) // 2
    st = jax.ShapeDtypeStruct((BATCH, n_ret, 2, RET_HEADS, RET_DK, RET_DV), F32)
    st_spec = pl.BlockSpec((seqs, None, 2, None, RET_DK, RET_DV), lambda t, h: (t, ret_idx, 0, h, 0, 0))
    in_specs = [
        pl.BlockSpec(memory_space=pltpu.SMEM),
        pl.BlockSpec((RET_GROUP, D_MODEL), lambda t, h: (t, 0)),
        _mod_spec(RET_GROUP, 0, 2, 0), _mod_spec(RET_GROUP, 1, 2, 0),
    ] + _head_proj_specs()
    args = (decay, x, mod, mod, w_in_bf, w_in_bf, w_in_bf, w_in_bf)
    aliases = {}
    if states is not None:
        in_specs.append(pl.BlockSpec(memory_space=pl.ANY))
        aliases = {len(args): 1}
        args += (states,)
    return pl.pallas_call(
        _ret_ctx_body,
        grid=(N_CTX // RET_GROUP, RET_HEADS),
        in_specs=in_specs,
        out_specs=[pl.BlockSpec((RET_GROUP, RET_DV), lambda t, h: (t, h)), st_spec],
        out_shape=[jax.ShapeDtypeStruct((N_CTX, RET_HEADS * RET_DV), BF16), st],
        input_output_aliases=aliases,
        scratch_shapes=[pltpu.VMEM((RET_HEADS, 3, RET_CHUNK, RET_CHUNK), F32)],
        compiler_params=pltpu.CompilerParams(
            dimension_semantics=("arbitrary", "arbitrary"), vmem_limit_bytes=VMEM_LIMIT),
        name="retention_ctx",
    )(*args)


def _rope(x, cos, sin):
    halves = [pltpu.roll(x[:, s:s + 128], 64, axis=1) for s in (0, 128)]
    return x * cos + jnp.concatenate(halves, axis=1) * sin


def _ret_lat_body(dec_ref, x_ref, sh_ref, sc_ref, wq_ref, wk_ref, wv_ref, wg_ref,
                  s0f_ref, s0b_ref, cos_ref, sin_ref,
                  a_ref, qr_ref, kr_ref, v_ref, g_ref, o_ref, st_ref):
    head = pl.program_id(1)
    c = RET_CHUNK
    nc = DEC_SEQ // c
    intra, qdec_f, qdec_b, kdec_f, kdec_b, cdec_f, cdec_b = _decay_tables(dec_ref, head)
    cdec_f = jnp.concatenate([cdec_f, cdec_f], axis=1)
    cdec_b = jnp.concatenate([cdec_b, cdec_b], axis=1)

    q, k, v, g = _head_proj(x_ref, sh_ref, sc_ref, wq_ref, wk_ref, wv_ref, wg_ref)
    qr_ref[...] = _rope(q, cos_ref[...], sin_ref[...])
    kr_ref[...] = _rope(k, cos_ref[...], sin_ref[...])
    v_ref[...] = v
    g_ref[...] = g

    st_ref[...] = s0f_ref[...]
    for ci in range(nc):
        rows = pl.ds(ci * c, c)
        q = qr_ref[rows, :]
        k = kr_ref[rows, :]
        v = v_ref[rows, :]
        scores = lax.dot_general(q.astype(BF16), k.astype(BF16), _NT, preferred_element_type=F32)
        p = (scores * intra).astype(BF16)
        o = jnp.dot(p, v, preferred_element_type=F32)
        qd = (q * qdec_f).astype(BF16)
        o = o + jnp.dot(qd, st_ref[...].astype(BF16), preferred_element_type=F32)
        o_ref[rows, :] = o
        if ci + 1 < nc:
            kd = (k * kdec_f).astype(BF16)
            st_ref[...] = cdec_f * st_ref[...] + lax.dot_general(kd, v, _TN, preferred_element_type=F32)

    st_ref[...] = s0b_ref[...]
    for ci in reversed(range(nc)):
        rows = pl.ds(ci * c, c)
        qd = (qr_ref[rows, :] * qdec_b).astype(BF16)
        o = o_ref[rows, :] + jnp.dot(qd, st_ref[...].astype(BF16), preferred_element_type=F32)
        a_ref[rows, :] = _head_norm_gate(o, g_ref[rows, :])
        if ci > 0:
            kd = (kr_ref[rows, :] * kdec_b).astype(BF16)
            st_ref[...] = cdec_b * st_ref[...] + lax.dot_general(kd, v_ref[rows, :], _TN,
                                                                 preferred_element_type=F32)


def _rope_tables():
    half = RET_DK // 4
    freqs = ROPE_BASE ** (-jnp.arange(half, dtype=F32) / half)
    t = jnp.arange(DEC_SEQ)
    row = (t // GRID_W).astype(F32)
    col = (t % GRID_W).astype(F32)
    ang_r = row[:, None] * freqs[None, :]
    ang_c = col[:, None] * freqs[None, :]
    cos = jnp.concatenate([jnp.cos(ang_r)] * 2 + [jnp.cos(ang_c)] * 2, axis=1)
    sin = jnp.concatenate([-jnp.sin(ang_r), jnp.sin(ang_r), -jnp.sin(ang_c), jnp.sin(ang_c)], axis=1)
    return cos, sin


def _retention_lat(x, mod, w_in_bf, decay, state_ret, ret_idx, cos, sin):
    n = DEC_SEQ
    assert n == RET_GROUP
    s0_spec = lambda d: pl.BlockSpec((None, None, None, None, RET_DK, RET_DV),
                                     lambda b, h: (b, ret_idx, d, h, 0, 0))
    tab_spec = pl.BlockSpec((n, RET_DK), lambda b, h: (0, 0))
    first_block = N_CTX // n
    return pl.pallas_call(
        _ret_lat_body,
        grid=(DEC_BATCH, RET_HEADS),
        in_specs=[
            pl.BlockSpec(memory_space=pltpu.SMEM),
            pl.BlockSpec((n, D_MODEL), lambda b, h: (b, 0)),
            _mod_spec(n, 0, 2, 0, first_block), _mod_spec(n, 1, 2, 0, first_block),
        ] + _head_proj_specs() + [s0_spec(0), s0_spec(1), tab_spec, tab_spec],
        out_specs=pl.BlockSpec((n, RET_DV), lambda b, h: (b, h)),
        out_shape=jax.ShapeDtypeStruct((N_LAT, RET_HEADS * RET_DV), BF16),
        scratch_shapes=[
            pltpu.VMEM((n, RET_DK), F32), pltpu.VMEM((n, RET_DK), F32),
            pltpu.VMEM((n, RET_DV), BF16), pltpu.VMEM((n, RET_DV), F32),
            pltpu.VMEM((n, RET_DV), F32), pltpu.VMEM((RET_DK, RET_DV), F32),
        ],
        compiler_params=pltpu.CompilerParams(
            dimension_semantics=("arbitrary", "arbitrary"), vmem_limit_bytes=VMEM_LIMIT),
        name="retention_lat",
    )(decay, x, mod, mod, w_in_bf, w_in_bf, w_in_bf, w_in_bf, state_ret, state_ret, cos, sin)


N_ROUTE_IN = 4
N_ROUTE_OUT = 5


def _ret_out_body(actx_ref, alat_ref, *refs):
    n_x = len(refs) - (4 + N_ROUTE_IN + 1 + N_ROUTE_OUT + 2)
    x_refs = refs[:n_x]
    g1_ref, w_ref, lng_ref, lnb_ref = refs[n_x:n_x + 4]
    route_in = refs[n_x + 4:n_x + 4 + N_ROUTE_IN]
    o_ref = refs[n_x + 4 + N_ROUTE_IN]
    route_out = refs[n_x + 5 + N_ROUTE_IN:n_x + 5 + N_ROUTE_IN + N_ROUTE_OUT]
    wbf_ref, carry_ref = refs[-2:]
    i = pl.program_id(0)
    tm = o_ref.shape[0]

    @pl.when(i == 0)
    def _():
        wbf_ref[...] = w_ref[...].astype(BF16)

    a = jnp.where(i * tm < N_CTX, actx_ref[...], alat_ref[...])
    y = jnp.dot(a, wbf_ref[...], preferred_element_type=F32)
    z = DEEPNORM_ALPHA * _x_block(x_refs, i, tm) + g1_ref[...] * y
    xn = _layer_norm(z, lng_ref[...], lnb_ref[...])
    o_ref[...] = xn
    _route_block(xn, *route_in, *route_out, carry_ref)


def _ret_out(a_ctx, a_lat, x, mod, w_all, w_idx, ln_g, ln_b, router_wt, bias_col):
    tm = MIX_BLOCK
    k_dim = w_all.shape[1]
    n_ctx_blk = N_CTX // tm
    vec = pl.BlockSpec((1, D_MODEL), lambda i: (0, 0))
    x_specs, x_args = _x_specs(x, tm, 0)
    r_in, r_out, r_shapes, r_scratch = _route_io(tm)
    return pl.pallas_call(
        _ret_out_body,
        grid=(N_TOK // tm,),
        in_specs=[
            pl.BlockSpec((tm, k_dim), lambda i: (jnp.minimum(i, n_ctx_blk - 1), 0)),
            pl.BlockSpec((tm, k_dim), lambda i: (jnp.maximum(i - n_ctx_blk, 0), 0)),
        ] + x_specs + [
            _mod_spec(tm, 2, 1, 0),
            pl.BlockSpec((None, k_dim, D_MODEL), lambda i: (w_idx, 0, 0)),
            vec, vec,
        ] + r_in,
        out_specs=[pl.BlockSpec((tm, D_MODEL), lambda i: (i, 0))] + r_out,
        out_shape=[jax.ShapeDtypeStruct((N_TOK, D_MODEL), F32)] + r_shapes,
        scratch_shapes=[pltpu.VMEM((k_dim, D_MODEL), BF16)] + r_scratch,
        compiler_params=pltpu.CompilerParams(
            dimension_semantics=("arbitrary",), vmem_limit_bytes=VMEM_LIMIT),
        name="ret_out_ln_route",
    )(a_ctx, a_lat, *x_args, mod, w_all, ln_g, ln_b, mod, mod, router_wt, bias_col)


def _conv_layer_body(*refs):
    n_x = len(refs) - (8 + N_ROUTE_IN + 1 + N_ROUTE_OUT + 3)
    x_refs = refs[:n_x]
    sh_ref, sc_ref, g1_ref, win_ref, wout_ref, cw_ref, lng_ref, lnb_ref = refs[n_x:n_x + 8]
    route_in = refs[n_x + 8:n_x + 8 + N_ROUTE_IN]
    o_ref = refs[n_x + 8 + N_ROUTE_IN]
    route_out = refs[n_x + 9 + N_ROUTE_IN:n_x + 9 + N_ROUTE_IN + N_ROUTE_OUT]
    winb_ref, woutb_ref, carry_ref = refs[-3:]
    tm = o_ref.shape[0]

    @pl.when(pl.program_id(0) == 0)
    def _():
        winb_ref[...] = win_ref[...].astype(BF16)
        woutb_ref[...] = wout_ref[...].astype(BF16)

    x = _x_block(x_refs, pl.program_id(0), tm)
    h = (x * (1.0 + sc_ref[...]) + sh_ref[...]).astype(BF16)
    proj = jnp.dot(h, winb_ref[...], preferred_element_type=F32)
    bg, cg, xt = (proj[:, c * D_MODEL:(c + 1) * D_MODEL] for c in range(3))
    seg = jnp.where(pl.program_id(0) * tm < N_CTX, SEQ, GRID_W)
    u = cg * xt
    pos = lax.broadcasted_iota(jnp.int32, (tm, D_MODEL), 0) & (seg - 1)
    u_prev = jnp.where(pos == 0, 0.0, pltpu.roll(u, 1, axis=0))
    u_next = jnp.where(pos == seg - 1, 0.0, pltpu.roll(u, tm - 1, axis=0))
    cu = u_prev * cw_ref[0:1, :] + u * cw_ref[1:2, :] + u_next * cw_ref[2:3, :]
    y = jnp.dot((bg * cu).astype(BF16), woutb_ref[...], preferred_element_type=F32)
    z = DEEPNORM_ALPHA * x + g1_ref[...] * y
    xn = _layer_norm(z, lng_ref[...], lnb_ref[...])
    o_ref[...] = xn
    _route_block(xn, *route_in, *route_out, carry_ref)


def _conv_layer(x, mod, w_in_all, w_out_all, w_idx, conv_w, ln_g, ln_b, router_wt, bias_col):
    tm = MIX_BLOCK
    vec = pl.BlockSpec((1, D_MODEL), lambda i: (0, 0))
    once = pl.Buffered(1)
    x_specs, x_args = _x_specs(x, tm, 0)
    r_in, r_out, r_shapes, r_scratch = _route_io(tm)
    return pl.pallas_call(
        _conv_layer_body,
        grid=(N_TOK // tm,),
        in_specs=x_specs + [
            _mod_spec(tm, 0, 1, 0), _mod_spec(tm, 1, 1, 0), _mod_spec(tm, 2, 1, 0),
            pl.BlockSpec((None, D_MODEL, 3 * D_MODEL), lambda i: (w_idx, 0, 0), pipeline_mode=once),
            pl.BlockSpec((None, D_MODEL, D_MODEL), lambda i: (w_idx, 0, 0), pipeline_mode=once),
            pl.BlockSpec((3, D_MODEL), lambda i: (0, 0)),
            vec, vec,
        ] + r_in,
        out_specs=[pl.BlockSpec((tm, D_MODEL), lambda i: (i, 0))] + r_out,
        out_shape=[jax.ShapeDtypeStruct((N_TOK, D_MODEL), F32)] + r_shapes,
        scratch_shapes=[pltpu.VMEM((D_MODEL, 3 * D_MODEL), BF16), pltpu.VMEM((D_MODEL, D_MODEL), BF16)]
        + r_scratch,
        compiler_params=pltpu.CompilerParams(
            dimension_semantics=("arbitrary",), vmem_limit_bytes=VMEM_LIMIT),
        name="conv_layer_route",
    )(*x_args, mod, mod, mod, w_in_all, w_out_all, conv_w, ln_g, ln_b, mod, mod, router_wt, bias_col)


def _split_bf16(v):
    hi = v.astype(BF16)
    lo = (v - hi.astype(F32)).astype(BF16)
    return hi, lo


def _pack_bf16_pair(lo_f32, hi_f32):
    lo = lax.bitcast_convert_type(lo_f32.astype(BF16).astype(F32), jnp.uint32) >> 16
    hi = lax.bitcast_convert_type(hi_f32.astype(BF16).astype(F32), jnp.uint32) & jnp.uint32(0xFFFF0000)
    return hi | lo


def _unpack_bf16_pair(u):
    lo = lax.bitcast_convert_type(u << 16, F32)
    hi = lax.bitcast_convert_type(u & jnp.uint32(0xFFFF0000), F32)
    return lo, hi


def _rows_to_tile(rows, n_sub, dtype):
    tm = rows[0].shape[1]
    sub = lax.broadcasted_iota(jnp.int32, (n_sub, tm), 0)
    out = jnp.zeros((n_sub, tm), dtype)
    for k, r in enumerate(rows):
        out = jnp.where(sub == k, jnp.broadcast_to(r.astype(dtype), (n_sub, tm)), out)
    return out


def _route_block(xn, sh_ref, sc_ref, rwt_ref, bias_ref,
                 hp_ref, eidx_ref, rank_ref, wt_ref, cnt_ref, carry_ref):
    tm = xn.shape[0]
    e = N_EXPERTS
    per = e // N_GROUPS
    neg = -jnp.inf

    @pl.when(pl.program_id(0) == 0)
    def _():
        carry_ref[...] = jnp.zeros_like(carry_ref)

    h = xn * (1.0 + sc_ref[...]) + sh_ref[...]
    hp_ref[...] = _pack_bf16_pair(h[:, :HALF], h[:, HALF:])
    h_hi, h_lo = _split_bf16(h)
    w_hi, w_lo = _split_bf16(rwt_ref[...])
    dot = lambda a, b: lax.dot_general(a, b, _NT, preferred_element_type=F32)
    logits = dot(w_hi, h_hi) + (dot(w_hi, h_lo) + dot(w_lo, h_hi))
    s = jax.nn.sigmoid(logits)
    sel = s + bias_ref[...]

    g3 = sel.reshape(N_GROUPS, per, tm)
    sub = lax.broadcasted_iota(jnp.int32, (N_GROUPS, per, tm), 1)
    m1 = jnp.max(g3, axis=1, keepdims=True)
    i1 = jnp.min(jnp.where(g3 == m1, sub, per), axis=1, keepdims=True)
    m2 = jnp.max(jnp.where(sub == i1, neg, g3), axis=1, keepdims=True)
    gs = (m1 + m2).reshape(N_GROUPS, tm)

    gi = lax.broadcasted_iota(jnp.int32, (N_GROUPS, tm), 0)
    gmask = jnp.zeros((N_GROUPS, tm), jnp.bool_)
    cur = gs
    for _ in range(TOPK_GROUPS):
        m = jnp.max(cur, axis=0, keepdims=True)
        idx = jnp.min(jnp.where(cur == m, gi, N_GROUPS), axis=0, keepdims=True)
        pick = gi == idx
        gmask = jnp.logical_or(gmask, pick)
        cur = jnp.where(pick, neg, cur)
    emask = jnp.broadcast_to(gmask.reshape(N_GROUPS, 1, tm), (N_GROUPS, per, tm)).reshape(e, tm)

    ei = lax.broadcasted_iota(jnp.int32, (e, tm), 0)
    picks, ids = [], []
    cur = jnp.where(emask, sel, neg)
    for _ in range(TOP_K):
        m = jnp.max(cur, axis=0, keepdims=True)
        idx = jnp.min(jnp.where(cur == m, ei, e), axis=0, keepdims=True)
        pick = ei == idx
        picks.append(pick)
        ids.append(idx)
        cur = jnp.where(pick, neg, cur)

    chosen = functools.reduce(jnp.logical_or, picks)
    cf = jnp.where(chosen, 1.0, 0.0)
    before = (lax.broadcasted_iota(jnp.int32, (tm, tm), 0)
              < lax.broadcasted_iota(jnp.int32, (tm, tm), 1)).astype(BF16)
    rank = carry_ref[:, 0:1] + jnp.dot(cf.astype(BF16), before, preferred_element_type=F32)
    carry_ref[...] = carry_ref[...] + jnp.sum(cf, axis=1, keepdims=True)
    cnt_ref[...] = carry_ref[...].astype(jnp.int32)

    w_rows = [jnp.sum(jnp.where(p, s, 0.0), axis=0, keepdims=True) for p in picks]
    r_rows = [jnp.sum(jnp.where(p, rank, 0.0), axis=0, keepdims=True) for p in picks]
    den = functools.reduce(lambda a, b: a + b, w_rows)
    w_rows = [w / den * ROUTED_SCALE for w in w_rows]
    eidx_ref[...] = _rows_to_tile(ids, K_PAD, jnp.int32)
    rank_ref[...] = _rows_to_tile(r_rows, K_PAD, F32).astype(jnp.int32)
    wt_ref[...] = _rows_to_tile(w_rows, 128, F32).T


def _route_io(tm):
    tok_major = lambda i: (i, 0)
    choice_major = lambda i: (0, i)
    in_specs = [
        _mod_spec(tm, 3, 1, 0),
        _mod_spec(tm, 4, 1, 0),
        pl.BlockSpec((N_EXPERTS, D_MODEL), lambda i: (0, 0)),
        pl.BlockSpec((N_EXPERTS, 1), lambda i: (0, 0)),
    ]
    out_specs = [
        pl.BlockSpec((tm, HALF), tok_major),
        pl.BlockSpec((K_PAD, tm), choice_major),
        pl.BlockSpec((K_PAD, tm), choice_major),
        pl.BlockSpec((tm, 128), tok_major),
        pl.BlockSpec((N_EXPERTS, 128), lambda i: (0, 0)),
    ]
    out_shapes = [
        jax.ShapeDtypeStruct((N_TOK, HALF), jnp.uint32),
        jax.ShapeDtypeStruct((K_PAD, N_TOK), jnp.int32),
        jax.ShapeDtypeStruct((K_PAD, N_TOK), jnp.int32),
        jax.ShapeDtypeStruct((N_TOK, 128), F32),
        jax.ShapeDtypeStruct((N_EXPERTS, 128), jnp.int32),
    ]
    return in_specs, out_specs, out_shapes, [pltpu.VMEM((N_EXPERTS, 128), F32)]


def _finalize_body(cnt_ref, eidx_ref, rank_ref, pos_ref, te_ref, nt_ref):
    pos_ref[...] = rank_ref[...]

    def per_expert(e, carry):
        off, t = carry
        n_tile = lax.div(cnt_ref[e] + (MOE_TILE - 1), MOE_TILE)
        pos_ref[...] = pos_ref[...] + jnp.where(eidx_ref[...] == e, off, 0)

        def fill(j, c):
            te_ref[t + j] = e
            return c
        lax.fori_loop(0, n_tile, fill, 0)
        return off + n_tile * MOE_TILE, t + n_tile

    _, n_used = lax.fori_loop(0, N_EXPERTS, per_expert, (jnp.int32(0), jnp.int32(0)))
    nt_ref[0] = n_used

    def fill_tail(j, c):
        te_ref[j] = 0
        return c
    lax.fori_loop(n_used, N_TILE, fill_tail, 0)


def _finalize(counts, eidx, rank):
    smem = pl.BlockSpec(memory_space=pltpu.SMEM)
    full = pl.BlockSpec((K_PAD, N_TOK), lambda: (0, 0))
    return pl.pallas_call(
        _finalize_body,
        in_specs=[smem, full, full],
        out_specs=[full, smem, smem],
        out_shape=[
            jax.ShapeDtypeStruct((K_PAD, N_TOK), jnp.int32),
            jax.ShapeDtypeStruct((N_TILE,), jnp.int32),
            jax.ShapeDtypeStruct((1,), jnp.int32),
        ],
        compiler_params=pltpu.CompilerParams(vmem_limit_bytes=VMEM_LIMIT),
        name="route_finalize",
    )(counts, eidx, rank)


def _ffn(h, wgu, wd):
    gu = jnp.dot(h, wgu, preferred_element_type=F32)
    hid = _silu(gu[:, :EXPERT_FF]) * gu[:, EXPERT_FF:]
    return jnp.dot(hid.astype(BF16), wd, preferred_element_type=F32)


def _cast_ffn_weights(wg_ref, wu_ref, wd_ref, wgub_ref, wdb_ref):
    wgub_ref[:, :EXPERT_FF] = wg_ref[...].astype(BF16)
    wgub_ref[:, EXPERT_FF:] = wu_ref[...].astype(BF16)
    wdb_ref[...] = wd_ref[...].astype(BF16)


_FFN_WEIGHT_SCRATCH = [pltpu.VMEM((D_MODEL, 2 * EXPERT_FF), BF16), pltpu.VMEM((EXPERT_FF, D_MODEL), BF16)]


def _sc_mesh():
    from jax.experimental.pallas import tpu_sc as plsc
    return plsc.VectorSubcoreMesh(core_axis_name="c", subcore_axis_name="s",
                                  num_cores=SC_CORES, num_subcores=SC_SUBCORES)


def _sc_worker_id():
    return lax.axis_index("s") * SC_CORES + lax.axis_index("c")


def _sc_dispatch(hp, pos_flat):
    win = SC_SCATTER_WIN
    per_worker = N_TOK // SC_WORKERS

    def body(rows_hbm, idx_hbm, out_hbm, *scratch):
        idx_v, rows_v, sem = scratch[:TOP_K], scratch[TOP_K], scratch[TOP_K + 1]
        base = _sc_worker_id() * per_worker

        @pl.loop(0, per_worker // win)
        def _(j):
            off = base + j * win
            pltpu.sync_copy(rows_hbm.at[pl.ds(off, win)], rows_v)
            for k in range(TOP_K):
                pltpu.sync_copy(idx_hbm.at[pl.ds(k * N_TOK + off, win)], idx_v[k])
            copies = [pltpu.async_copy(rows_v, out_hbm.at[idx_v[k]], sem) for k in range(TOP_K)]
            for cp in copies:
                cp.wait()

    return pl.kernel(
        body, mesh=_sc_mesh(),
        out_type=jax.ShapeDtypeStruct((N_SLOT, HALF), jnp.uint32),
        scratch_types=[pltpu.VMEM((win,), jnp.int32)] * TOP_K
        + [pltpu.VMEM((win, HALF), jnp.uint32), pltpu.SemaphoreType.DMA],
        name="sc_dispatch",
    )(hp, pos_flat)


def _sc_return(ys, pos_flat, trunk):
    win = SC_GATHER_WIN
    wins_per_choice = N_TRUNK // win
    log2_wins = wins_per_choice.bit_length() - 1
    assert wins_per_choice == 1 << log2_wins
    wins_per_worker = TOP_K * wins_per_choice // SC_WORKERS

    def body(src_hbm, idx_hbm, out_hbm, idx_v, rows_v, sem):
        first = _sc_worker_id() * wins_per_worker

        @pl.loop(0, wins_per_worker)
        def _(j):
            g = first + j
            k = lax.shift_right_logical(g, log2_wins)
            src_off = k * N_TOK + trunk * N_TRUNK + (g - k * wins_per_choice) * win
            pltpu.sync_copy(idx_hbm.at[pl.ds(src_off, win)], idx_v)
            pltpu.async_copy(src_hbm.at[idx_v], rows_v, sem).wait()
            pltpu.sync_copy(rows_v, out_hbm.at[pl.ds(g * win, win)])

    return pl.kernel(
        body, mesh=_sc_mesh(),
        out_type=jax.ShapeDtypeStruct((TOP_K * N_TRUNK, HALF), jnp.uint32),
        scratch_types=[pltpu.VMEM((win,), jnp.int32), pltpu.VMEM((win, HALF), jnp.uint32),
                       pltpu.SemaphoreType.DMA],
        name="sc_return",
    )(ys, pos_flat)


def _expert_body(te_ref, xs_ref, wg_ref, wu_ref, wd_ref, ys_ref, wgub_ref, wdb_ref):
    i = pl.program_id(0)

    @pl.when(jnp.logical_or(i == 0, te_ref[i] != te_ref[jnp.maximum(i - 1, 0)]))
    def _():
        _cast_ffn_weights(wg_ref, wu_ref, wd_ref, wgub_ref, wdb_ref)

    lo, hi = _unpack_bf16_pair(xs_ref[...])
    h = jnp.concatenate([lo.astype(BF16), hi.astype(BF16)], axis=1)
    y = _ffn(h, wgub_ref[...], wdb_ref[...])
    ys_ref[...] = _pack_bf16_pair(y[:, :HALF], y[:, HALF:])


def _expert_ffn(tile_expert, n_tiles, xs, layer, w_gate, w_up, w_down):
    tile = lambda i, te: (i, 0)
    ew = lambda shape: pl.BlockSpec((None, None) + shape, lambda i, te: (layer, te[i], 0, 0))
    return pl.pallas_call(
        _expert_body,
        grid_spec=pltpu.PrefetchScalarGridSpec(
            num_scalar_prefetch=1,
            grid=(n_tiles[0],),
            in_specs=[
                pl.BlockSpec((MOE_TILE, HALF), tile),
                ew((D_MODEL, EXPERT_FF)), ew((D_MODEL, EXPERT_FF)), ew((EXPERT_FF, D_MODEL)),
            ],
            out_specs=pl.BlockSpec((MOE_TILE, HALF), tile),
            scratch_shapes=_FFN_WEIGHT_SCRATCH,
        ),
        out_shape=jax.ShapeDtypeStruct((N_SLOT, HALF), jnp.uint32),
        compiler_params=pltpu.CompilerParams(
            dimension_semantics=("arbitrary",), vmem_limit_bytes=VMEM_LIMIT),
        name="expert_ffn",
    )(tile_expert, xs, w_gate, w_up, w_down)


def _combine_body(x_ref, sh_ref, sc_ref, g2_ref, yk_ref, wt_ref, sg_ref, su_ref, sd_ref, lng_ref, lnb_ref,
                  o_ref, sgub_ref, sdb_ref):
    @pl.when(pl.program_id(0) == 0)
    def _():
        _cast_ffn_weights(sg_ref, su_ref, sd_ref, sgub_ref, sdb_ref)

    x = x_ref[...]
    h = (x * (1.0 + sc_ref[...]) + sh_ref[...]).astype(BF16)
    y = _ffn(h, sgub_ref[...], sdb_ref[...])
    wt = wt_ref[...]
    lo_acc = jnp.zeros((x.shape[0], HALF), F32)
    hi_acc = jnp.zeros((x.shape[0], HALF), F32)
    for k in range(TOP_K):
        lo, hi = _unpack_bf16_pair(yk_ref[k])
        w = wt[:, k:k + 1]
        lo_acc = lo_acc + w * lo
        hi_acc = hi_acc + w * hi
    y = y + jnp.concatenate([lo_acc, hi_acc], axis=1)
    z = DEEPNORM_ALPHA * x + g2_ref[...] * y
    o_ref[...] = _layer_norm(z, lng_ref[...], lnb_ref[...])


def _combine(x, trunk, mod, yk, wt, layer, s_gate, s_up, s_down, ln_g, ln_b):
    tm = TOK_BLOCK
    blk0 = trunk * (N_TRUNK // tm)
    vec = pl.BlockSpec((1, D_MODEL), lambda i: (0, 0))
    sw = lambda shape: pl.BlockSpec((None,) + shape, lambda i: (layer, 0, 0))
    return pl.pallas_call(
        _combine_body,
        grid=(N_TRUNK // tm,),
        in_specs=[
            pl.BlockSpec((tm, D_MODEL), lambda i: (blk0 + i, 0)),
            _mod_spec(tm, 3, 1, 0, blk0), _mod_spec(tm, 4, 1, 0, blk0), _mod_spec(tm, 5, 1, 0, blk0),
            pl.BlockSpec((TOP_K, tm, HALF), lambda i: (0, i, 0)),
            pl.BlockSpec((tm, 128), lambda i: (blk0 + i, 0)),
            sw((D_MODEL, EXPERT_FF)), sw((D_MODEL, EXPERT_FF)), sw((EXPERT_FF, D_MODEL)),
            vec, vec,
        ],
        out_specs=pl.BlockSpec((tm, D_MODEL), lambda i: (i, 0)),
        out_shape=jax.ShapeDtypeStruct((N_TRUNK, D_MODEL), F32),
        scratch_shapes=_FFN_WEIGHT_SCRATCH,
        compiler_params=pltpu.CompilerParams(
            dimension_semantics=("arbitrary",), vmem_limit_bytes=VMEM_LIMIT),
        name="moe_combine_ln",
    )(x, mod, mod, mod, yk, wt, s_gate, s_up, s_down, ln_g, ln_b)


def _moe(x, routing, mod, layer, w_gate, w_up, w_down, s_gate, s_up, s_down, ln_g, ln_b):
    hp, eidx, rank, wt, counts = routing
    pos, tile_expert, n_tiles = _finalize(counts[:, 0], eidx, rank)
    pos_flat = pos.reshape(K_PAD * N_TOK)
    xs = _sc_dispatch(hp, pos_flat)
    ys = _expert_ffn(tile_expert, n_tiles, xs, layer, w_gate, w_up, w_down)
    yks = [_sc_return(ys, pos_flat, trunk).reshape(TOP_K, N_TRUNK, HALF) for trunk in range(2)]
    return tuple(_combine(x, trunk, mod, yks[trunk], wt, layer, s_gate, s_up, s_down, ln_g, ln_b)
                 for trunk in range(2))


def kernel(x_prompt, x_sample, state_ret, c, c_ctx, ada_w, ada_b, ln_g, ln_b, ret_w_in, ret_w_out, ret_decay, conv_w_in, conv_w, conv_w_out, moe_router, moe_bias, moe_w_gate, moe_w_up, moe_w_down, shared_w_gate, shared_w_up, shared_w_down):
    x = (x_prompt.reshape(N_CTX, D_MODEL), x_sample.reshape(N_LAT, D_MODEL))
    cond = jnp.concatenate(
        [c_ctx[None, :], c, jnp.zeros((N_COND - 1 - DEC_BATCH, D_MODEL), F32)], axis=0)
    mods = _ada_table(cond, ada_w, ada_b).reshape(DEPTH, N_COND, 1, 6 * D_MODEL)
    cos, sin = _rope_tables()
    router_wt = jnp.swapaxes(moe_router, 1, 2)

    states = None
    for i in range(DEPTH):
        j = i // 2
        mod = mods[i]
        lng = ln_g[i].reshape(2, 1, D_MODEL)
        lnb = ln_b[i].reshape(2, 1, D_MODEL)
        bias_col = moe_bias[i].reshape(N_EXPERTS, 1)
        if i % 2 == 0:
            w_in_bf = _to_bf16(ret_w_in, j)
            a_ctx, states = _retention_ctx(x[0], mod, w_in_bf, ret_decay[j], j, states)
            a_lat = _retention_lat(x[1], mod, w_in_bf, ret_decay[j], state_ret, j, cos, sin)
            xm, *routing = _ret_out(a_ctx, a_lat, x, mod, ret_w_out, j, lng[0], lnb[0], router_wt[i], bias_col)
        else:
            xm, *routing = _conv_layer(x, mod, conv_w_in, conv_w_out, j, conv_w[j], lng[0], lnb[0],
                                       router_wt[i], bias_col)
        x = _moe(xm, routing, mod, i, moe_w_gate, moe_w_up, moe_w_down,
                 shared_w_gate, shared_w_up, shared_w_down, lng[1], lnb[1])

    y_prompt = x[0].reshape(BATCH, SEQ, D_MODEL)
    y_sample = x[1].reshape(DEC_BATCH, DEC_SEQ, D_MODEL)
    return y_prompt, y_sample, states
```

```python
import functools

import jax
import jax.numpy as jnp
from jax import lax
from jax.experimental import pallas as pl
from jax.experimental.pallas import tpu as pltpu

F32 = jnp.float32
BF16 = jnp.bfloat16

D_MODEL = 1024
BATCH = 32
SEQ = 256
DEPTH = 4
DEC_BATCH = 8
DEC_SEQ = 1024
GRID_W = 64
RET_HEADS = 4
RET_DK = D_MODEL // RET_HEADS
RET_DV = 2 * D_MODEL // RET_HEADS
ROPE_BASE = 10000.0
N_EXPERTS = 64
TOP_K = 6
N_GROUPS = 8
TOPK_GROUPS = 4
EXPERT_FF = 256
ROUTED_SCALE = 2.5
LN_EPS = 1e-5
DEEPNORM_ALPHA = (2.0 * DEPTH) ** 0.25

N_CTX = BATCH * SEQ
N_LAT = DEC_BATCH * DEC_SEQ
N_TOK = N_CTX + N_LAT
N_COND = 16
RET_CHUNK = 256
assert N_CTX == N_LAT
N_TRUNK = N_CTX
TOK_BLOCK = 256
MIX_BLOCK = 512
VMEM_LIMIT = 56 * 1024 * 1024

HALF = D_MODEL // 2
MOE_TILE = 896
N_PAIR = N_TRUNK * TOP_K
N_TILE = -(-(N_PAIR + N_EXPERTS * (MOE_TILE - 1)) // MOE_TILE)
N_SLOT = N_TILE * MOE_TILE
K_PAD = 8

SC_CORES = 2
SC_SUBCORES = 16
SC_WORKERS = SC_CORES * SC_SUBCORES
SC_SCATTER_WIN = 64
SC_GATHER_WIN = 128


def _cond_row(tok_block_idx, tok_block):
    t0 = tok_block_idx * tok_block
    return jnp.where(t0 < N_CTX, 0, 1 + (t0 - N_CTX) // DEC_SEQ)


def _silu(x):
    return x * jax.nn.sigmoid(x)


def _layer_norm(z, g, b):
    mu = jnp.mean(z, axis=-1, keepdims=True)
    zc = z - mu
    var = jnp.mean(zc * zc, axis=-1, keepdims=True)
    return zc * lax.rsqrt(var + LN_EPS) * g + b


def _ada_body(cond_ref, w_ref, b_ref, o_ref):
    s = _silu(cond_ref[...]).astype(BF16)
    o_ref[...] = jnp.dot(s, w_ref[...].astype(BF16), preferred_element_type=F32) + b_ref[...]


def _ada_table(cond, ada_w, ada_b):
    tn = 2048
    return pl.pallas_call(
        _ada_body,
        grid=(DEPTH, 6 * D_MODEL // tn),
        in_specs=[
            pl.BlockSpec((N_COND, D_MODEL), lambda l, j: (0, 0)),
            pl.BlockSpec((None, D_MODEL, tn), lambda l, j: (l, 0, j)),
            pl.BlockSpec((None, 1, tn), lambda l, j: (l, 0, j)),
        ],
        out_specs=pl.BlockSpec((None, N_COND, tn), lambda l, j: (l, 0, j)),
        out_shape=jax.ShapeDtypeStruct((DEPTH, N_COND, 6 * D_MODEL), F32),
        compiler_params=pltpu.CompilerParams(
            dimension_semantics=("arbitrary", "arbitrary"), vmem_limit_bytes=VMEM_LIMIT),
        name="ada_table",
    )(cond, ada_w, ada_b.reshape(DEPTH, 1, 6 * D_MODEL))


def _mod_spec(tok_block, col, grid_rank, tok_axis, first_block=0):
    def index_map(*idx):
        return (_cond_row(first_block + idx[tok_axis], tok_block), 0, col)
    del grid_rank
    return pl.BlockSpec((None, 1, D_MODEL), index_map)


def _to_bf16_body(w_ref, o_ref):
    o_ref[...] = w_ref[...].astype(BF16)


def _to_bf16(w_all, w_idx):
    k_dim, n_out = w_all.shape[1:]
    tn = 1024
    return pl.pallas_call(
        _to_bf16_body,
        grid=(n_out // tn,),
        in_specs=[pl.BlockSpec((None, k_dim, tn), lambda j: (w_idx, 0, j))],
        out_specs=pl.BlockSpec((k_dim, tn), lambda j: (0, j)),
        out_shape=jax.ShapeDtypeStruct((k_dim, n_out), BF16),
        compiler_params=pltpu.CompilerParams(
            dimension_semantics=("arbitrary",), vmem_limit_bytes=VMEM_LIMIT),
        name="weight_to_bf16",
    )(w_all)


RET_GROUP = 1024


def _head_proj_specs():
    return [
        pl.BlockSpec((D_MODEL, RET_DK), lambda t, h: (0, h)),
        pl.BlockSpec((D_MODEL, RET_DK), lambda t, h: (0, RET_HEADS + h)),
        pl.BlockSpec((D_MODEL, RET_DV), lambda t, h: (0, RET_HEADS + h)),
        pl.BlockSpec((D_MODEL, RET_DV), lambda t, h: (0, 2 * RET_HEADS + h)),
    ]


def _head_proj(x_ref, sh_ref, sc_ref, wq_ref, wk_ref, wv_ref, wg_ref):
    h = (x_ref[...] * (1.0 + sc_ref[...]) + sh_ref[...]).astype(BF16)
    dot = lambda w_ref: jnp.dot(h, w_ref[...], preferred_element_type=F32)
    return dot(wq_ref), dot(wk_ref), dot(wv_ref).astype(BF16), dot(wg_ref)

def _log_sigmoid(v):
    return jnp.minimum(v, 0.0) - jnp.log1p(jnp.exp(-jnp.abs(v)))


def _decay_tables(dec_ref, head):
    c = RET_CHUNK
    lgf = _log_sigmoid(jnp.full((c, c), dec_ref[0, head], F32))
    lgb = _log_sigmoid(jnp.full((c, c), dec_ref[1, head], F32))
    row = lax.broadcasted_iota(jnp.int32, (c, c), 0).astype(F32)
    col = lax.broadcasted_iota(jnp.int32, (c, c), 1).astype(F32)
    diff = row - col
    kscale = RET_DK ** -0.5
    intra = jnp.where(diff > 0, jnp.exp(lgf * diff),
                      jnp.where(diff < 0, jnp.exp(-lgb * diff), 2.0)) * kscale
    qdec_f = jnp.exp(lgf * (row + 1.0))
    qdec_b = jnp.exp(lgb * (c - row))
    kdec_f = jnp.exp(lgf * (c - 1.0 - row)) * kscale
    kdec_b = jnp.exp(lgb * row) * kscale
    cdec_f = jnp.exp(lgf * c)
    cdec_b = jnp.exp(lgb * c)
    return intra, qdec_f, qdec_b, kdec_f, kdec_b, cdec_f, cdec_b


def _head_norm_gate(o, g):
    mu = jnp.mean(o, axis=-1, keepdims=True)
    oc = o - mu
    var = jnp.mean(oc * oc, axis=-1, keepdims=True)
    on = oc * lax.rsqrt(var + LN_EPS)
    return (_silu(g.astype(F32)) * on).astype(BF16)


_NT = (((1,), (1,)), ((), ()))
_TN = (((0,), (0,)), ((), ()))


def _ret_ctx_body(dec_ref, x_ref, sh_ref, sc_ref, wq_ref, wk_ref, wv_ref, wg_ref, *rest):
    a_ref, st_ref, tab_ref = rest[-3:]
    head = pl.program_id(1)

    @pl.when(pl.program_id(0) == 0)
    def _():
        intra, _, _, kdec_f, kdec_b, _, _ = _decay_tables(dec_ref, head)
        tab_ref[head, 0] = intra
        tab_ref[head, 1] = kdec_f
        tab_ref[head, 2] = kdec_b

    q, k, v, g = _head_proj(x_ref, sh_ref, sc_ref, wq_ref, wk_ref, wv_ref, wg_ref)
    for s in range(RET_GROUP // SEQ):
        rows = slice(s * SEQ, (s + 1) * SEQ)
        scores = lax.dot_general(q[rows].astype(BF16), k[rows].astype(BF16), _NT, preferred_element_type=F32)
        p = (scores * tab_ref[head, 0]).astype(BF16)
        o = jnp.dot(p, v[rows], preferred_element_type=F32)
        a_ref[rows, :] = _head_norm_gate(o, g[rows])
        st_ref[s, 0] = lax.dot_general((k[rows] * tab_ref[head, 1]).astype(BF16), v[rows], _TN,
                                       preferred_element_type=F32)
        st_ref[s, 1] = lax.dot_general((k[rows] * tab_ref[head, 2]).astype(BF16), v[rows], _TN,
                                       preferred_element_type=F32)


def _retention_ctx(x, mod, w_in_bf, decay, ret_idx, states):
    assert SEQ == RET_CHUNK
    seqs = RET_GROUP // SEQ
    n_ret = (DEPTH + 1) // 2
    st = jax.ShapeDtypeStruct((BATCH, n_ret, 2, RET_HEADS, RET_DK, RET_DV), F32)
    st_spec = pl.BlockSpec((seqs, None, 2, None, RET_DK, RET_DV), lambda t, h: (t, ret_idx, 0, h, 0, 0))
    in_specs = [
        pl.BlockSpec(memory_space=pltpu.SMEM),
        pl.BlockSpec((RET_GROUP, D_MODEL), lambda t, h: (t, 0)),
        _mod_spec(RET_GROUP, 0, 2, 0), _mod_spec(RET_GROUP, 1, 2, 0),
    ] + _head_proj_specs()
    args = (decay, x, mod, mod, w_in_bf, w_in_bf, w_in_bf, w_in_bf)
    aliases = {}
    if states is not None:
        in_specs.append(pl.BlockSpec(memory_space=pl.ANY))
        aliases = {len(args): 1}
        args += (states,)
    return pl.pallas_call(
        _ret_ctx_body,
        grid=(N_CTX // RET_GROUP, RET_HEADS),
        in_specs=in_specs,
        out_specs=[pl.BlockSpec((RET_GROUP, RET_DV), lambda t, h: (t, h)), st_spec],
        out_shape=[jax.ShapeDtypeStruct((N_CTX, RET_HEADS * RET_DV), BF16), st],
        input_output_aliases=aliases,
        scratch_shapes=[pltpu.VMEM((RET_HEADS, 3, RET_CHUNK, RET_CHUNK), F32)],
        compiler_params=pltpu.CompilerParams(
            dimension_semantics=("arbitrary", "arbitrary"), vmem_limit_bytes=VMEM_LIMIT),
        name="retention_ctx",
    )(*args)


def _rope(x, cos, sin):
    halves = [pltpu.roll(x[:, s:s + 128], 64, axis=1) for s in (0, 128)]
    return x * cos + jnp.concatenate(halves, axis=1) * sin


def _ret_lat_body(dec_ref, x_ref, sh_ref, sc_ref, wq_ref, wk_ref, wv_ref, wg_ref,
                  s0f_ref, s0b_ref, cos_ref, sin_ref,
                  a_ref, qr_ref, kr_ref, v_ref, g_ref, o_ref, st_ref):
    head = pl.program_id(1)
    c = RET_CHUNK
    nc = DEC_SEQ // c
    intra, qdec_f, qdec_b, kdec_f, kdec_b, cdec_f, cdec_b = _decay_tables(dec_ref, head)
    cdec_f = jnp.concatenate([cdec_f, cdec_f], axis=1)
    cdec_b = jnp.concatenate([cdec_b, cdec_b], axis=1)

    q, k, v, g = _head_proj(x_ref, sh_ref, sc_ref, wq_ref, wk_ref, wv_ref, wg_ref)
    qr_ref[...] = _rope(q, cos_ref[...], sin_ref[...])
    kr_ref[...] = _rope(k, cos_ref[...], sin_ref[...])
    v_ref[...] = v
    g_ref[...] = g

    st_ref[...] = s0f_ref[...]
    for ci in range(nc):
        rows = pl.ds(ci * c, c)
        q = qr_ref[rows, :]
        k = kr_ref[rows, :]
        v = v_ref[rows, :]
        scores = lax.dot_general(q.astype(BF16), k.astype(BF16), _NT, preferred_element_type=F32)
        p = (scores * intra).astype(BF16)
        o = jnp.dot(p, v, preferred_element_type=F32)
        qd = (q * qdec_f).astype(BF16)
        o = o + jnp.dot(qd, st_ref[...].astype(BF16), preferred_element_type=F32)
        o_ref[rows, :] = o
        if ci + 1 < nc:
            kd = (k * kdec_f).astype(BF16)
            st_ref[...] = cdec_f * st_ref[...] + lax.dot_general(kd, v, _TN, preferred_element_type=F32)

    st_ref[...] = s0b_ref[...]
    for ci in reversed(range(nc)):
        rows = pl.ds(ci * c, c)
        qd = (qr_ref[rows, :] * qdec_b).astype(BF16)
        o = o_ref[rows, :] + jnp.dot(qd, st_ref[...].astype(BF16), preferred_element_type=F32)
        a_ref[rows, :] = _head_norm_gate(o, g_ref[rows, :])
        if ci > 0:
            kd = (kr_ref[rows, :] * kdec_b).astype(BF16)
            st_ref[...] = cdec_b * st_ref[...] + lax.dot_general(kd, v_ref[rows, :], _TN,
                                                                 preferred_element_type=F32)


def _rope_tables():
    half = RET_DK // 4
    freqs = ROPE_BASE ** (-jnp.arange(half, dtype=F32) / half)
    t = jnp.arange(DEC_SEQ)
    row = (t // GRID_W).astype(F32)
    col = (t % GRID_W).astype(F32)
    ang_r = row[:, None] * freqs[None, :]
    ang_c = col[:, None] * freqs[None, :]
    cos = jnp.concatenate([jnp.cos(ang_r)] * 2 + [jnp.cos(ang_c)] * 2, axis=1)
    sin = jnp.concatenate([-jnp.sin(ang_r), jnp.sin(ang_r), -jnp.sin(ang_c), jnp.sin(ang_c)], axis=1)
    return cos, sin


def _retention_lat(x, mod, w_in_bf, decay, state_ret, ret_idx, cos, sin):
    n = DEC_SEQ
    assert n == RET_GROUP
    s0_spec = lambda d: pl.BlockSpec((None, None, None, None, RET_DK, RET_DV),
                                     lambda b, h: (b, ret_idx, d, h, 0, 0))
    tab_spec = pl.BlockSpec((n, RET_DK), lambda b, h: (0, 0))
    first_block = N_CTX // n
    return pl.pallas_call(
        _ret_lat_body,
        grid=(DEC_BATCH, RET_HEADS),
        in_specs=[
            pl.BlockSpec(memory_space=pltpu.SMEM),
            pl.BlockSpec((n, D_MODEL), lambda b, h: (b, 0)),
            _mod_spec(n, 0, 2, 0, first_block), _mod_spec(n, 1, 2, 0, first_block),
        ] + _head_proj_specs() + [s0_spec(0), s0_spec(1), tab_spec, tab_spec],
        out_specs=pl.BlockSpec((n, RET_DV), lambda b, h: (b, h)),
        out_shape=jax.ShapeDtypeStruct((N_LAT, RET_HEADS * RET_DV), BF16),
        scratch_shapes=[
            pltpu.VMEM((n, RET_DK), F32), pltpu.VMEM((n, RET_DK), F32),
            pltpu.VMEM((n, RET_DV), BF16), pltpu.VMEM((n, RET_DV), F32),
            pltpu.VMEM((n, RET_DV), F32), pltpu.VMEM((RET_DK, RET_DV), F32),
        ],
        compiler_params=pltpu.CompilerParams(
            dimension_semantics=("arbitrary", "arbitrary"), vmem_limit_bytes=VMEM_LIMIT),
        name="retention_lat",
    )(decay, x, mod, mod, w_in_bf, w_in_bf, w_in_bf, w_in_bf, state_ret, state_ret, cos, sin)


N_ROUTE_IN = 4
N_ROUTE_OUT = 5


def _split_route_refs(rest, n_scratch):
    route_in = rest[:N_ROUTE_IN]
    o_ref = rest[N_ROUTE_IN]
    route_out = rest[N_ROUTE_IN + 1:N_ROUTE_IN + 1 + N_ROUTE_OUT]
    return route_in, o_ref, route_out, rest[len(rest) - n_scratch:]


def _ret_out_body(a_ref, x_ref, g1_ref, w_ref, lng_ref, lnb_ref, *rest):
    route_in, o_ref, route_out, (wbf_ref, carry_ref) = _split_route_refs(rest, 2)

    @pl.when(pl.program_id(0) == 0)
    def _():
        wbf_ref[...] = w_ref[...].astype(BF16)

    y = jnp.dot(a_ref[...], wbf_ref[...], preferred_element_type=F32)
    z = DEEPNORM_ALPHA * x_ref[...] + g1_ref[...] * y
    xn = _layer_norm(z, lng_ref[...], lnb_ref[...])
    o_ref[...] = xn
    _route_block(xn, *route_in, *route_out, carry_ref)


def _ret_out(trunk, a, x, mod, w_all, w_idx, ln_g, ln_b, router_wt, bias_col):
    tm = MIX_BLOCK
    k_dim = w_all.shape[1]
    first_block = trunk * (N_TRUNK // tm)
    row = lambda i: (i, 0)
    vec = pl.BlockSpec((1, D_MODEL), lambda i: (0, 0))
    r_in, r_out, r_shapes, r_scratch = _route_io(tm, first_block)
    return pl.pallas_call(
        _ret_out_body,
        grid=(N_TRUNK // tm,),
        in_specs=[
            pl.BlockSpec((tm, k_dim), row),
            pl.BlockSpec((tm, D_MODEL), row),
            _mod_spec(tm, 2, 1, 0, first_block),
            pl.BlockSpec((None, k_dim, D_MODEL), lambda i: (w_idx, 0, 0)),
            vec, vec,
        ] + r_in,
        out_specs=[pl.BlockSpec((tm, D_MODEL), row)] + r_out,
        out_shape=[jax.ShapeDtypeStruct((N_TRUNK, D_MODEL), F32)] + r_shapes,
        scratch_shapes=[pltpu.VMEM((k_dim, D_MODEL), BF16)] + r_scratch,
        compiler_params=pltpu.CompilerParams(
            dimension_semantics=("arbitrary",), vmem_limit_bytes=VMEM_LIMIT),
        name="ret_out_ln_route",
    )(a, x, mod, w_all, ln_g, ln_b, mod, mod, router_wt, bias_col)


def _conv_layer_body(seg, x_ref, sh_ref, sc_ref, g1_ref, win_ref, wout_ref, cw_ref, lng_ref, lnb_ref, *rest):
    route_in, o_ref, route_out, (winb_ref, woutb_ref, carry_ref) = _split_route_refs(rest, 3)
    tm = o_ref.shape[0]

    @pl.when(pl.program_id(0) == 0)
    def _():
        winb_ref[...] = win_ref[...].astype(BF16)
        woutb_ref[...] = wout_ref[...].astype(BF16)

    x = x_ref[...]
    h = (x * (1.0 + sc_ref[...]) + sh_ref[...]).astype(BF16)
    proj = jnp.dot(h, winb_ref[...], preferred_element_type=F32)
    bg, cg, xt = (proj[:, c * D_MODEL:(c + 1) * D_MODEL] for c in range(3))
    u = cg * xt
    pos = lax.broadcasted_iota(jnp.int32, (tm, D_MODEL), 0) & (seg - 1)
    u_prev = jnp.where(pos == 0, 0.0, pltpu.roll(u, 1, axis=0))
    u_next = jnp.where(pos == seg - 1, 0.0, pltpu.roll(u, tm - 1, axis=0))
    cu = u_prev * cw_ref[0:1, :] + u * cw_ref[1:2, :] + u_next * cw_ref[2:3, :]
    y = jnp.dot((bg * cu).astype(BF16), woutb_ref[...], preferred_element_type=F32)
    z = DEEPNORM_ALPHA * x + g1_ref[...] * y
    xn = _layer_norm(z, lng_ref[...], lnb_ref[...])
    o_ref[...] = xn
    _route_block(xn, *route_in, *route_out, carry_ref)


def _conv_layer(trunk, x, mod, w_in_all, w_out_all, w_idx, conv_w, ln_g, ln_b, router_wt, bias_col):
    tm = MIX_BLOCK
    first_block = trunk * (N_TRUNK // tm)
    seg = SEQ if trunk == 0 else GRID_W
    assert tm % seg == 0 and seg & (seg - 1) == 0
    row = lambda i: (i, 0)
    vec = pl.BlockSpec((1, D_MODEL), lambda i: (0, 0))
    once = pl.Buffered(1)
    r_in, r_out, r_shapes, r_scratch = _route_io(tm, first_block)
    mod_spec = lambda col: _mod_spec(tm, col, 1, 0, first_block)
    return pl.pallas_call(
        functools.partial(_conv_layer_body, seg),
        grid=(N_TRUNK // tm,),
        in_specs=[
            pl.BlockSpec((tm, D_MODEL), row),
            mod_spec(0), mod_spec(1), mod_spec(2),
            pl.BlockSpec((None, D_MODEL, 3 * D_MODEL), lambda i: (w_idx, 0, 0), pipeline_mode=once),
            pl.BlockSpec((None, D_MODEL, D_MODEL), lambda i: (w_idx, 0, 0), pipeline_mode=once),
            pl.BlockSpec((3, D_MODEL), lambda i: (0, 0)),
            vec, vec,
        ] + r_in,
        out_specs=[pl.BlockSpec((tm, D_MODEL), row)] + r_out,
        out_shape=[jax.ShapeDtypeStruct((N_TRUNK, D_MODEL), F32)] + r_shapes,
        scratch_shapes=[pltpu.VMEM((D_MODEL, 3 * D_MODEL), BF16), pltpu.VMEM((D_MODEL, D_MODEL), BF16)]
        + r_scratch,
        compiler_params=pltpu.CompilerParams(
            dimension_semantics=("arbitrary",), vmem_limit_bytes=VMEM_LIMIT),
        name="conv_layer_route",
    )(x, mod, mod, mod, w_in_all, w_out_all, conv_w, ln_g, ln_b, mod, mod, router_wt, bias_col)


def _split_bf16(v):
    hi = v.astype(BF16)
    lo = (v - hi.astype(F32)).astype(BF16)
    return hi, lo


def _pack_bf16_pair(lo_f32, hi_f32):
    lo = lax.bitcast_convert_type(lo_f32.astype(BF16).astype(F32), jnp.uint32) >> 16
    hi = lax.bitcast_convert_type(hi_f32.astype(BF16).astype(F32), jnp.uint32) & jnp.uint32(0xFFFF0000)
    return hi | lo


def _unpack_bf16_pair(u):
    lo = lax.bitcast_convert_type(u << 16, F32)
    hi = lax.bitcast_convert_type(u & jnp.uint32(0xFFFF0000), F32)
    return lo, hi


def _rows_to_tile(rows, n_sub, dtype):
    tm = rows[0].shape[1]
    sub = lax.broadcasted_iota(jnp.int32, (n_sub, tm), 0)
    out = jnp.zeros((n_sub, tm), dtype)
    for k, r in enumerate(rows):
        out = jnp.where(sub == k, jnp.broadcast_to(r.astype(dtype), (n_sub, tm)), out)
    return out


def _route_block(xn, sh_ref, sc_ref, rwt_ref, bias_ref,
                 hp_ref, eidx_ref, rank_ref, wt_ref, cnt_ref, carry_ref):
    tm = xn.shape[0]
    e = N_EXPERTS
    per = e // N_GROUPS
    neg = -jnp.inf

    @pl.when(pl.program_id(0) == 0)
    def _():
        carry_ref[...] = jnp.zeros_like(carry_ref)

    h = xn * (1.0 + sc_ref[...]) + sh_ref[...]
    hp_ref[...] = _pack_bf16_pair(h[:, :HALF], h[:, HALF:])
    h_hi, h_lo = _split_bf16(h)
    w_hi, w_lo = _split_bf16(rwt_ref[...])
    dot = lambda a, b: lax.dot_general(a, b, _NT, preferred_element_type=F32)
    logits = dot(w_hi, h_hi) + (dot(w_hi, h_lo) + dot(w_lo, h_hi))
    s = jax.nn.sigmoid(logits)
    sel = s + bias_ref[...]

    g3 = sel.reshape(N_GROUPS, per, tm)
    sub = lax.broadcasted_iota(jnp.int32, (N_GROUPS, per, tm), 1)
    m1 = jnp.max(g3, axis=1, keepdims=True)
    i1 = jnp.min(jnp.where(g3 == m1, sub, per), axis=1, keepdims=True)
    m2 = jnp.max(jnp.where(sub == i1, neg, g3), axis=1, keepdims=True)
    gs = (m1 + m2).reshape(N_GROUPS, tm)

    gi = lax.broadcasted_iota(jnp.int32, (N_GROUPS, tm), 0)
    gmask = jnp.zeros((N_GROUPS, tm), jnp.bool_)
    cur = gs
    for _ in range(TOPK_GROUPS):
        m = jnp.max(cur, axis=0, keepdims=True)
        idx = jnp.min(jnp.where(cur == m, gi, N_GROUPS), axis=0, keepdims=True)
        pick = gi == idx
        gmask = jnp.logical_or(gmask, pick)
        cur = jnp.where(pick, neg, cur)
    emask = jnp.broadcast_to(gmask.reshape(N_GROUPS, 1, tm), (N_GROUPS, per, tm)).reshape(e, tm)

    ei = lax.broadcasted_iota(jnp.int32, (e, tm), 0)
    picks, ids = [], []
    cur = jnp.where(emask, sel, neg)
    for _ in range(TOP_K):
        m = jnp.max(cur, axis=0, keepdims=True)
        idx = jnp.min(jnp.where(cur == m, ei, e), axis=0, keepdims=True)
        pick = ei == idx
        picks.append(pick)
        ids.append(idx)
        cur = jnp.where(pick, neg, cur)

    chosen = functools.reduce(jnp.logical_or, picks)
    cf = jnp.where(chosen, 1.0, 0.0)
    before = (lax.broadcasted_iota(jnp.int32, (tm, tm), 0)
              < lax.broadcasted_iota(jnp.int32, (tm, tm), 1)).astype(BF16)
    rank = carry_ref[:, 0:1] + jnp.dot(cf.astype(BF16), before, preferred_element_type=F32)
    carry_ref[...] = carry_ref[...] + jnp.sum(cf, axis=1, keepdims=True)
    cnt_ref[...] = carry_ref[...].astype(jnp.int32)

    w_rows = [jnp.sum(jnp.where(p, s, 0.0), axis=0, keepdims=True) for p in picks]
    r_rows = [jnp.sum(jnp.where(p, rank, 0.0), axis=0, keepdims=True) for p in picks]
    den = functools.reduce(lambda a, b: a + b, w_rows)
    w_rows = [w / den * ROUTED_SCALE for w in w_rows]
    eidx_ref[...] = _rows_to_tile(ids, K_PAD, jnp.int32)
    rank_ref[...] = _rows_to_tile(r_rows, K_PAD, F32).astype(jnp.int32)
    wt_ref[...] = _rows_to_tile(w_rows, 128, F32).T


def _route_io(tm, first_block):
    tok_major = lambda i: (i, 0)
    choice_major = lambda i: (0, i)
    in_specs = [
        _mod_spec(tm, 3, 1, 0, first_block),
        _mod_spec(tm, 4, 1, 0, first_block),
        pl.BlockSpec((N_EXPERTS, D_MODEL), lambda i: (0, 0)),
        pl.BlockSpec((N_EXPERTS, 1), lambda i: (0, 0)),
    ]
    out_specs = [
        pl.BlockSpec((tm, HALF), tok_major),
        pl.BlockSpec((K_PAD, tm), choice_major),
        pl.BlockSpec((K_PAD, tm), choice_major),
        pl.BlockSpec((tm, 128), tok_major),
        pl.BlockSpec((N_EXPERTS, 128), lambda i: (0, 0)),
    ]
    out_shapes = [
        jax.ShapeDtypeStruct((N_TRUNK, HALF), jnp.uint32),
        jax.ShapeDtypeStruct((K_PAD, N_TRUNK), jnp.int32),
        jax.ShapeDtypeStruct((K_PAD, N_TRUNK), jnp.int32),
        jax.ShapeDtypeStruct((N_TRUNK, 128), F32),
        jax.ShapeDtypeStruct((N_EXPERTS, 128), jnp.int32),
    ]
    return in_specs, out_specs, out_shapes, [pltpu.VMEM((N_EXPERTS, 128), F32)]


def _finalize_body(cnt_ref, eidx_ref, rank_ref, pos_ref, te_ref, nt_ref):
    pos_ref[...] = rank_ref[...]

    def per_expert(e, carry):
        off, t = carry
        n_tile = lax.div(cnt_ref[e] + (MOE_TILE - 1), MOE_TILE)
        pos_ref[...] = pos_ref[...] + jnp.where(eidx_ref[...] == e, off, 0)

        def fill(j, c):
            te_ref[t + j] = e
            return c
        lax.fori_loop(0, n_tile, fill, 0)
        return off + n_tile * MOE_TILE, t + n_tile

    _, n_used = lax.fori_loop(0, N_EXPERTS, per_expert, (jnp.int32(0), jnp.int32(0)))
    nt_ref[0] = n_used

    def fill_tail(j, c):
        te_ref[j] = 0
        return c
    lax.fori_loop(n_used, N_TILE, fill_tail, 0)


def _finalize(counts, eidx, rank):
    smem = pl.BlockSpec(memory_space=pltpu.SMEM)
    full = pl.BlockSpec((K_PAD, N_TRUNK), lambda: (0, 0))
    return pl.pallas_call(
        _finalize_body,
        in_specs=[smem, full, full],
        out_specs=[full, smem, smem],
        out_shape=[
            jax.ShapeDtypeStruct((K_PAD, N_TRUNK), jnp.int32),
            jax.ShapeDtypeStruct((N_TILE,), jnp.int32),
            jax.ShapeDtypeStruct((1,), jnp.int32),
        ],
        compiler_params=pltpu.CompilerParams(vmem_limit_bytes=VMEM_LIMIT),
        name="route_finalize",
    )(counts, eidx, rank)


def _ffn(h, wgu, wd):
    gu = jnp.dot(h, wgu, preferred_element_type=F32)
    hid = _silu(gu[:, :EXPERT_FF]) * gu[:, EXPERT_FF:]
    return jnp.dot(hid.astype(BF16), wd, preferred_element_type=F32)


def _cast_ffn_weights(wg_ref, wu_ref, wd_ref, wgub_ref, wdb_ref):
    wgub_ref[:, :EXPERT_FF] = wg_ref[...].astype(BF16)
    wgub_ref[:, EXPERT_FF:] = wu_ref[...].astype(BF16)
    wdb_ref[...] = wd_ref[...].astype(BF16)


_FFN_WEIGHT_SCRATCH = [pltpu.VMEM((D_MODEL, 2 * EXPERT_FF), BF16), pltpu.VMEM((EXPERT_FF, D_MODEL), BF16)]


def _sc_mesh():
    from jax.experimental.pallas import tpu_sc as plsc
    return plsc.VectorSubcoreMesh(core_axis_name="c", subcore_axis_name="s",
                                  num_cores=SC_CORES, num_subcores=SC_SUBCORES)


def _sc_worker_id():
    return lax.axis_index("s") * SC_CORES + lax.axis_index("c")


def _sc_dispatch(hp, pos_flat):
    win = SC_SCATTER_WIN
    per_worker = N_TRUNK // SC_WORKERS

    def body(rows_hbm, idx_hbm, out_hbm, *scratch):
        idx_v, rows_v, sem = scratch[:TOP_K], scratch[TOP_K], scratch[TOP_K + 1]
        base = _sc_worker_id() * per_worker

        @pl.loop(0, per_worker // win)
        def _(j):
            off = base + j * win
            pltpu.sync_copy(rows_hbm.at[pl.ds(off, win)], rows_v)
            for k in range(TOP_K):
                pltpu.sync_copy(idx_hbm.at[pl.ds(k * N_TRUNK + off, win)], idx_v[k])
            copies = [pltpu.async_copy(rows_v, out_hbm.at[idx_v[k]], sem) for k in range(TOP_K)]
            for cp in copies:
                cp.wait()

    return pl.kernel(
        body, mesh=_sc_mesh(),
        out_type=jax.ShapeDtypeStruct((N_SLOT, HALF), jnp.uint32),
        scratch_types=[pltpu.VMEM((win,), jnp.int32)] * TOP_K
        + [pltpu.VMEM((win, HALF), jnp.uint32), pltpu.SemaphoreType.DMA],
        name="sc_dispatch",
    )(hp, pos_flat)


def _sc_return(ys, pos_flat):
    win = SC_GATHER_WIN
    per_worker = N_PAIR // SC_WORKERS

    def body(src_hbm, idx_hbm, out_hbm, idx_v, rows_v, sem):
        base = _sc_worker_id() * per_worker

        @pl.loop(0, per_worker // win)
        def _(j):
            off = base + j * win
            pltpu.sync_copy(idx_hbm.at[pl.ds(off, win)], idx_v)
            pltpu.async_copy(src_hbm.at[idx_v], rows_v, sem).wait()
            pltpu.sync_copy(rows_v, out_hbm.at[pl.ds(off, win)])

    return pl.kernel(
        body, mesh=_sc_mesh(),
        out_type=jax.ShapeDtypeStruct((N_PAIR, HALF), jnp.uint32),
        scratch_types=[pltpu.VMEM((win,), jnp.int32), pltpu.VMEM((win, HALF), jnp.uint32),
                       pltpu.SemaphoreType.DMA],
        name="sc_return",
    )(ys, pos_flat)


def _expert_body(te_ref, xs_ref, wg_ref, wu_ref, wd_ref, ys_ref, wgub_ref, wdb_ref):
    i = pl.program_id(0)

    @pl.when(jnp.logical_or(i == 0, te_ref[i] != te_ref[jnp.maximum(i - 1, 0)]))
    def _():
        _cast_ffn_weights(wg_ref, wu_ref, wd_ref, wgub_ref, wdb_ref)

    lo, hi = _unpack_bf16_pair(xs_ref[...])
    h = jnp.concatenate([lo.astype(BF16), hi.astype(BF16)], axis=1)
    y = _ffn(h, wgub_ref[...], wdb_ref[...])
    ys_ref[...] = _pack_bf16_pair(y[:, :HALF], y[:, HALF:])


def _expert_ffn(tile_expert, n_tiles, xs, layer, w_gate, w_up, w_down):
    tile = lambda i, te: (i, 0)
    ew = lambda shape: pl.BlockSpec((None, None) + shape, lambda i, te: (layer, te[i], 0, 0))
    return pl.pallas_call(
        _expert_body,
        grid_spec=pltpu.PrefetchScalarGridSpec(
            num_scalar_prefetch=1,
            grid=(n_tiles[0],),
            in_specs=[
                pl.BlockSpec((MOE_TILE, HALF), tile),
                ew((D_MODEL, EXPERT_FF)), ew((D_MODEL, EXPERT_FF)), ew((EXPERT_FF, D_MODEL)),
            ],
            out_specs=pl.BlockSpec((MOE_TILE, HALF), tile),
            scratch_shapes=_FFN_WEIGHT_SCRATCH,
        ),
        out_shape=jax.ShapeDtypeStruct((N_SLOT, HALF), jnp.uint32),
        compiler_params=pltpu.CompilerParams(
            dimension_semantics=("arbitrary",), vmem_limit_bytes=VMEM_LIMIT),
        name="expert_ffn",
    )(tile_expert, xs, w_gate, w_up, w_down)


def _combine_body(x_ref, sh_ref, sc_ref, g2_ref, yk_ref, wt_ref, sg_ref, su_ref, sd_ref, lng_ref, lnb_ref,
                  o_ref, sgub_ref, sdb_ref):
    @pl.when(pl.program_id(0) == 0)
    def _():
        _cast_ffn_weights(sg_ref, su_ref, sd_ref, sgub_ref, sdb_ref)

    x = x_ref[...]
    h = (x * (1.0 + sc_ref[...]) + sh_ref[...]).astype(BF16)
    y = _ffn(h, sgub_ref[...], sdb_ref[...])
    wt = wt_ref[...]
    lo_acc = jnp.zeros((x.shape[0], HALF), F32)
    hi_acc = jnp.zeros((x.shape[0], HALF), F32)
    for k in range(TOP_K):
        lo, hi = _unpack_bf16_pair(yk_ref[k])
        w = wt[:, k:k + 1]
        lo_acc = lo_acc + w * lo
        hi_acc = hi_acc + w * hi
    y = y + jnp.concatenate([lo_acc, hi_acc], axis=1)
    z = DEEPNORM_ALPHA * x + g2_ref[...] * y
    o_ref[...] = _layer_norm(z, lng_ref[...], lnb_ref[...])


def _combine(trunk, x, mod, yk, wt, layer, s_gate, s_up, s_down, ln_g, ln_b):
    tm = TOK_BLOCK
    first_block = trunk * (N_TRUNK // tm)
    row = lambda i: (i, 0)
    vec = pl.BlockSpec((1, D_MODEL), lambda i: (0, 0))
    sw = lambda shape: pl.BlockSpec((None,) + shape, lambda i: (layer, 0, 0))
    mod_spec = lambda col: _mod_spec(tm, col, 1, 0, first_block)
    return pl.pallas_call(
        _combine_body,
        grid=(N_TRUNK // tm,),
        in_specs=[
            pl.BlockSpec((tm, D_MODEL), row),
            mod_spec(3), mod_spec(4), mod_spec(5),
            pl.BlockSpec((TOP_K, tm, HALF), lambda i: (0, i, 0)),
            pl.BlockSpec((tm, 128), row),
            sw((D_MODEL, EXPERT_FF)), sw((D_MODEL, EXPERT_FF)), sw((EXPERT_FF, D_MODEL)),
            vec, vec,
        ],
        out_specs=pl.BlockSpec((tm, D_MODEL), lambda i: (i, 0)),
        out_shape=jax.ShapeDtypeStruct((N_TRUNK, D_MODEL), F32),
        scratch_shapes=_FFN_WEIGHT_SCRATCH,
        compiler_params=pltpu.CompilerParams(
            dimension_semantics=("arbitrary",), vmem_limit_bytes=VMEM_LIMIT),
        name="moe_combine_ln",
    )(x, mod, mod, mod, yk, wt, s_gate, s_up, s_down, ln_g, ln_b)


def _moe_dispatch(routing):
    hp, eidx, rank, wt, counts = routing
    pos, tile_expert, n_tiles = _finalize(counts[:, 0], eidx, rank)
    pos_flat = pos.reshape(K_PAD * N_TRUNK)
    return _sc_dispatch(hp, pos_flat), pos_flat, tile_expert, n_tiles, wt


def _moe_experts(plan, layer, w_gate, w_up, w_down):
    xs, pos_flat, tile_expert, n_tiles, _ = plan
    ys = _expert_ffn(tile_expert, n_tiles, xs, layer, w_gate, w_up, w_down)
    return _sc_return(ys, pos_flat).reshape(TOP_K, N_TRUNK, HALF)


def kernel(x_prompt, x_sample, state_ret, c, c_ctx, ada_w, ada_b, ln_g, ln_b, ret_w_in, ret_w_out, ret_decay, conv_w_in, conv_w, conv_w_out, moe_router, moe_bias, moe_w_gate, moe_w_up, moe_w_down, shared_w_gate, shared_w_up, shared_w_down):
    x = (x_prompt.reshape(N_CTX, D_MODEL), x_sample.reshape(N_LAT, D_MODEL))
    cond = jnp.concatenate(
        [c_ctx[None, :], c, jnp.zeros((N_COND - 1 - DEC_BATCH, D_MODEL), F32)], axis=0)
    mods = _ada_table(cond, ada_w, ada_b).reshape(DEPTH, N_COND, 1, 6 * D_MODEL)
    cos, sin = _rope_tables()
    router_wt = jnp.swapaxes(moe_router, 1, 2)

    states = None
    for i in range(DEPTH):
        j = i // 2
        mod = mods[i]
        lng = ln_g[i].reshape(2, 1, D_MODEL)
        lnb = ln_b[i].reshape(2, 1, D_MODEL)
        bias_col = moe_bias[i].reshape(N_EXPERTS, 1)
        if i % 2 == 0:
            w_in_bf = _to_bf16(ret_w_in, j)
        mixed, plans = [], []
        for trunk in range(2):
            if i % 2 == 0:
                if trunk == 0:
                    a, states = _retention_ctx(x[0], mod, w_in_bf, ret_decay[j], j, states)
                else:
                    a = _retention_lat(x[1], mod, w_in_bf, ret_decay[j], state_ret, j, cos, sin)
                xm, *routing = _ret_out(trunk, a, x[trunk], mod, ret_w_out, j, lng[0], lnb[0],
                                        router_wt[i], bias_col)
            else:
                xm, *routing = _conv_layer(trunk, x[trunk], mod, conv_w_in, conv_w_out, j, conv_w[j],
                                           lng[0], lnb[0], router_wt[i], bias_col)
            mixed.append(xm)
            plans.append(_moe_dispatch(routing))
        yks = [_moe_experts(plans[trunk], i, moe_w_gate, moe_w_up, moe_w_down) for trunk in range(2)]
        x = tuple(_combine(trunk, mixed[trunk], mod, yks[trunk], plans[trunk][4], i,
                           shared_w_gate, shared_w_up, shared_w_down, lng[1], lnb[1]) for trunk in range(2))

    y_prompt = x[0].reshape(BATCH, SEQ, D_MODEL)
    y_sample = x[1].reshape(DEC_BATCH, DEC_SEQ, D_MODEL)
    return y_prompt, y_sample, states
```

```python
import functools

import jax
import jax.numpy as jnp
from jax import lax
from jax.experimental import pallas as pl
from jax.experimental.pallas import tpu as pltpu

F32 = jnp.float32
BF16 = jnp.bfloat16

D_MODEL = 1024
BATCH = 32
SEQ = 256
DEPTH = 4
DEC_BATCH = 8
DEC_SEQ = 1024
GRID_W = 64
RET_HEADS = 4
RET_DK = D_MODEL // RET_HEADS
RET_DV = 2 * D_MODEL // RET_HEADS
ROPE_BASE = 10000.0
N_EXPERTS = 64
TOP_K = 6
N_GROUPS = 8
TOPK_GROUPS = 4
EXPERT_FF = 256
ROUTED_SCALE = 2.5
LN_EPS = 1e-5
DEEPNORM_ALPHA = (2.0 * DEPTH) ** 0.25

N_CTX = BATCH * SEQ
N_LAT = DEC_BATCH * DEC_SEQ
N_TOK = N_CTX + N_LAT
N_COND = 16
RET_CHUNK = 256
assert N_CTX == N_LAT
N_TRUNK = N_CTX
TOK_BLOCK = 256
MIX_BLOCK = 512
VMEM_LIMIT = 56 * 1024 * 1024

HALF = D_MODEL // 2
MOE_TILE = 896
N_PAIR = N_TRUNK * TOP_K
N_TILE = -(-(N_PAIR + N_EXPERTS * (MOE_TILE - 1)) // MOE_TILE)
N_SLOT = N_TILE * MOE_TILE
K_PAD = 8

SC_CORES = 2
SC_SUBCORES = 16
SC_WORKERS = SC_CORES * SC_SUBCORES
SC_SCATTER_WIN = 64
SC_GATHER_WIN = 128


def _cond_row(tok_block_idx, tok_block):
    t0 = tok_block_idx * tok_block
    return jnp.where(t0 < N_CTX, 0, 1 + (t0 - N_CTX) // DEC_SEQ)


def _silu(x):
    return x * jax.nn.sigmoid(x)


def _layer_norm(z, g, b):
    mu = jnp.mean(z, axis=-1, keepdims=True)
    zc = z - mu
    var = jnp.mean(zc * zc, axis=-1, keepdims=True)
    return zc * lax.rsqrt(var + LN_EPS) * g + b


def _ada_body(cond_ref, w_ref, b_ref, o_ref):
    s = _silu(cond_ref[...]).astype(BF16)
    o_ref[...] = jnp.dot(s, w_ref[...].astype(BF16), preferred_element_type=F32) + b_ref[...]


def _ada_table(cond, ada_w, ada_b):
    tn = 2048
    return pl.pallas_call(
        _ada_body,
        grid=(DEPTH, 6 * D_MODEL // tn),
        in_specs=[
            pl.BlockSpec((N_COND, D_MODEL), lambda l, j: (0, 0)),
            pl.BlockSpec((None, D_MODEL, tn), lambda l, j: (l, 0, j)),
            pl.BlockSpec((None, 1, tn), lambda l, j: (l, 0, j)),
        ],
        out_specs=pl.BlockSpec((None, N_COND, tn), lambda l, j: (l, 0, j)),
        out_shape=jax.ShapeDtypeStruct((DEPTH, N_COND, 6 * D_MODEL), F32),
        compiler_params=pltpu.CompilerParams(
            dimension_semantics=("arbitrary", "arbitrary"), vmem_limit_bytes=VMEM_LIMIT),
        name="ada_table",
    )(cond, ada_w, ada_b.reshape(DEPTH, 1, 6 * D_MODEL))


def _mod_spec(tok_block, col, grid_rank, tok_axis, first_block=0):
    def index_map(*idx):
        return (_cond_row(first_block + idx[tok_axis], tok_block), 0, col)
    del grid_rank
    return pl.BlockSpec((None, 1, D_MODEL), index_map)


def _to_bf16_body(w_ref, o_ref):
    o_ref[...] = w_ref[...].astype(BF16)


def _to_bf16(w_all, w_idx):
    k_dim, n_out = w_all.shape[1:]
    tn = 1024
    return pl.pallas_call(
        _to_bf16_body,
        grid=(n_out // tn,),
        in_specs=[pl.BlockSpec((None, k_dim, tn), lambda j: (w_idx, 0, j))],
        out_specs=pl.BlockSpec((k_dim, tn), lambda j: (0, j)),
        out_shape=jax.ShapeDtypeStruct((k_dim, n_out), BF16),
        compiler_params=pltpu.CompilerParams(
            dimension_semantics=("arbitrary",), vmem_limit_bytes=VMEM_LIMIT),
        name="weight_to_bf16",
    )(w_all)


RET_GROUP = 1024


def _head_proj_specs():
    return [
        pl.BlockSpec((D_MODEL, RET_DK), lambda t, h: (0, h)),
        pl.BlockSpec((D_MODEL, RET_DK), lambda t, h: (0, RET_HEADS + h)),
        pl.BlockSpec((D_MODEL, RET_DV), lambda t, h: (0, RET_HEADS + h)),
        pl.BlockSpec((D_MODEL, RET_DV), lambda t, h: (0, 2 * RET_HEADS + h)),
    ]


def _head_proj(x_ref, sh_ref, sc_ref, wq_ref, wk_ref, wv_ref, wg_ref):
    h = (x_ref[...] * (1.0 + sc_ref[...]) + sh_ref[...]).astype(BF16)
    dot = lambda w_ref: jnp.dot(h, w_ref[...], preferred_element_type=F32)
    return dot(wq_ref), dot(wk_ref), dot(wv_ref).astype(BF16), dot(wg_ref)

def _log_sigmoid(v):
    return jnp.minimum(v, 0.0) - jnp.log1p(jnp.exp(-jnp.abs(v)))


def _decay_tables(dec_ref, head):
    c = RET_CHUNK
    lgf = _log_sigmoid(jnp.full((c, c), dec_ref[0, head], F32))
    lgb = _log_sigmoid(jnp.full((c, c), dec_ref[1, head], F32))
    row = lax.broadcasted_iota(jnp.int32, (c, c), 0).astype(F32)
    col = lax.broadcasted_iota(jnp.int32, (c, c), 1).astype(F32)
    diff = row - col
    kscale = RET_DK ** -0.5
    intra = jnp.where(diff > 0, jnp.exp(lgf * diff),
                      jnp.where(diff < 0, jnp.exp(-lgb * diff), 2.0)) * kscale
    qdec_f = jnp.exp(lgf * (row + 1.0))
    qdec_b = jnp.exp(lgb * (c - row))
    kdec_f = jnp.exp(lgf * (c - 1.0 - row)) * kscale
    kdec_b = jnp.exp(lgb * row) * kscale
    cdec_f = jnp.exp(lgf * c)
    cdec_b = jnp.exp(lgb * c)
    return intra, qdec_f, qdec_b, kdec_f, kdec_b, cdec_f, cdec_b


def _head_norm_gate(o, g):
    mu = jnp.mean(o, axis=-1, keepdims=True)
    oc = o - mu
    var = jnp.mean(oc * oc, axis=-1, keepdims=True)
    on = oc * lax.rsqrt(var + LN_EPS)
    return (_silu(g.astype(F32)) * on).astype(BF16)


_NT = (((1,), (1,)), ((), ()))
_TN = (((0,), (0,)), ((), ()))


def _ret_ctx_body(dec_ref, x_ref, sh_ref, sc_ref, wq_ref, wk_ref, wv_ref, wg_ref, *rest):
    a_ref, st_ref, tab_ref = rest[-3:]
    head = pl.program_id(1)

    @pl.when(pl.program_id(0) == 0)
    def _():
        intra, _, _, kdec_f, kdec_b, _, _ = _decay_tables(dec_ref, head)
        tab_ref[head, 0] = intra
        tab_ref[head, 1] = kdec_f
        tab_ref[head, 2] = kdec_b

    q, k, v, g = _head_proj(x_ref, sh_ref, sc_ref, wq_ref, wk_ref, wv_ref, wg_ref)
    for s in range(RET_GROUP // SEQ):
        rows = slice(s * SEQ, (s + 1) * SEQ)
        scores = lax.dot_general(q[rows].astype(BF16), k[rows].astype(BF16), _NT, preferred_element_type=F32)
        p = (scores * tab_ref[head, 0]).astype(BF16)
        o = jnp.dot(p, v[rows], preferred_element_type=F32)
        a_ref[rows, :] = _head_norm_gate(o, g[rows])
        st_ref[s, 0] = lax.dot_general((k[rows] * tab_ref[head, 1]).astype(BF16), v[rows], _TN,
                                       preferred_element_type=F32)
        st_ref[s, 1] = lax.dot_general((k[rows] * tab_ref[head, 2]).astype(BF16), v[rows], _TN,
                                       preferred_element_type=F32)


def _retention_ctx(x, mod, w_in_bf, decay, ret_idx, states):
    assert SEQ == RET_CHUNK
    seqs = RET_GROUP // SEQ
    n_ret = (DEPTH + 1) // 2
    st = jax.ShapeDtypeStruct((BATCH, n_ret, 2, RET_HEADS, RET_DK, RET_DV), F32)
    st_spec = pl.BlockSpec((seqs, None, 2, None, RET_DK, RET_DV), lambda t, h: (t, ret_idx, 0, h, 0, 0))
    in_specs = [
        pl.BlockSpec(memory_space=pltpu.SMEM),
        pl.BlockSpec((RET_GROUP, D_MODEL), lambda t, h: (t, 0)),
        _mod_spec(RET_GROUP, 0, 2, 0), _mod_spec(RET_GROUP, 1, 2, 0),
    ] + _head_proj_specs()
    args = (decay, x, mod, mod, w_in_bf, w_in_bf, w_in_bf, w_in_bf)
    aliases = {}
    if states is not None:
        in_specs.append(pl.BlockSpec(memory_space=pl.ANY))
        aliases = {len(args): 1}
        args += (states,)
    return pl.pallas_call(
        _ret_ctx_body,
        grid=(N_CTX // RET_GROUP, RET_HEADS),
        in_specs=in_specs,
        out_specs=[pl.BlockSpec((RET_GROUP, RET_DV), lambda t, h: (t, h)), st_spec],
        out_shape=[jax.ShapeDtypeStruct((N_CTX, RET_HEADS * RET_DV), BF16), st],
        input_output_aliases=aliases,
        scratch_shapes=[pltpu.VMEM((RET_HEADS, 3, RET_CHUNK, RET_CHUNK), F32)],
        compiler_params=pltpu.CompilerParams(
            dimension_semantics=("arbitrary", "arbitrary"), vmem_limit_bytes=VMEM_LIMIT),
        name="retention_ctx",
    )(*args)


def _rope(x, cos, sin):
    halves = [pltpu.roll(x[:, s:s + 128], 64, axis=1) for s in (0, 128)]
    return x * cos + jnp.concatenate(halves, axis=1) * sin


def _ret_lat_body(dec_ref, x_ref, sh_ref, sc_ref, wq_ref, wk_ref, wv_ref, wg_ref,
                  s0f_ref, s0b_ref, cos_ref, sin_ref,
                  a_ref, qr_ref, kr_ref, v_ref, g_ref, o_ref, st_ref):
    head = pl.program_id(1)
    c = RET_CHUNK
    nc = DEC_SEQ // c
    intra, qdec_f, qdec_b, kdec_f, kdec_b, cdec_f, cdec_b = _decay_tables(dec_ref, head)
    cdec_f = jnp.concatenate([cdec_f, cdec_f], axis=1)
    cdec_b = jnp.concatenate([cdec_b, cdec_b], axis=1)

    q, k, v, g = _head_proj(x_ref, sh_ref, sc_ref, wq_ref, wk_ref, wv_ref, wg_ref)
    qr_ref[...] = _rope(q, cos_ref[...], sin_ref[...])
    kr_ref[...] = _rope(k, cos_ref[...], sin_ref[...])
    v_ref[...] = v
    g_ref[...] = g

    st_ref[...] = s0f_ref[...]
    for ci in range(nc):
        rows = pl.ds(ci * c, c)
        q = qr_ref[rows, :]
        k = kr_ref[rows, :]
        v = v_ref[rows, :]
        scores = lax.dot_general(q.astype(BF16), k.astype(BF16), _NT, preferred_element_type=F32)
        p = (scores * intra).astype(BF16)
        o = jnp.dot(p, v, preferred_element_type=F32)
        qd = (q * qdec_f).astype(BF16)
        o = o + jnp.dot(qd, st_ref[...].astype(BF16), preferred_element_type=F32)
        o_ref[rows, :] = o
        if ci + 1 < nc:
            kd = (k * kdec_f).astype(BF16)
            st_ref[...] = cdec_f * st_ref[...] + lax.dot_general(kd, v, _TN, preferred_element_type=F32)

    st_ref[...] = s0b_ref[...]
    for ci in reversed(range(nc)):
        rows = pl.ds(ci * c, c)
        qd = (qr_ref[rows, :] * qdec_b).astype(BF16)
        o = o_ref[rows, :] + jnp.dot(qd, st_ref[...].astype(BF16), preferred_element_type=F32)
        a_ref[rows, :] = _head_norm_gate(o, g_ref[rows, :])
        if ci > 0:
            kd = (kr_ref[rows, :] * kdec_b).astype(BF16)
            st_ref[...] = cdec_b * st_ref[...] + lax.dot_general(kd, v_ref[rows, :], _TN,
                                                                 preferred_element_type=F32)


def _rope_tables():
    half = RET_DK // 4
    freqs = ROPE_BASE ** (-jnp.arange(half, dtype=F32) / half)
    t = jnp.arange(DEC_SEQ)
    row = (t // GRID_W).astype(F32)
    col = (t % GRID_W).astype(F32)
    ang_r = row[:, None] * freqs[None, :]
    ang_c = col[:, None] * freqs[None, :]
    cos = jnp.concatenate([jnp.cos(ang_r)] * 2 + [jnp.cos(ang_c)] * 2, axis=1)
    sin = jnp.concatenate([-jnp.sin(ang_r), jnp.sin(ang_r), -jnp.sin(ang_c), jnp.sin(ang_c)], axis=1)
    return cos, sin


def _retention_lat(x, mod, w_in_bf, decay, state_ret, ret_idx, cos, sin):
    n = DEC_SEQ
    assert n == RET_GROUP
    s0_spec = lambda d: pl.BlockSpec((None, None, None, None, RET_DK, RET_DV),
                                     lambda b, h: (b, ret_idx, d, h, 0, 0))
    tab_spec = pl.BlockSpec((n, RET_DK), lambda b, h: (0, 0))
    first_block = N_CTX // n
    return pl.pallas_call(
        _ret_lat_body,
        grid=(DEC_BATCH, RET_HEADS),
        in_specs=[
            pl.BlockSpec(memory_space=pltpu.SMEM),
            pl.BlockSpec((n, D_MODEL), lambda b, h: (b, 0)),
            _mod_spec(n, 0, 2, 0, first_block), _mod_spec(n, 1, 2, 0, first_block),
        ] + _head_proj_specs() + [s0_spec(0), s0_spec(1), tab_spec, tab_spec],
        out_specs=pl.BlockSpec((n, RET_DV), lambda b, h: (b, h)),
        out_shape=jax.ShapeDtypeStruct((N_LAT, RET_HEADS * RET_DV), BF16),
        scratch_shapes=[
            pltpu.VMEM((n, RET_DK), F32), pltpu.VMEM((n, RET_DK), F32),
            pltpu.VMEM((n, RET_DV), BF16), pltpu.VMEM((n, RET_DV), F32),
            pltpu.VMEM((n, RET_DV), F32), pltpu.VMEM((RET_DK, RET_DV), F32),
        ],
        compiler_params=pltpu.CompilerParams(
            dimension_semantics=("arbitrary", "arbitrary"), vmem_limit_bytes=VMEM_LIMIT),
        name="retention_lat",
    )(decay, x, mod, mod, w_in_bf, w_in_bf, w_in_bf, w_in_bf, state_ret, state_ret, cos, sin)


N_ROUTE_IN = 4
N_ROUTE_OUT = 5


def _split_route_refs(rest, n_scratch):
    route_in = rest[:N_ROUTE_IN]
    o_ref = rest[N_ROUTE_IN]
    route_out = rest[N_ROUTE_IN + 1:N_ROUTE_IN + 1 + N_ROUTE_OUT]
    return route_in, o_ref, route_out, rest[len(rest) - n_scratch:]


def _ret_out_body(a_ref, x_ref, g1_ref, w_ref, lng_ref, lnb_ref, *rest):
    route_in, o_ref, route_out, (wbf_ref, carry_ref) = _split_route_refs(rest, 2)

    @pl.when(pl.program_id(0) == 0)
    def _():
        wbf_ref[...] = w_ref[...].astype(BF16)

    y = jnp.dot(a_ref[...], wbf_ref[...], preferred_element_type=F32)
    z = DEEPNORM_ALPHA * x_ref[...] + g1_ref[...] * y
    xn = _layer_norm(z, lng_ref[...], lnb_ref[...])
    o_ref[...] = xn
    _route_block(xn, *route_in, *route_out, carry_ref)


def _ret_out(trunk, a, x, mod, w_all, w_idx, ln_g, ln_b, router_wt, bias_col):
    tm = MIX_BLOCK
    k_dim = w_all.shape[1]
    first_block = trunk * (N_TRUNK // tm)
    row = lambda i: (i, 0)
    vec = pl.BlockSpec((1, D_MODEL), lambda i: (0, 0))
    r_in, r_out, r_shapes, r_scratch = _route_io(tm, first_block)
    return pl.pallas_call(
        _ret_out_body,
        grid=(N_TRUNK // tm,),
        in_specs=[
            pl.BlockSpec((tm, k_dim), row),
            pl.BlockSpec((tm, D_MODEL), row),
            _mod_spec(tm, 2, 1, 0, first_block),
            pl.BlockSpec((None, k_dim, D_MODEL), lambda i: (w_idx, 0, 0)),
            vec, vec,
        ] + r_in,
        out_specs=[pl.BlockSpec((tm, D_MODEL), row)] + r_out,
        out_shape=[jax.ShapeDtypeStruct((N_TRUNK, D_MODEL), F32)] + r_shapes,
        scratch_shapes=[pltpu.VMEM((k_dim, D_MODEL), BF16)] + r_scratch,
        compiler_params=pltpu.CompilerParams(
            dimension_semantics=("arbitrary",), vmem_limit_bytes=VMEM_LIMIT),
        name="ret_out_ln_route",
    )(a, x, mod, w_all, ln_g, ln_b, mod, mod, router_wt, bias_col)


def _conv_layer_body(seg, x_ref, sh_ref, sc_ref, g1_ref, win_ref, wout_ref, cw_ref, lng_ref, lnb_ref, *rest):
    route_in, o_ref, route_out, (winb_ref, woutb_ref, carry_ref) = _split_route_refs(rest, 3)
    tm = o_ref.shape[0]

    @pl.when(pl.program_id(0) == 0)
    def _():
        winb_ref[...] = win_ref[...].astype(BF16)
        woutb_ref[...] = wout_ref[...].astype(BF16)

    x = x_ref[...]
    h = (x * (1.0 + sc_ref[...]) + sh_ref[...]).astype(BF16)
    proj = jnp.dot(h, winb_ref[...], preferred_element_type=F32)
    bg, cg, xt = (proj[:, c * D_MODEL:(c + 1) * D_MODEL] for c in range(3))
    u = cg * xt
    pos = lax.broadcasted_iota(jnp.int32, (tm, D_MODEL), 0) & (seg - 1)
    u_prev = jnp.where(pos == 0, 0.0, pltpu.roll(u, 1, axis=0))
    u_next = jnp.where(pos == seg - 1, 0.0, pltpu.roll(u, tm - 1, axis=0))
    cu = u_prev * cw_ref[0:1, :] + u * cw_ref[1:2, :] + u_next * cw_ref[2:3, :]
    y = jnp.dot((bg * cu).astype(BF16), woutb_ref[...], preferred_element_type=F32)
    z = DEEPNORM_ALPHA * x + g1_ref[...] * y
    xn = _layer_norm(z, lng_ref[...], lnb_ref[...])
    o_ref[...] = xn
    _route_block(xn, *route_in, *route_out, carry_ref)


def _conv_layer(trunk, x, mod, w_in_all, w_out_all, w_idx, conv_w, ln_g, ln_b, router_wt, bias_col):
    tm = MIX_BLOCK
    first_block = trunk * (N_TRUNK // tm)
    seg = SEQ if trunk == 0 else GRID_W
    assert tm % seg == 0 and seg & (seg - 1) == 0
    row = lambda i: (i, 0)
    vec = pl.BlockSpec((1, D_MODEL), lambda i: (0, 0))
    once = pl.Buffered(1)
    r_in, r_out, r_shapes, r_scratch = _route_io(tm, first_block)
    mod_spec = lambda col: _mod_spec(tm, col, 1, 0, first_block)
    return pl.pallas_call(
        functools.partial(_conv_layer_body, seg),
        grid=(N_TRUNK // tm,),
        in_specs=[
            pl.BlockSpec((tm, D_MODEL), row),
            mod_spec(0), mod_spec(1), mod_spec(2),
            pl.BlockSpec((None, D_MODEL, 3 * D_MODEL), lambda i: (w_idx, 0, 0), pipeline_mode=once),
            pl.BlockSpec((None, D_MODEL, D_MODEL), lambda i: (w_idx, 0, 0), pipeline_mode=once),
            pl.BlockSpec((3, D_MODEL), lambda i: (0, 0)),
            vec, vec,
        ] + r_in,
        out_specs=[pl.BlockSpec((tm, D_MODEL), row)] + r_out,
        out_shape=[jax.ShapeDtypeStruct((N_TRUNK, D_MODEL), F32)] + r_shapes,
        scratch_shapes=[pltpu.VMEM((D_MODEL, 3 * D_MODEL), BF16), pltpu.VMEM((D_MODEL, D_MODEL), BF16)]
        + r_scratch,
        compiler_params=pltpu.CompilerParams(
            dimension_semantics=("arbitrary",), vmem_limit_bytes=VMEM_LIMIT),
        name="conv_layer_route",
    )(x, mod, mod, mod, w_in_all, w_out_all, conv_w, ln_g, ln_b, mod, mod, router_wt, bias_col)


def _split_bf16(v):
    hi = v.astype(BF16)
    lo = (v - hi.astype(F32)).astype(BF16)
    return hi, lo


def _pack_bf16_pair(lo_f32, hi_f32):
    lo = lax.bitcast_convert_type(lo_f32.astype(BF16).astype(F32), jnp.uint32) >> 16
    hi = lax.bitcast_convert_type(hi_f32.astype(BF16).astype(F32), jnp.uint32) & jnp.uint32(0xFFFF0000)
    return hi | lo


def _unpack_bf16_pair(u):
    lo = lax.bitcast_convert_type(u << 16, F32)
    hi = lax.bitcast_convert_type(u & jnp.uint32(0xFFFF0000), F32)
    return lo, hi


def _rows_to_tile(rows, n_sub, dtype):
    tm = rows[0].shape[1]
    sub = lax.broadcasted_iota(jnp.int32, (n_sub, tm), 0)
    out = jnp.zeros((n_sub, tm), dtype)
    for k, r in enumerate(rows):
        out = jnp.where(sub == k, jnp.broadcast_to(r.astype(dtype), (n_sub, tm)), out)
    return out


def _route_block(xn, sh_ref, sc_ref, rwt_ref, bias_ref,
                 hp_ref, eidx_ref, rank_ref, wt_ref, cnt_ref, carry_ref):
    tm = xn.shape[0]
    e = N_EXPERTS
    per = e // N_GROUPS
    neg = -jnp.inf

    @pl.when(pl.program_id(0) == 0)
    def _():
        carry_ref[...] = jnp.zeros_like(carry_ref)

    h = xn * (1.0 + sc_ref[...]) + sh_ref[...]
    hp_ref[...] = _pack_bf16_pair(h[:, :HALF], h[:, HALF:])
    h_hi, h_lo = _split_bf16(h)
    w_hi, w_lo = _split_bf16(rwt_ref[...])
    dot = lambda a, b: lax.dot_general(a, b, _NT, preferred_element_type=F32)
    logits = dot(w_hi, h_hi) + (dot(w_hi, h_lo) + dot(w_lo, h_hi))
    s = jax.nn.sigmoid(logits)
    sel = s + bias_ref[...]

    g3 = sel.reshape(N_GROUPS, per, tm)
    sub = lax.broadcasted_iota(jnp.int32, (N_GROUPS, per, tm), 1)
    m1 = jnp.max(g3, axis=1, keepdims=True)
    i1 = jnp.min(jnp.where(g3 == m1, sub, per), axis=1, keepdims=True)
    m2 = jnp.max(jnp.where(sub == i1, neg, g3), axis=1, keepdims=True)
    gs = (m1 + m2).reshape(N_GROUPS, tm)

    gi = lax.broadcasted_iota(jnp.int32, (N_GROUPS, tm), 0)
    gmask = jnp.zeros((N_GROUPS, tm), jnp.bool_)
    cur = gs
    for _ in range(TOPK_GROUPS):
        m = jnp.max(cur, axis=0, keepdims=True)
        idx = jnp.min(jnp.where(cur == m, gi, N_GROUPS), axis=0, keepdims=True)
        pick = gi == idx
        gmask = jnp.logical_or(gmask, pick)
        cur = jnp.where(pick, neg, cur)
    emask = jnp.broadcast_to(gmask.reshape(N_GROUPS, 1, tm), (N_GROUPS, per, tm)).reshape(e, tm)

    ei = lax.broadcasted_iota(jnp.int32, (e, tm), 0)
    picks, ids = [], []
    cur = jnp.where(emask, sel, neg)
    for _ in range(TOP_K):
        m = jnp.max(cur, axis=0, keepdims=True)
        idx = jnp.min(jnp.where(cur == m, ei, e), axis=0, keepdims=True)
        pick = ei == idx
        picks.append(pick)
        ids.append(idx)
        cur = jnp.where(pick, neg, cur)

    chosen = functools.reduce(jnp.logical_or, picks)
    cf = jnp.where(chosen, 1.0, 0.0)
    before = (lax.broadcasted_iota(jnp.int32, (tm, tm), 0)
              < lax.broadcasted_iota(jnp.int32, (tm, tm), 1)).astype(BF16)
    rank = carry_ref[:, 0:1] + jnp.dot(cf.astype(BF16), before, preferred_element_type=F32)
    carry_ref[...] = carry_ref[...] + jnp.sum(cf, axis=1, keepdims=True)
    cnt_ref[...] = carry_ref[...].astype(jnp.int32)

    w_rows = [jnp.sum(jnp.where(p, s, 0.0), axis=0, keepdims=True) for p in picks]
    r_rows = [jnp.sum(jnp.where(p, rank, 0.0), axis=0, keepdims=True) for p in picks]
    den = functools.reduce(lambda a, b: a + b, w_rows)
    w_rows = [w / den * ROUTED_SCALE for w in w_rows]
    eidx_ref[...] = _rows_to_tile(ids, K_PAD, jnp.int32)
    rank_ref[...] = _rows_to_tile(r_rows, K_PAD, F32).astype(jnp.int32)
    wt_ref[...] = _rows_to_tile(w_rows, 128, F32).T


def _route_io(tm, first_block):
    tok_major = lambda i: (i, 0)
    choice_major = lambda i: (0, i)
    in_specs = [
        _mod_spec(tm, 3, 1, 0, first_block),
        _mod_spec(tm, 4, 1, 0, first_block),
        pl.BlockSpec((N_EXPERTS, D_MODEL), lambda i: (0, 0)),
        pl.BlockSpec((N_EXPERTS, 1), lambda i: (0, 0)),
    ]
    out_specs = [
        pl.BlockSpec((tm, HALF), tok_major),
        pl.BlockSpec((K_PAD, tm), choice_major),
        pl.BlockSpec((K_PAD, tm), choice_major),
        pl.BlockSpec((tm, 128), tok_major),
        pl.BlockSpec((N_EXPERTS, 128), lambda i: (0, 0)),
    ]
    out_shapes = [
        jax.ShapeDtypeStruct((N_TRUNK, HALF), jnp.uint32),
        jax.ShapeDtypeStruct((K_PAD, N_TRUNK), jnp.int32),
        jax.ShapeDtypeStruct((K_PAD, N_TRUNK), jnp.int32),
        jax.ShapeDtypeStruct((N_TRUNK, 128), F32),
        jax.ShapeDtypeStruct((N_EXPERTS, 128), jnp.int32),
    ]
    return in_specs, out_specs, out_shapes, [pltpu.VMEM((N_EXPERTS, 128), F32)]


def _finalize_body(cnt_ref, eidx_ref, rank_ref, pos_ref, te_ref, nt_ref):
    pos_ref[...] = rank_ref[...]

    def per_expert(e, carry):
        off, t = carry
        n_tile = lax.div(cnt_ref[e] + (MOE_TILE - 1), MOE_TILE)
        pos_ref[...] = pos_ref[...] + jnp.where(eidx_ref[...] == e, off, 0)

        def fill(j, c):
            te_ref[t + j] = e
            return c
        lax.fori_loop(0, n_tile, fill, 0)
        return off + n_tile * MOE_TILE, t + n_tile

    _, n_used = lax.fori_loop(0, N_EXPERTS, per_expert, (jnp.int32(0), jnp.int32(0)))
    nt_ref[0] = n_used

    def fill_tail(j, c):
        te_ref[j] = 0
        return c
    lax.fori_loop(n_used, N_TILE, fill_tail, 0)


def _finalize(counts, eidx, rank):
    smem = pl.BlockSpec(memory_space=pltpu.SMEM)
    full = pl.BlockSpec((K_PAD, N_TRUNK), lambda: (0, 0))
    return pl.pallas_call(
        _finalize_body,
        in_specs=[smem, full, full],
        out_specs=[full, smem, smem],
        out_shape=[
            jax.ShapeDtypeStruct((K_PAD, N_TRUNK), jnp.int32),
            jax.ShapeDtypeStruct((N_TILE,), jnp.int32),
            jax.ShapeDtypeStruct((1,), jnp.int32),
        ],
        compiler_params=pltpu.CompilerParams(vmem_limit_bytes=VMEM_LIMIT),
        name="route_finalize",
    )(counts, eidx, rank)


def _ffn(h, wgu, wd):
    gu = jnp.dot(h, wgu, preferred_element_type=F32)
    hid = _silu(gu[:, :EXPERT_FF]) * gu[:, EXPERT_FF:]
    return jnp.dot(hid.astype(BF16), wd, preferred_element_type=F32)


def _cast_ffn_weights(wg_ref, wu_ref, wd_ref, wgub_ref, wdb_ref):
    wgub_ref[:, :EXPERT_FF] = wg_ref[...].astype(BF16)
    wgub_ref[:, EXPERT_FF:] = wu_ref[...].astype(BF16)
    wdb_ref[...] = wd_ref[...].astype(BF16)


_FFN_WEIGHT_SCRATCH = [pltpu.VMEM((D_MODEL, 2 * EXPERT_FF), BF16), pltpu.VMEM((EXPERT_FF, D_MODEL), BF16)]


def _sc_mesh():
    from jax.experimental.pallas import tpu_sc as plsc
    return plsc.VectorSubcoreMesh(core_axis_name="c", subcore_axis_name="s",
                                  num_cores=SC_CORES, num_subcores=SC_SUBCORES)


def _sc_worker_id():
    return lax.axis_index("s") * SC_CORES + lax.axis_index("c")


def _sc_dispatch(hp, pos_flat):
    win = SC_SCATTER_WIN
    per_worker = N_TRUNK // SC_WORKERS

    def body(rows_hbm, idx_hbm, out_hbm, *scratch):
        idx_v, rows_v, sem = scratch[:TOP_K], scratch[TOP_K], scratch[TOP_K + 1]
        base = _sc_worker_id() * per_worker

        @pl.loop(0, per_worker // win)
        def _(j):
            off = base + j * win
            pltpu.sync_copy(rows_hbm.at[pl.ds(off, win)], rows_v)
            for k in range(TOP_K):
                pltpu.sync_copy(idx_hbm.at[pl.ds(k * N_TRUNK + off, win)], idx_v[k])
            copies = [pltpu.async_copy(rows_v, out_hbm.at[idx_v[k]], sem) for k in range(TOP_K)]
            for cp in copies:
                cp.wait()

    return pl.kernel(
        body, mesh=_sc_mesh(),
        out_type=jax.ShapeDtypeStruct((N_SLOT, HALF), jnp.uint32),
        scratch_types=[pltpu.VMEM((win,), jnp.int32)] * TOP_K
        + [pltpu.VMEM((win, HALF), jnp.uint32), pltpu.SemaphoreType.DMA],
        name="sc_dispatch",
    )(hp, pos_flat)


def _sc_return(ys, pos_flat):
    win = SC_GATHER_WIN
    per_worker = N_PAIR // SC_WORKERS

    def body(src_hbm, idx_hbm, out_hbm, idx_v, rows_v, sem):
        base = _sc_worker_id() * per_worker

        @pl.loop(0, per_worker // win)
        def _(j):
            off = base + j * win
            pltpu.sync_copy(idx_hbm.at[pl.ds(off, win)], idx_v)
            pltpu.async_copy(src_hbm.at[idx_v], rows_v, sem).wait()
            pltpu.sync_copy(rows_v, out_hbm.at[pl.ds(off, win)])

    return pl.kernel(
        body, mesh=_sc_mesh(),
        out_type=jax.ShapeDtypeStruct((N_PAIR, HALF), jnp.uint32),
        scratch_types=[pltpu.VMEM((win,), jnp.int32), pltpu.VMEM((win, HALF), jnp.uint32),
                       pltpu.SemaphoreType.DMA],
        name="sc_return",
    )(ys, pos_flat)


def _tile_plan_body(te0_ref, nt0_ref, te1_ref, nt1_ref, exp_ref, trunk_ref, blk0_ref, blk1_ref, n_ref):
    n0, n1 = nt0_ref[0], nt1_ref[0]

    def step(s, carry):
        p0, p1 = carry
        take0 = jnp.logical_and(p0 < n0, jnp.logical_or(p1 >= n1, te0_ref[jnp.minimum(p0, N_TILE - 1)]
                                                        <= te1_ref[jnp.minimum(p1, N_TILE - 1)]))
        exp_ref[s] = jnp.where(take0, te0_ref[jnp.minimum(p0, N_TILE - 1)], te1_ref[jnp.minimum(p1, N_TILE - 1)])
        trunk_ref[s] = jnp.where(take0, 0, 1)
        blk0_ref[s] = jnp.where(take0, p0, jnp.maximum(p0 - 1, 0))
        blk1_ref[s] = jnp.where(take0, jnp.maximum(p1 - 1, 0), p1)
        return p0 + jnp.where(take0, 1, 0), p1 + jnp.where(take0, 0, 1)

    lax.fori_loop(0, n0 + n1, step, (jnp.int32(0), jnp.int32(0)))
    n_ref[0] = n0 + n1

    def fill_tail(s, c):
        exp_ref[s] = 0
        trunk_ref[s] = 0
        blk0_ref[s] = 0
        blk1_ref[s] = 0
        return c
    lax.fori_loop(n0 + n1, 2 * N_TILE, fill_tail, 0)


def _tile_plan(te0, nt0, te1, nt1):
    smem = pl.BlockSpec(memory_space=pltpu.SMEM)
    steps = jax.ShapeDtypeStruct((2 * N_TILE,), jnp.int32)
    return pl.pallas_call(
        _tile_plan_body,
        in_specs=[smem] * 4,
        out_specs=[smem] * 5,
        out_shape=[steps, steps, steps, steps, jax.ShapeDtypeStruct((1,), jnp.int32)],
        name="tile_plan",
    )(te0, nt0, te1, nt1)


def _expert_body(exp_ref, trunk_ref, blk0_ref, blk1_ref, xs0_ref, xs1_ref, wg_ref, wu_ref, wd_ref,
                 ys0_ref, ys1_ref, wgub_ref, wdb_ref):
    i = pl.program_id(0)

    @pl.when(jnp.logical_or(i == 0, exp_ref[i] != exp_ref[jnp.maximum(i - 1, 0)]))
    def _():
        _cast_ffn_weights(wg_ref, wu_ref, wd_ref, wgub_ref, wdb_ref)

    is_first = trunk_ref[i] == 0
    lo, hi = _unpack_bf16_pair(jnp.where(is_first, xs0_ref[...], xs1_ref[...]))
    h = jnp.concatenate([lo.astype(BF16), hi.astype(BF16)], axis=1)
    y = _ffn(h, wgub_ref[...], wdb_ref[...])
    out = _pack_bf16_pair(y[:, :HALF], y[:, HALF:])

    @pl.when(is_first)
    def _():
        ys0_ref[...] = out

    @pl.when(jnp.logical_not(is_first))
    def _():
        ys1_ref[...] = out


def _expert_ffn(plan, xs0, xs1, layer, w_gate, w_up, w_down):
    exp, trunk, blk0, blk1, n_steps = plan
    tile0 = lambda i, exp, trunk, blk0, blk1: (blk0[i], 0)
    tile1 = lambda i, exp, trunk, blk0, blk1: (blk1[i], 0)
    ew = lambda shape: pl.BlockSpec((None, None) + shape,
                                    lambda i, exp, trunk, blk0, blk1: (layer, exp[i], 0, 0))
    slots = jax.ShapeDtypeStruct((N_SLOT, HALF), jnp.uint32)
    return pl.pallas_call(
        _expert_body,
        grid_spec=pltpu.PrefetchScalarGridSpec(
            num_scalar_prefetch=4,
            grid=(n_steps[0],),
            in_specs=[
                pl.BlockSpec((MOE_TILE, HALF), tile0), pl.BlockSpec((MOE_TILE, HALF), tile1),
                ew((D_MODEL, EXPERT_FF)), ew((D_MODEL, EXPERT_FF)), ew((EXPERT_FF, D_MODEL)),
            ],
            out_specs=[pl.BlockSpec((MOE_TILE, HALF), tile0), pl.BlockSpec((MOE_TILE, HALF), tile1)],
            scratch_shapes=_FFN_WEIGHT_SCRATCH,
        ),
        out_shape=[slots, slots],
        compiler_params=pltpu.CompilerParams(
            dimension_semantics=("arbitrary",), vmem_limit_bytes=VMEM_LIMIT),
        name="expert_ffn",
    )(exp, trunk, blk0, blk1, xs0, xs1, w_gate, w_up, w_down)


def _combine_body(x_ref, sh_ref, sc_ref, g2_ref, yk_ref, wt_ref, sg_ref, su_ref, sd_ref, lng_ref, lnb_ref,
                  o_ref, sgub_ref, sdb_ref):
    @pl.when(pl.program_id(0) == 0)
    def _():
        _cast_ffn_weights(sg_ref, su_ref, sd_ref, sgub_ref, sdb_ref)

    x = x_ref[...]
    h = (x * (1.0 + sc_ref[...]) + sh_ref[...]).astype(BF16)
    y = _ffn(h, sgub_ref[...], sdb_ref[...])
    wt = wt_ref[...]
    lo_acc = jnp.zeros((x.shape[0], HALF), F32)
    hi_acc = jnp.zeros((x.shape[0], HALF), F32)
    for k in range(TOP_K):
        lo, hi = _unpack_bf16_pair(yk_ref[k])
        w = wt[:, k:k + 1]
        lo_acc = lo_acc + w * lo
        hi_acc = hi_acc + w * hi
    y = y + jnp.concatenate([lo_acc, hi_acc], axis=1)
    z = DEEPNORM_ALPHA * x + g2_ref[...] * y
    o_ref[...] = _layer_norm(z, lng_ref[...], lnb_ref[...])


def _combine(trunk, x, mod, yk, wt, layer, s_gate, s_up, s_down, ln_g, ln_b):
    tm = TOK_BLOCK
    first_block = trunk * (N_TRUNK // tm)
    row = lambda i: (i, 0)
    vec = pl.BlockSpec((1, D_MODEL), lambda i: (0, 0))
    sw = lambda shape: pl.BlockSpec((None,) + shape, lambda i: (layer, 0, 0))
    mod_spec = lambda col: _mod_spec(tm, col, 1, 0, first_block)
    return pl.pallas_call(
        _combine_body,
        grid=(N_TRUNK // tm,),
        in_specs=[
            pl.BlockSpec((tm, D_MODEL), row),
            mod_spec(3), mod_spec(4), mod_spec(5),
            pl.BlockSpec((TOP_K, tm, HALF), lambda i: (0, i, 0)),
            pl.BlockSpec((tm, 128), row),
            sw((D_MODEL, EXPERT_FF)), sw((D_MODEL, EXPERT_FF)), sw((EXPERT_FF, D_MODEL)),
            vec, vec,
        ],
        out_specs=pl.BlockSpec((tm, D_MODEL), lambda i: (i, 0)),
        out_shape=jax.ShapeDtypeStruct((N_TRUNK, D_MODEL), F32),
        scratch_shapes=_FFN_WEIGHT_SCRATCH,
        compiler_params=pltpu.CompilerParams(
            dimension_semantics=("arbitrary",), vmem_limit_bytes=VMEM_LIMIT),
        name="moe_combine_ln",
    )(x, mod, mod, mod, yk, wt, s_gate, s_up, s_down, ln_g, ln_b)


def _moe_dispatch(routing):
    hp, eidx, rank, wt, counts = routing
    pos, tile_expert, n_tiles = _finalize(counts[:, 0], eidx, rank)
    pos_flat = pos.reshape(K_PAD * N_TRUNK)
    return _sc_dispatch(hp, pos_flat), pos_flat, tile_expert, n_tiles, wt


def _moe_experts(plans, layer, w_gate, w_up, w_down):
    (xs0, pos0, te0, nt0, _), (xs1, pos1, te1, nt1, _) = plans
    ys = _expert_ffn(_tile_plan(te0, nt0, te1, nt1), xs0, xs1, layer, w_gate, w_up, w_down)
    return [_sc_return(y, pos).reshape(TOP_K, N_TRUNK, HALF) for y, pos in zip(ys, (pos0, pos1))]


def kernel(x_prompt, x_sample, state_ret, c, c_ctx, ada_w, ada_b, ln_g, ln_b, ret_w_in, ret_w_out, ret_decay, conv_w_in, conv_w, conv_w_out, moe_router, moe_bias, moe_w_gate, moe_w_up, moe_w_down, shared_w_gate, shared_w_up, shared_w_down):
    x = (x_prompt.reshape(N_CTX, D_MODEL), x_sample.reshape(N_LAT, D_MODEL))
    cond = jnp.concatenate(
        [c_ctx[None, :], c, jnp.zeros((N_COND - 1 - DEC_BATCH, D_MODEL), F32)], axis=0)
    mods = _ada_table(cond, ada_w, ada_b).reshape(DEPTH, N_COND, 1, 6 * D_MODEL)
    cos, sin = _rope_tables()
    router_wt = jnp.swapaxes(moe_router, 1, 2)

    states = None
    for i in range(DEPTH):
        j = i // 2
        mod = mods[i]
        lng = ln_g[i].reshape(2, 1, D_MODEL)
        lnb = ln_b[i].reshape(2, 1, D_MODEL)
        bias_col = moe_bias[i].reshape(N_EXPERTS, 1)
        if i % 2 == 0:
            w_in_bf = _to_bf16(ret_w_in, j)
        mixed, plans = [], []
        for trunk in range(2):
            if i % 2 == 0:
                if trunk == 0:
                    a, states = _retention_ctx(x[0], mod, w_in_bf, ret_decay[j], j, states)
                else:
                    a = _retention_lat(x[1], mod, w_in_bf, ret_decay[j], state_ret, j, cos, sin)
                xm, *routing = _ret_out(trunk, a, x[trunk], mod, ret_w_out, j, lng[0], lnb[0],
                                        router_wt[i], bias_col)
            else:
                xm, *routing = _conv_layer(trunk, x[trunk], mod, conv_w_in, conv_w_out, j, conv_w[j],
                                           lng[0], lnb[0], router_wt[i], bias_col)
            mixed.append(xm)
            plans.append(_moe_dispatch(routing))
        yks = _moe_experts(plans, i, moe_w_gate, moe_w_up, moe_w_down)
        x = tuple(_combine(trunk, mixed[trunk], mod, yks[trunk], plans[trunk][4], i,
                           shared_w_gate, shared_w_up, shared_w_down, lng[1], lnb[1]) for trunk in range(2))

    y_prompt = x[0].reshape(BATCH, SEQ, D_MODEL)
    y_sample = x[1].reshape(DEC_BATCH, DEC_SEQ, D_MODEL)
    return y_prompt, y_sample, states
```

```python
import functools

import jax
import jax.numpy as jnp
from jax import lax
from jax.experimental import pallas as pl
from jax.experimental.pallas import tpu as pltpu

F32 = jnp.float32
BF16 = jnp.bfloat16

D_MODEL = 1024
BATCH = 32
SEQ = 256
DEPTH = 4
DEC_BATCH = 8
DEC_SEQ = 1024
GRID_W = 64
RET_HEADS = 4
RET_DK = D_MODEL // RET_HEADS
RET_DV = 2 * D_MODEL // RET_HEADS
ROPE_BASE = 10000.0
N_EXPERTS = 64
TOP_K = 6
N_GROUPS = 8
TOPK_GROUPS = 4
EXPERT_FF = 256
ROUTED_SCALE = 2.5
LN_EPS = 1e-5
DEEPNORM_ALPHA = (2.0 * DEPTH) ** 0.25

N_CTX = BATCH * SEQ
N_LAT = DEC_BATCH * DEC_SEQ
N_TOK = N_CTX + N_LAT
N_COND = 16
RET_CHUNK = 256
assert N_CTX == N_LAT
N_TRUNK = N_CTX
TOK_BLOCK = 256
MIX_BLOCK = 512
VMEM_LIMIT = 56 * 1024 * 1024
RES_DTYPE = BF16

HALF = D_MODEL // 2
MOE_TILE = 896
N_PAIR = N_TRUNK * TOP_K
N_TILE = -(-(N_PAIR + N_EXPERTS * (MOE_TILE - 1)) // MOE_TILE)
N_SLOT = N_TILE * MOE_TILE
K_PAD = 8

SC_CORES = 2
SC_SUBCORES = 16
SC_WORKERS = SC_CORES * SC_SUBCORES
SC_SCATTER_WIN = 64
SC_GATHER_WIN = 128


def _cond_row(tok_block_idx, tok_block):
    t0 = tok_block_idx * tok_block
    return jnp.where(t0 < N_CTX, 0, 1 + (t0 - N_CTX) // DEC_SEQ)


def _silu(x):
    return x * jax.nn.sigmoid(x)


def _layer_norm(z, g, b):
    mu = jnp.mean(z, axis=-1, keepdims=True)
    zc = z - mu
    var = jnp.mean(zc * zc, axis=-1, keepdims=True)
    return zc * lax.rsqrt(var + LN_EPS) * g + b


def _ada_body(cond_ref, w_ref, b_ref, o_ref):
    s = _silu(cond_ref[...]).astype(BF16)
    o_ref[...] = jnp.dot(s, w_ref[...].astype(BF16), preferred_element_type=F32) + b_ref[...]


def _ada_table(cond, ada_w, ada_b):
    tn = 2048
    return pl.pallas_call(
        _ada_body,
        grid=(DEPTH, 6 * D_MODEL // tn),
        in_specs=[
            pl.BlockSpec((N_COND, D_MODEL), lambda l, j: (0, 0)),
            pl.BlockSpec((None, D_MODEL, tn), lambda l, j: (l, 0, j)),
            pl.BlockSpec((None, 1, tn), lambda l, j: (l, 0, j)),
        ],
        out_specs=pl.BlockSpec((None, N_COND, tn), lambda l, j: (l, 0, j)),
        out_shape=jax.ShapeDtypeStruct((DEPTH, N_COND, 6 * D_MODEL), F32),
        compiler_params=pltpu.CompilerParams(
            dimension_semantics=("arbitrary", "arbitrary"), vmem_limit_bytes=VMEM_LIMIT),
        name="ada_table",
    )(cond, ada_w, ada_b.reshape(DEPTH, 1, 6 * D_MODEL))


def _mod_spec(tok_block, col, grid_rank, tok_axis, first_block=0):
    def index_map(*idx):
        return (_cond_row(first_block + idx[tok_axis], tok_block), 0, col)
    del grid_rank
    return pl.BlockSpec((None, 1, D_MODEL), index_map)


def _to_bf16_body(w_ref, o_ref):
    o_ref[...] = w_ref[...].astype(BF16)


def _to_bf16(w_all, w_idx):
    k_dim, n_out = w_all.shape[1:]
    tn = 1024
    return pl.pallas_call(
        _to_bf16_body,
        grid=(n_out // tn,),
        in_specs=[pl.BlockSpec((None, k_dim, tn), lambda j: (w_idx, 0, j))],
        out_specs=pl.BlockSpec((k_dim, tn), lambda j: (0, j)),
        out_shape=jax.ShapeDtypeStruct((k_dim, n_out), BF16),
        compiler_params=pltpu.CompilerParams(
            dimension_semantics=("arbitrary",), vmem_limit_bytes=VMEM_LIMIT),
        name="weight_to_bf16",
    )(w_all)


RET_GROUP = 1024


def _head_proj_specs():
    return [
        pl.BlockSpec((D_MODEL, RET_DK), lambda t, h: (0, h)),
        pl.BlockSpec((D_MODEL, RET_DK), lambda t, h: (0, RET_HEADS + h)),
        pl.BlockSpec((D_MODEL, RET_DV), lambda t, h: (0, RET_HEADS + h)),
        pl.BlockSpec((D_MODEL, RET_DV), lambda t, h: (0, 2 * RET_HEADS + h)),
    ]


def _head_proj(x_ref, sh_ref, sc_ref, wq_ref, wk_ref, wv_ref, wg_ref):
    h = (x_ref[...].astype(F32) * (1.0 + sc_ref[...]) + sh_ref[...]).astype(BF16)
    dot = lambda w_ref: jnp.dot(h, w_ref[...], preferred_element_type=F32)
    return dot(wq_ref), dot(wk_ref), dot(wv_ref).astype(BF16), dot(wg_ref)

def _log_sigmoid(v):
    return jnp.minimum(v, 0.0) - jnp.log1p(jnp.exp(-jnp.abs(v)))


def _decay_tables(dec_ref, head):
    c = RET_CHUNK
    lgf = _log_sigmoid(jnp.full((c, c), dec_ref[0, head], F32))
    lgb = _log_sigmoid(jnp.full((c, c), dec_ref[1, head], F32))
    row = lax.broadcasted_iota(jnp.int32, (c, c), 0).astype(F32)
    col = lax.broadcasted_iota(jnp.int32, (c, c), 1).astype(F32)
    diff = row - col
    kscale = RET_DK ** -0.5
    intra = jnp.where(diff > 0, jnp.exp(lgf * diff),
                      jnp.where(diff < 0, jnp.exp(-lgb * diff), 2.0)) * kscale
    qdec_f = jnp.exp(lgf * (row + 1.0))
    qdec_b = jnp.exp(lgb * (c - row))
    kdec_f = jnp.exp(lgf * (c - 1.0 - row)) * kscale
    kdec_b = jnp.exp(lgb * row) * kscale
    cdec_f = jnp.exp(lgf * c)
    cdec_b = jnp.exp(lgb * c)
    return intra, qdec_f, qdec_b, kdec_f, kdec_b, cdec_f, cdec_b


def _head_norm_gate(o, g):
    mu = jnp.mean(o, axis=-1, keepdims=True)
    oc = o - mu
    var = jnp.mean(oc * oc, axis=-1, keepdims=True)
    on = oc * lax.rsqrt(var + LN_EPS)
    return (_silu(g.astype(F32)) * on).astype(BF16)


_NT = (((1,), (1,)), ((), ()))
_TN = (((0,), (0,)), ((), ()))


def _ret_ctx_body(dec_ref, x_ref, sh_ref, sc_ref, wq_ref, wk_ref, wv_ref, wg_ref, *rest):
    a_ref, st_ref, tab_ref = rest[-3:]
    head = pl.program_id(1)

    @pl.when(pl.program_id(0) == 0)
    def _():
        intra, _, _, kdec_f, kdec_b, _, _ = _decay_tables(dec_ref, head)
        tab_ref[head, 0] = intra
        tab_ref[head, 1] = kdec_f
        tab_ref[head, 2] = kdec_b

    q, k, v, g = _head_proj(x_ref, sh_ref, sc_ref, wq_ref, wk_ref, wv_ref, wg_ref)
    for s in range(RET_GROUP // SEQ):
        rows = slice(s * SEQ, (s + 1) * SEQ)
        scores = lax.dot_general(q[rows].astype(BF16), k[rows].astype(BF16), _NT, preferred_element_type=F32)
        p = (scores * tab_ref[head, 0]).astype(BF16)
        o = jnp.dot(p, v[rows], preferred_element_type=F32)
        a_ref[rows, :] = _head_norm_gate(o, g[rows])
        st_ref[s, 0] = lax.dot_general((k[rows] * tab_ref[head, 1]).astype(BF16), v[rows], _TN,
                                       preferred_element_type=F32)
        st_ref[s, 1] = lax.dot_general((k[rows] * tab_ref[head, 2]).astype(BF16), v[rows], _TN,
                                       preferred_element_type=F32)


def _retention_ctx(x, mod, w_in_bf, decay, ret_idx, states):
    assert SEQ == RET_CHUNK
    seqs = RET_GROUP // SEQ
    n_ret = (DEPTH + 1) // 2
    st = jax.ShapeDtypeStruct((BATCH, n_ret, 2, RET_HEADS, RET_DK, RET_DV), F32)
    st_spec = pl.BlockSpec((seqs, None, 2, None, RET_DK, RET_DV), lambda t, h: (t, ret_idx, 0, h, 0, 0))
    in_specs = [
        pl.BlockSpec(memory_space=pltpu.SMEM),
        pl.BlockSpec((RET_GROUP, D_MODEL), lambda t, h: (t, 0)),
        _mod_spec(RET_GROUP, 0, 2, 0), _mod_spec(RET_GROUP, 1, 2, 0),
    ] + _head_proj_specs()
    args = (decay, x, mod, mod, w_in_bf, w_in_bf, w_in_bf, w_in_bf)
    aliases = {}
    if states is not None:
        in_specs.append(pl.BlockSpec(memory_space=pl.ANY))
        aliases = {len(args): 1}
        args += (states,)
    return pl.pallas_call(
        _ret_ctx_body,
        grid=(N_CTX // RET_GROUP, RET_HEADS),
        in_specs=in_specs,
        out_specs=[pl.BlockSpec((RET_GROUP, RET_DV), lambda t, h: (t, h)), st_spec],
        out_shape=[jax.ShapeDtypeStruct((N_CTX, RET_HEADS * RET_DV), BF16), st],
        input_output_aliases=aliases,
        scratch_shapes=[pltpu.VMEM((RET_HEADS, 3, RET_CHUNK, RET_CHUNK), F32)],
        compiler_params=pltpu.CompilerParams(
            dimension_semantics=("arbitrary", "arbitrary"), vmem_limit_bytes=VMEM_LIMIT),
        name="retention_ctx",
    )(*args)


def _rope(x, cos, sin):
    halves = [pltpu.roll(x[:, s:s + 128], 64, axis=1) for s in (0, 128)]
    return x * cos + jnp.concatenate(halves, axis=1) * sin


def _ret_lat_body(dec_ref, x_ref, sh_ref, sc_ref, wq_ref, wk_ref, wv_ref, wg_ref,
                  s0f_ref, s0b_ref, cos_ref, sin_ref,
                  a_ref, qr_ref, kr_ref, v_ref, g_ref, o_ref, st_ref):
    head = pl.program_id(1)
    c = RET_CHUNK
    nc = DEC_SEQ // c
    intra, qdec_f, qdec_b, kdec_f, kdec_b, cdec_f, cdec_b = _decay_tables(dec_ref, head)
    cdec_f = jnp.concatenate([cdec_f, cdec_f], axis=1)
    cdec_b = jnp.concatenate([cdec_b, cdec_b], axis=1)

    q, k, v, g = _head_proj(x_ref, sh_ref, sc_ref, wq_ref, wk_ref, wv_ref, wg_ref)
    qr_ref[...] = _rope(q, cos_ref[...], sin_ref[...])
    kr_ref[...] = _rope(k, cos_ref[...], sin_ref[...])
    v_ref[...] = v
    g_ref[...] = g

    st_ref[...] = s0f_ref[...]
    for ci in range(nc):
        rows = pl.ds(ci * c, c)
        q = qr_ref[rows, :]
        k = kr_ref[rows, :]
        v = v_ref[rows, :]
        scores = lax.dot_general(q.astype(BF16), k.astype(BF16), _NT, preferred_element_type=F32)
        p = (scores * intra).astype(BF16)
        o = jnp.dot(p, v, preferred_element_type=F32)
        qd = (q * qdec_f).astype(BF16)
        o = o + jnp.dot(qd, st_ref[...].astype(BF16), preferred_element_type=F32)
        o_ref[rows, :] = o
        if ci + 1 < nc:
            kd = (k * kdec_f).astype(BF16)
            st_ref[...] = cdec_f * st_ref[...] + lax.dot_general(kd, v, _TN, preferred_element_type=F32)

    st_ref[...] = s0b_ref[...]
    for ci in reversed(range(nc)):
        rows = pl.ds(ci * c, c)
        qd = (qr_ref[rows, :] * qdec_b).astype(BF16)
        o = o_ref[rows, :] + jnp.dot(qd, st_ref[...].astype(BF16), preferred_element_type=F32)
        a_ref[rows, :] = _head_norm_gate(o, g_ref[rows, :])
        if ci > 0:
            kd = (kr_ref[rows, :] * kdec_b).astype(BF16)
            st_ref[...] = cdec_b * st_ref[...] + lax.dot_general(kd, v_ref[rows, :], _TN,
                                                                 preferred_element_type=F32)


def _rope_tables():
    half = RET_DK // 4
    freqs = ROPE_BASE ** (-jnp.arange(half, dtype=F32) / half)
    t = jnp.arange(DEC_SEQ)
    row = (t // GRID_W).astype(F32)
    col = (t % GRID_W).astype(F32)
    ang_r = row[:, None] * freqs[None, :]
    ang_c = col[:, None] * freqs[None, :]
    cos = jnp.concatenate([jnp.cos(ang_r)] * 2 + [jnp.cos(ang_c)] * 2, axis=1)
    sin = jnp.concatenate([-jnp.sin(ang_r), jnp.sin(ang_r), -jnp.sin(ang_c), jnp.sin(ang_c)], axis=1)
    return cos, sin


def _retention_lat(x, mod, w_in_bf, decay, state_ret, ret_idx, cos, sin):
    n = DEC_SEQ
    assert n == RET_GROUP
    s0_spec = lambda d: pl.BlockSpec((None, None, None, None, RET_DK, RET_DV),
                                     lambda b, h: (b, ret_idx, d, h, 0, 0))
    tab_spec = pl.BlockSpec((n, RET_DK), lambda b, h: (0, 0))
    first_block = N_CTX // n
    return pl.pallas_call(
        _ret_lat_body,
        grid=(DEC_BATCH, RET_HEADS),
        in_specs=[
            pl.BlockSpec(memory_space=pltpu.SMEM),
            pl.BlockSpec((n, D_MODEL), lambda b, h: (b, 0)),
            _mod_spec(n, 0, 2, 0, first_block), _mod_spec(n, 1, 2, 0, first_block),
        ] + _head_proj_specs() + [s0_spec(0), s0_spec(1), tab_spec, tab_spec],
        out_specs=pl.BlockSpec((n, RET_DV), lambda b, h: (b, h)),
        out_shape=jax.ShapeDtypeStruct((N_LAT, RET_HEADS * RET_DV), BF16),
        scratch_shapes=[
            pltpu.VMEM((n, RET_DK), F32), pltpu.VMEM((n, RET_DK), F32),
            pltpu.VMEM((n, RET_DV), BF16), pltpu.VMEM((n, RET_DV), F32),
            pltpu.VMEM((n, RET_DV), F32), pltpu.VMEM((RET_DK, RET_DV), F32),
        ],
        compiler_params=pltpu.CompilerParams(
            dimension_semantics=("arbitrary", "arbitrary"), vmem_limit_bytes=VMEM_LIMIT),
        name="retention_lat",
    )(decay, x, mod, mod, w_in_bf, w_in_bf, w_in_bf, w_in_bf, state_ret, state_ret, cos, sin)


N_ROUTE_IN = 4
N_ROUTE_OUT = 5


def _split_route_refs(rest, n_scratch):
    route_in = rest[:N_ROUTE_IN]
    o_ref = rest[N_ROUTE_IN]
    route_out = rest[N_ROUTE_IN + 1:N_ROUTE_IN + 1 + N_ROUTE_OUT]
    return route_in, o_ref, route_out, rest[len(rest) - n_scratch:]


def _ret_out_body(a_ref, x_ref, g1_ref, w_ref, lng_ref, lnb_ref, *rest):
    route_in, o_ref, route_out, (wbf_ref, carry_ref) = _split_route_refs(rest, 2)

    @pl.when(pl.program_id(0) == 0)
    def _():
        wbf_ref[...] = w_ref[...].astype(BF16)

    y = jnp.dot(a_ref[...], wbf_ref[...], preferred_element_type=F32)
    z = DEEPNORM_ALPHA * x_ref[...].astype(F32) + g1_ref[...] * y
    xn = _layer_norm(z, lng_ref[...], lnb_ref[...])
    o_ref[...] = xn.astype(o_ref.dtype)
    _route_block(xn, *route_in, *route_out, carry_ref)


def _ret_out(trunk, a, x, mod, w_all, w_idx, ln_g, ln_b, router_wt, bias_col):
    tm = MIX_BLOCK
    k_dim = w_all.shape[1]
    first_block = trunk * (N_TRUNK // tm)
    row = lambda i: (i, 0)
    vec = pl.BlockSpec((1, D_MODEL), lambda i: (0, 0))
    r_in, r_out, r_shapes, r_scratch = _route_io(tm, first_block)
    return pl.pallas_call(
        _ret_out_body,
        grid=(N_TRUNK // tm,),
        in_specs=[
            pl.BlockSpec((tm, k_dim), row),
            pl.BlockSpec((tm, D_MODEL), row),
            _mod_spec(tm, 2, 1, 0, first_block),
            pl.BlockSpec((None, k_dim, D_MODEL), lambda i: (w_idx, 0, 0)),
            vec, vec,
        ] + r_in,
        out_specs=[pl.BlockSpec((tm, D_MODEL), row)] + r_out,
        out_shape=[jax.ShapeDtypeStruct((N_TRUNK, D_MODEL), RES_DTYPE)] + r_shapes,
        scratch_shapes=[pltpu.VMEM((k_dim, D_MODEL), BF16)] + r_scratch,
        compiler_params=pltpu.CompilerParams(
            dimension_semantics=("arbitrary",), vmem_limit_bytes=VMEM_LIMIT),
        name="ret_out_ln_route",
    )(a, x, mod, w_all, ln_g, ln_b, mod, mod, router_wt, bias_col)


def _conv_layer_body(seg, x_ref, sh_ref, sc_ref, g1_ref, win_ref, wout_ref, cw_ref, lng_ref, lnb_ref, *rest):
    route_in, o_ref, route_out, (winb_ref, woutb_ref, carry_ref) = _split_route_refs(rest, 3)
    tm = o_ref.shape[0]

    @pl.when(pl.program_id(0) == 0)
    def _():
        winb_ref[...] = win_ref[...].astype(BF16)
        woutb_ref[...] = wout_ref[...].astype(BF16)

    x = x_ref[...].astype(F32)
    h = (x * (1.0 + sc_ref[...]) + sh_ref[...]).astype(BF16)
    proj = jnp.dot(h, winb_ref[...], preferred_element_type=F32)
    bg, cg, xt = (proj[:, c * D_MODEL:(c + 1) * D_MODEL] for c in range(3))
    u = cg * xt
    pos = lax.broadcasted_iota(jnp.int32, (tm, D_MODEL), 0) & (seg - 1)
    u_prev = jnp.where(pos == 0, 0.0, pltpu.roll(u, 1, axis=0))
    u_next = jnp.where(pos == seg - 1, 0.0, pltpu.roll(u, tm - 1, axis=0))
    cu = u_prev * cw_ref[0:1, :] + u * cw_ref[1:2, :] + u_next * cw_ref[2:3, :]
    y = jnp.dot((bg * cu).astype(BF16), woutb_ref[...], preferred_element_type=F32)
    z = DEEPNORM_ALPHA * x + g1_ref[...] * y
    xn = _layer_norm(z, lng_ref[...], lnb_ref[...])
    o_ref[...] = xn.astype(o_ref.dtype)
    _route_block(xn, *route_in, *route_out, carry_ref)


def _conv_layer(trunk, x, mod, w_in_all, w_out_all, w_idx, conv_w, ln_g, ln_b, router_wt, bias_col):
    tm = MIX_BLOCK
    first_block = trunk * (N_TRUNK // tm)
    seg = SEQ if trunk == 0 else GRID_W
    assert tm % seg == 0 and seg & (seg - 1) == 0
    row = lambda i: (i, 0)
    vec = pl.BlockSpec((1, D_MODEL), lambda i: (0, 0))
    once = pl.Buffered(1)
    r_in, r_out, r_shapes, r_scratch = _route_io(tm, first_block)
    mod_spec = lambda col: _mod_spec(tm, col, 1, 0, first_block)
    return pl.pallas_call(
        functools.partial(_conv_layer_body, seg),
        grid=(N_TRUNK // tm,),
        in_specs=[
            pl.BlockSpec((tm, D_MODEL), row),
            mod_spec(0), mod_spec(1), mod_spec(2),
            pl.BlockSpec((None, D_MODEL, 3 * D_MODEL), lambda i: (w_idx, 0, 0), pipeline_mode=once),
            pl.BlockSpec((None, D_MODEL, D_MODEL), lambda i: (w_idx, 0, 0), pipeline_mode=once),
            pl.BlockSpec((3, D_MODEL), lambda i: (0, 0)),
            vec, vec,
        ] + r_in,
        out_specs=[pl.BlockSpec((tm, D_MODEL), row)] + r_out,
        out_shape=[jax.ShapeDtypeStruct((N_TRUNK, D_MODEL), RES_DTYPE)] + r_shapes,
        scratch_shapes=[pltpu.VMEM((D_MODEL, 3 * D_MODEL), BF16), pltpu.VMEM((D_MODEL, D_MODEL), BF16)]
        + r_scratch,
        compiler_params=pltpu.CompilerParams(
            dimension_semantics=("arbitrary",), vmem_limit_bytes=VMEM_LIMIT),
        name="conv_layer_route",
    )(x, mod, mod, mod, w_in_all, w_out_all, conv_w, ln_g, ln_b, mod, mod, router_wt, bias_col)


def _split_bf16(v):
    hi = v.astype(BF16)
    lo = (v - hi.astype(F32)).astype(BF16)
    return hi, lo


def _pack_bf16_pair(lo_f32, hi_f32):
    lo = lax.bitcast_convert_type(lo_f32.astype(BF16).astype(F32), jnp.uint32) >> 16
    hi = lax.bitcast_convert_type(hi_f32.astype(BF16).astype(F32), jnp.uint32) & jnp.uint32(0xFFFF0000)
    return hi | lo


def _unpack_bf16_pair(u):
    lo = lax.bitcast_convert_type(u << 16, F32)
    hi = lax.bitcast_convert_type(u & jnp.uint32(0xFFFF0000), F32)
    return lo, hi


def _rows_to_tile(rows, n_sub, dtype):
    tm = rows[0].shape[1]
    sub = lax.broadcasted_iota(jnp.int32, (n_sub, tm), 0)
    out = jnp.zeros((n_sub, tm), dtype)
    for k, r in enumerate(rows):
        out = jnp.where(sub == k, jnp.broadcast_to(r.astype(dtype), (n_sub, tm)), out)
    return out


def _route_block(xn, sh_ref, sc_ref, rwt_ref, bias_ref,
                 hp_ref, eidx_ref, rank_ref, wt_ref, cnt_ref, carry_ref):
    tm = xn.shape[0]
    e = N_EXPERTS
    per = e // N_GROUPS
    neg = -jnp.inf

    @pl.when(pl.program_id(0) == 0)
    def _():
        carry_ref[...] = jnp.zeros_like(carry_ref)

    h = xn * (1.0 + sc_ref[...]) + sh_ref[...]
    hp_ref[...] = _pack_bf16_pair(h[:, :HALF], h[:, HALF:])
    h_hi, h_lo = _split_bf16(h)
    w_hi, w_lo = _split_bf16(rwt_ref[...])
    dot = lambda a, b: lax.dot_general(a, b, _NT, preferred_element_type=F32)
    logits = dot(w_hi, h_hi) + (dot(w_hi, h_lo) + dot(w_lo, h_hi))
    s = jax.nn.sigmoid(logits)
    sel = s + bias_ref[...]

    g3 = sel.reshape(N_GROUPS, per, tm)
    sub = lax.broadcasted_iota(jnp.int32, (N_GROUPS, per, tm), 1)
    m1 = jnp.max(g3, axis=1, keepdims=True)
    i1 = jnp.min(jnp.where(g3 == m1, sub, per), axis=1, keepdims=True)
    m2 = jnp.max(jnp.where(sub == i1, neg, g3), axis=1, keepdims=True)
    gs = (m1 + m2).reshape(N_GROUPS, tm)

    gi = lax.broadcasted_iota(jnp.int32, (N_GROUPS, tm), 0)
    gmask = jnp.zeros((N_GROUPS, tm), jnp.bool_)
    cur = gs
    for _ in range(TOPK_GROUPS):
        m = jnp.max(cur, axis=0, keepdims=True)
        idx = jnp.min(jnp.where(cur == m, gi, N_GROUPS), axis=0, keepdims=True)
        pick = gi == idx
        gmask = jnp.logical_or(gmask, pick)
        cur = jnp.where(pick, neg, cur)
    emask = jnp.broadcast_to(gmask.reshape(N_GROUPS, 1, tm), (N_GROUPS, per, tm)).reshape(e, tm)

    ei = lax.broadcasted_iota(jnp.int32, (e, tm), 0)
    picks, ids = [], []
    cur = jnp.where(emask, sel, neg)
    for _ in range(TOP_K):
        m = jnp.max(cur, axis=0, keepdims=True)
        idx = jnp.min(jnp.where(cur == m, ei, e), axis=0, keepdims=True)
        pick = ei == idx
        picks.append(pick)
        ids.append(idx)
        cur = jnp.where(pick, neg, cur)

    chosen = functools.reduce(jnp.logical_or, picks)
    cf = jnp.where(chosen, 1.0, 0.0)
    before = (lax.broadcasted_iota(jnp.int32, (tm, tm), 0)
              < lax.broadcasted_iota(jnp.int32, (tm, tm), 1)).astype(BF16)
    rank = carry_ref[:, 0:1] + jnp.dot(cf.astype(BF16), before, preferred_element_type=F32)
    carry_ref[...] = carry_ref[...] + jnp.sum(cf, axis=1, keepdims=True)
    cnt_ref[...] = carry_ref[...].astype(jnp.int32)

    w_rows = [jnp.sum(jnp.where(p, s, 0.0), axis=0, keepdims=True) for p in picks]
    r_rows = [jnp.sum(jnp.where(p, rank, 0.0), axis=0, keepdims=True) for p in picks]
    den = functools.reduce(lambda a, b: a + b, w_rows)
    w_rows = [w / den * ROUTED_SCALE for w in w_rows]
    eidx_ref[...] = _rows_to_tile(ids, K_PAD, jnp.int32)
    rank_ref[...] = _rows_to_tile(r_rows, K_PAD, F32).astype(jnp.int32)
    wt_ref[...] = _rows_to_tile(w_rows, 128, F32).T


def _route_io(tm, first_block):
    tok_major = lambda i: (i, 0)
    choice_major = lambda i: (0, i)
    in_specs = [
        _mod_spec(tm, 3, 1, 0, first_block),
        _mod_spec(tm, 4, 1, 0, first_block),
        pl.BlockSpec((N_EXPERTS, D_MODEL), lambda i: (0, 0)),
        pl.BlockSpec((N_EXPERTS, 1), lambda i: (0, 0)),
    ]
    out_specs = [
        pl.BlockSpec((tm, HALF), tok_major),
        pl.BlockSpec((K_PAD, tm), choice_major),
        pl.BlockSpec((K_PAD, tm), choice_major),
        pl.BlockSpec((tm, 128), tok_major),
        pl.BlockSpec((N_EXPERTS, 128), lambda i: (0, 0)),
    ]
    out_shapes = [
        jax.ShapeDtypeStruct((N_TRUNK, HALF), jnp.uint32),
        jax.ShapeDtypeStruct((K_PAD, N_TRUNK), jnp.int32),
        jax.ShapeDtypeStruct((K_PAD, N_TRUNK), jnp.int32),
        jax.ShapeDtypeStruct((N_TRUNK, 128), F32),
        jax.ShapeDtypeStruct((N_EXPERTS, 128), jnp.int32),
    ]
    return in_specs, out_specs, out_shapes, [pltpu.VMEM((N_EXPERTS, 128), F32)]


def _finalize_body(cnt_ref, eidx_ref, rank_ref, pos_ref, te_ref, nt_ref):
    pos_ref[...] = rank_ref[...]

    def per_expert(e, carry):
        off, t = carry
        n_tile = lax.div(cnt_ref[e] + (MOE_TILE - 1), MOE_TILE)
        pos_ref[...] = pos_ref[...] + jnp.where(eidx_ref[...] == e, off, 0)

        def fill(j, c):
            te_ref[t + j] = e
            return c
        lax.fori_loop(0, n_tile, fill, 0)
        return off + n_tile * MOE_TILE, t + n_tile

    _, n_used = lax.fori_loop(0, N_EXPERTS, per_expert, (jnp.int32(0), jnp.int32(0)))
    nt_ref[0] = n_used

    def fill_tail(j, c):
        te_ref[j] = 0
        return c
    lax.fori_loop(n_used, N_TILE, fill_tail, 0)


def _finalize(counts, eidx, rank):
    smem = pl.BlockSpec(memory_space=pltpu.SMEM)
    full = pl.BlockSpec((K_PAD, N_TRUNK), lambda: (0, 0))
    return pl.pallas_call(
        _finalize_body,
        in_specs=[smem, full, full],
        out_specs=[full, smem, smem],
        out_shape=[
            jax.ShapeDtypeStruct((K_PAD, N_TRUNK), jnp.int32),
            jax.ShapeDtypeStruct((N_TILE,), jnp.int32),
            jax.ShapeDtypeStruct((1,), jnp.int32),
        ],
        compiler_params=pltpu.CompilerParams(vmem_limit_bytes=VMEM_LIMIT),
        name="route_finalize",
    )(counts, eidx, rank)


def _ffn(h, wgu, wd):
    gu = jnp.dot(h, wgu, preferred_element_type=F32)
    hid = _silu(gu[:, :EXPERT_FF]) * gu[:, EXPERT_FF:]
    return jnp.dot(hid.astype(BF16), wd, preferred_element_type=F32)


def _cast_ffn_weights(wg_ref, wu_ref, wd_ref, wgub_ref, wdb_ref):
    wgub_ref[:, :EXPERT_FF] = wg_ref[...].astype(BF16)
    wgub_ref[:, EXPERT_FF:] = wu_ref[...].astype(BF16)
    wdb_ref[...] = wd_ref[...].astype(BF16)


_FFN_WEIGHT_SCRATCH = [pltpu.VMEM((D_MODEL, 2 * EXPERT_FF), BF16), pltpu.VMEM((EXPERT_FF, D_MODEL), BF16)]


def _sc_mesh():
    from jax.experimental.pallas import tpu_sc as plsc
    return plsc.VectorSubcoreMesh(core_axis_name="c", subcore_axis_name="s",
                                  num_cores=SC_CORES, num_subcores=SC_SUBCORES)


def _sc_worker_id():
    return lax.axis_index("s") * SC_CORES + lax.axis_index("c")


def _sc_dispatch(hp, pos_flat):
    win = SC_SCATTER_WIN
    per_worker = N_TRUNK // SC_WORKERS

    def body(rows_hbm, idx_hbm, out_hbm, *scratch):
        idx_v, rows_v, sem = scratch[:TOP_K], scratch[TOP_K], scratch[TOP_K + 1]
        base = _sc_worker_id() * per_worker

        @pl.loop(0, per_worker // win)
        def _(j):
            off = base + j * win
            pltpu.sync_copy(rows_hbm.at[pl.ds(off, win)], rows_v)
            for k in range(TOP_K):
                pltpu.sync_copy(idx_hbm.at[pl.ds(k * N_TRUNK + off, win)], idx_v[k])
            copies = [pltpu.async_copy(rows_v, out_hbm.at[idx_v[k]], sem) for k in range(TOP_K)]
            for cp in copies:
                cp.wait()

    return pl.kernel(
        body, mesh=_sc_mesh(),
        out_type=jax.ShapeDtypeStruct((N_SLOT, HALF), jnp.uint32),
        scratch_types=[pltpu.VMEM((win,), jnp.int32)] * TOP_K
        + [pltpu.VMEM((win, HALF), jnp.uint32), pltpu.SemaphoreType.DMA],
        name="sc_dispatch",
    )(hp, pos_flat)


def _sc_return(ys, pos_flat):
    win = SC_GATHER_WIN
    per_worker = N_PAIR // SC_WORKERS

    def body(src_hbm, idx_hbm, out_hbm, idx_v, rows_v, sem):
        base = _sc_worker_id() * per_worker

        @pl.loop(0, per_worker // win)
        def _(j):
            off = base + j * win
            pltpu.sync_copy(idx_hbm.at[pl.ds(off, win)], idx_v)
            pltpu.async_copy(src_hbm.at[idx_v], rows_v, sem).wait()
            pltpu.sync_copy(rows_v, out_hbm.at[pl.ds(off, win)])

    return pl.kernel(
        body, mesh=_sc_mesh(),
        out_type=jax.ShapeDtypeStruct((N_PAIR, HALF), jnp.uint32),
        scratch_types=[pltpu.VMEM((win,), jnp.int32), pltpu.VMEM((win, HALF), jnp.uint32),
                       pltpu.SemaphoreType.DMA],
        name="sc_return",
    )(ys, pos_flat)


def _expert_body(te_ref, xs_ref, wg_ref, wu_ref, wd_ref, ys_ref, wgub_ref, wdb_ref):
    i = pl.program_id(0)

    @pl.when(jnp.logical_or(i == 0, te_ref[i] != te_ref[jnp.maximum(i - 1, 0)]))
    def _():
        _cast_ffn_weights(wg_ref, wu_ref, wd_ref, wgub_ref, wdb_ref)

    lo, hi = _unpack_bf16_pair(xs_ref[...])
    h = jnp.concatenate([lo.astype(BF16), hi.astype(BF16)], axis=1)
    y = _ffn(h, wgub_ref[...], wdb_ref[...])
    ys_ref[...] = _pack_bf16_pair(y[:, :HALF], y[:, HALF:])


def _expert_ffn(tile_expert, n_tiles, xs, layer, w_gate, w_up, w_down):
    tile = lambda i, te: (i, 0)
    ew = lambda shape: pl.BlockSpec((None, None) + shape, lambda i, te: (layer, te[i], 0, 0))
    return pl.pallas_call(
        _expert_body,
        grid_spec=pltpu.PrefetchScalarGridSpec(
            num_scalar_prefetch=1,
            grid=(n_tiles[0],),
            in_specs=[
                pl.BlockSpec((MOE_TILE, HALF), tile),
                ew((D_MODEL, EXPERT_FF)), ew((D_MODEL, EXPERT_FF)), ew((EXPERT_FF, D_MODEL)),
            ],
            out_specs=pl.BlockSpec((MOE_TILE, HALF), tile),
            scratch_shapes=_FFN_WEIGHT_SCRATCH,
        ),
        out_shape=jax.ShapeDtypeStruct((N_SLOT, HALF), jnp.uint32),
        compiler_params=pltpu.CompilerParams(
            dimension_semantics=("arbitrary",), vmem_limit_bytes=VMEM_LIMIT),
        name="expert_ffn",
    )(tile_expert, xs, w_gate, w_up, w_down)


def _combine_body(x_ref, sh_ref, sc_ref, g2_ref, yk_ref, wt_ref, sg_ref, su_ref, sd_ref, lng_ref, lnb_ref,
                  o_ref, sgub_ref, sdb_ref):
    @pl.when(pl.program_id(0) == 0)
    def _():
        _cast_ffn_weights(sg_ref, su_ref, sd_ref, sgub_ref, sdb_ref)

    x = x_ref[...].astype(F32)
    h = (x * (1.0 + sc_ref[...]) + sh_ref[...]).astype(BF16)
    y = _ffn(h, sgub_ref[...], sdb_ref[...])
    wt = wt_ref[...]
    lo_acc = jnp.zeros((x.shape[0], HALF), F32)
    hi_acc = jnp.zeros((x.shape[0], HALF), F32)
    for k in range(TOP_K):
        lo, hi = _unpack_bf16_pair(yk_ref[k])
        w = wt[:, k:k + 1]
        lo_acc = lo_acc + w * lo
        hi_acc = hi_acc + w * hi
    y = y + jnp.concatenate([lo_acc, hi_acc], axis=1)
    z = DEEPNORM_ALPHA * x + g2_ref[...] * y
    o_ref[...] = _layer_norm(z, lng_ref[...], lnb_ref[...]).astype(o_ref.dtype)


def _combine(trunk, x, mod, yk, wt, layer, s_gate, s_up, s_down, ln_g, ln_b, out_dtype):
    tm = TOK_BLOCK
    first_block = trunk * (N_TRUNK // tm)
    row = lambda i: (i, 0)
    vec = pl.BlockSpec((1, D_MODEL), lambda i: (0, 0))
    sw = lambda shape: pl.BlockSpec((None,) + shape, lambda i: (layer, 0, 0))
    mod_spec = lambda col: _mod_spec(tm, col, 1, 0, first_block)
    return pl.pallas_call(
        _combine_body,
        grid=(N_TRUNK // tm,),
        in_specs=[
            pl.BlockSpec((tm, D_MODEL), row),
            mod_spec(3), mod_spec(4), mod_spec(5),
            pl.BlockSpec((TOP_K, tm, HALF), lambda i: (0, i, 0)),
            pl.BlockSpec((tm, 128), row),
            sw((D_MODEL, EXPERT_FF)), sw((D_MODEL, EXPERT_FF)), sw((EXPERT_FF, D_MODEL)),
            vec, vec,
        ],
        out_specs=pl.BlockSpec((tm, D_MODEL), lambda i: (i, 0)),
        out_shape=jax.ShapeDtypeStruct((N_TRUNK, D_MODEL), out_dtype),
        scratch_shapes=_FFN_WEIGHT_SCRATCH,
        compiler_params=pltpu.CompilerParams(
            dimension_semantics=("arbitrary",), vmem_limit_bytes=VMEM_LIMIT),
        name="moe_combine_ln",
    )(x, mod, mod, mod, yk, wt, s_gate, s_up, s_down, ln_g, ln_b)


def _moe_dispatch(routing):
    hp, eidx, rank, wt, counts = routing
    pos, tile_expert, n_tiles = _finalize(counts[:, 0], eidx, rank)
    pos_flat = pos.reshape(K_PAD * N_TRUNK)
    return _sc_dispatch(hp, pos_flat), pos_flat, tile_expert, n_tiles, wt


def _moe_experts(plan, layer, w_gate, w_up, w_down):
    xs, pos_flat, tile_expert, n_tiles, _ = plan
    ys = _expert_ffn(tile_expert, n_tiles, xs, layer, w_gate, w_up, w_down)
    return _sc_return(ys, pos_flat).reshape(TOP_K, N_TRUNK, HALF)


def kernel(x_prompt, x_sample, state_ret, c, c_ctx, ada_w, ada_b, ln_g, ln_b, ret_w_in, ret_w_out, ret_decay, conv_w_in, conv_w, conv_w_out, moe_router, moe_bias, moe_w_gate, moe_w_up, moe_w_down, shared_w_gate, shared_w_up, shared_w_down):
    x = (x_prompt.reshape(N_CTX, D_MODEL), x_sample.reshape(N_LAT, D_MODEL))
    cond = jnp.concatenate(
        [c_ctx[None, :], c, jnp.zeros((N_COND - 1 - DEC_BATCH, D_MODEL), F32)], axis=0)
    mods = _ada_table(cond, ada_w, ada_b).reshape(DEPTH, N_COND, 1, 6 * D_MODEL)
    cos, sin = _rope_tables()
    router_wt = jnp.swapaxes(moe_router, 1, 2)

    states = None
    for i in range(DEPTH):
        j = i // 2
        mod = mods[i]
        lng = ln_g[i].reshape(2, 1, D_MODEL)
        lnb = ln_b[i].reshape(2, 1, D_MODEL)
        bias_col = moe_bias[i].reshape(N_EXPERTS, 1)
        if i % 2 == 0:
            w_in_bf = _to_bf16(ret_w_in, j)
        mixed, plans = [], []
        for trunk in range(2):
            if i % 2 == 0:
                if trunk == 0:
                    a, states = _retention_ctx(x[0], mod, w_in_bf, ret_decay[j], j, states)
                else:
                    a = _retention_lat(x[1], mod, w_in_bf, ret_decay[j], state_ret, j, cos, sin)
                xm, *routing = _ret_out(trunk, a, x[trunk], mod, ret_w_out, j, lng[0], lnb[0],
                                        router_wt[i], bias_col)
            else:
                xm, *routing = _conv_layer(trunk, x[trunk], mod, conv_w_in, conv_w_out, j, conv_w[j],
                                           lng[0], lnb[0], router_wt[i], bias_col)
            mixed.append(xm)
            plans.append(_moe_dispatch(routing))
        yks = [_moe_experts(plans[trunk], i, moe_w_gate, moe_w_up, moe_w_down) for trunk in range(2)]
        out_dtype = F32 if i == DEPTH - 1 else RES_DTYPE
        x = tuple(_combine(trunk, mixed[trunk], mod, yks[trunk], plans[trunk][4], i, shared_w_gate, shared_w_up,
                           shared_w_down, lng[1], lnb[1], out_dtype) for trunk in range(2))

    y_prompt = x[0].reshape(BATCH, SEQ, D_MODEL)
    y_sample = x[1].reshape(DEC_BATCH, DEC_SEQ, D_MODEL)
    return y_prompt, y_sample, states
```

```python
import functools

import jax
import jax.numpy as jnp
from jax import lax
from jax.experimental import pallas as pl
from jax.experimental.pallas import tpu as pltpu

F32 = jnp.float32
BF16 = jnp.bfloat16

D_MODEL = 1024
BATCH = 32
SEQ = 256
DEPTH = 4
DEC_BATCH = 8
DEC_SEQ = 1024
GRID_W = 64
RET_HEADS = 4
RET_DK = D_MODEL // RET_HEADS
RET_DV = 2 * D_MODEL // RET_HEADS
ROPE_BASE = 10000.0
N_EXPERTS = 64
TOP_K = 6
N_GROUPS = 8
TOPK_GROUPS = 4
EXPERT_FF = 256
ROUTED_SCALE = 2.5
LN_EPS = 1e-5
DEEPNORM_ALPHA = (2.0 * DEPTH) ** 0.25

N_CTX = BATCH * SEQ
N_LAT = DEC_BATCH * DEC_SEQ
N_TOK = N_CTX + N_LAT
N_COND = 16
RET_CHUNK = 256
assert N_CTX == N_LAT
N_TRUNK = N_CTX
TOK_BLOCK = 256
MIX_BLOCK = 512
VMEM_LIMIT = 56 * 1024 * 1024
RES_DTYPE = BF16

HALF = D_MODEL // 2
MOE_TILE = 896
N_PAIR = N_TRUNK * TOP_K
N_TILE = -(-(N_PAIR + N_EXPERTS * (MOE_TILE - 1)) // MOE_TILE)
N_SLOT = N_TILE * MOE_TILE
K_PAD = 8

SC_CORES = 2
SC_SUBCORES = 16
SC_WORKERS = SC_CORES * SC_SUBCORES
SC_SCATTER_WIN = 64
SC_GATHER_WIN = 128


def _cond_row(tok_block_idx, tok_block):
    t0 = tok_block_idx * tok_block
    return jnp.where(t0 < N_CTX, 0, 1 + (t0 - N_CTX) // DEC_SEQ)


def _silu(x):
    return x * jax.nn.sigmoid(x)


def _layer_norm(z, g, b):
    mu = jnp.mean(z, axis=-1, keepdims=True)
    zc = z - mu
    var = jnp.mean(zc * zc, axis=-1, keepdims=True)
    return zc * lax.rsqrt(var + LN_EPS) * g + b


def _ada_body(cond_ref, w_ref, b_ref, o_ref):
    s = _silu(cond_ref[...]).astype(BF16)
    o_ref[...] = jnp.dot(s, w_ref[...].astype(BF16), preferred_element_type=F32) + b_ref[...]


def _ada_table(cond, ada_w, ada_b):
    tn = 2048
    return pl.pallas_call(
        _ada_body,
        grid=(DEPTH, 6 * D_MODEL // tn),
        in_specs=[
            pl.BlockSpec((N_COND, D_MODEL), lambda l, j: (0, 0)),
            pl.BlockSpec((None, D_MODEL, tn), lambda l, j: (l, 0, j)),
            pl.BlockSpec((None, 1, tn), lambda l, j: (l, 0, j)),
        ],
        out_specs=pl.BlockSpec((None, N_COND, tn), lambda l, j: (l, 0, j)),
        out_shape=jax.ShapeDtypeStruct((DEPTH, N_COND, 6 * D_MODEL), F32),
        compiler_params=pltpu.CompilerParams(
            dimension_semantics=("arbitrary", "arbitrary"), vmem_limit_bytes=VMEM_LIMIT),
        name="ada_table",
    )(cond, ada_w, ada_b.reshape(DEPTH, 1, 6 * D_MODEL))


def _mod_spec(tok_block, col, grid_rank, tok_axis, first_block=0):
    def index_map(*idx):
        return (_cond_row(first_block + idx[tok_axis], tok_block), 0, col)
    del grid_rank
    return pl.BlockSpec((None, 1, D_MODEL), index_map)


def _to_bf16_body(w_ref, o_ref):
    o_ref[...] = w_ref[...].astype(BF16)


def _to_bf16(w_all, w_idx):
    k_dim, n_out = w_all.shape[1:]
    tn = 1024
    return pl.pallas_call(
        _to_bf16_body,
        grid=(n_out // tn,),
        in_specs=[pl.BlockSpec((None, k_dim, tn), lambda j: (w_idx, 0, j))],
        out_specs=pl.BlockSpec((k_dim, tn), lambda j: (0, j)),
        out_shape=jax.ShapeDtypeStruct((k_dim, n_out), BF16),
        compiler_params=pltpu.CompilerParams(
            dimension_semantics=("arbitrary",), vmem_limit_bytes=VMEM_LIMIT),
        name="weight_to_bf16",
    )(w_all)


RET_GROUP = 1024


def _head_proj_specs():
    return [
        pl.BlockSpec((D_MODEL, RET_DK), lambda t, h: (0, h)),
        pl.BlockSpec((D_MODEL, RET_DK), lambda t, h: (0, RET_HEADS + h)),
        pl.BlockSpec((D_MODEL, RET_DV), lambda t, h: (0, RET_HEADS + h)),
        pl.BlockSpec((D_MODEL, RET_DV), lambda t, h: (0, 2 * RET_HEADS + h)),
    ]


def _head_proj(x_ref, sh_ref, sc_ref, wq_ref, wk_ref, wv_ref, wg_ref):
    h = (x_ref[...].astype(F32) * (1.0 + sc_ref[...]) + sh_ref[...]).astype(BF16)
    dot = lambda w_ref: jnp.dot(h, w_ref[...], preferred_element_type=F32)
    return dot(wq_ref), dot(wk_ref), dot(wv_ref).astype(BF16), dot(wg_ref)

def _log_sigmoid(v):
    return jnp.minimum(v, 0.0) - jnp.log1p(jnp.exp(-jnp.abs(v)))


def _decay_tables(dec_ref, head):
    c = RET_CHUNK
    lgf = _log_sigmoid(jnp.full((c, c), dec_ref[0, head], F32))
    lgb = _log_sigmoid(jnp.full((c, c), dec_ref[1, head], F32))
    row = lax.broadcasted_iota(jnp.int32, (c, c), 0).astype(F32)
    col = lax.broadcasted_iota(jnp.int32, (c, c), 1).astype(F32)
    diff = row - col
    kscale = RET_DK ** -0.5
    intra = jnp.where(diff > 0, jnp.exp(lgf * diff),
                      jnp.where(diff < 0, jnp.exp(-lgb * diff), 2.0)) * kscale
    qdec_f = jnp.exp(lgf * (row + 1.0))
    qdec_b = jnp.exp(lgb * (c - row))
    kdec_f = jnp.exp(lgf * (c - 1.0 - row)) * kscale
    kdec_b = jnp.exp(lgb * row) * kscale
    cdec_f = jnp.exp(lgf * c)
    cdec_b = jnp.exp(lgb * c)
    return intra, qdec_f, qdec_b, kdec_f, kdec_b, cdec_f, cdec_b


def _head_norm_gate(o, g):
    mu = jnp.mean(o, axis=-1, keepdims=True)
    oc = o - mu
    var = jnp.mean(oc * oc, axis=-1, keepdims=True)
    on = oc * lax.rsqrt(var + LN_EPS)
    return (_silu(g.astype(F32)) * on).astype(BF16)


_NT = (((1,), (1,)), ((), ()))
_TN = (((0,), (0,)), ((), ()))


def _ret_ctx_body(dec_ref, x_ref, sh_ref, sc_ref, wq_ref, wk_ref, wv_ref, wg_ref, *rest):
    a_ref, st_ref, tab_ref = rest[-3:]
    head = pl.program_id(1)

    @pl.when(pl.program_id(0) == 0)
    def _():
        intra, _, _, kdec_f, kdec_b, _, _ = _decay_tables(dec_ref, head)
        tab_ref[head, 0] = intra
        tab_ref[head, 1] = kdec_f
        tab_ref[head, 2] = kdec_b

    q, k, v, g = _head_proj(x_ref, sh_ref, sc_ref, wq_ref, wk_ref, wv_ref, wg_ref)
    for s in range(RET_GROUP // SEQ):
        rows = slice(s * SEQ, (s + 1) * SEQ)
        scores = lax.dot_general(q[rows].astype(BF16), k[rows].astype(BF16), _NT, preferred_element_type=F32)
        p = (scores * tab_ref[head, 0]).astype(BF16)
        o = jnp.dot(p, v[rows], preferred_element_type=F32)
        a_ref[rows, :] = _head_norm_gate(o, g[rows])
        st_ref[s, 0] = lax.dot_general((k[rows] * tab_ref[head, 1]).astype(BF16), v[rows], _TN,
                                       preferred_element_type=F32)
        st_ref[s, 1] = lax.dot_general((k[rows] * tab_ref[head, 2]).astype(BF16), v[rows], _TN,
                                       preferred_element_type=F32)


def _retention_ctx(x, mod, w_in_bf, decay, ret_idx, states):
    assert SEQ == RET_CHUNK
    seqs = RET_GROUP // SEQ
    n_ret = (DEPTH + 1) // 2
    st = jax.ShapeDtypeStruct((BATCH, n_ret, 2, RET_HEADS, RET_DK, RET_DV), F32)
    st_spec = pl.BlockSpec((seqs, None, 2, None, RET_DK, RET_DV), lambda t, h: (t, ret_idx, 0, h, 0, 0))
    in_specs = [
        pl.BlockSpec(memory_space=pltpu.SMEM),
        pl.BlockSpec((RET_GROUP, D_MODEL), lambda t, h: (t, 0)),
        _mod_spec(RET_GROUP, 0, 2, 0), _mod_spec(RET_GROUP, 1, 2, 0),
    ] + _head_proj_specs()
    args = (decay, x, mod, mod, w_in_bf, w_in_bf, w_in_bf, w_in_bf)
    aliases = {}
    if states is not None:
        in_specs.append(pl.BlockSpec(memory_space=pl.ANY))
        aliases = {len(args): 1}
        args += (states,)
    return pl.pallas_call(
        _ret_ctx_body,
        grid=(N_CTX // RET_GROUP, RET_HEADS),
        in_specs=in_specs,
        out_specs=[pl.BlockSpec((RET_GROUP, RET_DV), lambda t, h: (t, h)), st_spec],
        out_shape=[jax.ShapeDtypeStruct((N_CTX, RET_HEADS * RET_DV), BF16), st],
        input_output_aliases=aliases,
        scratch_shapes=[pltpu.VMEM((RET_HEADS, 3, RET_CHUNK, RET_CHUNK), F32)],
        compiler_params=pltpu.CompilerParams(
            dimension_semantics=("arbitrary", "arbitrary"), vmem_limit_bytes=VMEM_LIMIT),
        name="retention_ctx",
    )(*args)


def _rope(x, cos, sin):
    halves = [pltpu.roll(x[:, s:s + 128], 64, axis=1) for s in (0, 128)]
    return x * cos + jnp.concatenate(halves, axis=1) * sin


def _ret_lat_body(dec_ref, x_ref, sh_ref, sc_ref, wq_ref, wk_ref, wv_ref, wg_ref,
                  s0f_ref, s0b_ref, cos_ref, sin_ref,
                  a_ref, qr_ref, kr_ref, v_ref, g_ref, o_ref, st_ref):
    head = pl.program_id(1)
    c = RET_CHUNK
    nc = DEC_SEQ // c
    intra, qdec_f, qdec_b, kdec_f, kdec_b, cdec_f, cdec_b = _decay_tables(dec_ref, head)
    cdec_f = jnp.concatenate([cdec_f, cdec_f], axis=1)
    cdec_b = jnp.concatenate([cdec_b, cdec_b], axis=1)

    q, k, v, g = _head_proj(x_ref, sh_ref, sc_ref, wq_ref, wk_ref, wv_ref, wg_ref)
    qr_ref[...] = _rope(q, cos_ref[...], sin_ref[...])
    kr_ref[...] = _rope(k, cos_ref[...], sin_ref[...])
    v_ref[...] = v
    g_ref[...] = g

    st_ref[...] = s0f_ref[...]
    for ci in range(nc):
        rows = pl.ds(ci * c, c)
        q = qr_ref[rows, :]
        k = kr_ref[rows, :]
        v = v_ref[rows, :]
        scores = lax.dot_general(q.astype(BF16), k.astype(BF16), _NT, preferred_element_type=F32)
        p = (scores * intra).astype(BF16)
        o = jnp.dot(p, v, preferred_element_type=F32)
        qd = (q * qdec_f).astype(BF16)
        o = o + jnp.dot(qd, st_ref[...].astype(BF16), preferred_element_type=F32)
        o_ref[rows, :] = o
        if ci + 1 < nc:
            kd = (k * kdec_f).astype(BF16)
            st_ref[...] = cdec_f * st_ref[...] + lax.dot_general(kd, v, _TN, preferred_element_type=F32)

    st_ref[...] = s0b_ref[...]
    for ci in reversed(range(nc)):
        rows = pl.ds(ci * c, c)
        qd = (qr_ref[rows, :] * qdec_b).astype(BF16)
        o = o_ref[rows, :] + jnp.dot(qd, st_ref[...].astype(BF16), preferred_element_type=F32)
        a_ref[rows, :] = _head_norm_gate(o, g_ref[rows, :])
        if ci > 0:
            kd = (kr_ref[rows, :] * kdec_b).astype(BF16)
            st_ref[...] = cdec_b * st_ref[...] + lax.dot_general(kd, v_ref[rows, :], _TN,
                                                                 preferred_element_type=F32)


def _rope_tables():
    half = RET_DK // 4
    freqs = ROPE_BASE ** (-jnp.arange(half, dtype=F32) / half)
    t = jnp.arange(DEC_SEQ)
    row = (t // GRID_W).astype(F32)
    col = (t % GRID_W).astype(F32)
    ang_r = row[:, None] * freqs[None, :]
    ang_c = col[:, None] * freqs[None, :]
    cos = jnp.concatenate([jnp.cos(ang_r)] * 2 + [jnp.cos(ang_c)] * 2, axis=1)
    sin = jnp.concatenate([-jnp.sin(ang_r), jnp.sin(ang_r), -jnp.sin(ang_c), jnp.sin(ang_c)], axis=1)
    return cos, sin


def _retention_lat(x, mod, w_in_bf, decay, state_ret, ret_idx, cos, sin):
    n = DEC_SEQ
    assert n == RET_GROUP
    s0_spec = lambda d: pl.BlockSpec((None, None, None, None, RET_DK, RET_DV),
                                     lambda b, h: (b, ret_idx, d, h, 0, 0))
    tab_spec = pl.BlockSpec((n, RET_DK), lambda b, h: (0, 0))
    first_block = N_CTX // n
    return pl.pallas_call(
        _ret_lat_body,
        grid=(DEC_BATCH, RET_HEADS),
        in_specs=[
            pl.BlockSpec(memory_space=pltpu.SMEM),
            pl.BlockSpec((n, D_MODEL), lambda b, h: (b, 0)),
            _mod_spec(n, 0, 2, 0, first_block), _mod_spec(n, 1, 2, 0, first_block),
        ] + _head_proj_specs() + [s0_spec(0), s0_spec(1), tab_spec, tab_spec],
        out_specs=pl.BlockSpec((n, RET_DV), lambda b, h: (b, h)),
        out_shape=jax.ShapeDtypeStruct((N_LAT, RET_HEADS * RET_DV), BF16),
        scratch_shapes=[
            pltpu.VMEM((n, RET_DK), F32), pltpu.VMEM((n, RET_DK), F32),
            pltpu.VMEM((n, RET_DV), BF16), pltpu.VMEM((n, RET_DV), F32),
            pltpu.VMEM((n, RET_DV), F32), pltpu.VMEM((RET_DK, RET_DV), F32),
        ],
        compiler_params=pltpu.CompilerParams(
            dimension_semantics=("arbitrary", "arbitrary"), vmem_limit_bytes=VMEM_LIMIT),
        name="retention_lat",
    )(decay, x, mod, mod, w_in_bf, w_in_bf, w_in_bf, w_in_bf, state_ret, state_ret, cos, sin)


N_ROUTE_IN = 4
N_ROUTE_OUT = 5


def _split_route_refs(rest, n_scratch):
    route_in = rest[:N_ROUTE_IN]
    o_ref = rest[N_ROUTE_IN]
    route_out = rest[N_ROUTE_IN + 1:N_ROUTE_IN + 1 + N_ROUTE_OUT]
    return route_in, o_ref, route_out, rest[len(rest) - n_scratch:]


def _ret_out_body(a_ref, x_ref, g1_ref, w_ref, lng_ref, lnb_ref, *rest):
    route_in, o_ref, route_out, (wbf_ref, carry_ref) = _split_route_refs(rest, 2)

    @pl.when(pl.program_id(0) == 0)
    def _():
        wbf_ref[...] = w_ref[...].astype(BF16)

    y = jnp.dot(a_ref[...], wbf_ref[...], preferred_element_type=F32)
    z = DEEPNORM_ALPHA * x_ref[...].astype(F32) + g1_ref[...] * y
    xn = _layer_norm(z, lng_ref[...], lnb_ref[...])
    o_ref[...] = xn.astype(o_ref.dtype)
    _route_block(xn, *route_in, *route_out, carry_ref)


def _ret_out(trunk, a, x, mod, w_all, w_idx, ln_g, ln_b, router_wt, bias_col):
    tm = MIX_BLOCK
    k_dim = w_all.shape[1]
    first_block = trunk * (N_TRUNK // tm)
    row = lambda i: (i, 0)
    vec = pl.BlockSpec((1, D_MODEL), lambda i: (0, 0))
    r_in, r_out, r_shapes, r_scratch = _route_io(tm, first_block)
    return pl.pallas_call(
        _ret_out_body,
        grid=(N_TRUNK // tm,),
        in_specs=[
            pl.BlockSpec((tm, k_dim), row),
            pl.BlockSpec((tm, D_MODEL), row),
            _mod_spec(tm, 2, 1, 0, first_block),
            pl.BlockSpec((None, k_dim, D_MODEL), lambda i: (w_idx, 0, 0)),
            vec, vec,
        ] + r_in,
        out_specs=[pl.BlockSpec((tm, D_MODEL), row)] + r_out,
        out_shape=[jax.ShapeDtypeStruct((N_TRUNK, D_MODEL), RES_DTYPE)] + r_shapes,
        scratch_shapes=[pltpu.VMEM((k_dim, D_MODEL), BF16)] + r_scratch,
        compiler_params=pltpu.CompilerParams(
            dimension_semantics=("arbitrary",), vmem_limit_bytes=VMEM_LIMIT),
        name="ret_out_ln_route",
    )(a, x, mod, w_all, ln_g, ln_b, mod, mod, router_wt, bias_col)


def _conv_layer_body(seg, x_ref, sh_ref, sc_ref, g1_ref, win_ref, wout_ref, cw_ref, lng_ref, lnb_ref, *rest):
    route_in, o_ref, route_out, (winb_ref, woutb_ref, carry_ref) = _split_route_refs(rest, 3)
    tm = o_ref.shape[0]

    @pl.when(pl.program_id(0) == 0)
    def _():
        winb_ref[...] = win_ref[...].astype(BF16)
        woutb_ref[...] = wout_ref[...].astype(BF16)

    x = x_ref[...].astype(F32)
    h = (x * (1.0 + sc_ref[...]) + sh_ref[...]).astype(BF16)
    proj = jnp.dot(h, winb_ref[...], preferred_element_type=F32)
    bg, cg, xt = (proj[:, c * D_MODEL:(c + 1) * D_MODEL] for c in range(3))
    u = cg * xt
    pos = lax.broadcasted_iota(jnp.int32, (tm, D_MODEL), 0) & (seg - 1)
    u_prev = jnp.where(pos == 0, 0.0, pltpu.roll(u, 1, axis=0))
    u_next = jnp.where(pos == seg - 1, 0.0, pltpu.roll(u, tm - 1, axis=0))
    cu = u_prev * cw_ref[0:1, :] + u * cw_ref[1:2, :] + u_next * cw_ref[2:3, :]
    y = jnp.dot((bg * cu).astype(BF16), woutb_ref[...], preferred_element_type=F32)
    z = DEEPNORM_ALPHA * x + g1_ref[...] * y
    xn = _layer_norm(z, lng_ref[...], lnb_ref[...])
    o_ref[...] = xn.astype(o_ref.dtype)
    _route_block(xn, *route_in, *route_out, carry_ref)


def _conv_layer(trunk, x, mod, w_in_all, w_out_all, w_idx, conv_w, ln_g, ln_b, router_wt, bias_col):
    tm = MIX_BLOCK
    first_block = trunk * (N_TRUNK // tm)
    seg = SEQ if trunk == 0 else GRID_W
    assert tm % seg == 0 and seg & (seg - 1) == 0
    row = lambda i: (i, 0)
    vec = pl.BlockSpec((1, D_MODEL), lambda i: (0, 0))
    once = pl.Buffered(1)
    r_in, r_out, r_shapes, r_scratch = _route_io(tm, first_block)
    mod_spec = lambda col: _mod_spec(tm, col, 1, 0, first_block)
    return pl.pallas_call(
        functools.partial(_conv_layer_body, seg),
        grid=(N_TRUNK // tm,),
        in_specs=[
            pl.BlockSpec((tm, D_MODEL), row),
            mod_spec(0), mod_spec(1), mod_spec(2),
            pl.BlockSpec((None, D_MODEL, 3 * D_MODEL), lambda i: (w_idx, 0, 0), pipeline_mode=once),
            pl.BlockSpec((None, D_MODEL, D_MODEL), lambda i: (w_idx, 0, 0), pipeline_mode=once),
            pl.BlockSpec((3, D_MODEL), lambda i: (0, 0)),
            vec, vec,
        ] + r_in,
        out_specs=[pl.BlockSpec((tm, D_MODEL), row)] + r_out,
        out_shape=[jax.ShapeDtypeStruct((N_TRUNK, D_MODEL), RES_DTYPE)] + r_shapes,
        scratch_shapes=[pltpu.VMEM((D_MODEL, 3 * D_MODEL), BF16), pltpu.VMEM((D_MODEL, D_MODEL), BF16)]
        + r_scratch,
        compiler_params=pltpu.CompilerParams(
            dimension_semantics=("arbitrary",), vmem_limit_bytes=VMEM_LIMIT),
        name="conv_layer_route",
    )(x, mod, mod, mod, w_in_all, w_out_all, conv_w, ln_g, ln_b, mod, mod, router_wt, bias_col)


def _split_bf16(v):
    hi = v.astype(BF16)
    lo = (v - hi.astype(F32)).astype(BF16)
    return hi, lo


def _pack_bf16_pair(lo_f32, hi_f32):
    lo = lax.bitcast_convert_type(lo_f32.astype(BF16).astype(F32), jnp.uint32) >> 16
    hi = lax.bitcast_convert_type(hi_f32.astype(BF16).astype(F32), jnp.uint32) & jnp.uint32(0xFFFF0000)
    return hi | lo


def _unpack_bf16_pair(u):
    lo = lax.bitcast_convert_type(u << 16, F32)
    hi = lax.bitcast_convert_type(u & jnp.uint32(0xFFFF0000), F32)
    return lo, hi


def _rows_to_tile(rows, n_sub, dtype):
    tm = rows[0].shape[1]
    sub = lax.broadcasted_iota(jnp.int32, (n_sub, tm), 0)
    out = jnp.zeros((n_sub, tm), dtype)
    for k, r in enumerate(rows):
        out = jnp.where(sub == k, jnp.broadcast_to(r.astype(dtype), (n_sub, tm)), out)
    return out


def _route_block(xn, sh_ref, sc_ref, rwt_ref, bias_ref,
                 hp_ref, eidx_ref, rank_ref, wt_ref, cnt_ref, carry_ref):
    tm = xn.shape[0]
    e = N_EXPERTS
    per = e // N_GROUPS
    neg = -jnp.inf

    @pl.when(pl.program_id(0) == 0)
    def _():
        carry_ref[...] = jnp.zeros_like(carry_ref)

    h = xn * (1.0 + sc_ref[...]) + sh_ref[...]
    hp_ref[...] = _pack_bf16_pair(h[:, :HALF], h[:, HALF:])
    h_hi, h_lo = _split_bf16(h)
    w_hi, w_lo = _split_bf16(rwt_ref[...])
    dot = lambda a, b: lax.dot_general(a, b, _NT, preferred_element_type=F32)
    logits = dot(w_hi, h_hi) + (dot(w_hi, h_lo) + dot(w_lo, h_hi))
    s = jax.nn.sigmoid(logits)
    sel = s + bias_ref[...]

    g3 = sel.reshape(N_GROUPS, per, tm)
    sub = lax.broadcasted_iota(jnp.int32, (N_GROUPS, per, tm), 1)
    m1 = jnp.max(g3, axis=1, keepdims=True)
    i1 = jnp.min(jnp.where(g3 == m1, sub, per), axis=1, keepdims=True)
    m2 = jnp.max(jnp.where(sub == i1, neg, g3), axis=1, keepdims=True)
    gs = (m1 + m2).reshape(N_GROUPS, tm)

    gi = lax.broadcasted_iota(jnp.int32, (N_GROUPS, tm), 0)
    gmask = jnp.zeros((N_GROUPS, tm), jnp.bool_)
    cur = gs
    for _ in range(TOPK_GROUPS):
        m = jnp.max(cur, axis=0, keepdims=True)
        idx = jnp.min(jnp.where(cur == m, gi, N_GROUPS), axis=0, keepdims=True)
        pick = gi == idx
        gmask = jnp.logical_or(gmask, pick)
        cur = jnp.where(pick, neg, cur)
    emask = jnp.broadcast_to(gmask.reshape(N_GROUPS, 1, tm), (N_GROUPS, per, tm)).reshape(e, tm)

    ei = lax.broadcasted_iota(jnp.int32, (e, tm), 0)
    picks, ids = [], []
    cur = jnp.where(emask, sel, neg)
    for _ in range(TOP_K):
        m = jnp.max(cur, axis=0, keepdims=True)
        idx = jnp.min(jnp.where(cur == m, ei, e), axis=0, keepdims=True)
        pick = ei == idx
        picks.append(pick)
        ids.append(idx)
        cur = jnp.where(pick, neg, cur)

    chosen = functools.reduce(jnp.logical_or, picks)
    cf = jnp.where(chosen, 1.0, 0.0)
    before = (lax.broadcasted_iota(jnp.int32, (tm, tm), 0)
              < lax.broadcasted_iota(jnp.int32, (tm, tm), 1)).astype(BF16)
    rank = carry_ref[:, 0:1] + jnp.dot(cf.astype(BF16), before, preferred_element_type=F32)
    carry_ref[...] = carry_ref[...] + jnp.sum(cf, axis=1, keepdims=True)
    cnt_ref[...] = carry_ref[...].astype(jnp.int32)

    w_rows = [jnp.sum(jnp.where(p, s, 0.0), axis=0, keepdims=True) for p in picks]
    r_rows = [jnp.sum(jnp.where(p, rank, 0.0), axis=0, keepdims=True) for p in picks]
    den = functools.reduce(lambda a, b: a + b, w_rows)
    w_rows = [w / den * ROUTED_SCALE for w in w_rows]
    eidx_ref[...] = _rows_to_tile(ids, K_PAD, jnp.int32)
    rank_ref[...] = _rows_to_tile(r_rows, K_PAD, F32).astype(jnp.int32)
    wt_ref[...] = _rows_to_tile(w_rows, 128, F32).T


def _route_io(tm, first_block):
    tok_major = lambda i: (i, 0)
    choice_major = lambda i: (0, i)
    in_specs = [
        _mod_spec(tm, 3, 1, 0, first_block),
        _mod_spec(tm, 4, 1, 0, first_block),
        pl.BlockSpec((N_EXPERTS, D_MODEL), lambda i: (0, 0)),
        pl.BlockSpec((N_EXPERTS, 1), lambda i: (0, 0)),
    ]
    out_specs = [
        pl.BlockSpec((tm, HALF), tok_major),
        pl.BlockSpec((K_PAD, tm), choice_major),
        pl.BlockSpec((K_PAD, tm), choice_major),
        pl.BlockSpec((tm, 128), tok_major),
        pl.BlockSpec((N_EXPERTS, 128), lambda i: (0, 0)),
    ]
    out_shapes = [
        jax.ShapeDtypeStruct((N_TRUNK, HALF), jnp.uint32),
        jax.ShapeDtypeStruct((K_PAD, N_TRUNK), jnp.int32),
        jax.ShapeDtypeStruct((K_PAD, N_TRUNK), jnp.int32),
        jax.ShapeDtypeStruct((N_TRUNK, 128), F32),
        jax.ShapeDtypeStruct((N_EXPERTS, 128), jnp.int32),
    ]
    return in_specs, out_specs, out_shapes, [pltpu.VMEM((N_EXPERTS, 128), F32)]


def _finalize_body(cnt_ref, eidx_ref, rank_ref, pos_ref, te_ref, nt_ref):
    pos_ref[...] = rank_ref[...]

    def per_expert(e, carry):
        off, t = carry
        n_tile = lax.div(cnt_ref[e] + (MOE_TILE - 1), MOE_TILE)
        pos_ref[...] = pos_ref[...] + jnp.where(eidx_ref[...] == e, off, 0)

        def fill(j, c):
            te_ref[t + j] = e
            return c
        lax.fori_loop(0, n_tile, fill, 0)
        return off + n_tile * MOE_TILE, t + n_tile

    _, n_used = lax.fori_loop(0, N_EXPERTS, per_expert, (jnp.int32(0), jnp.int32(0)))
    nt_ref[0] = n_used

    def fill_tail(j, c):
        te_ref[j] = 0
        return c
    lax.fori_loop(n_used, N_TILE, fill_tail, 0)


def _finalize(counts, eidx, rank):
    smem = pl.BlockSpec(memory_space=pltpu.SMEM)
    full = pl.BlockSpec((K_PAD, N_TRUNK), lambda: (0, 0))
    return pl.pallas_call(
        _finalize_body,
        in_specs=[smem, full, full],
        out_specs=[full, smem, smem],
        out_shape=[
            jax.ShapeDtypeStruct((K_PAD, N_TRUNK), jnp.int32),
            jax.ShapeDtypeStruct((N_TILE,), jnp.int32),
            jax.ShapeDtypeStruct((1,), jnp.int32),
        ],
        compiler_params=pltpu.CompilerParams(vmem_limit_bytes=VMEM_LIMIT),
        name="route_finalize",
    )(counts, eidx, rank)


def _ffn(h, wgu, wd):
    gu = jnp.dot(h, wgu, preferred_element_type=F32)
    hid = _silu(gu[:, :EXPERT_FF]) * gu[:, EXPERT_FF:]
    return jnp.dot(hid.astype(BF16), wd, preferred_element_type=F32)


def _cast_ffn_weights(wg_ref, wu_ref, wd_ref, wgub_ref, wdb_ref):
    wgub_ref[:, :EXPERT_FF] = wg_ref[...].astype(BF16)
    wgub_ref[:, EXPERT_FF:] = wu_ref[...].astype(BF16)
    wdb_ref[...] = wd_ref[...].astype(BF16)


_FFN_WEIGHT_SCRATCH = [pltpu.VMEM((D_MODEL, 2 * EXPERT_FF), BF16), pltpu.VMEM((EXPERT_FF, D_MODEL), BF16)]


def _sc_mesh():
    from jax.experimental.pallas import tpu_sc as plsc
    return plsc.VectorSubcoreMesh(core_axis_name="c", subcore_axis_name="s",
                                  num_cores=SC_CORES, num_subcores=SC_SUBCORES)


def _sc_worker_id():
    return lax.axis_index("s") * SC_CORES + lax.axis_index("c")


def _sc_dispatch(hp, pos_flat):
    win = SC_SCATTER_WIN
    per_worker = N_TRUNK // SC_WORKERS

    def body(rows_hbm, idx_hbm, out_hbm, *scratch):
        idx_v, rows_v, sem = scratch[:TOP_K], scratch[TOP_K], scratch[TOP_K + 1]
        base = _sc_worker_id() * per_worker

        @pl.loop(0, per_worker // win)
        def _(j):
            off = base + j * win
            pltpu.sync_copy(rows_hbm.at[pl.ds(off, win)], rows_v)
            for k in range(TOP_K):
                pltpu.sync_copy(idx_hbm.at[pl.ds(k * N_TRUNK + off, win)], idx_v[k])
            copies = [pltpu.async_copy(rows_v, out_hbm.at[idx_v[k]], sem) for k in range(TOP_K)]
            for cp in copies:
                cp.wait()

    return pl.kernel(
        body, mesh=_sc_mesh(),
        out_type=jax.ShapeDtypeStruct((N_SLOT, HALF), jnp.uint32),
        scratch_types=[pltpu.VMEM((win,), jnp.int32)] * TOP_K
        + [pltpu.VMEM((win, HALF), jnp.uint32), pltpu.SemaphoreType.DMA],
        name="sc_dispatch",
    )(hp, pos_flat)


def _sc_return(ys, pos_flat):
    win = SC_GATHER_WIN
    per_worker = N_PAIR // SC_WORKERS

    def body(src_hbm, idx_hbm, out_hbm, idx_v, rows_v, sem):
        base = _sc_worker_id() * per_worker

        @pl.loop(0, per_worker // win)
        def _(j):
            off = base + j * win
            pltpu.sync_copy(idx_hbm.at[pl.ds(off, win)], idx_v)
            pltpu.async_copy(src_hbm.at[idx_v], rows_v, sem).wait()
            pltpu.sync_copy(rows_v, out_hbm.at[pl.ds(off, win)])

    return pl.kernel(
        body, mesh=_sc_mesh(),
        out_type=jax.ShapeDtypeStruct((N_PAIR, HALF), jnp.uint32),
        scratch_types=[pltpu.VMEM((win,), jnp.int32), pltpu.VMEM((win, HALF), jnp.uint32),
                       pltpu.SemaphoreType.DMA],
        name="sc_return",
    )(ys, pos_flat)


N_PLAN = 2 * N_TILE


def _tile_plan_body(cnt0_ref, cnt1_ref, exp_ref, blk0_ref, blk1_ref, live_ref, n_ref):
    def per_expert(e, carry):
        s, b0, b1 = carry
        n0 = lax.div(cnt0_ref[e] + (MOE_TILE - 1), MOE_TILE)
        n1 = lax.div(cnt1_ref[e] + (MOE_TILE - 1), MOE_TILE)

        def fill(j, c):
            exp_ref[s + j] = e
            blk0_ref[s + j] = jnp.maximum(b0 + jnp.minimum(j, n0 - 1), 0)
            blk1_ref[s + j] = jnp.maximum(b1 + jnp.minimum(j, n1 - 1), 0)
            live_ref[s + j] = jnp.where(j < n0, 1, 0) + jnp.where(j < n1, 2, 0)
            return c
        n = jnp.maximum(n0, n1)
        lax.fori_loop(0, n, fill, 0)
        return s + n, b0 + n0, b1 + n1

    zero = jnp.int32(0)
    n_steps, _, _ = lax.fori_loop(0, N_EXPERTS, per_expert, (zero, zero, zero))
    n_ref[0] = n_steps

    def fill_tail(s, c):
        exp_ref[s] = 0
        blk0_ref[s] = 0
        blk1_ref[s] = 0
        live_ref[s] = 0
        return c
    lax.fori_loop(n_steps, N_PLAN, fill_tail, 0)


def _tile_plan(counts0, counts1):
    smem = pl.BlockSpec(memory_space=pltpu.SMEM)
    steps = jax.ShapeDtypeStruct((N_PLAN,), jnp.int32)
    return pl.pallas_call(
        _tile_plan_body,
        in_specs=[smem, smem],
        out_specs=[smem] * 5,
        out_shape=[steps, steps, steps, steps, jax.ShapeDtypeStruct((1,), jnp.int32)],
        name="tile_plan",
    )(counts0, counts1)


def _expert_body(exp_ref, blk0_ref, blk1_ref, live_ref, xs0_ref, xs1_ref, wg_ref, wu_ref, wd_ref,
                 ys0_ref, ys1_ref, wgub_ref, wdb_ref):
    i = pl.program_id(0)

    @pl.when(jnp.logical_or(i == 0, exp_ref[i] != exp_ref[jnp.maximum(i - 1, 0)]))
    def _():
        _cast_ffn_weights(wg_ref, wu_ref, wd_ref, wgub_ref, wdb_ref)

    for bit, xs_ref, ys_ref in ((1, xs0_ref, ys0_ref), (2, xs1_ref, ys1_ref)):
        @pl.when((live_ref[i] & bit) != 0)
        def _():
            lo, hi = _unpack_bf16_pair(xs_ref[...])
            h = jnp.concatenate([lo.astype(BF16), hi.astype(BF16)], axis=1)
            y = _ffn(h, wgub_ref[...], wdb_ref[...])
            ys_ref[...] = _pack_bf16_pair(y[:, :HALF], y[:, HALF:])


def _expert_ffn(plan, xs0, xs1, layer, w_gate, w_up, w_down):
    exp, blk0, blk1, live, n_steps = plan
    tile0 = lambda i, exp, blk0, blk1, live: (blk0[i], 0)
    tile1 = lambda i, exp, blk0, blk1, live: (blk1[i], 0)
    ew = lambda shape: pl.BlockSpec((None, None) + shape,
                                    lambda i, exp, blk0, blk1, live: (layer, exp[i], 0, 0))
    slots = jax.ShapeDtypeStruct((N_SLOT, HALF), jnp.uint32)
    return pl.pallas_call(
        _expert_body,
        grid_spec=pltpu.PrefetchScalarGridSpec(
            num_scalar_prefetch=4,
            grid=(n_steps[0],),
            in_specs=[
                pl.BlockSpec((MOE_TILE, HALF), tile0), pl.BlockSpec((MOE_TILE, HALF), tile1),
                ew((D_MODEL, EXPERT_FF)), ew((D_MODEL, EXPERT_FF)), ew((EXPERT_FF, D_MODEL)),
            ],
            out_specs=[pl.BlockSpec((MOE_TILE, HALF), tile0), pl.BlockSpec((MOE_TILE, HALF), tile1)],
            scratch_shapes=_FFN_WEIGHT_SCRATCH,
        ),
        out_shape=[slots, slots],
        compiler_params=pltpu.CompilerParams(
            dimension_semantics=("arbitrary",), vmem_limit_bytes=VMEM_LIMIT),
        name="expert_ffn",
    )(exp, blk0, blk1, live, xs0, xs1, w_gate, w_up, w_down)


def _combine_body(x_ref, sh_ref, sc_ref, g2_ref, yk_ref, wt_ref, sg_ref, su_ref, sd_ref, lng_ref, lnb_ref,
                  o_ref, sgub_ref, sdb_ref):
    @pl.when(pl.program_id(0) == 0)
    def _():
        _cast_ffn_weights(sg_ref, su_ref, sd_ref, sgub_ref, sdb_ref)

    x = x_ref[...].astype(F32)
    h = (x * (1.0 + sc_ref[...]) + sh_ref[...]).astype(BF16)
    y = _ffn(h, sgub_ref[...], sdb_ref[...])
    wt = wt_ref[...]
    lo_acc = jnp.zeros((x.shape[0], HALF), F32)
    hi_acc = jnp.zeros((x.shape[0], HALF), F32)
    for k in range(TOP_K):
        lo, hi = _unpack_bf16_pair(yk_ref[k])
        w = wt[:, k:k + 1]
        lo_acc = lo_acc + w * lo
        hi_acc = hi_acc + w * hi
    y = y + jnp.concatenate([lo_acc, hi_acc], axis=1)
    z = DEEPNORM_ALPHA * x + g2_ref[...] * y
    o_ref[...] = _layer_norm(z, lng_ref[...], lnb_ref[...]).astype(o_ref.dtype)


def _combine(trunk, x, mod, yk, wt, layer, s_gate, s_up, s_down, ln_g, ln_b, out_dtype):
    tm = TOK_BLOCK
    first_block = trunk * (N_TRUNK // tm)
    row = lambda i: (i, 0)
    vec = pl.BlockSpec((1, D_MODEL), lambda i: (0, 0))
    sw = lambda shape: pl.BlockSpec((None,) + shape, lambda i: (layer, 0, 0))
    mod_spec = lambda col: _mod_spec(tm, col, 1, 0, first_block)
    return pl.pallas_call(
        _combine_body,
        grid=(N_TRUNK // tm,),
        in_specs=[
            pl.BlockSpec((tm, D_MODEL), row),
            mod_spec(3), mod_spec(4), mod_spec(5),
            pl.BlockSpec((TOP_K, tm, HALF), lambda i: (0, i, 0)),
            pl.BlockSpec((tm, 128), row),
            sw((D_MODEL, EXPERT_FF)), sw((D_MODEL, EXPERT_FF)), sw((EXPERT_FF, D_MODEL)),
            vec, vec,
        ],
        out_specs=pl.BlockSpec((tm, D_MODEL), lambda i: (i, 0)),
        out_shape=jax.ShapeDtypeStruct((N_TRUNK, D_MODEL), out_dtype),
        scratch_shapes=_FFN_WEIGHT_SCRATCH,
        compiler_params=pltpu.CompilerParams(
            dimension_semantics=("arbitrary",), vmem_limit_bytes=VMEM_LIMIT),
        name="moe_combine_ln",
    )(x, mod, mod, mod, yk, wt, s_gate, s_up, s_down, ln_g, ln_b)


def _moe_dispatch(routing):
    hp, eidx, rank, wt, counts = routing
    counts = counts[:, 0]
    pos, _, _ = _finalize(counts, eidx, rank)
    pos_flat = pos.reshape(K_PAD * N_TRUNK)
    return _sc_dispatch(hp, pos_flat), pos_flat, counts, wt


def _moe_experts(plans, layer, w_gate, w_up, w_down):
    (xs0, pos0, counts0, _), (xs1, pos1, counts1, _) = plans
    ys = _expert_ffn(_tile_plan(counts0, counts1), xs0, xs1, layer, w_gate, w_up, w_down)
    return [_sc_return(y, pos).reshape(TOP_K, N_TRUNK, HALF) for y, pos in zip(ys, (pos0, pos1))]


def kernel(x_prompt, x_sample, state_ret, c, c_ctx, ada_w, ada_b, ln_g, ln_b, ret_w_in, ret_w_out, ret_decay, conv_w_in, conv_w, conv_w_out, moe_router, moe_bias, moe_w_gate, moe_w_up, moe_w_down, shared_w_gate, shared_w_up, shared_w_down):
    x = (x_prompt.reshape(N_CTX, D_MODEL), x_sample.reshape(N_LAT, D_MODEL))
    cond = jnp.concatenate(
        [c_ctx[None, :], c, jnp.zeros((N_COND - 1 - DEC_BATCH, D_MODEL), F32)], axis=0)
    mods = _ada_table(cond, ada_w, ada_b).reshape(DEPTH, N_COND, 1, 6 * D_MODEL)
    cos, sin = _rope_tables()
    router_wt = jnp.swapaxes(moe_router, 1, 2)

    states = None
    for i in range(DEPTH):
        j = i // 2
        mod = mods[i]
        lng = ln_g[i].reshape(2, 1, D_MODEL)
        lnb = ln_b[i].reshape(2, 1, D_MODEL)
        bias_col = moe_bias[i].reshape(N_EXPERTS, 1)
        if i % 2 == 0:
            w_in_bf = _to_bf16(ret_w_in, j)
        mixed, plans = [], []
        for trunk in range(2):
            if i % 2 == 0:
                if trunk == 0:
                    a, states = _retention_ctx(x[0], mod, w_in_bf, ret_decay[j], j, states)
                else:
                    a = _retention_lat(x[1], mod, w_in_bf, ret_decay[j], state_ret, j, cos, sin)
                xm, *routing = _ret_out(trunk, a, x[trunk], mod, ret_w_out, j, lng[0], lnb[0],
                                        router_wt[i], bias_col)
            else:
                xm, *routing = _conv_layer(trunk, x[trunk], mod, conv_w_in, conv_w_out, j, conv_w[j],
                                           lng[0], lnb[0], router_wt[i], bias_col)
            mixed.append(xm)
            plans.append(_moe_dispatch(routing))
        yks = _moe_experts(plans, i, moe_w_gate, moe_w_up, moe_w_down)
        out_dtype = F32 if i == DEPTH - 1 else RES_DTYPE
        x = tuple(_combine(trunk, mixed[trunk], mod, yks[trunk], plans[trunk][3], i, shared_w_gate, shared_w_up,
                           shared_w_down, lng[1], lnb[1], out_dtype) for trunk in range(2))

    y_prompt = x[0].reshape(BATCH, SEQ, D_MODEL)
    y_sample = x[1].reshape(DEC_BATCH, DEC_SEQ, D_MODEL)
    return y_prompt, y_sample, states
```

```python
import functools

import jax
import jax.numpy as jnp
from jax import lax
from jax.experimental import pallas as pl
from jax.experimental.pallas import tpu as pltpu

F32 = jnp.float32
BF16 = jnp.bfloat16

D_MODEL = 1024
BATCH = 32
SEQ = 256
DEPTH = 4
DEC_BATCH = 8
DEC_SEQ = 1024
GRID_W = 64
RET_HEADS = 4
RET_DK = D_MODEL // RET_HEADS
RET_DV = 2 * D_MODEL // RET_HEADS
ROPE_BASE = 10000.0
N_EXPERTS = 64
TOP_K = 6
N_GROUPS = 8
TOPK_GROUPS = 4
EXPERT_FF = 256
ROUTED_SCALE = 2.5
LN_EPS = 1e-5
DEEPNORM_ALPHA = (2.0 * DEPTH) ** 0.25

N_CTX = BATCH * SEQ
N_LAT = DEC_BATCH * DEC_SEQ
N_COND = 16
RET_CHUNK = 256
assert N_CTX == N_LAT
N_TRUNK = N_CTX
TOK_BLOCK = 512
MIX_BLOCK = 512
VMEM_LIMIT = 56 * 1024 * 1024
RES_DTYPE = BF16

HALF = D_MODEL // 2
MOE_TILE = 896
N_PAIR = N_TRUNK * TOP_K
N_TILE = -(-(N_PAIR + N_EXPERTS * (MOE_TILE - 1)) // MOE_TILE)
N_SLOT = N_TILE * MOE_TILE
K_PAD = 8

SC_CORES = 2
SC_SUBCORES = 16
SC_WORKERS = SC_CORES * SC_SUBCORES
SC_SCATTER_WIN = 64
SC_GATHER_WIN = 128


def _cond_row(tok_block_idx, tok_block):
    t0 = tok_block_idx * tok_block
    return jnp.where(t0 < N_CTX, 0, 1 + (t0 - N_CTX) // DEC_SEQ)


def _silu(x):
    return x * jax.nn.sigmoid(x)


def _layer_norm(z, g, b):
    mu = jnp.mean(z, axis=-1, keepdims=True)
    zc = z - mu
    var = jnp.mean(zc * zc, axis=-1, keepdims=True)
    return zc * lax.rsqrt(var + LN_EPS) * g + b


def _ada_body(cond_ref, w_ref, b_ref, o_ref):
    s = _silu(cond_ref[...]).astype(BF16)
    o_ref[...] = jnp.dot(s, w_ref[...].astype(BF16), preferred_element_type=F32) + b_ref[...]


def _ada_table(cond, ada_w, ada_b):
    tn = 2048
    return pl.pallas_call(
        _ada_body,
        grid=(DEPTH, 6 * D_MODEL // tn),
        in_specs=[
            pl.BlockSpec((N_COND, D_MODEL), lambda l, j: (0, 0)),
            pl.BlockSpec((None, D_MODEL, tn), lambda l, j: (l, 0, j)),
            pl.BlockSpec((None, 1, tn), lambda l, j: (l, 0, j)),
        ],
        out_specs=pl.BlockSpec((None, N_COND, tn), lambda l, j: (l, 0, j)),
        out_shape=jax.ShapeDtypeStruct((DEPTH, N_COND, 6 * D_MODEL), F32),
        compiler_params=pltpu.CompilerParams(
            dimension_semantics=("arbitrary", "arbitrary"), vmem_limit_bytes=VMEM_LIMIT),
        name="ada_table",
    )(cond, ada_w, ada_b.reshape(DEPTH, 1, 6 * D_MODEL))


def _mod_spec(tok_block, col, first_block=0):
    def index_map(*idx):
        return (_cond_row(first_block + idx[0], tok_block), 0, col)
    return pl.BlockSpec((None, 1, D_MODEL), index_map)


def _to_bf16_body(w_ref, o_ref):
    o_ref[...] = w_ref[...].astype(BF16)


def _to_bf16(w_all, w_idx):
    k_dim, n_out = w_all.shape[1:]
    tn = 1024
    return pl.pallas_call(
        _to_bf16_body,
        grid=(n_out // tn,),
        in_specs=[pl.BlockSpec((None, k_dim, tn), lambda j: (w_idx, 0, j))],
        out_specs=pl.BlockSpec((k_dim, tn), lambda j: (0, j)),
        out_shape=jax.ShapeDtypeStruct((k_dim, n_out), BF16),
        compiler_params=pltpu.CompilerParams(
            dimension_semantics=("arbitrary",), vmem_limit_bytes=VMEM_LIMIT),
        name="weight_to_bf16",
    )(w_all)


RET_GROUP = 1024


def _head_proj_specs():
    return [
        pl.BlockSpec((D_MODEL, RET_DK), lambda t, h: (0, h)),
        pl.BlockSpec((D_MODEL, RET_DK), lambda t, h: (0, RET_HEADS + h)),
        pl.BlockSpec((D_MODEL, RET_DV), lambda t, h: (0, RET_HEADS + h)),
        pl.BlockSpec((D_MODEL, RET_DV), lambda t, h: (0, 2 * RET_HEADS + h)),
    ]


def _head_proj(x_ref, sh_ref, sc_ref, wq_ref, wk_ref, wv_ref, wg_ref):
    h = (x_ref[...].astype(F32) * (1.0 + sc_ref[...]) + sh_ref[...]).astype(BF16)
    dot = lambda w_ref: jnp.dot(h, w_ref[...], preferred_element_type=F32)
    return dot(wq_ref), dot(wk_ref), dot(wv_ref).astype(BF16), dot(wg_ref)

def _log_sigmoid(v):
    return jnp.minimum(v, 0.0) - jnp.log1p(jnp.exp(-jnp.abs(v)))


def _decay_tables(dec_ref, head):
    c = RET_CHUNK
    lgf = _log_sigmoid(jnp.full((c, c), dec_ref[0, head], F32))
    lgb = _log_sigmoid(jnp.full((c, c), dec_ref[1, head], F32))
    row = lax.broadcasted_iota(jnp.int32, (c, c), 0).astype(F32)
    col = lax.broadcasted_iota(jnp.int32, (c, c), 1).astype(F32)
    diff = row - col
    kscale = RET_DK ** -0.5
    intra = jnp.where(diff > 0, jnp.exp(lgf * diff),
                      jnp.where(diff < 0, jnp.exp(-lgb * diff), 2.0)) * kscale
    qdec_f = jnp.exp(lgf * (row + 1.0))
    qdec_b = jnp.exp(lgb * (c - row))
    kdec_f = jnp.exp(lgf * (c - 1.0 - row)) * kscale
    kdec_b = jnp.exp(lgb * row) * kscale
    cdec_f = jnp.exp(lgf * c)
    cdec_b = jnp.exp(lgb * c)
    return intra, qdec_f, qdec_b, kdec_f, kdec_b, cdec_f, cdec_b


def _head_norm_gate(o, g):
    mu = jnp.mean(o, axis=-1, keepdims=True)
    oc = o - mu
    var = jnp.mean(oc * oc, axis=-1, keepdims=True)
    on = oc * lax.rsqrt(var + LN_EPS)
    return (_silu(g.astype(F32)) * on).astype(BF16)


_NT = (((1,), (1,)), ((), ()))
_TN = (((0,), (0,)), ((), ()))


def _ret_ctx_body(dec_ref, x_ref, sh_ref, sc_ref, wq_ref, wk_ref, wv_ref, wg_ref, *rest):
    a_ref, st_ref, tab_ref = rest[-3:]
    head = pl.program_id(1)

    @pl.when(pl.program_id(0) == 0)
    def _():
        intra, _, _, kdec_f, kdec_b, _, _ = _decay_tables(dec_ref, head)
        tab_ref[head, 0] = intra
        tab_ref[head, 1] = kdec_f
        tab_ref[head, 2] = kdec_b

    q, k, v, g = _head_proj(x_ref, sh_ref, sc_ref, wq_ref, wk_ref, wv_ref, wg_ref)
    for s in range(RET_GROUP // SEQ):
        rows = slice(s * SEQ, (s + 1) * SEQ)
        scores = lax.dot_general(q[rows].astype(BF16), k[rows].astype(BF16), _NT, preferred_element_type=F32)
        p = (scores * tab_ref[head, 0]).astype(BF16)
        o = jnp.dot(p, v[rows], preferred_element_type=F32)
        a_ref[rows, :] = _head_norm_gate(o, g[rows])
        st_ref[s, 0] = lax.dot_general((k[rows] * tab_ref[head, 1]).astype(BF16), v[rows], _TN,
                                       preferred_element_type=F32)
        st_ref[s, 1] = lax.dot_general((k[rows] * tab_ref[head, 2]).astype(BF16), v[rows], _TN,
                                       preferred_element_type=F32)


def _retention_ctx(x, mod, w_in_bf, decay, ret_idx, states):
    assert SEQ == RET_CHUNK
    seqs = RET_GROUP // SEQ
    n_ret = (DEPTH + 1) // 2
    st = jax.ShapeDtypeStruct((BATCH, n_ret, 2, RET_HEADS, RET_DK, RET_DV), F32)
    st_spec = pl.BlockSpec((seqs, None, 2, None, RET_DK, RET_DV), lambda t, h: (t, ret_idx, 0, h, 0, 0))
    in_specs = [
        pl.BlockSpec(memory_space=pltpu.SMEM),
        pl.BlockSpec((RET_GROUP, D_MODEL), lambda t, h: (t, 0)),
        _mod_spec(RET_GROUP, 0), _mod_spec(RET_GROUP, 1),
    ] + _head_proj_specs()
    args = (decay, x, mod, mod, w_in_bf, w_in_bf, w_in_bf, w_in_bf)
    aliases = {}
    if states is not None:
        in_specs.append(pl.BlockSpec(memory_space=pl.ANY))
        aliases = {len(args): 1}
        args += (states,)
    return pl.pallas_call(
        _ret_ctx_body,
        grid=(N_CTX // RET_GROUP, RET_HEADS),
        in_specs=in_specs,
        out_specs=[pl.BlockSpec((RET_GROUP, RET_DV), lambda t, h: (t, h)), st_spec],
        out_shape=[jax.ShapeDtypeStruct((N_CTX, RET_HEADS * RET_DV), BF16), st],
        input_output_aliases=aliases,
        scratch_shapes=[pltpu.VMEM((RET_HEADS, 3, RET_CHUNK, RET_CHUNK), F32)],
        compiler_params=pltpu.CompilerParams(
            dimension_semantics=("arbitrary", "arbitrary"), vmem_limit_bytes=VMEM_LIMIT),
        name="retention_ctx",
    )(*args)


def _rope(x, cos, sin):
    halves = [pltpu.roll(x[:, s:s + 128], 64, axis=1) for s in (0, 128)]
    return x * cos + jnp.concatenate(halves, axis=1) * sin


def _ret_lat_body(dec_ref, x_ref, sh_ref, sc_ref, wq_ref, wk_ref, wv_ref, wg_ref,
                  s0f_ref, s0b_ref, cos_ref, sin_ref,
                  a_ref, qr_ref, kr_ref, v_ref, g_ref, o_ref, st_ref):
    head = pl.program_id(1)
    c = RET_CHUNK
    nc = DEC_SEQ // c
    intra, qdec_f, qdec_b, kdec_f, kdec_b, cdec_f, cdec_b = _decay_tables(dec_ref, head)
    cdec_f = jnp.concatenate([cdec_f, cdec_f], axis=1)
    cdec_b = jnp.concatenate([cdec_b, cdec_b], axis=1)

    q, k, v, g = _head_proj(x_ref, sh_ref, sc_ref, wq_ref, wk_ref, wv_ref, wg_ref)
    qr_ref[...] = _rope(q, cos_ref[...], sin_ref[...])
    kr_ref[...] = _rope(k, cos_ref[...], sin_ref[...])
    v_ref[...] = v
    g_ref[...] = g

    st_ref[...] = s0f_ref[...]
    for ci in range(nc):
        rows = pl.ds(ci * c, c)
        q = qr_ref[rows, :]
        k = kr_ref[rows, :]
        v = v_ref[rows, :]
        scores = lax.dot_general(q.astype(BF16), k.astype(BF16), _NT, preferred_element_type=F32)
        p = (scores * intra).astype(BF16)
        o = jnp.dot(p, v, preferred_element_type=F32)
        qd = (q * qdec_f).astype(BF16)
        o = o + jnp.dot(qd, st_ref[...].astype(BF16), preferred_element_type=F32)
        o_ref[rows, :] = o
        if ci + 1 < nc:
            kd = (k * kdec_f).astype(BF16)
            st_ref[...] = cdec_f * st_ref[...] + lax.dot_general(kd, v, _TN, preferred_element_type=F32)

    st_ref[...] = s0b_ref[...]
    for ci in reversed(range(nc)):
        rows = pl.ds(ci * c, c)
        qd = (qr_ref[rows, :] * qdec_b).astype(BF16)
        o = o_ref[rows, :] + jnp.dot(qd, st_ref[...].astype(BF16), preferred_element_type=F32)
        a_ref[rows, :] = _head_norm_gate(o, g_ref[rows, :])
        if ci > 0:
            kd = (kr_ref[rows, :] * kdec_b).astype(BF16)
            st_ref[...] = cdec_b * st_ref[...] + lax.dot_general(kd, v_ref[rows, :], _TN,
                                                                 preferred_element_type=F32)


def _rope_tables():
    half = RET_DK // 4
    freqs = ROPE_BASE ** (-jnp.arange(half, dtype=F32) / half)
    t = jnp.arange(DEC_SEQ)
    row = (t // GRID_W).astype(F32)
    col = (t % GRID_W).astype(F32)
    ang_r = row[:, None] * freqs[None, :]
    ang_c = col[:, None] * freqs[None, :]
    cos = jnp.concatenate([jnp.cos(ang_r)] * 2 + [jnp.cos(ang_c)] * 2, axis=1)
    sin = jnp.concatenate([-jnp.sin(ang_r), jnp.sin(ang_r), -jnp.sin(ang_c), jnp.sin(ang_c)], axis=1)
    return cos, sin


def _retention_lat(x, mod, w_in_bf, decay, state_ret, ret_idx, cos, sin):
    n = DEC_SEQ
    assert n == RET_GROUP
    s0_spec = lambda d: pl.BlockSpec((None, None, None, None, RET_DK, RET_DV),
                                     lambda b, h: (b, ret_idx, d, h, 0, 0))
    tab_spec = pl.BlockSpec((n, RET_DK), lambda b, h: (0, 0))
    first_block = N_CTX // n
    return pl.pallas_call(
        _ret_lat_body,
        grid=(DEC_BATCH, RET_HEADS),
        in_specs=[
            pl.BlockSpec(memory_space=pltpu.SMEM),
            pl.BlockSpec((n, D_MODEL), lambda b, h: (b, 0)),
            _mod_spec(n, 0, first_block), _mod_spec(n, 1, first_block),
        ] + _head_proj_specs() + [s0_spec(0), s0_spec(1), tab_spec, tab_spec],
        out_specs=pl.BlockSpec((n, RET_DV), lambda b, h: (b, h)),
        out_shape=jax.ShapeDtypeStruct((N_LAT, RET_HEADS * RET_DV), BF16),
        scratch_shapes=[
            pltpu.VMEM((n, RET_DK), F32), pltpu.VMEM((n, RET_DK), F32),
            pltpu.VMEM((n, RET_DV), BF16), pltpu.VMEM((n, RET_DV), F32),
            pltpu.VMEM((n, RET_DV), F32), pltpu.VMEM((RET_DK, RET_DV), F32),
        ],
        compiler_params=pltpu.CompilerParams(
            dimension_semantics=("arbitrary", "arbitrary"), vmem_limit_bytes=VMEM_LIMIT),
        name="retention_lat",
    )(decay, x, mod, mod, w_in_bf, w_in_bf, w_in_bf, w_in_bf, state_ret, state_ret, cos, sin)


N_ROUTE_IN = 4
N_ROUTE_OUT = 5


def _split_route_refs(rest, n_scratch):
    route_in = rest[:N_ROUTE_IN]
    o_ref = rest[N_ROUTE_IN]
    route_out = rest[N_ROUTE_IN + 1:N_ROUTE_IN + 1 + N_ROUTE_OUT]
    return route_in, o_ref, route_out, rest[len(rest) - n_scratch:]


def _ret_out_body(a_ref, x_ref, g1_ref, w_ref, lng_ref, lnb_ref, *rest):
    route_in, o_ref, route_out, (wbf_ref, carry_ref) = _split_route_refs(rest, 2)

    @pl.when(pl.program_id(0) == 0)
    def _():
        wbf_ref[...] = w_ref[...].astype(BF16)
        carry_ref[...] = jnp.zeros_like(carry_ref)

    y = jnp.dot(a_ref[...], wbf_ref[...], preferred_element_type=F32)
    z = DEEPNORM_ALPHA * x_ref[...].astype(F32) + g1_ref[...] * y
    xn = _layer_norm(z, lng_ref[...], lnb_ref[...])
    o_ref[...] = xn.astype(o_ref.dtype)
    _route_block(xn, *route_in, *route_out, carry_ref)


def _ret_out(trunk, a, x, mod, w_all, w_idx, ln_g, ln_b, router_wt, bias_col):
    tm = MIX_BLOCK
    k_dim = w_all.shape[1]
    first_block = trunk * (N_TRUNK // tm)
    row = lambda i: (i, 0)
    vec = pl.BlockSpec((1, D_MODEL), lambda i: (0, 0))
    r_in, r_out, r_shapes, r_scratch = _route_io(tm, first_block)
    return pl.pallas_call(
        _ret_out_body,
        grid=(N_TRUNK // tm,),
        in_specs=[
            pl.BlockSpec((tm, k_dim), row),
            pl.BlockSpec((tm, D_MODEL), row),
            _mod_spec(tm, 2, first_block),
            pl.BlockSpec((None, k_dim, D_MODEL), lambda i: (w_idx, 0, 0)),
            vec, vec,
        ] + r_in,
        out_specs=[pl.BlockSpec((tm, D_MODEL), row)] + r_out,
        out_shape=[jax.ShapeDtypeStruct((N_TRUNK, D_MODEL), RES_DTYPE)] + r_shapes,
        scratch_shapes=[pltpu.VMEM((k_dim, D_MODEL), BF16)] + r_scratch,
        compiler_params=pltpu.CompilerParams(
            dimension_semantics=("arbitrary",), vmem_limit_bytes=VMEM_LIMIT),
        name="ret_out_ln_route",
    )(a, x, mod, w_all, ln_g, ln_b, mod, mod, router_wt, bias_col)


def _conv_layer_body(seg, x_ref, sh_ref, sc_ref, g1_ref, win_ref, wout_ref, cw_ref, lng_ref, lnb_ref, *rest):
    route_in, o_ref, route_out, (winb_ref, woutb_ref, carry_ref) = _split_route_refs(rest, 3)
    tm = o_ref.shape[0]

    @pl.when(pl.program_id(0) == 0)
    def _():
        winb_ref[...] = win_ref[...].astype(BF16)
        woutb_ref[...] = wout_ref[...].astype(BF16)
        carry_ref[...] = jnp.zeros_like(carry_ref)

    x = x_ref[...].astype(F32)
    h = (x * (1.0 + sc_ref[...]) + sh_ref[...]).astype(BF16)
    proj = jnp.dot(h, winb_ref[...], preferred_element_type=F32)
    bg, cg, xt = (proj[:, c * D_MODEL:(c + 1) * D_MODEL] for c in range(3))
    u = cg * xt
    pos = lax.broadcasted_iota(jnp.int32, (tm, D_MODEL), 0) & (seg - 1)
    u_prev = jnp.where(pos == 0, 0.0, pltpu.roll(u, 1, axis=0))
    u_next = jnp.where(pos == seg - 1, 0.0, pltpu.roll(u, tm - 1, axis=0))
    cu = u_prev * cw_ref[0:1, :] + u * cw_ref[1:2, :] + u_next * cw_ref[2:3, :]
    y = jnp.dot((bg * cu).astype(BF16), woutb_ref[...], preferred_element_type=F32)
    z = DEEPNORM_ALPHA * x + g1_ref[...] * y
    xn = _layer_norm(z, lng_ref[...], lnb_ref[...])
    o_ref[...] = xn.astype(o_ref.dtype)
    _route_block(xn, *route_in, *route_out, carry_ref)


def _conv_layer(trunk, x, mod, w_in_all, w_out_all, w_idx, conv_w, ln_g, ln_b, router_wt, bias_col):
    tm = MIX_BLOCK
    first_block = trunk * (N_TRUNK // tm)
    seg = SEQ if trunk == 0 else GRID_W
    assert tm % seg == 0 and seg & (seg - 1) == 0
    row = lambda i: (i, 0)
    vec = pl.BlockSpec((1, D_MODEL), lambda i: (0, 0))
    once = pl.Buffered(1)
    r_in, r_out, r_shapes, r_scratch = _route_io(tm, first_block)
    mod_spec = lambda col: _mod_spec(tm, col, first_block)
    return pl.pallas_call(
        functools.partial(_conv_layer_body, seg),
        grid=(N_TRUNK // tm,),
        in_specs=[
            pl.BlockSpec((tm, D_MODEL), row),
            mod_spec(0), mod_spec(1), mod_spec(2),
            pl.BlockSpec((None, D_MODEL, 3 * D_MODEL), lambda i: (w_idx, 0, 0), pipeline_mode=once),
            pl.BlockSpec((None, D_MODEL, D_MODEL), lambda i: (w_idx, 0, 0), pipeline_mode=once),
            pl.BlockSpec((3, D_MODEL), lambda i: (0, 0)),
            vec, vec,
        ] + r_in,
        out_specs=[pl.BlockSpec((tm, D_MODEL), row)] + r_out,
        out_shape=[jax.ShapeDtypeStruct((N_TRUNK, D_MODEL), RES_DTYPE)] + r_shapes,
        scratch_shapes=[pltpu.VMEM((D_MODEL, 3 * D_MODEL), BF16), pltpu.VMEM((D_MODEL, D_MODEL), BF16)]
        + r_scratch,
        compiler_params=pltpu.CompilerParams(
            dimension_semantics=("arbitrary",), vmem_limit_bytes=VMEM_LIMIT),
        name="conv_layer_route",
    )(x, mod, mod, mod, w_in_all, w_out_all, conv_w, ln_g, ln_b, mod, mod, router_wt, bias_col)


def _split_bf16(v):
    hi = v.astype(BF16)
    lo = (v - hi.astype(F32)).astype(BF16)
    return hi, lo


def _pack_bf16_pair(lo_f32, hi_f32):
    lo = lax.bitcast_convert_type(lo_f32.astype(BF16).astype(F32), jnp.uint32) >> 16
    hi = lax.bitcast_convert_type(hi_f32.astype(BF16).astype(F32), jnp.uint32) & jnp.uint32(0xFFFF0000)
    return hi | lo


def _unpack_bf16_pair(u):
    lo = lax.bitcast_convert_type(u << 16, F32)
    hi = lax.bitcast_convert_type(u & jnp.uint32(0xFFFF0000), F32)
    return lo, hi


def _rows_to_tile(rows, n_sub, dtype):
    tm = rows[0].shape[1]
    sub = lax.broadcasted_iota(jnp.int32, (n_sub, tm), 0)
    out = jnp.zeros((n_sub, tm), dtype)
    for k, r in enumerate(rows):
        out = jnp.where(sub == k, jnp.broadcast_to(r.astype(dtype), (n_sub, tm)), out)
    return out


def _route_block(xn, sh_ref, sc_ref, rwt_ref, bias_ref,
                 hp_ref, eidx_ref, rank_ref, wt_ref, cnt_ref, carry_ref):
    tm = xn.shape[0]
    e = N_EXPERTS
    per = e // N_GROUPS
    neg = -jnp.inf
    h = xn * (1.0 + sc_ref[...]) + sh_ref[...]
    hp_ref[...] = _pack_bf16_pair(h[:, :HALF], h[:, HALF:])
    h_hi, h_lo = _split_bf16(h)
    w_hi, w_lo = _split_bf16(rwt_ref[...])
    dot = lambda a, b: lax.dot_general(a, b, _NT, preferred_element_type=F32)
    logits = dot(w_hi, h_hi) + (dot(w_hi, h_lo) + dot(w_lo, h_hi))
    s = jax.nn.sigmoid(logits)
    sel = s + bias_ref[...]

    g3 = sel.reshape(N_GROUPS, per, tm)
    sub = lax.broadcasted_iota(jnp.int32, (N_GROUPS, per, tm), 1)
    m1 = jnp.max(g3, axis=1, keepdims=True)
    i1 = jnp.min(jnp.where(g3 == m1, sub, per), axis=1, keepdims=True)
    m2 = jnp.max(jnp.where(sub == i1, neg, g3), axis=1, keepdims=True)
    gs = (m1 + m2).reshape(N_GROUPS, tm)

    gi = lax.broadcasted_iota(jnp.int32, (N_GROUPS, tm), 0)
    gmask = jnp.zeros((N_GROUPS, tm), jnp.bool_)
    cur = gs
    for _ in range(TOPK_GROUPS):
        m = jnp.max(cur, axis=0, keepdims=True)
        idx = jnp.min(jnp.where(cur == m, gi, N_GROUPS), axis=0, keepdims=True)
        pick = gi == idx
        gmask = jnp.logical_or(gmask, pick)
        cur = jnp.where(pick, neg, cur)
    emask = jnp.broadcast_to(gmask.reshape(N_GROUPS, 1, tm), (N_GROUPS, per, tm)).reshape(e, tm)

    ei = lax.broadcasted_iota(jnp.int32, (e, tm), 0)
    picks, ids = [], []
    cur = jnp.where(emask, sel, neg)
    for _ in range(TOP_K):
        m = jnp.max(cur, axis=0, keepdims=True)
        idx = jnp.min(jnp.where(cur == m, ei, e), axis=0, keepdims=True)
        pick = ei == idx
        picks.append(pick)
        ids.append(idx)
        cur = jnp.where(pick, neg, cur)

    chosen = functools.reduce(jnp.logical_or, picks)
    cf = jnp.where(chosen, 1.0, 0.0)
    before = (lax.broadcasted_iota(jnp.int32, (tm, tm), 0)
              < lax.broadcasted_iota(jnp.int32, (tm, tm), 1)).astype(BF16)
    rank = carry_ref[:, 0:1] + jnp.dot(cf.astype(BF16), before, preferred_element_type=F32)
    carry_ref[...] = carry_ref[...] + jnp.sum(cf, axis=1, keepdims=True)
    cnt_ref[...] = carry_ref[...].astype(jnp.int32)

    w_rows = [jnp.sum(jnp.where(p, s, 0.0), axis=0, keepdims=True) for p in picks]
    r_rows = [jnp.sum(jnp.where(p, rank, 0.0), axis=0, keepdims=True) for p in picks]
    den = functools.reduce(lambda a, b: a + b, w_rows)
    w_rows = [w / den * ROUTED_SCALE for w in w_rows]
    eidx_ref[...] = _rows_to_tile(ids, K_PAD, jnp.int32)
    rank_ref[...] = _rows_to_tile(r_rows, K_PAD, F32).astype(jnp.int32)
    wt_ref[...] = _rows_to_tile(w_rows, 128, F32).T


def _route_io(tm, first_block):
    tok_major = lambda i: (i, 0)
    choice_major = lambda i: (0, i)
    in_specs = [
        _mod_spec(tm, 3, first_block),
        _mod_spec(tm, 4, first_block),
        pl.BlockSpec((N_EXPERTS, D_MODEL), lambda i: (0, 0)),
        pl.BlockSpec((N_EXPERTS, 1), lambda i: (0, 0)),
    ]
    out_specs = [
        pl.BlockSpec((tm, HALF), tok_major),
        pl.BlockSpec((K_PAD, tm), choice_major),
        pl.BlockSpec((K_PAD, tm), choice_major),
        pl.BlockSpec((tm, 128), tok_major),
        pl.BlockSpec((N_EXPERTS, 128), lambda i: (0, 0)),
    ]
    out_shapes = [
        jax.ShapeDtypeStruct((N_TRUNK, HALF), jnp.uint32),
        jax.ShapeDtypeStruct((K_PAD, N_TRUNK), jnp.int32),
        jax.ShapeDtypeStruct((K_PAD, N_TRUNK), jnp.int32),
        jax.ShapeDtypeStruct((N_TRUNK, 128), F32),
        jax.ShapeDtypeStruct((N_EXPERTS, 128), jnp.int32),
    ]
    return in_specs, out_specs, out_shapes, [pltpu.VMEM((N_EXPERTS, 128), F32)]


def _finalize_body(cnt_ref, eidx_ref, rank_ref, pos_ref):
    pos_ref[...] = rank_ref[...]

    def per_expert(e, off):
        n_tile = lax.div(cnt_ref[e] + (MOE_TILE - 1), MOE_TILE)
        pos_ref[...] = pos_ref[...] + jnp.where(eidx_ref[...] == e, off, 0)
        return off + n_tile * MOE_TILE

    lax.fori_loop(0, N_EXPERTS, per_expert, jnp.int32(0))


def _finalize(counts, eidx, rank):
    full = pl.BlockSpec((K_PAD, N_TRUNK), lambda: (0, 0))
    return pl.pallas_call(
        _finalize_body,
        in_specs=[pl.BlockSpec(memory_space=pltpu.SMEM), full, full],
        out_specs=full,
        out_shape=jax.ShapeDtypeStruct((K_PAD, N_TRUNK), jnp.int32),
        compiler_params=pltpu.CompilerParams(vmem_limit_bytes=VMEM_LIMIT),
        name="route_finalize",
    )(counts, eidx, rank)


def _ffn(h, wgu, wd):
    gu = jnp.dot(h, wgu, preferred_element_type=F32)
    hid = _silu(gu[:, :EXPERT_FF]) * gu[:, EXPERT_FF:]
    return jnp.dot(hid.astype(BF16), wd, preferred_element_type=F32)


def _cast_ffn_weights(wg_ref, wu_ref, wd_ref, wgub_ref, wdb_ref):
    wgub_ref[:, :EXPERT_FF] = wg_ref[...].astype(BF16)
    wgub_ref[:, EXPERT_FF:] = wu_ref[...].astype(BF16)
    wdb_ref[...] = wd_ref[...].astype(BF16)


_FFN_WEIGHT_SCRATCH = [pltpu.VMEM((D_MODEL, 2 * EXPERT_FF), BF16), pltpu.VMEM((EXPERT_FF, D_MODEL), BF16)]


def _sc_mesh():
    from jax.experimental.pallas import tpu_sc as plsc
    return plsc.VectorSubcoreMesh(core_axis_name="c", subcore_axis_name="s",
                                  num_cores=SC_CORES, num_subcores=SC_SUBCORES)


def _sc_worker_id():
    return lax.axis_index("s") * SC_CORES + lax.axis_index("c")


def _sc_dispatch(hp, pos_flat):
    win = SC_SCATTER_WIN
    per_worker = N_TRUNK // SC_WORKERS

    def body(rows_hbm, idx_hbm, out_hbm, *scratch):
        idx_v, rows_v, sem = scratch[:TOP_K], scratch[TOP_K], scratch[TOP_K + 1]
        base = _sc_worker_id() * per_worker

        @pl.loop(0, per_worker // win)
        def _(j):
            off = base + j * win
            pltpu.sync_copy(rows_hbm.at[pl.ds(off, win)], rows_v)
            for k in range(TOP_K):
                pltpu.sync_copy(idx_hbm.at[pl.ds(k * N_TRUNK + off, win)], idx_v[k])
            copies = [pltpu.async_copy(rows_v, out_hbm.at[idx_v[k]], sem) for k in range(TOP_K)]
            for cp in copies:
                cp.wait()

    return pl.kernel(
        body, mesh=_sc_mesh(),
        out_type=jax.ShapeDtypeStruct((N_SLOT, HALF), jnp.uint32),
        scratch_types=[pltpu.VMEM((win,), jnp.int32)] * TOP_K
        + [pltpu.VMEM((win, HALF), jnp.uint32), pltpu.SemaphoreType.DMA],
        name="sc_dispatch",
    )(hp, pos_flat)


def _sc_return(ys, pos_flat):
    win = SC_GATHER_WIN
    per_worker = N_PAIR // SC_WORKERS

    def body(src_hbm, idx_hbm, out_hbm, idx_v, rows_v, sem):
        base = _sc_worker_id() * per_worker

        @pl.loop(0, per_worker // win)
        def _(j):
            off = base + j * win
            pltpu.sync_copy(idx_hbm.at[pl.ds(off, win)], idx_v)
            pltpu.async_copy(src_hbm.at[idx_v], rows_v, sem).wait()
            pltpu.sync_copy(rows_v, out_hbm.at[pl.ds(off, win)])

    return pl.kernel(
        body, mesh=_sc_mesh(),
        out_type=jax.ShapeDtypeStruct((N_PAIR, HALF), jnp.uint32),
        scratch_types=[pltpu.VMEM((win,), jnp.int32), pltpu.VMEM((win, HALF), jnp.uint32),
                       pltpu.SemaphoreType.DMA],
        name="sc_return",
    )(ys, pos_flat)


N_PLAN = 2 * N_TILE


def _tile_plan_body(cnt0_ref, cnt1_ref, exp_ref, blk0_ref, blk1_ref, live_ref, n_ref):
    def per_expert(e, carry):
        s, b0, b1 = carry
        n0 = lax.div(cnt0_ref[e] + (MOE_TILE - 1), MOE_TILE)
        n1 = lax.div(cnt1_ref[e] + (MOE_TILE - 1), MOE_TILE)

        def fill(j, c):
            exp_ref[s + j] = e
            blk0_ref[s + j] = jnp.maximum(b0 + jnp.minimum(j, n0 - 1), 0)
            blk1_ref[s + j] = jnp.maximum(b1 + jnp.minimum(j, n1 - 1), 0)
            live_ref[s + j] = jnp.where(j < n0, 1, 0) + jnp.where(j < n1, 2, 0)
            return c
        n = jnp.maximum(n0, n1)
        lax.fori_loop(0, n, fill, 0)
        return s + n, b0 + n0, b1 + n1

    zero = jnp.int32(0)
    n_steps, _, _ = lax.fori_loop(0, N_EXPERTS, per_expert, (zero, zero, zero))
    n_ref[0] = n_steps

    def fill_tail(s, c):
        exp_ref[s] = 0
        blk0_ref[s] = 0
        blk1_ref[s] = 0
        live_ref[s] = 0
        return c
    lax.fori_loop(n_steps, N_PLAN, fill_tail, 0)


def _tile_plan(counts0, counts1):
    smem = pl.BlockSpec(memory_space=pltpu.SMEM)
    steps = jax.ShapeDtypeStruct((N_PLAN,), jnp.int32)
    return pl.pallas_call(
        _tile_plan_body,
        in_specs=[smem, smem],
        out_specs=[smem] * 5,
        out_shape=[steps, steps, steps, steps, jax.ShapeDtypeStruct((1,), jnp.int32)],
        name="tile_plan",
    )(counts0, counts1)


def _expert_body(exp_ref, blk0_ref, blk1_ref, live_ref, xs0_ref, xs1_ref, wg_ref, wu_ref, wd_ref,
                 ys0_ref, ys1_ref, wgub_ref, wdb_ref):
    i = pl.program_id(0)

    @pl.when(jnp.logical_or(i == 0, exp_ref[i] != exp_ref[jnp.maximum(i - 1, 0)]))
    def _():
        _cast_ffn_weights(wg_ref, wu_ref, wd_ref, wgub_ref, wdb_ref)

    for bit, xs_ref, ys_ref in ((1, xs0_ref, ys0_ref), (2, xs1_ref, ys1_ref)):
        @pl.when((live_ref[i] & bit) != 0)
        def _():
            lo, hi = _unpack_bf16_pair(xs_ref[...])
            h = jnp.concatenate([lo.astype(BF16), hi.astype(BF16)], axis=1)
            y = _ffn(h, wgub_ref[...], wdb_ref[...])
            ys_ref[...] = _pack_bf16_pair(y[:, :HALF], y[:, HALF:])


def _expert_ffn(plan, xs0, xs1, layer, w_gate, w_up, w_down):
    exp, blk0, blk1, live, n_steps = plan
    tile0 = lambda i, exp, blk0, blk1, live: (blk0[i], 0)
    tile1 = lambda i, exp, blk0, blk1, live: (blk1[i], 0)
    ew = lambda shape: pl.BlockSpec((None, None) + shape,
                                    lambda i, exp, blk0, blk1, live: (layer, exp[i], 0, 0))
    slots = jax.ShapeDtypeStruct((N_SLOT, HALF), jnp.uint32)
    return pl.pallas_call(
        _expert_body,
        grid_spec=pltpu.PrefetchScalarGridSpec(
            num_scalar_prefetch=4,
            grid=(n_steps[0],),
            in_specs=[
                pl.BlockSpec((MOE_TILE, HALF), tile0), pl.BlockSpec((MOE_TILE, HALF), tile1),
                ew((D_MODEL, EXPERT_FF)), ew((D_MODEL, EXPERT_FF)), ew((EXPERT_FF, D_MODEL)),
            ],
            out_specs=[pl.BlockSpec((MOE_TILE, HALF), tile0), pl.BlockSpec((MOE_TILE, HALF), tile1)],
            scratch_shapes=_FFN_WEIGHT_SCRATCH,
        ),
        out_shape=[slots, slots],
        compiler_params=pltpu.CompilerParams(
            dimension_semantics=("arbitrary",), vmem_limit_bytes=VMEM_LIMIT),
        name="expert_ffn",
    )(exp, blk0, blk1, live, xs0, xs1, w_gate, w_up, w_down)


def _combine_body(x_ref, sh_ref, sc_ref, g2_ref, yk_ref, wt_ref, sg_ref, su_ref, sd_ref, lng_ref, lnb_ref,
                  o_ref, sgub_ref, sdb_ref):
    @pl.when(pl.program_id(0) == 0)
    def _():
        _cast_ffn_weights(sg_ref, su_ref, sd_ref, sgub_ref, sdb_ref)

    x = x_ref[...].astype(F32)
    h = (x * (1.0 + sc_ref[...]) + sh_ref[...]).astype(BF16)
    y = _ffn(h, sgub_ref[...], sdb_ref[...])
    wt = wt_ref[...]
    lo_acc = jnp.zeros((x.shape[0], HALF), F32)
    hi_acc = jnp.zeros((x.shape[0], HALF), F32)
    for k in range(TOP_K):
        lo, hi = _unpack_bf16_pair(yk_ref[k])
        w = wt[:, k:k + 1]
        lo_acc = lo_acc + w * lo
        hi_acc = hi_acc + w * hi
    y = y + jnp.concatenate([lo_acc, hi_acc], axis=1)
    z = DEEPNORM_ALPHA * x + g2_ref[...] * y
    o_ref[...] = _layer_norm(z, lng_ref[...], lnb_ref[...]).astype(o_ref.dtype)


def _combine(trunk, x, mod, yk, wt, layer, s_gate, s_up, s_down, ln_g, ln_b, out_dtype):
    tm = TOK_BLOCK
    first_block = trunk * (N_TRUNK // tm)
    row = lambda i: (i, 0)
    vec = pl.BlockSpec((1, D_MODEL), lambda i: (0, 0))
    sw = lambda shape: pl.BlockSpec((None,) + shape, lambda i: (layer, 0, 0))
    mod_spec = lambda col: _mod_spec(tm, col, first_block)
    return pl.pallas_call(
        _combine_body,
        grid=(N_TRUNK // tm,),
        in_specs=[
            pl.BlockSpec((tm, D_MODEL), row),
            mod_spec(3), mod_spec(4), mod_spec(5),
            pl.BlockSpec((TOP_K, tm, HALF), lambda i: (0, i, 0)),
            pl.BlockSpec((tm, 128), row),
            sw((D_MODEL, EXPERT_FF)), sw((D_MODEL, EXPERT_FF)), sw((EXPERT_FF, D_MODEL)),
            vec, vec,
        ],
        out_specs=pl.BlockSpec((tm, D_MODEL), lambda i: (i, 0)),
        out_shape=jax.ShapeDtypeStruct((N_TRUNK, D_MODEL), out_dtype),
        scratch_shapes=_FFN_WEIGHT_SCRATCH,
        compiler_params=pltpu.CompilerParams(
            dimension_semantics=("arbitrary",), vmem_limit_bytes=VMEM_LIMIT),
        name="moe_combine_ln",
    )(x, mod, mod, mod, yk, wt, s_gate, s_up, s_down, ln_g, ln_b)


def _moe_dispatch(routing):
    hp, eidx, rank, wt, counts = routing
    counts = counts[:, 0]
    pos_flat = _finalize(counts, eidx, rank).reshape(K_PAD * N_TRUNK)
    return _sc_dispatch(hp, pos_flat), pos_flat, counts, wt


def _moe_experts(plans, layer, w_gate, w_up, w_down):
    (xs0, pos0, counts0, _), (xs1, pos1, counts1, _) = plans
    ys = _expert_ffn(_tile_plan(counts0, counts1), xs0, xs1, layer, w_gate, w_up, w_down)
    return [_sc_return(y, pos).reshape(TOP_K, N_TRUNK, HALF) for y, pos in zip(ys, (pos0, pos1))]


def kernel(x_prompt, x_sample, state_ret, c, c_ctx, ada_w, ada_b, ln_g, ln_b, ret_w_in, ret_w_out, ret_decay, conv_w_in, conv_w, conv_w_out, moe_router, moe_bias, moe_w_gate, moe_w_up, moe_w_down, shared_w_gate, shared_w_up, shared_w_down):
    x = (x_prompt.reshape(N_CTX, D_MODEL), x_sample.reshape(N_LAT, D_MODEL))
    cond = jnp.concatenate(
        [c_ctx[None, :], c, jnp.zeros((N_COND - 1 - DEC_BATCH, D_MODEL), F32)], axis=0)
    mods = _ada_table(cond, ada_w, ada_b).reshape(DEPTH, N_COND, 1, 6 * D_MODEL)
    cos, sin = _rope_tables()
    router_wt = jnp.swapaxes(moe_router, 1, 2)

    states = None
    for i in range(DEPTH):
        j = i // 2
        mod = mods[i]
        lng = ln_g[i].reshape(2, 1, D_MODEL)
        lnb = ln_b[i].reshape(2, 1, D_MODEL)
        bias_col = moe_bias[i].reshape(N_EXPERTS, 1)
        if i % 2 == 0:
            w_in_bf = _to_bf16(ret_w_in, j)
        mixed, plans = [], []
        for trunk in range(2):
            if i % 2 == 0:
                if trunk == 0:
                    a, states = _retention_ctx(x[0], mod, w_in_bf, ret_decay[j], j, states)
                else:
                    a = _retention_lat(x[1], mod, w_in_bf, ret_decay[j], state_ret, j, cos, sin)
                xm, *routing = _ret_out(trunk, a, x[trunk], mod, ret_w_out, j, lng[0], lnb[0],
                                        router_wt[i], bias_col)
            else:
                xm, *routing = _conv_layer(trunk, x[trunk], mod, conv_w_in, conv_w_out, j, conv_w[j],
                                           lng[0], lnb[0], router_wt[i], bias_col)
            mixed.append(xm)
            plans.append(_moe_dispatch(routing))
        yks = _moe_experts(plans, i, moe_w_gate, moe_w_up, moe_w_down)
        out_dtype = F32 if i == DEPTH - 1 else RES_DTYPE
        x = tuple(_combine(trunk, mixed[trunk], mod, yks[trunk], plans[trunk][3], i, shared_w_gate, shared_w_up,
                           shared_w_down, lng[1], lnb[1], out_dtype) for trunk in range(2))

    y_prompt = x[0].reshape(BATCH, SEQ, D_MODEL)
    y_sample = x[1].reshape(DEC_BATCH, DEC_SEQ, D_MODEL)
    return y_prompt, y_sample, states
```

```python
import functools

import jax
import jax.numpy as jnp
from jax import lax
from jax.experimental import pallas as pl
from jax.experimental.pallas import tpu as pltpu

F32 = jnp.float32
BF16 = jnp.bfloat16

D_MODEL = 1024
BATCH = 32
SEQ = 256
DEPTH = 4
DEC_BATCH = 8
DEC_SEQ = 1024
GRID_W = 64
RET_HEADS = 4
RET_DK = D_MODEL // RET_HEADS
RET_DV = 2 * D_MODEL // RET_HEADS
ROPE_BASE = 10000.0
N_EXPERTS = 64
TOP_K = 6
N_GROUPS = 8
TOPK_GROUPS = 4
EXPERT_FF = 256
ROUTED_SCALE = 2.5
LN_EPS = 1e-5
DEEPNORM_ALPHA = (2.0 * DEPTH) ** 0.25

N_CTX = BATCH * SEQ
N_LAT = DEC_BATCH * DEC_SEQ
N_COND = 16
RET_CHUNK = 256
assert N_CTX == N_LAT
N_TRUNK = N_CTX
TOK_BLOCK = 512
MIX_BLOCK = 512
VMEM_LIMIT = 56 * 1024 * 1024
RES_DTYPE = BF16

HALF = D_MODEL // 2
MOE_TILE = 896
N_PAIR = N_TRUNK * TOP_K
N_TILE = -(-(N_PAIR + N_EXPERTS * (MOE_TILE - 1)) // MOE_TILE)
N_SLOT = N_TILE * MOE_TILE
K_PAD = 8

SC_CORES = 2
SC_SUBCORES = 16
SC_WORKERS = SC_CORES * SC_SUBCORES
SC_SCATTER_WIN = 64
SC_GATHER_WIN = 128


def _cond_row(tok_block_idx, tok_block):
    t0 = tok_block_idx * tok_block
    return jnp.where(t0 < N_CTX, 0, 1 + (t0 - N_CTX) // DEC_SEQ)


def _silu(x):
    return x * jax.nn.sigmoid(x)


def _layer_norm(z, g, b):
    mu = jnp.mean(z, axis=-1, keepdims=True)
    zc = z - mu
    var = jnp.mean(zc * zc, axis=-1, keepdims=True)
    return zc * lax.rsqrt(var + LN_EPS) * g + b


def _ada_body(cond_ref, w_ref, b_ref, o_ref):
    s = _silu(cond_ref[...]).astype(BF16)
    o_ref[...] = jnp.dot(s, w_ref[...].astype(BF16), preferred_element_type=F32) + b_ref[...]


def _ada_table(cond, ada_w, ada_b):
    tn = 2048
    return pl.pallas_call(
        _ada_body,
        grid=(DEPTH, 6 * D_MODEL // tn),
        in_specs=[
            pl.BlockSpec((N_COND, D_MODEL), lambda l, j: (0, 0)),
            pl.BlockSpec((None, D_MODEL, tn), lambda l, j: (l, 0, j)),
            pl.BlockSpec((None, 1, tn), lambda l, j: (l, 0, j)),
        ],
        out_specs=pl.BlockSpec((None, N_COND, tn), lambda l, j: (l, 0, j)),
        out_shape=jax.ShapeDtypeStruct((DEPTH, N_COND, 6 * D_MODEL), F32),
        compiler_params=pltpu.CompilerParams(
            dimension_semantics=("arbitrary", "arbitrary"), vmem_limit_bytes=VMEM_LIMIT),
        name="ada_table",
    )(cond, ada_w, ada_b.reshape(DEPTH, 1, 6 * D_MODEL))


def _mod_spec(tok_block, col, first_block=0):
    def index_map(*idx):
        return (_cond_row(first_block + idx[0], tok_block), 0, col)
    return pl.BlockSpec((None, 1, D_MODEL), index_map)


def _to_bf16_body(w_ref, o_ref):
    o_ref[...] = w_ref[...].astype(BF16)


def _to_bf16(w_all, w_idx):
    k_dim, n_out = w_all.shape[1:]
    tn = 1024
    return pl.pallas_call(
        _to_bf16_body,
        grid=(n_out // tn,),
        in_specs=[pl.BlockSpec((None, k_dim, tn), lambda j: (w_idx, 0, j))],
        out_specs=pl.BlockSpec((k_dim, tn), lambda j: (0, j)),
        out_shape=jax.ShapeDtypeStruct((k_dim, n_out), BF16),
        compiler_params=pltpu.CompilerParams(
            dimension_semantics=("arbitrary",), vmem_limit_bytes=VMEM_LIMIT),
        name="weight_to_bf16",
    )(w_all)


RET_GROUP = 1024


def _head_proj_specs():
    return [
        pl.BlockSpec((D_MODEL, RET_DK), lambda t, h: (0, h)),
        pl.BlockSpec((D_MODEL, RET_DK), lambda t, h: (0, RET_HEADS + h)),
        pl.BlockSpec((D_MODEL, RET_DV), lambda t, h: (0, RET_HEADS + h)),
        pl.BlockSpec((D_MODEL, RET_DV), lambda t, h: (0, 2 * RET_HEADS + h)),
    ]


def _head_proj(x_ref, sh_ref, sc_ref, wq_ref, wk_ref, wv_ref, wg_ref):
    h = (x_ref[...].astype(F32) * (1.0 + sc_ref[...]) + sh_ref[...]).astype(BF16)
    dot = lambda w_ref: jnp.dot(h, w_ref[...], preferred_element_type=F32)
    return dot(wq_ref), dot(wk_ref), dot(wv_ref).astype(BF16), dot(wg_ref)

def _log_sigmoid(v):
    return jnp.minimum(v, 0.0) - jnp.log1p(jnp.exp(-jnp.abs(v)))


def _decay_tables(dec_ref, head):
    c = RET_CHUNK
    lgf = _log_sigmoid(jnp.full((c, c), dec_ref[0, head], F32))
    lgb = _log_sigmoid(jnp.full((c, c), dec_ref[1, head], F32))
    row = lax.broadcasted_iota(jnp.int32, (c, c), 0).astype(F32)
    col = lax.broadcasted_iota(jnp.int32, (c, c), 1).astype(F32)
    diff = row - col
    kscale = RET_DK ** -0.5
    intra = jnp.where(diff > 0, jnp.exp(lgf * diff),
                      jnp.where(diff < 0, jnp.exp(-lgb * diff), 2.0)) * kscale
    qdec_f = jnp.exp(lgf * (row + 1.0))
    qdec_b = jnp.exp(lgb * (c - row))
    kdec_f = jnp.exp(lgf * (c - 1.0 - row)) * kscale
    kdec_b = jnp.exp(lgb * row) * kscale
    cdec_f = jnp.exp(lgf * c)
    cdec_b = jnp.exp(lgb * c)
    return intra, qdec_f, qdec_b, kdec_f, kdec_b, cdec_f, cdec_b


def _head_norm_gate(o, g):
    mu = jnp.mean(o, axis=-1, keepdims=True)
    oc = o - mu
    var = jnp.mean(oc * oc, axis=-1, keepdims=True)
    on = oc * lax.rsqrt(var + LN_EPS)
    return (_silu(g.astype(F32)) * on).astype(BF16)


_NT = (((1,), (1,)), ((), ()))
_TN = (((0,), (0,)), ((), ()))


def _ret_ctx_body(dec_ref, x_ref, sh_ref, sc_ref, wq_ref, wk_ref, wv_ref, wg_ref, *rest):
    a_ref, st_ref, tab_ref = rest[-3:]
    head = pl.program_id(1)

    @pl.when(pl.program_id(0) == 0)
    def _():
        intra, _, _, kdec_f, kdec_b, _, _ = _decay_tables(dec_ref, head)
        tab_ref[head, 0] = intra
        tab_ref[head, 1] = kdec_f
        tab_ref[head, 2] = kdec_b

    q, k, v, g = _head_proj(x_ref, sh_ref, sc_ref, wq_ref, wk_ref, wv_ref, wg_ref)
    for s in range(RET_GROUP // SEQ):
        rows = slice(s * SEQ, (s + 1) * SEQ)
        scores = lax.dot_general(q[rows].astype(BF16), k[rows].astype(BF16), _NT, preferred_element_type=F32)
        p = (scores * tab_ref[head, 0]).astype(BF16)
        o = jnp.dot(p, v[rows], preferred_element_type=F32)
        a_ref[rows, :] = _head_norm_gate(o, g[rows])
        st_ref[s, 0] = lax.dot_general((k[rows] * tab_ref[head, 1]).astype(BF16), v[rows], _TN,
                                       preferred_element_type=F32)
        st_ref[s, 1] = lax.dot_general((k[rows] * tab_ref[head, 2]).astype(BF16), v[rows], _TN,
                                       preferred_element_type=F32)


def _retention_ctx(x, mod, w_in_bf, decay, ret_idx, states):
    assert SEQ == RET_CHUNK
    seqs = RET_GROUP // SEQ
    n_ret = (DEPTH + 1) // 2
    st = jax.ShapeDtypeStruct((BATCH, n_ret, 2, RET_HEADS, RET_DK, RET_DV), F32)
    st_spec = pl.BlockSpec((seqs, None, 2, None, RET_DK, RET_DV), lambda t, h: (t, ret_idx, 0, h, 0, 0))
    in_specs = [
        pl.BlockSpec(memory_space=pltpu.SMEM),
        pl.BlockSpec((RET_GROUP, D_MODEL), lambda t, h: (t, 0)),
        _mod_spec(RET_GROUP, 0), _mod_spec(RET_GROUP, 1),
    ] + _head_proj_specs()
    args = (decay, x, mod, mod, w_in_bf, w_in_bf, w_in_bf, w_in_bf)
    aliases = {}
    if states is not None:
        in_specs.append(pl.BlockSpec(memory_space=pl.ANY))
        aliases = {len(args): 1}
        args += (states,)
    return pl.pallas_call(
        _ret_ctx_body,
        grid=(N_CTX // RET_GROUP, RET_HEADS),
        in_specs=in_specs,
        out_specs=[pl.BlockSpec((RET_GROUP, RET_DV), lambda t, h: (t, h)), st_spec],
        out_shape=[jax.ShapeDtypeStruct((N_CTX, RET_HEADS * RET_DV), BF16), st],
        input_output_aliases=aliases,
        scratch_shapes=[pltpu.VMEM((RET_HEADS, 3, RET_CHUNK, RET_CHUNK), F32)],
        compiler_params=pltpu.CompilerParams(
            dimension_semantics=("arbitrary", "arbitrary"), vmem_limit_bytes=VMEM_LIMIT),
        name="retention_ctx",
    )(*args)


def _rope(x, cos, sin):
    halves = [pltpu.roll(x[:, s:s + 128], 64, axis=1) for s in (0, 128)]
    return x * cos + jnp.concatenate(halves, axis=1) * sin


def _ret_lat_body(dec_ref, x_ref, sh_ref, sc_ref, wq_ref, wk_ref, wv_ref, wg_ref,
                  s0f_ref, s0b_ref, cos_ref, sin_ref,
                  a_ref, qr_ref, kr_ref, v_ref, g_ref, o_ref, st_ref):
    head = pl.program_id(1)
    c = RET_CHUNK
    nc = DEC_SEQ // c
    intra, qdec_f, qdec_b, kdec_f, kdec_b, cdec_f, cdec_b = _decay_tables(dec_ref, head)
    cdec_f = jnp.concatenate([cdec_f, cdec_f], axis=1)
    cdec_b = jnp.concatenate([cdec_b, cdec_b], axis=1)

    q, k, v, g = _head_proj(x_ref, sh_ref, sc_ref, wq_ref, wk_ref, wv_ref, wg_ref)
    qr_ref[...] = _rope(q, cos_ref[...], sin_ref[...])
    kr_ref[...] = _rope(k, cos_ref[...], sin_ref[...])
    v_ref[...] = v
    g_ref[...] = g

    st_ref[...] = s0f_ref[...]
    for ci in range(nc):
        rows = pl.ds(ci * c, c)
        q = qr_ref[rows, :]
        k = kr_ref[rows, :]
        v = v_ref[rows, :]
        scores = lax.dot_general(q.astype(BF16), k.astype(BF16), _NT, preferred_element_type=F32)
        p = (scores * intra).astype(BF16)
        o = jnp.dot(p, v, preferred_element_type=F32)
        qd = (q * qdec_f).astype(BF16)
        o = o + jnp.dot(qd, st_ref[...].astype(BF16), preferred_element_type=F32)
        o_ref[rows, :] = o
        if ci + 1 < nc:
            kd = (k * kdec_f).astype(BF16)
            st_ref[...] = cdec_f * st_ref[...] + lax.dot_general(kd, v, _TN, preferred_element_type=F32)

    st_ref[...] = s0b_ref[...]
    for ci in reversed(range(nc)):
        rows = pl.ds(ci * c, c)
        qd = (qr_ref[rows, :] * qdec_b).astype(BF16)
        o = o_ref[rows, :] + jnp.dot(qd, st_ref[...].astype(BF16), preferred_element_type=F32)
        a_ref[rows, :] = _head_norm_gate(o, g_ref[rows, :])
        if ci > 0:
            kd = (kr_ref[rows, :] * kdec_b).astype(BF16)
            st_ref[...] = cdec_b * st_ref[...] + lax.dot_general(kd, v_ref[rows, :], _TN,
                                                                 preferred_element_type=F32)


def _rope_tables():
    half = RET_DK // 4
    freqs = ROPE_BASE ** (-jnp.arange(half, dtype=F32) / half)
    t = jnp.arange(DEC_SEQ)
    row = (t // GRID_W).astype(F32)
    col = (t % GRID_W).astype(F32)
    ang_r = row[:, None] * freqs[None, :]
    ang_c = col[:, None] * freqs[None, :]
    cos = jnp.concatenate([jnp.cos(ang_r)] * 2 + [jnp.cos(ang_c)] * 2, axis=1)
    sin = jnp.concatenate([-jnp.sin(ang_r), jnp.sin(ang_r), -jnp.sin(ang_c), jnp.sin(ang_c)], axis=1)
    return cos, sin


def _retention_lat(x, mod, w_in_bf, decay, state_ret, ret_idx, cos, sin):
    n = DEC_SEQ
    assert n == RET_GROUP
    s0_spec = lambda d: pl.BlockSpec((None, None, None, None, RET_DK, RET_DV),
                                     lambda b, h: (b, ret_idx, d, h, 0, 0))
    tab_spec = pl.BlockSpec((n, RET_DK), lambda b, h: (0, 0))
    first_block = N_CTX // n
    return pl.pallas_call(
        _ret_lat_body,
        grid=(DEC_BATCH, RET_HEADS),
        in_specs=[
            pl.BlockSpec(memory_space=pltpu.SMEM),
            pl.BlockSpec((n, D_MODEL), lambda b, h: (b, 0)),
            _mod_spec(n, 0, first_block), _mod_spec(n, 1, first_block),
        ] + _head_proj_specs() + [s0_spec(0), s0_spec(1), tab_spec, tab_spec],
        out_specs=pl.BlockSpec((n, RET_DV), lambda b, h: (b, h)),
        out_shape=jax.ShapeDtypeStruct((N_LAT, RET_HEADS * RET_DV), BF16),
        scratch_shapes=[
            pltpu.VMEM((n, RET_DK), F32), pltpu.VMEM((n, RET_DK), F32),
            pltpu.VMEM((n, RET_DV), BF16), pltpu.VMEM((n, RET_DV), F32),
            pltpu.VMEM((n, RET_DV), F32), pltpu.VMEM((RET_DK, RET_DV), F32),
        ],
        compiler_params=pltpu.CompilerParams(
            dimension_semantics=("arbitrary", "arbitrary"), vmem_limit_bytes=VMEM_LIMIT),
        name="retention_lat",
    )(decay, x, mod, mod, w_in_bf, w_in_bf, w_in_bf, w_in_bf, state_ret, state_ret, cos, sin)


N_ROUTE_IN = 4
N_ROUTE_OUT = 5


def _split_route_refs(rest, n_scratch):
    route_in = rest[:N_ROUTE_IN]
    o_ref = rest[N_ROUTE_IN]
    route_out = rest[N_ROUTE_IN + 1:N_ROUTE_IN + 1 + N_ROUTE_OUT]
    return route_in, o_ref, route_out, rest[len(rest) - n_scratch:]


def _ret_out_body(a_ref, x_ref, g1_ref, w_ref, lng_ref, lnb_ref, *rest):
    route_in, o_ref, route_out, (wbf_ref, carry_ref) = _split_route_refs(rest, 2)

    @pl.when(pl.program_id(0) == 0)
    def _():
        wbf_ref[...] = w_ref[...].astype(BF16)
        carry_ref[...] = jnp.zeros_like(carry_ref)

    y = jnp.dot(a_ref[...], wbf_ref[...], preferred_element_type=F32)
    z = DEEPNORM_ALPHA * x_ref[...].astype(F32) + g1_ref[...] * y
    xn = _layer_norm(z, lng_ref[...], lnb_ref[...])
    o_ref[...] = xn.astype(o_ref.dtype)
    _route_block(xn, *route_in, *route_out, carry_ref)


def _ret_out(trunk, a, x, mod, w_all, w_idx, ln_g, ln_b, router_wt, bias_col):
    tm = MIX_BLOCK
    k_dim = w_all.shape[1]
    first_block = trunk * (N_TRUNK // tm)
    row = lambda i: (i, 0)
    vec = pl.BlockSpec((1, D_MODEL), lambda i: (0, 0))
    r_in, r_out, r_shapes, r_scratch = _route_io(tm, first_block)
    return pl.pallas_call(
        _ret_out_body,
        grid=(N_TRUNK // tm,),
        in_specs=[
            pl.BlockSpec((tm, k_dim), row),
            pl.BlockSpec((tm, D_MODEL), row),
            _mod_spec(tm, 2, first_block),
            pl.BlockSpec((None, k_dim, D_MODEL), lambda i: (w_idx, 0, 0)),
            vec, vec,
        ] + r_in,
        out_specs=[pl.BlockSpec((tm, D_MODEL), row)] + r_out,
        out_shape=[jax.ShapeDtypeStruct((N_TRUNK, D_MODEL), RES_DTYPE)] + r_shapes,
        scratch_shapes=[pltpu.VMEM((k_dim, D_MODEL), BF16)] + r_scratch,
        compiler_params=pltpu.CompilerParams(
            dimension_semantics=("arbitrary",), vmem_limit_bytes=VMEM_LIMIT),
        name="ret_out_ln_route",
    )(a, x, mod, w_all, ln_g, ln_b, mod, mod, router_wt, bias_col)


def _conv_layer_body(seg, x_ref, sh_ref, sc_ref, g1_ref, win_ref, wout_ref, cw_ref, lng_ref, lnb_ref, *rest):
    route_in, o_ref, route_out, (winb_ref, woutb_ref, carry_ref) = _split_route_refs(rest, 3)
    tm = o_ref.shape[0]

    @pl.when(pl.program_id(0) == 0)
    def _():
        winb_ref[...] = win_ref[...].astype(BF16)
        woutb_ref[...] = wout_ref[...].astype(BF16)
        carry_ref[...] = jnp.zeros_like(carry_ref)

    x = x_ref[...].astype(F32)
    h = (x * (1.0 + sc_ref[...]) + sh_ref[...]).astype(BF16)
    proj = jnp.dot(h, winb_ref[...], preferred_element_type=F32)
    bg, cg, xt = (proj[:, c * D_MODEL:(c + 1) * D_MODEL] for c in range(3))
    u = cg * xt
    pos = lax.broadcasted_iota(jnp.int32, (tm, D_MODEL), 0) & (seg - 1)
    u_prev = jnp.where(pos == 0, 0.0, pltpu.roll(u, 1, axis=0))
    u_next = jnp.where(pos == seg - 1, 0.0, pltpu.roll(u, tm - 1, axis=0))
    cu = u_prev * cw_ref[0:1, :] + u * cw_ref[1:2, :] + u_next * cw_ref[2:3, :]
    y = jnp.dot((bg * cu).astype(BF16), woutb_ref[...], preferred_element_type=F32)
    z = DEEPNORM_ALPHA * x + g1_ref[...] * y
    xn = _layer_norm(z, lng_ref[...], lnb_ref[...])
    o_ref[...] = xn.astype(o_ref.dtype)
    _route_block(xn, *route_in, *route_out, carry_ref)


def _conv_layer(trunk, x, mod, w_in_all, w_out_all, w_idx, conv_w, ln_g, ln_b, router_wt, bias_col):
    tm = MIX_BLOCK
    first_block = trunk * (N_TRUNK // tm)
    seg = SEQ if trunk == 0 else GRID_W
    assert tm % seg == 0 and seg & (seg - 1) == 0
    row = lambda i: (i, 0)
    vec = pl.BlockSpec((1, D_MODEL), lambda i: (0, 0))
    once = pl.Buffered(1)
    r_in, r_out, r_shapes, r_scratch = _route_io(tm, first_block)
    mod_spec = lambda col: _mod_spec(tm, col, first_block)
    return pl.pallas_call(
        functools.partial(_conv_layer_body, seg),
        grid=(N_TRUNK // tm,),
        in_specs=[
            pl.BlockSpec((tm, D_MODEL), row),
            mod_spec(0), mod_spec(1), mod_spec(2),
            pl.BlockSpec((None, D_MODEL, 3 * D_MODEL), lambda i: (w_idx, 0, 0), pipeline_mode=once),
            pl.BlockSpec((None, D_MODEL, D_MODEL), lambda i: (w_idx, 0, 0), pipeline_mode=once),
            pl.BlockSpec((3, D_MODEL), lambda i: (0, 0)),
            vec, vec,
        ] + r_in,
        out_specs=[pl.BlockSpec((tm, D_MODEL), row)] + r_out,
        out_shape=[jax.ShapeDtypeStruct((N_TRUNK, D_MODEL), RES_DTYPE)] + r_shapes,
        scratch_shapes=[pltpu.VMEM((D_MODEL, 3 * D_MODEL), BF16), pltpu.VMEM((D_MODEL, D_MODEL), BF16)]
        + r_scratch,
        compiler_params=pltpu.CompilerParams(
            dimension_semantics=("arbitrary",), vmem_limit_bytes=VMEM_LIMIT),
        name="conv_layer_route",
    )(x, mod, mod, mod, w_in_all, w_out_all, conv_w, ln_g, ln_b, mod, mod, router_wt, bias_col)


def _split_bf16(v):
    hi = v.astype(BF16)
    lo = (v - hi.astype(F32)).astype(BF16)
    return hi, lo


def _pack_bf16_pair(lo_f32, hi_f32):
    lo = lax.bitcast_convert_type(lo_f32.astype(BF16).astype(F32), jnp.uint32) >> 16
    hi = lax.bitcast_convert_type(hi_f32.astype(BF16).astype(F32), jnp.uint32) & jnp.uint32(0xFFFF0000)
    return hi | lo


def _unpack_bf16_pair(u):
    lo = lax.bitcast_convert_type(u << 16, F32)
    hi = lax.bitcast_convert_type(u & jnp.uint32(0xFFFF0000), F32)
    return lo, hi


def _rows_to_tile(rows, n_sub, dtype):
    tm = rows[0].shape[1]
    sub = lax.broadcasted_iota(jnp.int32, (n_sub, tm), 0)
    out = jnp.zeros((n_sub, tm), dtype)
    for k, r in enumerate(rows):
        out = jnp.where(sub == k, jnp.broadcast_to(r.astype(dtype), (n_sub, tm)), out)
    return out


def _route_block(xn, sh_ref, sc_ref, rwt_ref, bias_ref,
                 hp_ref, eidx_ref, rank_ref, wt_ref, cnt_ref, carry_ref):
    tm = xn.shape[0]
    e = N_EXPERTS
    per = e // N_GROUPS
    neg = -jnp.inf
    h = xn * (1.0 + sc_ref[...]) + sh_ref[...]
    hp_ref[...] = _pack_bf16_pair(h[:, :HALF], h[:, HALF:])
    h_hi, h_lo = _split_bf16(h)
    w_hi, w_lo = _split_bf16(rwt_ref[...])
    dot = lambda a, b: lax.dot_general(a, b, _NT, preferred_element_type=F32)
    logits = dot(w_hi, h_hi) + (dot(w_hi, h_lo) + dot(w_lo, h_hi))
    s = jax.nn.sigmoid(logits)
    sel = s + bias_ref[...]

    g3 = sel.reshape(N_GROUPS, per, tm)
    sub = lax.broadcasted_iota(jnp.int32, (N_GROUPS, per, tm), 1)
    m1 = jnp.max(g3, axis=1, keepdims=True)
    i1 = jnp.min(jnp.where(g3 == m1, sub, per), axis=1, keepdims=True)
    m2 = jnp.max(jnp.where(sub == i1, neg, g3), axis=1, keepdims=True)
    gs = (m1 + m2).reshape(N_GROUPS, tm)

    gi = lax.broadcasted_iota(jnp.int32, (N_GROUPS, tm), 0)
    gmask = jnp.zeros((N_GROUPS, tm), jnp.bool_)
    cur = gs
    for _ in range(TOPK_GROUPS):
        m = jnp.max(cur, axis=0, keepdims=True)
        idx = jnp.min(jnp.where(cur == m, gi, N_GROUPS), axis=0, keepdims=True)
        pick = gi == idx
        gmask = jnp.logical_or(gmask, pick)
        cur = jnp.where(pick, neg, cur)
    emask = jnp.broadcast_to(gmask.reshape(N_GROUPS, 1, tm), (N_GROUPS, per, tm)).reshape(e, tm)

    ei = lax.broadcasted_iota(jnp.int32, (e, tm), 0)
    picks, ids = [], []
    cur = jnp.where(emask, sel, neg)
    for _ in range(TOP_K):
        m = jnp.max(cur, axis=0, keepdims=True)
        idx = jnp.min(jnp.where(cur == m, ei, e), axis=0, keepdims=True)
        pick = ei == idx
        picks.append(pick)
        ids.append(idx)
        cur = jnp.where(pick, neg, cur)

    chosen = functools.reduce(jnp.logical_or, picks)
    cf = jnp.where(chosen, 1.0, 0.0)
    before = (lax.broadcasted_iota(jnp.int32, (tm, tm), 0)
              < lax.broadcasted_iota(jnp.int32, (tm, tm), 1)).astype(BF16)
    rank = carry_ref[:, 0:1] + jnp.dot(cf.astype(BF16), before, preferred_element_type=F32)
    carry_ref[...] = carry_ref[...] + jnp.sum(cf, axis=1, keepdims=True)
    cnt_ref[...] = carry_ref[...].astype(jnp.int32)

    w_rows = [jnp.sum(jnp.where(p, s, 0.0), axis=0, keepdims=True) for p in picks]
    r_rows = [jnp.sum(jnp.where(p, rank, 0.0), axis=0, keepdims=True) for p in picks]
    den = functools.reduce(lambda a, b: a + b, w_rows)
    w_rows = [w / den * ROUTED_SCALE for w in w_rows]
    eidx_ref[...] = _rows_to_tile(ids, K_PAD, jnp.int32)
    rank_ref[...] = _rows_to_tile(r_rows, K_PAD, F32).astype(jnp.int32)
    wt_ref[...] = _rows_to_tile(w_rows, 128, F32).T


def _route_io(tm, first_block):
    tok_major = lambda i: (i, 0)
    choice_major = lambda i: (0, i)
    in_specs = [
        _mod_spec(tm, 3, first_block),
        _mod_spec(tm, 4, first_block),
        pl.BlockSpec((N_EXPERTS, D_MODEL), lambda i: (0, 0)),
        pl.BlockSpec((N_EXPERTS, 1), lambda i: (0, 0)),
    ]
    out_specs = [
        pl.BlockSpec((tm, HALF), tok_major),
        pl.BlockSpec((K_PAD, tm), choice_major),
        pl.BlockSpec((K_PAD, tm), choice_major),
        pl.BlockSpec((tm, 128), tok_major),
        pl.BlockSpec((N_EXPERTS, 128), lambda i: (0, 0)),
    ]
    out_shapes = [
        jax.ShapeDtypeStruct((N_TRUNK, HALF), jnp.uint32),
        jax.ShapeDtypeStruct((K_PAD, N_TRUNK), jnp.int32),
        jax.ShapeDtypeStruct((K_PAD, N_TRUNK), jnp.int32),
        jax.ShapeDtypeStruct((N_TRUNK, 128), F32),
        jax.ShapeDtypeStruct((N_EXPERTS, 128), jnp.int32),
    ]
    return in_specs, out_specs, out_shapes, [pltpu.VMEM((N_EXPERTS, 128), F32)]


def _finalize_body(cnt_ref, eidx_ref, rank_ref, pos_ref):
    pos_ref[...] = rank_ref[...]

    def per_expert(e, off):
        n_tile = lax.div(cnt_ref[e] + (MOE_TILE - 1), MOE_TILE)
        pos_ref[...] = pos_ref[...] + jnp.where(eidx_ref[...] == e, off, 0)
        return off + n_tile * MOE_TILE

    lax.fori_loop(0, N_EXPERTS, per_expert, jnp.int32(0))


def _finalize(counts, eidx, rank):
    full = pl.BlockSpec((K_PAD, N_TRUNK), lambda: (0, 0))
    return pl.pallas_call(
        _finalize_body,
        in_specs=[pl.BlockSpec(memory_space=pltpu.SMEM), full, full],
        out_specs=full,
        out_shape=jax.ShapeDtypeStruct((K_PAD, N_TRUNK), jnp.int32),
        compiler_params=pltpu.CompilerParams(vmem_limit_bytes=VMEM_LIMIT),
        name="route_finalize",
    )(counts, eidx, rank)


def _ffn(h, wgu, wd):
    gu = jnp.dot(h, wgu, preferred_element_type=F32)
    hid = _silu(gu[:, :EXPERT_FF]) * gu[:, EXPERT_FF:]
    return jnp.dot(hid.astype(BF16), wd, preferred_element_type=F32)


def _cast_ffn_weights(wg_ref, wu_ref, wd_ref, wgub_ref, wdb_ref):
    wgub_ref[:, :EXPERT_FF] = wg_ref[...].astype(BF16)
    wgub_ref[:, EXPERT_FF:] = wu_ref[...].astype(BF16)
    wdb_ref[...] = wd_ref[...].astype(BF16)


_FFN_WEIGHT_SCRATCH = [pltpu.VMEM((D_MODEL, 2 * EXPERT_FF), BF16), pltpu.VMEM((EXPERT_FF, D_MODEL), BF16)]


def _sc_mesh():
    from jax.experimental.pallas import tpu_sc as plsc
    return plsc.VectorSubcoreMesh(core_axis_name="c", subcore_axis_name="s",
                                  num_cores=SC_CORES, num_subcores=SC_SUBCORES)


def _sc_worker_id():
    return lax.axis_index("s") * SC_CORES + lax.axis_index("c")


def _sc_dispatch(hp, pos_flat):
    win = SC_SCATTER_WIN
    per_worker = N_TRUNK // SC_WORKERS

    def body(rows_hbm, idx_hbm, out_hbm, *scratch):
        idx_v, rows_v, sem = scratch[:TOP_K], scratch[TOP_K], scratch[TOP_K + 1]
        base = _sc_worker_id() * per_worker

        @pl.loop(0, per_worker // win)
        def _(j):
            off = base + j * win
            pltpu.sync_copy(rows_hbm.at[pl.ds(off, win)], rows_v)
            for k in range(TOP_K):
                pltpu.sync_copy(idx_hbm.at[pl.ds(k * N_TRUNK + off, win)], idx_v[k])
            copies = [pltpu.async_copy(rows_v, out_hbm.at[idx_v[k]], sem) for k in range(TOP_K)]
            for cp in copies:
                cp.wait()

    return pl.kernel(
        body, mesh=_sc_mesh(),
        out_type=jax.ShapeDtypeStruct((N_SLOT, HALF), jnp.uint32),
        scratch_types=[pltpu.VMEM((win,), jnp.int32)] * TOP_K
        + [pltpu.VMEM((win, HALF), jnp.uint32), pltpu.SemaphoreType.DMA],
        name="sc_dispatch",
    )(hp, pos_flat)


def _sc_return(ys, pos_flat):
    win = SC_GATHER_WIN
    per_worker = N_PAIR // SC_WORKERS

    def body(src_hbm, idx_hbm, out_hbm, idx_v, rows_v, sem):
        base = _sc_worker_id() * per_worker

        @pl.loop(0, per_worker // win)
        def _(j):
            off = base + j * win
            pltpu.sync_copy(idx_hbm.at[pl.ds(off, win)], idx_v)
            pltpu.async_copy(src_hbm.at[idx_v], rows_v, sem).wait()
            pltpu.sync_copy(rows_v, out_hbm.at[pl.ds(off, win)])

    return pl.kernel(
        body, mesh=_sc_mesh(),
        out_type=jax.ShapeDtypeStruct((N_PAIR, HALF), jnp.uint32),
        scratch_types=[pltpu.VMEM((win,), jnp.int32), pltpu.VMEM((win, HALF), jnp.uint32),
                       pltpu.SemaphoreType.DMA],
        name="sc_return",
    )(ys, pos_flat)


N_PLAN = 2 * N_TILE


def _tile_plan_body(cnt0_ref, cnt1_ref, exp_ref, blk0_ref, blk1_ref, live_ref, n_ref):
    def per_expert(e, carry):
        s, b0, b1 = carry
        n0 = lax.div(cnt0_ref[e] + (MOE_TILE - 1), MOE_TILE)
        n1 = lax.div(cnt1_ref[e] + (MOE_TILE - 1), MOE_TILE)

        def fill(j, c):
            exp_ref[s + j] = e
            blk0_ref[s + j] = jnp.maximum(b0 + jnp.minimum(j, n0 - 1), 0)
            blk1_ref[s + j] = jnp.maximum(b1 + jnp.minimum(j, n1 - 1), 0)
            live_ref[s + j] = jnp.where(j < n0, 1, 0) + jnp.where(j < n1, 2, 0)
            return c
        n = jnp.maximum(n0, n1)
        lax.fori_loop(0, n, fill, 0)
        return s + n, b0 + n0, b1 + n1

    zero = jnp.int32(0)
    n_steps, _, _ = lax.fori_loop(0, N_EXPERTS, per_expert, (zero, zero, zero))
    n_ref[0] = n_steps

    def fill_tail(s, c):
        exp_ref[s] = 0
        blk0_ref[s] = 0
        blk1_ref[s] = 0
        live_ref[s] = 0
        return c
    lax.fori_loop(n_steps, N_PLAN, fill_tail, 0)


def _tile_plan(counts0, counts1):
    smem = pl.BlockSpec(memory_space=pltpu.SMEM)
    steps = jax.ShapeDtypeStruct((N_PLAN,), jnp.int32)
    return pl.pallas_call(
        _tile_plan_body,
        in_specs=[smem, smem],
        out_specs=[smem] * 5,
        out_shape=[steps, steps, steps, steps, jax.ShapeDtypeStruct((1,), jnp.int32)],
        name="tile_plan",
    )(counts0, counts1)


def _expert_body(exp_ref, blk0_ref, blk1_ref, live_ref, xs0_ref, xs1_ref, wg_ref, wu_ref, wd_ref,
                 ys0_ref, ys1_ref, wgub_ref, wdb_ref):
    i = pl.program_id(0)

    @pl.when(jnp.logical_or(i == 0, exp_ref[i] != exp_ref[jnp.maximum(i - 1, 0)]))
    def _():
        _cast_ffn_weights(wg_ref, wu_ref, wd_ref, wgub_ref, wdb_ref)

    for bit, xs_ref, ys_ref in ((1, xs0_ref, ys0_ref), (2, xs1_ref, ys1_ref)):
        @pl.when((live_ref[i] & bit) != 0)
        def _():
            lo, hi = _unpack_bf16_pair(xs_ref[...])
            h = jnp.concatenate([lo.astype(BF16), hi.astype(BF16)], axis=1)
            y = _ffn(h, wgub_ref[...], wdb_ref[...])
            ys_ref[...] = _pack_bf16_pair(y[:, :HALF], y[:, HALF:])


def _expert_ffn(plan, xs0, xs1, layer, w_gate, w_up, w_down):
    exp, blk0, blk1, live, n_steps = plan
    tile0 = lambda i, exp, blk0, blk1, live: (blk0[i], 0)
    tile1 = lambda i, exp, blk0, blk1, live: (blk1[i], 0)
    ew = lambda shape: pl.BlockSpec((None, None) + shape,
                                    lambda i, exp, blk0, blk1, live: (layer, exp[i], 0, 0))
    slots = jax.ShapeDtypeStruct((N_SLOT, HALF), jnp.uint32)
    return pl.pallas_call(
        _expert_body,
        grid_spec=pltpu.PrefetchScalarGridSpec(
            num_scalar_prefetch=4,
            grid=(n_steps[0],),
            in_specs=[
                pl.BlockSpec((MOE_TILE, HALF), tile0), pl.BlockSpec((MOE_TILE, HALF), tile1),
                ew((D_MODEL, EXPERT_FF)), ew((D_MODEL, EXPERT_FF)), ew((EXPERT_FF, D_MODEL)),
            ],
            out_specs=[pl.BlockSpec((MOE_TILE, HALF), tile0), pl.BlockSpec((MOE_TILE, HALF), tile1)],
            scratch_shapes=_FFN_WEIGHT_SCRATCH,
        ),
        out_shape=[slots, slots],
        compiler_params=pltpu.CompilerParams(
            dimension_semantics=("arbitrary",), vmem_limit_bytes=VMEM_LIMIT),
        name="expert_ffn",
    )(exp, blk0, blk1, live, xs0, xs1, w_gate, w_up, w_down)


def _shared_body(x_ref, sh_ref, sc_ref, sg_ref, su_ref, sd_ref, o_ref, sgub_ref, sdb_ref):
    @pl.when(pl.program_id(0) == 0)
    def _():
        _cast_ffn_weights(sg_ref, su_ref, sd_ref, sgub_ref, sdb_ref)

    h = (x_ref[...].astype(F32) * (1.0 + sc_ref[...]) + sh_ref[...]).astype(BF16)
    o_ref[...] = _ffn(h, sgub_ref[...], sdb_ref[...]).astype(o_ref.dtype)


def _shared_ffn(trunk, x, mod, layer, s_gate, s_up, s_down):
    tm = TOK_BLOCK
    first_block = trunk * (N_TRUNK // tm)
    row = lambda i: (i, 0)
    sw = lambda shape: pl.BlockSpec((None,) + shape, lambda i: (layer, 0, 0))
    return pl.pallas_call(
        _shared_body,
        grid=(N_TRUNK // tm,),
        in_specs=[
            pl.BlockSpec((tm, D_MODEL), row),
            _mod_spec(tm, 3, first_block), _mod_spec(tm, 4, first_block),
            sw((D_MODEL, EXPERT_FF)), sw((D_MODEL, EXPERT_FF)), sw((EXPERT_FF, D_MODEL)),
        ],
        out_specs=pl.BlockSpec((tm, D_MODEL), row),
        out_shape=jax.ShapeDtypeStruct((N_TRUNK, D_MODEL), RES_DTYPE),
        scratch_shapes=_FFN_WEIGHT_SCRATCH,
        compiler_params=pltpu.CompilerParams(
            dimension_semantics=("arbitrary",), vmem_limit_bytes=VMEM_LIMIT),
        name="shared_ffn",
    )(x, mod, mod, s_gate, s_up, s_down)


def _combine_body(x_ref, g2_ref, yk_ref, wt_ref, ysh_ref, lng_ref, lnb_ref, o_ref):
    x = x_ref[...].astype(F32)
    wt = wt_ref[...]
    lo_acc = jnp.zeros((x.shape[0], HALF), F32)
    hi_acc = jnp.zeros((x.shape[0], HALF), F32)
    for k in range(TOP_K):
        lo, hi = _unpack_bf16_pair(yk_ref[k])
        w = wt[:, k:k + 1]
        lo_acc = lo_acc + w * lo
        hi_acc = hi_acc + w * hi
    y = ysh_ref[...].astype(F32) + jnp.concatenate([lo_acc, hi_acc], axis=1)
    z = DEEPNORM_ALPHA * x + g2_ref[...] * y
    o_ref[...] = _layer_norm(z, lng_ref[...], lnb_ref[...]).astype(o_ref.dtype)


def _combine(trunk, x, mod, yk, wt, ysh, ln_g, ln_b, out_dtype):
    tm = TOK_BLOCK
    first_block = trunk * (N_TRUNK // tm)
    row = lambda i: (i, 0)
    vec = pl.BlockSpec((1, D_MODEL), lambda i: (0, 0))
    return pl.pallas_call(
        _combine_body,
        grid=(N_TRUNK // tm,),
        in_specs=[
            pl.BlockSpec((tm, D_MODEL), row),
            _mod_spec(tm, 5, first_block),
            pl.BlockSpec((TOP_K, tm, HALF), lambda i: (0, i, 0)),
            pl.BlockSpec((tm, 128), row),
            pl.BlockSpec((tm, D_MODEL), row),
            vec, vec,
        ],
        out_specs=pl.BlockSpec((tm, D_MODEL), row),
        out_shape=jax.ShapeDtypeStruct((N_TRUNK, D_MODEL), out_dtype),
        compiler_params=pltpu.CompilerParams(
            dimension_semantics=("arbitrary",), vmem_limit_bytes=VMEM_LIMIT),
        name="moe_combine_ln",
    )(x, mod, yk, wt, ysh, ln_g, ln_b)


def _moe_dispatch(routing):
    hp, eidx, rank, wt, counts = routing
    counts = counts[:, 0]
    pos_flat = _finalize(counts, eidx, rank).reshape(K_PAD * N_TRUNK)
    return _sc_dispatch(hp, pos_flat), pos_flat, counts, wt


def _moe_experts(plans, layer, w_gate, w_up, w_down):
    (xs0, pos0, counts0, _), (xs1, pos1, counts1, _) = plans
    ys = _expert_ffn(_tile_plan(counts0, counts1), xs0, xs1, layer, w_gate, w_up, w_down)
    return [_sc_return(y, pos).reshape(TOP_K, N_TRUNK, HALF) for y, pos in zip(ys, (pos0, pos1))]


def kernel(x_prompt, x_sample, state_ret, c, c_ctx, ada_w, ada_b, ln_g, ln_b, ret_w_in, ret_w_out, ret_decay, conv_w_in, conv_w, conv_w_out, moe_router, moe_bias, moe_w_gate, moe_w_up, moe_w_down, shared_w_gate, shared_w_up, shared_w_down):
    x = (x_prompt.reshape(N_CTX, D_MODEL), x_sample.reshape(N_LAT, D_MODEL))
    cond = jnp.concatenate(
        [c_ctx[None, :], c, jnp.zeros((N_COND - 1 - DEC_BATCH, D_MODEL), F32)], axis=0)
    mods = _ada_table(cond, ada_w, ada_b).reshape(DEPTH, N_COND, 1, 6 * D_MODEL)
    cos, sin = _rope_tables()
    router_wt = jnp.swapaxes(moe_router, 1, 2)

    states = None
    for i in range(DEPTH):
        j = i // 2
        mod = mods[i]
        lng = ln_g[i].reshape(2, 1, D_MODEL)
        lnb = ln_b[i].reshape(2, 1, D_MODEL)
        bias_col = moe_bias[i].reshape(N_EXPERTS, 1)
        if i % 2 == 0:
            w_in_bf = _to_bf16(ret_w_in, j)
        mixed, plans, shared = [], [], []
        for trunk in range(2):
            if i % 2 == 0:
                if trunk == 0:
                    a, states = _retention_ctx(x[0], mod, w_in_bf, ret_decay[j], j, states)
                else:
                    a = _retention_lat(x[1], mod, w_in_bf, ret_decay[j], state_ret, j, cos, sin)
                xm, *routing = _ret_out(trunk, a, x[trunk], mod, ret_w_out, j, lng[0], lnb[0],
                                        router_wt[i], bias_col)
            else:
                xm, *routing = _conv_layer(trunk, x[trunk], mod, conv_w_in, conv_w_out, j, conv_w[j],
                                           lng[0], lnb[0], router_wt[i], bias_col)
            mixed.append(xm)
            plans.append(_moe_dispatch(routing))
            shared.append(_shared_ffn(trunk, xm, mod, i, shared_w_gate, shared_w_up, shared_w_down))
        yks = _moe_experts(plans, i, moe_w_gate, moe_w_up, moe_w_down)
        out_dtype = F32 if i == DEPTH - 1 else RES_DTYPE
        x = tuple(_combine(trunk, mixed[trunk], mod, yks[trunk], plans[trunk][3], shared[trunk],
                           lng[1], lnb[1], out_dtype) for trunk in range(2))

    y_prompt = x[0].reshape(BATCH, SEQ, D_MODEL)
    y_sample = x[1].reshape(DEC_BATCH, DEC_SEQ, D_MODEL)
    return y_prompt, y_sample, states
```

```python
import functools

import jax
import jax.numpy as jnp
from jax import lax
from jax.experimental import pallas as pl
from jax.experimental.pallas import tpu as pltpu

F32 = jnp.float32
BF16 = jnp.bfloat16

D_MODEL = 1024
BATCH = 32
SEQ = 256
DEPTH = 4
DEC_BATCH = 8
DEC_SEQ = 1024
GRID_W = 64
RET_HEADS = 4
RET_DK = D_MODEL // RET_HEADS
RET_DV = 2 * D_MODEL // RET_HEADS
ROPE_BASE = 10000.0
N_EXPERTS = 64
TOP_K = 6
N_GROUPS = 8
TOPK_GROUPS = 4
EXPERT_FF = 256
ROUTED_SCALE = 2.5
LN_EPS = 1e-5
DEEPNORM_ALPHA = (2.0 * DEPTH) ** 0.25

N_CTX = BATCH * SEQ
N_LAT = DEC_BATCH * DEC_SEQ
N_COND = 16
RET_CHUNK = 256
assert N_CTX == N_LAT
N_TRUNK = N_CTX
TOK_BLOCK = 512
MIX_BLOCK = 512
VMEM_LIMIT = 56 * 1024 * 1024
RES_DTYPE = BF16

HALF = D_MODEL // 2
MOE_TILE = 896
N_PAIR = N_TRUNK * TOP_K
N_TILE = -(-(N_PAIR + N_EXPERTS * (MOE_TILE - 1)) // MOE_TILE)
N_SLOT = N_TILE * MOE_TILE
K_PAD = 8

SC_CORES = 2
SC_SUBCORES = 16
SC_WORKERS = SC_CORES * SC_SUBCORES
SC_SCATTER_WIN = 64
SC_GATHER_WIN = 128


def _cond_row(tok_block_idx, tok_block):
    t0 = tok_block_idx * tok_block
    return jnp.where(t0 < N_CTX, 0, 1 + (t0 - N_CTX) // DEC_SEQ)


def _silu(x):
    return x * jax.nn.sigmoid(x)


def _layer_norm(z, g, b):
    mu = jnp.mean(z, axis=-1, keepdims=True)
    zc = z - mu
    var = jnp.mean(zc * zc, axis=-1, keepdims=True)
    return zc * lax.rsqrt(var + LN_EPS) * g + b


def _ada_body(cond_ref, w_ref, b_ref, o_ref):
    s = _silu(cond_ref[...]).astype(BF16)
    o_ref[...] = jnp.dot(s, w_ref[...].astype(BF16), preferred_element_type=F32) + b_ref[...]


def _ada_table(cond, ada_w, ada_b):
    tn = 2048
    return pl.pallas_call(
        _ada_body,
        grid=(DEPTH, 6 * D_MODEL // tn),
        in_specs=[
            pl.BlockSpec((N_COND, D_MODEL), lambda l, j: (0, 0)),
            pl.BlockSpec((None, D_MODEL, tn), lambda l, j: (l, 0, j)),
            pl.BlockSpec((None, 1, tn), lambda l, j: (l, 0, j)),
        ],
        out_specs=pl.BlockSpec((None, N_COND, tn), lambda l, j: (l, 0, j)),
        out_shape=jax.ShapeDtypeStruct((DEPTH, N_COND, 6 * D_MODEL), F32),
        compiler_params=pltpu.CompilerParams(
            dimension_semantics=("arbitrary", "arbitrary"), vmem_limit_bytes=VMEM_LIMIT),
        name="ada_table",
    )(cond, ada_w, ada_b.reshape(DEPTH, 1, 6 * D_MODEL))


def _mod_spec(tok_block, col, first_block=0):
    def index_map(*idx):
        return (_cond_row(first_block + idx[0], tok_block), 0, col)
    return pl.BlockSpec((None, 1, D_MODEL), index_map)


def _to_bf16_body(w_ref, o_ref):
    o_ref[...] = w_ref[...].astype(BF16)


def _to_bf16(w_all, w_idx):
    k_dim, n_out = w_all.shape[1:]
    tn = 1024
    return pl.pallas_call(
        _to_bf16_body,
        grid=(n_out // tn,),
        in_specs=[pl.BlockSpec((None, k_dim, tn), lambda j: (w_idx, 0, j))],
        out_specs=pl.BlockSpec((k_dim, tn), lambda j: (0, j)),
        out_shape=jax.ShapeDtypeStruct((k_dim, n_out), BF16),
        compiler_params=pltpu.CompilerParams(
            dimension_semantics=("arbitrary",), vmem_limit_bytes=VMEM_LIMIT),
        name="weight_to_bf16",
    )(w_all)


RET_GROUP = 1024


def _head_proj_specs():
    return [
        pl.BlockSpec((D_MODEL, RET_DK), lambda t, h: (0, h)),
        pl.BlockSpec((D_MODEL, RET_DK), lambda t, h: (0, RET_HEADS + h)),
        pl.BlockSpec((D_MODEL, RET_DV), lambda t, h: (0, RET_HEADS + h)),
        pl.BlockSpec((D_MODEL, RET_DV), lambda t, h: (0, 2 * RET_HEADS + h)),
    ]


def _head_proj(x_ref, sh_ref, sc_ref, wq_ref, wk_ref, wv_ref, wg_ref):
    h = (x_ref[...].astype(F32) * (1.0 + sc_ref[...]) + sh_ref[...]).astype(BF16)
    dot = lambda w_ref: jnp.dot(h, w_ref[...], preferred_element_type=F32)
    return dot(wq_ref), dot(wk_ref), dot(wv_ref).astype(BF16), dot(wg_ref)

def _log_sigmoid(v):
    return jnp.minimum(v, 0.0) - jnp.log1p(jnp.exp(-jnp.abs(v)))


def _decay_tables(dec_ref, head):
    c = RET_CHUNK
    lgf = _log_sigmoid(jnp.full((c, c), dec_ref[0, head], F32))
    lgb = _log_sigmoid(jnp.full((c, c), dec_ref[1, head], F32))
    row = lax.broadcasted_iota(jnp.int32, (c, c), 0).astype(F32)
    col = lax.broadcasted_iota(jnp.int32, (c, c), 1).astype(F32)
    diff = row - col
    kscale = RET_DK ** -0.5
    intra = jnp.where(diff > 0, jnp.exp(lgf * diff),
                      jnp.where(diff < 0, jnp.exp(-lgb * diff), 2.0)) * kscale
    qdec_f = jnp.exp(lgf * (row + 1.0))
    qdec_b = jnp.exp(lgb * (c - row))
    kdec_f = jnp.exp(lgf * (c - 1.0 - row)) * kscale
    kdec_b = jnp.exp(lgb * row) * kscale
    cdec_f = jnp.exp(lgf * c)
    cdec_b = jnp.exp(lgb * c)
    return intra, qdec_f, qdec_b, kdec_f, kdec_b, cdec_f, cdec_b


def _head_norm_gate(o, g):
    mu = jnp.mean(o, axis=-1, keepdims=True)
    oc = o - mu
    var = jnp.mean(oc * oc, axis=-1, keepdims=True)
    on = oc * lax.rsqrt(var + LN_EPS)
    return (_silu(g.astype(F32)) * on).astype(BF16)


_NT = (((1,), (1,)), ((), ()))
_TN = (((0,), (0,)), ((), ()))


def _ret_ctx_body(dec_ref, x_ref, sh_ref, sc_ref, wq_ref, wk_ref, wv_ref, wg_ref, *rest):
    a_ref, st_ref, tab_ref = rest[-3:]
    head = pl.program_id(1)

    @pl.when(pl.program_id(0) == 0)
    def _():
        intra, _, _, kdec_f, kdec_b, _, _ = _decay_tables(dec_ref, head)
        tab_ref[head, 0] = intra
        tab_ref[head, 1] = kdec_f
        tab_ref[head, 2] = kdec_b

    q, k, v, g = _head_proj(x_ref, sh_ref, sc_ref, wq_ref, wk_ref, wv_ref, wg_ref)
    for s in range(RET_GROUP // SEQ):
        rows = slice(s * SEQ, (s + 1) * SEQ)
        scores = lax.dot_general(q[rows].astype(BF16), k[rows].astype(BF16), _NT, preferred_element_type=F32)
        p = (scores * tab_ref[head, 0]).astype(BF16)
        o = jnp.dot(p, v[rows], preferred_element_type=F32)
        a_ref[rows, :] = _head_norm_gate(o, g[rows])
        st_ref[s, 0] = lax.dot_general((k[rows] * tab_ref[head, 1]).astype(BF16), v[rows], _TN,
                                       preferred_element_type=F32)
        st_ref[s, 1] = lax.dot_general((k[rows] * tab_ref[head, 2]).astype(BF16), v[rows], _TN,
                                       preferred_element_type=F32)


def _retention_ctx(x, mod, w_in_bf, decay, ret_idx, states):
    assert SEQ == RET_CHUNK
    seqs = RET_GROUP // SEQ
    n_ret = (DEPTH + 1) // 2
    st = jax.ShapeDtypeStruct((BATCH, n_ret, 2, RET_HEADS, RET_DK, RET_DV), F32)
    st_spec = pl.BlockSpec((seqs, None, 2, None, RET_DK, RET_DV), lambda t, h: (t, ret_idx, 0, h, 0, 0))
    in_specs = [
        pl.BlockSpec(memory_space=pltpu.SMEM),
        pl.BlockSpec((RET_GROUP, D_MODEL), lambda t, h: (t, 0)),
        _mod_spec(RET_GROUP, 0), _mod_spec(RET_GROUP, 1),
    ] + _head_proj_specs()
    args = (decay, x, mod, mod, w_in_bf, w_in_bf, w_in_bf, w_in_bf)
    aliases = {}
    if states is not None:
        in_specs.append(pl.BlockSpec(memory_space=pl.ANY))
        aliases = {len(args): 1}
        args += (states,)
    return pl.pallas_call(
        _ret_ctx_body,
        grid=(N_CTX // RET_GROUP, RET_HEADS),
        in_specs=in_specs,
        out_specs=[pl.BlockSpec((RET_GROUP, RET_DV), lambda t, h: (t, h)), st_spec],
        out_shape=[jax.ShapeDtypeStruct((N_CTX, RET_HEADS * RET_DV), BF16), st],
        input_output_aliases=aliases,
        scratch_shapes=[pltpu.VMEM((RET_HEADS, 3, RET_CHUNK, RET_CHUNK), F32)],
        compiler_params=pltpu.CompilerParams(
            dimension_semantics=("arbitrary", "arbitrary"), vmem_limit_bytes=VMEM_LIMIT),
        name="retention_ctx",
    )(*args)


def _rope(x, cos, sin):
    halves = [pltpu.roll(x[:, s:s + 128], 64, axis=1) for s in (0, 128)]
    return x * cos + jnp.concatenate(halves, axis=1) * sin


def _ret_lat_body(dec_ref, x_ref, sh_ref, sc_ref, wq_ref, wk_ref, wv_ref, wg_ref,
                  s0f_ref, s0b_ref, cos_ref, sin_ref,
                  a_ref, qr_ref, kr_ref, v_ref, g_ref, o_ref, st_ref):
    head = pl.program_id(1)
    c = RET_CHUNK
    nc = DEC_SEQ // c
    intra, qdec_f, qdec_b, kdec_f, kdec_b, cdec_f, cdec_b = _decay_tables(dec_ref, head)
    cdec_f = jnp.concatenate([cdec_f, cdec_f], axis=1)
    cdec_b = jnp.concatenate([cdec_b, cdec_b], axis=1)

    q, k, v, g = _head_proj(x_ref, sh_ref, sc_ref, wq_ref, wk_ref, wv_ref, wg_ref)
    qr_ref[...] = _rope(q, cos_ref[...], sin_ref[...])
    kr_ref[...] = _rope(k, cos_ref[...], sin_ref[...])
    v_ref[...] = v
    g_ref[...] = g

    st_ref[...] = s0f_ref[...]
    for ci in range(nc):
        rows = pl.ds(ci * c, c)
        q = qr_ref[rows, :]
        k = kr_ref[rows, :]
        v = v_ref[rows, :]
        scores = lax.dot_general(q.astype(BF16), k.astype(BF16), _NT, preferred_element_type=F32)
        p = (scores * intra).astype(BF16)
        o = jnp.dot(p, v, preferred_element_type=F32)
        qd = (q * qdec_f).astype(BF16)
        o = o + jnp.dot(qd, st_ref[...].astype(BF16), preferred_element_type=F32)
        o_ref[rows, :] = o
        if ci + 1 < nc:
            kd = (k * kdec_f).astype(BF16)
            st_ref[...] = cdec_f * st_ref[...] + lax.dot_general(kd, v, _TN, preferred_element_type=F32)

    st_ref[...] = s0b_ref[...]
    for ci in reversed(range(nc)):
        rows = pl.ds(ci * c, c)
        qd = (qr_ref[rows, :] * qdec_b).astype(BF16)
        o = o_ref[rows, :] + jnp.dot(qd, st_ref[...].astype(BF16), preferred_element_type=F32)
        a_ref[rows, :] = _head_norm_gate(o, g_ref[rows, :])
        if ci > 0:
            kd = (kr_ref[rows, :] * kdec_b).astype(BF16)
            st_ref[...] = cdec_b * st_ref[...] + lax.dot_general(kd, v_ref[rows, :], _TN,
                                                                 preferred_element_type=F32)


def _rope_tables():
    half = RET_DK // 4
    freqs = ROPE_BASE ** (-jnp.arange(half, dtype=F32) / half)
    t = jnp.arange(DEC_SEQ)
    row = (t // GRID_W).astype(F32)
    col = (t % GRID_W).astype(F32)
    ang_r = row[:, None] * freqs[None, :]
    ang_c = col[:, None] * freqs[None, :]
    cos = jnp.concatenate([jnp.cos(ang_r)] * 2 + [jnp.cos(ang_c)] * 2, axis=1)
    sin = jnp.concatenate([-jnp.sin(ang_r), jnp.sin(ang_r), -jnp.sin(ang_c), jnp.sin(ang_c)], axis=1)
    return cos, sin


def _retention_lat(x, mod, w_in_bf, decay, state_ret, ret_idx, cos, sin):
    n = DEC_SEQ
    assert n == RET_GROUP
    s0_spec = lambda d: pl.BlockSpec((None, None, None, None, RET_DK, RET_DV),
                                     lambda b, h: (b, ret_idx, d, h, 0, 0))
    tab_spec = pl.BlockSpec((n, RET_DK), lambda b, h: (0, 0))
    first_block = N_CTX // n
    return pl.pallas_call(
        _ret_lat_body,
        grid=(DEC_BATCH, RET_HEADS),
        in_specs=[
            pl.BlockSpec(memory_space=pltpu.SMEM),
            pl.BlockSpec((n, D_MODEL), lambda b, h: (b, 0)),
            _mod_spec(n, 0, first_block), _mod_spec(n, 1, first_block),
        ] + _head_proj_specs() + [s0_spec(0), s0_spec(1), tab_spec, tab_spec],
        out_specs=pl.BlockSpec((n, RET_DV), lambda b, h: (b, h)),
        out_shape=jax.ShapeDtypeStruct((N_LAT, RET_HEADS * RET_DV), BF16),
        scratch_shapes=[
            pltpu.VMEM((n, RET_DK), F32), pltpu.VMEM((n, RET_DK), F32),
            pltpu.VMEM((n, RET_DV), BF16), pltpu.VMEM((n, RET_DV), F32),
            pltpu.VMEM((n, RET_DV), F32), pltpu.VMEM((RET_DK, RET_DV), F32),
        ],
        compiler_params=pltpu.CompilerParams(
            dimension_semantics=("arbitrary", "arbitrary"), vmem_limit_bytes=VMEM_LIMIT),
        name="retention_lat",
    )(decay, x, mod, mod, w_in_bf, w_in_bf, w_in_bf, w_in_bf, state_ret, state_ret, cos, sin)


N_ROUTE_IN = 4
N_ROUTE_OUT = 5


def _split_route_refs(rest, n_scratch):
    route_in = rest[:N_ROUTE_IN]
    o_ref = rest[N_ROUTE_IN]
    route_out = rest[N_ROUTE_IN + 1:N_ROUTE_IN + 1 + N_ROUTE_OUT]
    return route_in, o_ref, route_out, rest[len(rest) - n_scratch:]


def _ret_out_body(a_ref, x_ref, g1_ref, w_ref, lng_ref, lnb_ref, *rest):
    route_in, o_ref, route_out, (wbf_ref, carry_ref) = _split_route_refs(rest, 2)

    @pl.when(pl.program_id(0) == 0)
    def _():
        wbf_ref[...] = w_ref[...].astype(BF16)
        carry_ref[...] = jnp.zeros_like(carry_ref)

    y = jnp.dot(a_ref[...], wbf_ref[...], preferred_element_type=F32)
    z = DEEPNORM_ALPHA * x_ref[...].astype(F32) + g1_ref[...] * y
    xn = _layer_norm(z, lng_ref[...], lnb_ref[...])
    o_ref[...] = xn.astype(o_ref.dtype)
    _route_block(xn, *route_in, *route_out, carry_ref)


def _ret_out(trunk, a, x, mod, w_all, w_idx, ln_g, ln_b, router_wt, bias_col):
    tm = MIX_BLOCK
    k_dim = w_all.shape[1]
    first_block = trunk * (N_TRUNK // tm)
    row = lambda i: (i, 0)
    vec = pl.BlockSpec((1, D_MODEL), lambda i: (0, 0))
    r_in, r_out, r_shapes, r_scratch = _route_io(tm, first_block)
    return pl.pallas_call(
        _ret_out_body,
        grid=(N_TRUNK // tm,),
        in_specs=[
            pl.BlockSpec((tm, k_dim), row),
            pl.BlockSpec((tm, D_MODEL), row),
            _mod_spec(tm, 2, first_block),
            pl.BlockSpec((None, k_dim, D_MODEL), lambda i: (w_idx, 0, 0)),
            vec, vec,
        ] + r_in,
        out_specs=[pl.BlockSpec((tm, D_MODEL), row)] + r_out,
        out_shape=[jax.ShapeDtypeStruct((N_TRUNK, D_MODEL), RES_DTYPE)] + r_shapes,
        scratch_shapes=[pltpu.VMEM((k_dim, D_MODEL), BF16)] + r_scratch,
        compiler_params=pltpu.CompilerParams(
            dimension_semantics=("arbitrary",), vmem_limit_bytes=VMEM_LIMIT),
        name="ret_out_ln_route",
    )(a, x, mod, w_all, ln_g, ln_b, mod, mod, router_wt, bias_col)


def _conv_layer_body(seg, x_ref, sh_ref, sc_ref, g1_ref, win_ref, wout_ref, cw_ref, lng_ref, lnb_ref, *rest):
    route_in, o_ref, route_out, (winb_ref, woutb_ref, carry_ref) = _split_route_refs(rest, 3)
    tm = o_ref.shape[0]

    @pl.when(pl.program_id(0) == 0)
    def _():
        winb_ref[...] = win_ref[...].astype(BF16)
        woutb_ref[...] = wout_ref[...].astype(BF16)
        carry_ref[...] = jnp.zeros_like(carry_ref)

    x = x_ref[...].astype(F32)
    h = (x * (1.0 + sc_ref[...]) + sh_ref[...]).astype(BF16)
    proj = jnp.dot(h, winb_ref[...], preferred_element_type=F32)
    bg, cg, xt = (proj[:, c * D_MODEL:(c + 1) * D_MODEL] for c in range(3))
    u = cg * xt
    pos = lax.broadcasted_iota(jnp.int32, (tm, D_MODEL), 0) & (seg - 1)
    u_prev = jnp.where(pos == 0, 0.0, pltpu.roll(u, 1, axis=0))
    u_next = jnp.where(pos == seg - 1, 0.0, pltpu.roll(u, tm - 1, axis=0))
    cu = u_prev * cw_ref[0:1, :] + u * cw_ref[1:2, :] + u_next * cw_ref[2:3, :]
    y = jnp.dot((bg * cu).astype(BF16), woutb_ref[...], preferred_element_type=F32)
    z = DEEPNORM_ALPHA * x + g1_ref[...] * y
    xn = _layer_norm(z, lng_ref[...], lnb_ref[...])
    o_ref[...] = xn.astype(o_ref.dtype)
    _route_block(xn, *route_in, *route_out, carry_ref)


def _conv_layer(trunk, x, mod, w_in_all, w_out_all, w_idx, conv_w, ln_g, ln_b, router_wt, bias_col):
    tm = MIX_BLOCK
    first_block = trunk * (N_TRUNK // tm)
    seg = SEQ if trunk == 0 else GRID_W
    assert tm % seg == 0 and seg & (seg - 1) == 0
    row = lambda i: (i, 0)
    vec = pl.BlockSpec((1, D_MODEL), lambda i: (0, 0))
    once = pl.Buffered(1)
    r_in, r_out, r_shapes, r_scratch = _route_io(tm, first_block)
    mod_spec = lambda col: _mod_spec(tm, col, first_block)
    return pl.pallas_call(
        functools.partial(_conv_layer_body, seg),
        grid=(N_TRUNK // tm,),
        in_specs=[
            pl.BlockSpec((tm, D_MODEL), row),
            mod_spec(0), mod_spec(1), mod_spec(2),
            pl.BlockSpec((None, D_MODEL, 3 * D_MODEL), lambda i: (w_idx, 0, 0), pipeline_mode=once),
            pl.BlockSpec((None, D_MODEL, D_MODEL), lambda i: (w_idx, 0, 0), pipeline_mode=once),
            pl.BlockSpec((3, D_MODEL), lambda i: (0, 0)),
            vec, vec,
        ] + r_in,
        out_specs=[pl.BlockSpec((tm, D_MODEL), row)] + r_out,
        out_shape=[jax.ShapeDtypeStruct((N_TRUNK, D_MODEL), RES_DTYPE)] + r_shapes,
        scratch_shapes=[pltpu.VMEM((D_MODEL, 3 * D_MODEL), BF16), pltpu.VMEM((D_MODEL, D_MODEL), BF16)]
        + r_scratch,
        compiler_params=pltpu.CompilerParams(
            dimension_semantics=("arbitrary",), vmem_limit_bytes=VMEM_LIMIT),
        name="conv_layer_route",
    )(x, mod, mod, mod, w_in_all, w_out_all, conv_w, ln_g, ln_b, mod, mod, router_wt, bias_col)


def _split_bf16(v):
    hi = v.astype(BF16)
    lo = (v - hi.astype(F32)).astype(BF16)
    return hi, lo


def _pack_bf16_pair(lo_f32, hi_f32):
    lo = lax.bitcast_convert_type(lo_f32.astype(BF16).astype(F32), jnp.uint32) >> 16
    hi = lax.bitcast_convert_type(hi_f32.astype(BF16).astype(F32), jnp.uint32) & jnp.uint32(0xFFFF0000)
    return hi | lo


def _unpack_bf16_pair(u):
    lo = lax.bitcast_convert_type(u << 16, F32)
    hi = lax.bitcast_convert_type(u & jnp.uint32(0xFFFF0000), F32)
    return lo, hi


def _rows_to_tile(rows, n_sub, dtype):
    tm = rows[0].shape[1]
    sub = lax.broadcasted_iota(jnp.int32, (n_sub, tm), 0)
    out = jnp.zeros((n_sub, tm), dtype)
    for k, r in enumerate(rows):
        out = jnp.where(sub == k, jnp.broadcast_to(r.astype(dtype), (n_sub, tm)), out)
    return out


def _route_block(xn, sh_ref, sc_ref, rwt_ref, bias_ref,
                 hp_ref, eidx_ref, rank_ref, wt_ref, cnt_ref, carry_ref):
    tm = xn.shape[0]
    e = N_EXPERTS
    per = e // N_GROUPS
    neg = -jnp.inf
    h = xn * (1.0 + sc_ref[...]) + sh_ref[...]
    hp_ref[...] = _pack_bf16_pair(h[:, :HALF], h[:, HALF:])
    h_hi, h_lo = _split_bf16(h)
    w_hi, w_lo = _split_bf16(rwt_ref[...])
    dot = lambda a, b: lax.dot_general(a, b, _NT, preferred_element_type=F32)
    logits = dot(w_hi, h_hi) + (dot(w_hi, h_lo) + dot(w_lo, h_hi))
    s = jax.nn.sigmoid(logits)
    sel = s + bias_ref[...]

    g3 = sel.reshape(N_GROUPS, per, tm)
    sub = lax.broadcasted_iota(jnp.int32, (N_GROUPS, per, tm), 1)
    m1 = jnp.max(g3, axis=1, keepdims=True)
    i1 = jnp.min(jnp.where(g3 == m1, sub, per), axis=1, keepdims=True)
    m2 = jnp.max(jnp.where(sub == i1, neg, g3), axis=1, keepdims=True)
    gs = (m1 + m2).reshape(N_GROUPS, tm)

    gi = lax.broadcasted_iota(jnp.int32, (N_GROUPS, tm), 0)
    gmask = jnp.zeros((N_GROUPS, tm), jnp.bool_)
    cur = gs
    for _ in range(TOPK_GROUPS):
        m = jnp.max(cur, axis=0, keepdims=True)
        idx = jnp.min(jnp.where(cur == m, gi, N_GROUPS), axis=0, keepdims=True)
        pick = gi == idx
        gmask = jnp.logical_or(gmask, pick)
        cur = jnp.where(pick, neg, cur)
    emask = jnp.broadcast_to(gmask.reshape(N_GROUPS, 1, tm), (N_GROUPS, per, tm)).reshape(e, tm)

    ei = lax.broadcasted_iota(jnp.int32, (e, tm), 0)
    picks, ids = [], []
    cur = jnp.where(emask, sel, neg)
    for _ in range(TOP_K):
        m = jnp.max(cur, axis=0, keepdims=True)
        idx = jnp.min(jnp.where(cur == m, ei, e), axis=0, keepdims=True)
        pick = ei == idx
        picks.append(pick)
        ids.append(idx)
        cur = jnp.where(pick, neg, cur)

    chosen = functools.reduce(jnp.logical_or, picks)
    cf = jnp.where(chosen, 1.0, 0.0)
    before = (lax.broadcasted_iota(jnp.int32, (tm, tm), 0)
              < lax.broadcasted_iota(jnp.int32, (tm, tm), 1)).astype(BF16)
    rank = carry_ref[:, 0:1] + jnp.dot(cf.astype(BF16), before, preferred_element_type=F32)
    carry_ref[...] = carry_ref[...] + jnp.sum(cf, axis=1, keepdims=True)
    cnt_ref[...] = carry_ref[...].astype(jnp.int32)

    w_rows = [jnp.sum(jnp.where(p, s, 0.0), axis=0, keepdims=True) for p in picks]
    r_rows = [jnp.sum(jnp.where(p, rank, 0.0), axis=0, keepdims=True) for p in picks]
    den = functools.reduce(lambda a, b: a + b, w_rows)
    w_rows = [w / den * ROUTED_SCALE for w in w_rows]
    eidx_ref[...] = _rows_to_tile(ids, K_PAD, jnp.int32)
    rank_ref[...] = _rows_to_tile(r_rows, K_PAD, F32).astype(jnp.int32)
    wt_ref[...] = _rows_to_tile(w_rows, 128, F32).T


def _route_io(tm, first_block):
    tok_major = lambda i: (i, 0)
    choice_major = lambda i: (0, i)
    in_specs = [
        _mod_spec(tm, 3, first_block),
        _mod_spec(tm, 4, first_block),
        pl.BlockSpec((N_EXPERTS, D_MODEL), lambda i: (0, 0)),
        pl.BlockSpec((N_EXPERTS, 1), lambda i: (0, 0)),
    ]
    out_specs = [
        pl.BlockSpec((tm, HALF), tok_major),
        pl.BlockSpec((K_PAD, tm), choice_major),
        pl.BlockSpec((K_PAD, tm), choice_major),
        pl.BlockSpec((tm, 128), tok_major),
        pl.BlockSpec((N_EXPERTS, 128), lambda i: (0, 0)),
    ]
    out_shapes = [
        jax.ShapeDtypeStruct((N_TRUNK, HALF), jnp.uint32),
        jax.ShapeDtypeStruct((K_PAD, N_TRUNK), jnp.int32),
        jax.ShapeDtypeStruct((K_PAD, N_TRUNK), jnp.int32),
        jax.ShapeDtypeStruct((N_TRUNK, 128), F32),
        jax.ShapeDtypeStruct((N_EXPERTS, 128), jnp.int32),
    ]
    return in_specs, out_specs, out_shapes, [pltpu.VMEM((N_EXPERTS, 128), F32)]


def _finalize_body(cnt_ref, eidx_ref, rank_ref, pos_ref):
    pos_ref[...] = rank_ref[...]

    def per_expert(e, off):
        n_tile = lax.div(cnt_ref[e] + (MOE_TILE - 1), MOE_TILE)
        pos_ref[...] = pos_ref[...] + jnp.where(eidx_ref[...] == e, off, 0)
        return off + n_tile * MOE_TILE

    lax.fori_loop(0, N_EXPERTS, per_expert, jnp.int32(0))


def _finalize(counts, eidx, rank):
    full = pl.BlockSpec((K_PAD, N_TRUNK), lambda: (0, 0))
    return pl.pallas_call(
        _finalize_body,
        in_specs=[pl.BlockSpec(memory_space=pltpu.SMEM), full, full],
        out_specs=full,
        out_shape=jax.ShapeDtypeStruct((K_PAD, N_TRUNK), jnp.int32),
        compiler_params=pltpu.CompilerParams(vmem_limit_bytes=VMEM_LIMIT),
        name="route_finalize",
    )(counts, eidx, rank)


def _ffn(h, wgu, wd):
    gu = jnp.dot(h, wgu, preferred_element_type=F32)
    hid = _silu(gu[:, :EXPERT_FF]) * gu[:, EXPERT_FF:]
    return jnp.dot(hid.astype(BF16), wd, preferred_element_type=F32)


def _cast_ffn_weights(wg_ref, wu_ref, wd_ref, wgub_ref, wdb_ref):
    wgub_ref[:, :EXPERT_FF] = wg_ref[...].astype(BF16)
    wgub_ref[:, EXPERT_FF:] = wu_ref[...].astype(BF16)
    wdb_ref[...] = wd_ref[...].astype(BF16)


_FFN_WEIGHT_SCRATCH = [pltpu.VMEM((D_MODEL, 2 * EXPERT_FF), BF16), pltpu.VMEM((EXPERT_FF, D_MODEL), BF16)]


def _sc_mesh():
    from jax.experimental.pallas import tpu_sc as plsc
    return plsc.VectorSubcoreMesh(core_axis_name="c", subcore_axis_name="s",
                                  num_cores=SC_CORES, num_subcores=SC_SUBCORES)


def _sc_worker_id():
    return lax.axis_index("s") * SC_CORES + lax.axis_index("c")


def _sc_dispatch(hp, pos_flat):
    win = SC_SCATTER_WIN
    per_worker = N_TRUNK // SC_WORKERS

    def body(rows_hbm, idx_hbm, out_hbm, *scratch):
        idx_v, rows_v, sem = scratch[:TOP_K], scratch[TOP_K], scratch[TOP_K + 1]
        base = _sc_worker_id() * per_worker

        @pl.loop(0, per_worker // win)
        def _(j):
            off = base + j * win
            pltpu.sync_copy(rows_hbm.at[pl.ds(off, win)], rows_v)
            for k in range(TOP_K):
                pltpu.sync_copy(idx_hbm.at[pl.ds(k * N_TRUNK + off, win)], idx_v[k])
            copies = [pltpu.async_copy(rows_v, out_hbm.at[idx_v[k]], sem) for k in range(TOP_K)]
            for cp in copies:
                cp.wait()

    return pl.kernel(
        body, mesh=_sc_mesh(),
        out_type=jax.ShapeDtypeStruct((N_SLOT, HALF), jnp.uint32),
        scratch_types=[pltpu.VMEM((win,), jnp.int32)] * TOP_K
        + [pltpu.VMEM((win, HALF), jnp.uint32), pltpu.SemaphoreType.DMA],
        name="sc_dispatch",
    )(hp, pos_flat)


RETURN_PARTS = 2
N_PART = N_TRUNK // RETURN_PARTS


def _sc_return(ys, pos_flat, part):
    win = SC_GATHER_WIN
    wins_per_choice = N_PART // win
    log2_wins = wins_per_choice.bit_length() - 1
    assert wins_per_choice == 1 << log2_wins
    wins_per_worker = TOP_K * wins_per_choice // SC_WORKERS
    assert wins_per_worker * SC_WORKERS == TOP_K * wins_per_choice

    def body(src_hbm, idx_hbm, out_hbm, idx_v, rows_v, sem):
        first = _sc_worker_id() * wins_per_worker

        @pl.loop(0, wins_per_worker)
        def _(j):
            g = first + j
            k = lax.shift_right_logical(g, log2_wins)
            src_off = k * N_TRUNK + part * N_PART + (g - k * wins_per_choice) * win
            pltpu.sync_copy(idx_hbm.at[pl.ds(src_off, win)], idx_v)
            pltpu.async_copy(src_hbm.at[idx_v], rows_v, sem).wait()
            pltpu.sync_copy(rows_v, out_hbm.at[pl.ds(g * win, win)])

    return pl.kernel(
        body, mesh=_sc_mesh(),
        out_type=jax.ShapeDtypeStruct((TOP_K * N_PART, HALF), jnp.uint32),
        scratch_types=[pltpu.VMEM((win,), jnp.int32), pltpu.VMEM((win, HALF), jnp.uint32),
                       pltpu.SemaphoreType.DMA],
        name="sc_return",
    )(ys, pos_flat)


N_PLAN = 2 * N_TILE


def _tile_plan_body(cnt0_ref, cnt1_ref, exp_ref, blk0_ref, blk1_ref, live_ref, n_ref):
    def per_expert(e, carry):
        s, b0, b1 = carry
        n0 = lax.div(cnt0_ref[e] + (MOE_TILE - 1), MOE_TILE)
        n1 = lax.div(cnt1_ref[e] + (MOE_TILE - 1), MOE_TILE)

        def fill(j, c):
            exp_ref[s + j] = e
            blk0_ref[s + j] = jnp.maximum(b0 + jnp.minimum(j, n0 - 1), 0)
            blk1_ref[s + j] = jnp.maximum(b1 + jnp.minimum(j, n1 - 1), 0)
            live_ref[s + j] = jnp.where(j < n0, 1, 0) + jnp.where(j < n1, 2, 0)
            return c
        n = jnp.maximum(n0, n1)
        lax.fori_loop(0, n, fill, 0)
        return s + n, b0 + n0, b1 + n1

    zero = jnp.int32(0)
    n_steps, _, _ = lax.fori_loop(0, N_EXPERTS, per_expert, (zero, zero, zero))
    n_ref[0] = n_steps

    def fill_tail(s, c):
        exp_ref[s] = 0
        blk0_ref[s] = 0
        blk1_ref[s] = 0
        live_ref[s] = 0
        return c
    lax.fori_loop(n_steps, N_PLAN, fill_tail, 0)


def _tile_plan(counts0, counts1):
    smem = pl.BlockSpec(memory_space=pltpu.SMEM)
    steps = jax.ShapeDtypeStruct((N_PLAN,), jnp.int32)
    return pl.pallas_call(
        _tile_plan_body,
        in_specs=[smem, smem],
        out_specs=[smem] * 5,
        out_shape=[steps, steps, steps, steps, jax.ShapeDtypeStruct((1,), jnp.int32)],
        name="tile_plan",
    )(counts0, counts1)


def _expert_body(exp_ref, blk0_ref, blk1_ref, live_ref, xs0_ref, xs1_ref, wg_ref, wu_ref, wd_ref,
                 ys0_ref, ys1_ref, wgub_ref, wdb_ref):
    i = pl.program_id(0)

    @pl.when(jnp.logical_or(i == 0, exp_ref[i] != exp_ref[jnp.maximum(i - 1, 0)]))
    def _():
        _cast_ffn_weights(wg_ref, wu_ref, wd_ref, wgub_ref, wdb_ref)

    for bit, xs_ref, ys_ref in ((1, xs0_ref, ys0_ref), (2, xs1_ref, ys1_ref)):
        @pl.when((live_ref[i] & bit) != 0)
        def _():
            lo, hi = _unpack_bf16_pair(xs_ref[...])
            h = jnp.concatenate([lo.astype(BF16), hi.astype(BF16)], axis=1)
            y = _ffn(h, wgub_ref[...], wdb_ref[...])
            ys_ref[...] = _pack_bf16_pair(y[:, :HALF], y[:, HALF:])


def _expert_ffn(plan, xs0, xs1, layer, w_gate, w_up, w_down):
    exp, blk0, blk1, live, n_steps = plan
    tile0 = lambda i, exp, blk0, blk1, live: (blk0[i], 0)
    tile1 = lambda i, exp, blk0, blk1, live: (blk1[i], 0)
    ew = lambda shape: pl.BlockSpec((None, None) + shape,
                                    lambda i, exp, blk0, blk1, live: (layer, exp[i], 0, 0))
    slots = jax.ShapeDtypeStruct((N_SLOT, HALF), jnp.uint32)
    return pl.pallas_call(
        _expert_body,
        grid_spec=pltpu.PrefetchScalarGridSpec(
            num_scalar_prefetch=4,
            grid=(n_steps[0],),
            in_specs=[
                pl.BlockSpec((MOE_TILE, HALF), tile0), pl.BlockSpec((MOE_TILE, HALF), tile1),
                ew((D_MODEL, EXPERT_FF)), ew((D_MODEL, EXPERT_FF)), ew((EXPERT_FF, D_MODEL)),
            ],
            out_specs=[pl.BlockSpec((MOE_TILE, HALF), tile0), pl.BlockSpec((MOE_TILE, HALF), tile1)],
            scratch_shapes=_FFN_WEIGHT_SCRATCH,
        ),
        out_shape=[slots, slots],
        compiler_params=pltpu.CompilerParams(
            dimension_semantics=("arbitrary",), vmem_limit_bytes=VMEM_LIMIT),
        name="expert_ffn",
    )(exp, blk0, blk1, live, xs0, xs1, w_gate, w_up, w_down)


def _combine_body(x_ref, sh_ref, sc_ref, g2_ref, yk_ref, wt_ref, sg_ref, su_ref, sd_ref, lng_ref, lnb_ref,
                  *rest):
    o_ref, sgub_ref, sdb_ref = rest[-3:]

    @pl.when(pl.program_id(0) == 0)
    def _():
        _cast_ffn_weights(sg_ref, su_ref, sd_ref, sgub_ref, sdb_ref)

    x = x_ref[...].astype(F32)
    h = (x * (1.0 + sc_ref[...]) + sh_ref[...]).astype(BF16)
    y = _ffn(h, sgub_ref[...], sdb_ref[...])
    wt = wt_ref[...]
    lo_acc = jnp.zeros((x.shape[0], HALF), F32)
    hi_acc = jnp.zeros((x.shape[0], HALF), F32)
    for k in range(TOP_K):
        lo, hi = _unpack_bf16_pair(yk_ref[k])
        w = wt[:, k:k + 1]
        lo_acc = lo_acc + w * lo
        hi_acc = hi_acc + w * hi
    y = y + jnp.concatenate([lo_acc, hi_acc], axis=1)
    z = DEEPNORM_ALPHA * x + g2_ref[...] * y
    o_ref[...] = _layer_norm(z, lng_ref[...], lnb_ref[...]).astype(o_ref.dtype)


def _combine(trunk, part, x, mod, yk, wt, layer, s_gate, s_up, s_down, ln_g, ln_b, out_dtype, earlier):
    tm = TOK_BLOCK
    part_block = part * (N_PART // tm)
    first_block = trunk * (N_TRUNK // tm) + part_block
    row = lambda i: (part_block + i, 0)
    vec = pl.BlockSpec((1, D_MODEL), lambda i: (0, 0))
    sw = lambda shape: pl.BlockSpec((None,) + shape, lambda i: (layer, 0, 0))
    mod_spec = lambda col: _mod_spec(tm, col, first_block)
    in_specs = [
        pl.BlockSpec((tm, D_MODEL), row),
        mod_spec(3), mod_spec(4), mod_spec(5),
        pl.BlockSpec((TOP_K, tm, HALF), lambda i: (0, i, 0)),
        pl.BlockSpec((tm, 128), row),
        sw((D_MODEL, EXPERT_FF)), sw((D_MODEL, EXPERT_FF)), sw((EXPERT_FF, D_MODEL)),
        vec, vec,
    ]
    args = (x, mod, mod, mod, yk, wt, s_gate, s_up, s_down, ln_g, ln_b)
    aliases = {}
    if earlier is not None:
        in_specs.append(pl.BlockSpec(memory_space=pl.ANY))
        aliases = {len(args): 0}
        args += (earlier,)
    return pl.pallas_call(
        _combine_body,
        grid=(N_PART // tm,),
        in_specs=in_specs,
        out_specs=pl.BlockSpec((tm, D_MODEL), row),
        out_shape=jax.ShapeDtypeStruct((N_TRUNK, D_MODEL), out_dtype),
        input_output_aliases=aliases,
        scratch_shapes=_FFN_WEIGHT_SCRATCH,
        compiler_params=pltpu.CompilerParams(
            dimension_semantics=("arbitrary",), vmem_limit_bytes=VMEM_LIMIT),
        name="moe_combine_ln",
    )(*args)


def _moe_dispatch(routing):
    hp, eidx, rank, wt, counts = routing
    counts = counts[:, 0]
    pos_flat = _finalize(counts, eidx, rank).reshape(K_PAD * N_TRUNK)
    return _sc_dispatch(hp, pos_flat), pos_flat, counts, wt


def _moe_experts(plans, layer, w_gate, w_up, w_down):
    (xs0, pos0, counts0, _), (xs1, pos1, counts1, _) = plans
    ys = _expert_ffn(_tile_plan(counts0, counts1), xs0, xs1, layer, w_gate, w_up, w_down)
    return [[_sc_return(y, pos, part).reshape(TOP_K, N_PART, HALF) for part in range(RETURN_PARTS)]
            for y, pos in zip(ys, (pos0, pos1))]


def kernel(x_prompt, x_sample, state_ret, c, c_ctx, ada_w, ada_b, ln_g, ln_b, ret_w_in, ret_w_out, ret_decay, conv_w_in, conv_w, conv_w_out, moe_router, moe_bias, moe_w_gate, moe_w_up, moe_w_down, shared_w_gate, shared_w_up, shared_w_down):
    x = (x_prompt.reshape(N_CTX, D_MODEL), x_sample.reshape(N_LAT, D_MODEL))
    cond = jnp.concatenate(
        [c_ctx[None, :], c, jnp.zeros((N_COND - 1 - DEC_BATCH, D_MODEL), F32)], axis=0)
    mods = _ada_table(cond, ada_w, ada_b).reshape(DEPTH, N_COND, 1, 6 * D_MODEL)
    cos, sin = _rope_tables()
    router_wt = jnp.swapaxes(moe_router, 1, 2)

    states = None
    for i in range(DEPTH):
        j = i // 2
        mod = mods[i]
        lng = ln_g[i].reshape(2, 1, D_MODEL)
        lnb = ln_b[i].reshape(2, 1, D_MODEL)
        bias_col = moe_bias[i].reshape(N_EXPERTS, 1)
        if i % 2 == 0:
            w_in_bf = _to_bf16(ret_w_in, j)
        mixed, plans = [], []
        for trunk in range(2):
            if i % 2 == 0:
                if trunk == 0:
                    a, states = _retention_ctx(x[0], mod, w_in_bf, ret_decay[j], j, states)
                else:
                    a = _retention_lat(x[1], mod, w_in_bf, ret_decay[j], state_ret, j, cos, sin)
                xm, *routing = _ret_out(trunk, a, x[trunk], mod, ret_w_out, j, lng[0], lnb[0],
                                        router_wt[i], bias_col)
            else:
                xm, *routing = _conv_layer(trunk, x[trunk], mod, conv_w_in, conv_w_out, j, conv_w[j],
                                           lng[0], lnb[0], router_wt[i], bias_col)
            mixed.append(xm)
            plans.append(_moe_dispatch(routing))
        yks = _moe_experts(plans, i, moe_w_gate, moe_w_up, moe_w_down)
        out_dtype = F32 if i == DEPTH - 1 else RES_DTYPE
        x = []
        for trunk in range(2):
            out = None
            for part in range(RETURN_PARTS):
                out = _combine(trunk, part, mixed[trunk], mod, yks[trunk][part], plans[trunk][3], i,
                               shared_w_gate, shared_w_up, shared_w_down, lng[1], lnb[1], out_dtype, out)
            x.append(out)

    y_prompt = x[0].reshape(BATCH, SEQ, D_MODEL)
    y_sample = x[1].reshape(DEC_BATCH, DEC_SEQ, D_MODEL)
    return y_prompt, y_sample, states
```

```python
import functools

import jax
import jax.numpy as jnp
from jax import lax
from jax.experimental import pallas as pl
from jax.experimental.pallas import tpu as pltpu

F32 = jnp.float32
BF16 = jnp.bfloat16

D_MODEL = 1024
BATCH = 32
SEQ = 256
DEPTH = 4
DEC_BATCH = 8
DEC_SEQ = 1024
GRID_W = 64
RET_HEADS = 4
RET_DK = D_MODEL // RET_HEADS
RET_DV = 2 * D_MODEL // RET_HEADS
ROPE_BASE = 10000.0
N_EXPERTS = 64
TOP_K = 6
N_GROUPS = 8
TOPK_GROUPS = 4
EXPERT_FF = 256
ROUTED_SCALE = 2.5
LN_EPS = 1e-5
DEEPNORM_ALPHA = (2.0 * DEPTH) ** 0.25

N_CTX = BATCH * SEQ
N_LAT = DEC_BATCH * DEC_SEQ
N_COND = 16
RET_CHUNK = 256
assert N_CTX == N_LAT
N_TRUNK = N_CTX
TOK_BLOCK = 512
MIX_BLOCK = 512
VMEM_LIMIT = 56 * 1024 * 1024
RES_DTYPE = BF16

HALF = D_MODEL // 2
MOE_TILE = 896
N_PAIR = N_TRUNK * TOP_K
N_TILE = -(-(N_PAIR + N_EXPERTS * (MOE_TILE - 1)) // MOE_TILE)
N_SLOT = N_TILE * MOE_TILE
K_PAD = 8

SC_CORES = 2
SC_SUBCORES = 16
SC_WORKERS = SC_CORES * SC_SUBCORES
SC_SCATTER_WIN = 64
SC_GATHER_WIN = 128


def _cond_row(tok_block_idx, tok_block):
    t0 = tok_block_idx * tok_block
    return jnp.where(t0 < N_CTX, 0, 1 + (t0 - N_CTX) // DEC_SEQ)


def _silu(x):
    return x * jax.nn.sigmoid(x)


def _layer_norm(z, g, b):
    mu = jnp.mean(z, axis=-1, keepdims=True)
    zc = z - mu
    var = jnp.mean(zc * zc, axis=-1, keepdims=True)
    return zc * lax.rsqrt(var + LN_EPS) * g + b


def _ada_body(cond_ref, w_ref, b_ref, o_ref):
    s = _silu(cond_ref[...]).astype(BF16)
    o_ref[...] = jnp.dot(s, w_ref[...].astype(BF16), preferred_element_type=F32) + b_ref[...]


def _ada_table(cond, ada_w, ada_b):
    tn = 2048
    return pl.pallas_call(
        _ada_body,
        grid=(DEPTH, 6 * D_MODEL // tn),
        in_specs=[
            pl.BlockSpec((N_COND, D_MODEL), lambda l, j: (0, 0)),
            pl.BlockSpec((None, D_MODEL, tn), lambda l, j: (l, 0, j)),
            pl.BlockSpec((None, 1, tn), lambda l, j: (l, 0, j)),
        ],
        out_specs=pl.BlockSpec((None, N_COND, tn), lambda l, j: (l, 0, j)),
        out_shape=jax.ShapeDtypeStruct((DEPTH, N_COND, 6 * D_MODEL), F32),
        compiler_params=pltpu.CompilerParams(
            dimension_semantics=("arbitrary", "arbitrary"), vmem_limit_bytes=VMEM_LIMIT),
        name="ada_table",
    )(cond, ada_w, ada_b.reshape(DEPTH, 1, 6 * D_MODEL))


def _mod_spec(tok_block, col, first_block=0):
    def index_map(*idx):
        return (_cond_row(first_block + idx[0], tok_block), 0, col)
    return pl.BlockSpec((None, 1, D_MODEL), index_map)


def _to_bf16_body(w_ref, o_ref):
    o_ref[...] = w_ref[...].astype(BF16)


def _to_bf16(w_all, w_idx):
    k_dim, n_out = w_all.shape[1:]
    tn = 1024
    return pl.pallas_call(
        _to_bf16_body,
        grid=(n_out // tn,),
        in_specs=[pl.BlockSpec((None, k_dim, tn), lambda j: (w_idx, 0, j))],
        out_specs=pl.BlockSpec((k_dim, tn), lambda j: (0, j)),
        out_shape=jax.ShapeDtypeStruct((k_dim, n_out), BF16),
        compiler_params=pltpu.CompilerParams(
            dimension_semantics=("arbitrary",), vmem_limit_bytes=VMEM_LIMIT),
        name="weight_to_bf16",
    )(w_all)


RET_GROUP = 1024


def _head_proj_specs():
    return [
        pl.BlockSpec((D_MODEL, RET_DK), lambda t, h: (0, h)),
        pl.BlockSpec((D_MODEL, RET_DK), lambda t, h: (0, RET_HEADS + h)),
        pl.BlockSpec((D_MODEL, RET_DV), lambda t, h: (0, RET_HEADS + h)),
        pl.BlockSpec((D_MODEL, RET_DV), lambda t, h: (0, 2 * RET_HEADS + h)),
    ]


def _head_proj(x_ref, sh_ref, sc_ref, wq_ref, wk_ref, wv_ref, wg_ref):
    h = (x_ref[...].astype(F32) * (1.0 + sc_ref[...]) + sh_ref[...]).astype(BF16)
    dot = lambda w_ref: jnp.dot(h, w_ref[...], preferred_element_type=F32)
    return dot(wq_ref), dot(wk_ref), dot(wv_ref).astype(BF16), dot(wg_ref)

def _log_sigmoid(v):
    return jnp.minimum(v, 0.0) - jnp.log1p(jnp.exp(-jnp.abs(v)))


def _decay_tables(dec_ref, head):
    c = RET_CHUNK
    lgf = _log_sigmoid(jnp.full((c, c), dec_ref[0, head], F32))
    lgb = _log_sigmoid(jnp.full((c, c), dec_ref[1, head], F32))
    row = lax.broadcasted_iota(jnp.int32, (c, c), 0).astype(F32)
    col = lax.broadcasted_iota(jnp.int32, (c, c), 1).astype(F32)
    diff = row - col
    kscale = RET_DK ** -0.5
    intra = jnp.where(diff > 0, jnp.exp(lgf * diff),
                      jnp.where(diff < 0, jnp.exp(-lgb * diff), 2.0)) * kscale
    qdec_f = jnp.exp(lgf * (row + 1.0))
    qdec_b = jnp.exp(lgb * (c - row))
    kdec_f = jnp.exp(lgf * (c - 1.0 - row)) * kscale
    kdec_b = jnp.exp(lgb * row) * kscale
    cdec_f = jnp.exp(lgf * c)
    cdec_b = jnp.exp(lgb * c)
    return intra, qdec_f, qdec_b, kdec_f, kdec_b, cdec_f, cdec_b


def _head_norm_gate(o, g):
    mu = jnp.mean(o, axis=-1, keepdims=True)
    oc = o - mu
    var = jnp.mean(oc * oc, axis=-1, keepdims=True)
    on = oc * lax.rsqrt(var + LN_EPS)
    return (_silu(g.astype(F32)) * on).astype(BF16)


_NT = (((1,), (1,)), ((), ()))
_TN = (((0,), (0,)), ((), ()))


def _ret_ctx_body(dec_ref, x_ref, sh_ref, sc_ref, wq_ref, wk_ref, wv_ref, wg_ref, *rest):
    a_ref, st_ref, tab_ref = rest[-3:]
    head = pl.program_id(1)

    @pl.when(pl.program_id(0) == 0)
    def _():
        intra, _, _, kdec_f, kdec_b, _, _ = _decay_tables(dec_ref, head)
        tab_ref[head, 0] = intra
        tab_ref[head, 1] = kdec_f
        tab_ref[head, 2] = kdec_b

    q, k, v, g = _head_proj(x_ref, sh_ref, sc_ref, wq_ref, wk_ref, wv_ref, wg_ref)
    for s in range(RET_GROUP // SEQ):
        rows = slice(s * SEQ, (s + 1) * SEQ)
        scores = lax.dot_general(q[rows].astype(BF16), k[rows].astype(BF16), _NT, preferred_element_type=F32)
        p = (scores * tab_ref[head, 0]).astype(BF16)
        o = jnp.dot(p, v[rows], preferred_element_type=F32)
        a_ref[rows, :] = _head_norm_gate(o, g[rows])
        st_ref[s, 0] = lax.dot_general((k[rows] * tab_ref[head, 1]).astype(BF16), v[rows], _TN,
                                       preferred_element_type=F32)
        st_ref[s, 1] = lax.dot_general((k[rows] * tab_ref[head, 2]).astype(BF16), v[rows], _TN,
                                       preferred_element_type=F32)


def _retention_ctx(x, mod, w_in_bf, decay, ret_idx, states):
    assert SEQ == RET_CHUNK
    seqs = RET_GROUP // SEQ
    n_ret = (DEPTH + 1) // 2
    st = jax.ShapeDtypeStruct((BATCH, n_ret, 2, RET_HEADS, RET_DK, RET_DV), F32)
    st_spec = pl.BlockSpec((seqs, None, 2, None, RET_DK, RET_DV), lambda t, h: (t, ret_idx, 0, h, 0, 0))
    in_specs = [
        pl.BlockSpec(memory_space=pltpu.SMEM),
        pl.BlockSpec((RET_GROUP, D_MODEL), lambda t, h: (t, 0)),
        _mod_spec(RET_GROUP, 0), _mod_spec(RET_GROUP, 1),
    ] + _head_proj_specs()
    args = (decay, x, mod, mod, w_in_bf, w_in_bf, w_in_bf, w_in_bf)
    aliases = {}
    if states is not None:
        in_specs.append(pl.BlockSpec(memory_space=pl.ANY))
        aliases = {len(args): 1}
        args += (states,)
    return pl.pallas_call(
        _ret_ctx_body,
        grid=(N_CTX // RET_GROUP, RET_HEADS),
        in_specs=in_specs,
        out_specs=[pl.BlockSpec((RET_GROUP, RET_DV), lambda t, h: (t, h)), st_spec],
        out_shape=[jax.ShapeDtypeStruct((N_CTX, RET_HEADS * RET_DV), BF16), st],
        input_output_aliases=aliases,
        scratch_shapes=[pltpu.VMEM((RET_HEADS, 3, RET_CHUNK, RET_CHUNK), F32)],
        compiler_params=pltpu.CompilerParams(
            dimension_semantics=("arbitrary", "arbitrary"), vmem_limit_bytes=VMEM_LIMIT),
        name="retention_ctx",
    )(*args)


def _rope(x, cos, sin):
    halves = [pltpu.roll(x[:, s:s + 128], 64, axis=1) for s in (0, 128)]
    return x * cos + jnp.concatenate(halves, axis=1) * sin


def _ret_lat_body(dec_ref, x_ref, sh_ref, sc_ref, wq_ref, wk_ref, wv_ref, wg_ref,
                  s0f_ref, s0b_ref, cos_ref, sin_ref,
                  a_ref, qr_ref, kr_ref, v_ref, g_ref, o_ref, st_ref):
    head = pl.program_id(1)
    c = RET_CHUNK
    nc = DEC_SEQ // c
    intra, qdec_f, qdec_b, kdec_f, kdec_b, cdec_f, cdec_b = _decay_tables(dec_ref, head)
    cdec_f = jnp.concatenate([cdec_f, cdec_f], axis=1)
    cdec_b = jnp.concatenate([cdec_b, cdec_b], axis=1)

    q, k, v, g = _head_proj(x_ref, sh_ref, sc_ref, wq_ref, wk_ref, wv_ref, wg_ref)
    qr_ref[...] = _rope(q, cos_ref[...], sin_ref[...])
    kr_ref[...] = _rope(k, cos_ref[...], sin_ref[...])
    v_ref[...] = v
    g_ref[...] = g

    st_ref[...] = s0f_ref[...]
    for ci in range(nc):
        rows = pl.ds(ci * c, c)
        q = qr_ref[rows, :]
        k = kr_ref[rows, :]
        v = v_ref[rows, :]
        scores = lax.dot_general(q.astype(BF16), k.astype(BF16), _NT, preferred_element_type=F32)
        p = (scores * intra).astype(BF16)
        o = jnp.dot(p, v, preferred_element_type=F32)
        qd = (q * qdec_f).astype(BF16)
        o = o + jnp.dot(qd, st_ref[...].astype(BF16), preferred_element_type=F32)
        o_ref[rows, :] = o
        if ci + 1 < nc:
            kd = (k * kdec_f).astype(BF16)
            st_ref[...] = cdec_f * st_ref[...] + lax.dot_general(kd, v, _TN, preferred_element_type=F32)

    st_ref[...] = s0b_ref[...]
    for ci in reversed(range(nc)):
        rows = pl.ds(ci * c, c)
        qd = (qr_ref[rows, :] * qdec_b).astype(BF16)
        o = o_ref[rows, :] + jnp.dot(qd, st_ref[...].astype(BF16), preferred_element_type=F32)
        a_ref[rows, :] = _head_norm_gate(o, g_ref[rows, :])
        if ci > 0:
            kd = (kr_ref[rows, :] * kdec_b).astype(BF16)
            st_ref[...] = cdec_b * st_ref[...] + lax.dot_general(kd, v_ref[rows, :], _TN,
                                                                 preferred_element_type=F32)


def _rope_tables():
    half = RET_DK // 4
    freqs = ROPE_BASE ** (-jnp.arange(half, dtype=F32) / half)
    t = jnp.arange(DEC_SEQ)
    row = (t // GRID_W).astype(F32)
    col = (t % GRID_W).astype(F32)
    ang_r = row[:, None] * freqs[None, :]
    ang_c = col[:, None] * freqs[None, :]
    cos = jnp.concatenate([jnp.cos(ang_r)] * 2 + [jnp.cos(ang_c)] * 2, axis=1)
    sin = jnp.concatenate([-jnp.sin(ang_r), jnp.sin(ang_r), -jnp.sin(ang_c), jnp.sin(ang_c)], axis=1)
    return cos, sin


def _retention_lat(x, mod, w_in_bf, decay, state_ret, ret_idx, cos, sin):
    n = DEC_SEQ
    assert n == RET_GROUP
    s0_spec = lambda d: pl.BlockSpec((None, None, None, None, RET_DK, RET_DV),
                                     lambda b, h: (b, ret_idx, d, h, 0, 0))
    tab_spec = pl.BlockSpec((n, RET_DK), lambda b, h: (0, 0))
    first_block = N_CTX // n
    return pl.pallas_call(
        _ret_lat_body,
        grid=(DEC_BATCH, RET_HEADS),
        in_specs=[
            pl.BlockSpec(memory_space=pltpu.SMEM),
            pl.BlockSpec((n, D_MODEL), lambda b, h: (b, 0)),
            _mod_spec(n, 0, first_block), _mod_spec(n, 1, first_block),
        ] + _head_proj_specs() + [s0_spec(0), s0_spec(1), tab_spec, tab_spec],
        out_specs=pl.BlockSpec((n, RET_DV), lambda b, h: (b, h)),
        out_shape=jax.ShapeDtypeStruct((N_LAT, RET_HEADS * RET_DV), BF16),
        scratch_shapes=[
            pltpu.VMEM((n, RET_DK), F32), pltpu.VMEM((n, RET_DK), F32),
            pltpu.VMEM((n, RET_DV), BF16), pltpu.VMEM((n, RET_DV), F32),
            pltpu.VMEM((n, RET_DV), F32), pltpu.VMEM((RET_DK, RET_DV), F32),
        ],
        compiler_params=pltpu.CompilerParams(
            dimension_semantics=("arbitrary", "arbitrary"), vmem_limit_bytes=VMEM_LIMIT),
        name="retention_lat",
    )(decay, x, mod, mod, w_in_bf, w_in_bf, w_in_bf, w_in_bf, state_ret, state_ret, cos, sin)


N_ROUTE_IN = 4
N_ROUTE_OUT = 5


def _split_route_refs(rest, n_scratch):
    route_in = rest[:N_ROUTE_IN]
    o_ref = rest[N_ROUTE_IN]
    route_out = rest[N_ROUTE_IN + 1:N_ROUTE_IN + 1 + N_ROUTE_OUT]
    return route_in, o_ref, route_out, rest[len(rest) - n_scratch:]


def _ret_out_body(a_ref, x_ref, g1_ref, w_ref, lng_ref, lnb_ref, *rest):
    route_in, o_ref, route_out, (wbf_ref, carry_ref) = _split_route_refs(rest, 2)

    @pl.when(pl.program_id(0) == 0)
    def _():
        wbf_ref[...] = w_ref[...].astype(BF16)
        carry_ref[...] = jnp.zeros_like(carry_ref)

    y = jnp.dot(a_ref[...], wbf_ref[...], preferred_element_type=F32)
    z = DEEPNORM_ALPHA * x_ref[...].astype(F32) + g1_ref[...] * y
    xn = _layer_norm(z, lng_ref[...], lnb_ref[...])
    o_ref[...] = xn.astype(o_ref.dtype)
    _route_block(xn, *route_in, *route_out, carry_ref)


def _ret_out(trunk, a, x, mod, w_all, w_idx, ln_g, ln_b, router_wt, bias_col):
    tm = MIX_BLOCK
    k_dim = w_all.shape[1]
    first_block = trunk * (N_TRUNK // tm)
    row = lambda i: (i, 0)
    vec = pl.BlockSpec((1, D_MODEL), lambda i: (0, 0))
    r_in, r_out, r_shapes, r_scratch = _route_io(tm, first_block)
    return pl.pallas_call(
        _ret_out_body,
        grid=(N_TRUNK // tm,),
        in_specs=[
            pl.BlockSpec((tm, k_dim), row),
            pl.BlockSpec((tm, D_MODEL), row),
            _mod_spec(tm, 2, first_block),
            pl.BlockSpec((None, k_dim, D_MODEL), lambda i: (w_idx, 0, 0)),
            vec, vec,
        ] + r_in,
        out_specs=[pl.BlockSpec((tm, D_MODEL), row)] + r_out,
        out_shape=[jax.ShapeDtypeStruct((N_TRUNK, D_MODEL), RES_DTYPE)] + r_shapes,
        scratch_shapes=[pltpu.VMEM((k_dim, D_MODEL), BF16)] + r_scratch,
        compiler_params=pltpu.CompilerParams(
            dimension_semantics=("arbitrary",), vmem_limit_bytes=VMEM_LIMIT),
        name="ret_out_ln_route",
    )(a, x, mod, w_all, ln_g, ln_b, mod, mod, router_wt, bias_col)


def _conv_layer_body(seg, x_ref, sh_ref, sc_ref, g1_ref, win_ref, wout_ref, cw_ref, lng_ref, lnb_ref, *rest):
    route_in, o_ref, route_out, (winb_ref, woutb_ref, carry_ref) = _split_route_refs(rest, 3)
    tm = o_ref.shape[0]

    @pl.when(pl.program_id(0) == 0)
    def _():
        winb_ref[...] = win_ref[...].astype(BF16)
        woutb_ref[...] = wout_ref[...].astype(BF16)
        carry_ref[...] = jnp.zeros_like(carry_ref)

    x = x_ref[...].astype(F32)
    h = (x * (1.0 + sc_ref[...]) + sh_ref[...]).astype(BF16)
    proj = jnp.dot(h, winb_ref[...], preferred_element_type=F32)
    bg, cg, xt = (proj[:, c * D_MODEL:(c + 1) * D_MODEL] for c in range(3))
    u = cg * xt
    pos = lax.broadcasted_iota(jnp.int32, (tm, D_MODEL), 0) & (seg - 1)
    u_prev = jnp.where(pos == 0, 0.0, pltpu.roll(u, 1, axis=0))
    u_next = jnp.where(pos == seg - 1, 0.0, pltpu.roll(u, tm - 1, axis=0))
    cu = u_prev * cw_ref[0:1, :] + u * cw_ref[1:2, :] + u_next * cw_ref[2:3, :]
    y = jnp.dot((bg * cu).astype(BF16), woutb_ref[...], preferred_element_type=F32)
    z = DEEPNORM_ALPHA * x + g1_ref[...] * y
    xn = _layer_norm(z, lng_ref[...], lnb_ref[...])
    o_ref[...] = xn.astype(o_ref.dtype)
    _route_block(xn, *route_in, *route_out, carry_ref)


def _conv_layer(trunk, x, mod, w_in_all, w_out_all, w_idx, conv_w, ln_g, ln_b, router_wt, bias_col):
    tm = MIX_BLOCK
    first_block = trunk * (N_TRUNK // tm)
    seg = SEQ if trunk == 0 else GRID_W
    assert tm % seg == 0 and seg & (seg - 1) == 0
    row = lambda i: (i, 0)
    vec = pl.BlockSpec((1, D_MODEL), lambda i: (0, 0))
    once = pl.Buffered(1)
    r_in, r_out, r_shapes, r_scratch = _route_io(tm, first_block)
    mod_spec = lambda col: _mod_spec(tm, col, first_block)
    return pl.pallas_call(
        functools.partial(_conv_layer_body, seg),
        grid=(N_TRUNK // tm,),
        in_specs=[
            pl.BlockSpec((tm, D_MODEL), row),
            mod_spec(0), mod_spec(1), mod_spec(2),
            pl.BlockSpec((None, D_MODEL, 3 * D_MODEL), lambda i: (w_idx, 0, 0), pipeline_mode=once),
            pl.BlockSpec((None, D_MODEL, D_MODEL), lambda i: (w_idx, 0, 0), pipeline_mode=once),
            pl.BlockSpec((3, D_MODEL), lambda i: (0, 0)),
            vec, vec,
        ] + r_in,
        out_specs=[pl.BlockSpec((tm, D_MODEL), row)] + r_out,
        out_shape=[jax.ShapeDtypeStruct((N_TRUNK, D_MODEL), RES_DTYPE)] + r_shapes,
        scratch_shapes=[pltpu.VMEM((D_MODEL, 3 * D_MODEL), BF16), pltpu.VMEM((D_MODEL, D_MODEL), BF16)]
        + r_scratch,
        compiler_params=pltpu.CompilerParams(
            dimension_semantics=("arbitrary",), vmem_limit_bytes=VMEM_LIMIT),
        name="conv_layer_route",
    )(x, mod, mod, mod, w_in_all, w_out_all, conv_w, ln_g, ln_b, mod, mod, router_wt, bias_col)


def _split_bf16(v):
    hi = v.astype(BF16)
    lo = (v - hi.astype(F32)).astype(BF16)
    return hi, lo


def _pack_bf16_pair(lo_f32, hi_f32):
    lo = lax.bitcast_convert_type(lo_f32.astype(BF16).astype(F32), jnp.uint32) >> 16
    hi = lax.bitcast_convert_type(hi_f32.astype(BF16).astype(F32), jnp.uint32) & jnp.uint32(0xFFFF0000)
    return hi | lo


def _unpack_bf16_pair(u):
    lo = lax.bitcast_convert_type(u << 16, F32)
    hi = lax.bitcast_convert_type(u & jnp.uint32(0xFFFF0000), F32)
    return lo, hi


def _rows_to_tile(rows, n_sub, dtype):
    tm = rows[0].shape[1]
    sub = lax.broadcasted_iota(jnp.int32, (n_sub, tm), 0)
    out = jnp.zeros((n_sub, tm), dtype)
    for k, r in enumerate(rows):
        out = jnp.where(sub == k, jnp.broadcast_to(r.astype(dtype), (n_sub, tm)), out)
    return out


def _route_block(xn, sh_ref, sc_ref, rwt_ref, bias_ref,
                 hp_ref, eidx_ref, rank_ref, wt_ref, cnt_ref, carry_ref):
    tm = xn.shape[0]
    e = N_EXPERTS
    per = e // N_GROUPS
    neg = -jnp.inf
    h = xn * (1.0 + sc_ref[...]) + sh_ref[...]
    hp_ref[...] = _pack_bf16_pair(h[:, :HALF], h[:, HALF:])
    h_hi, h_lo = _split_bf16(h)
    w_hi, w_lo = _split_bf16(rwt_ref[...])
    dot = lambda a, b: lax.dot_general(a, b, _NT, preferred_element_type=F32)
    logits = dot(w_hi, h_hi) + (dot(w_hi, h_lo) + dot(w_lo, h_hi))
    s = jax.nn.sigmoid(logits)
    sel = s + bias_ref[...]

    g3 = sel.reshape(N_GROUPS, per, tm)
    sub = lax.broadcasted_iota(jnp.int32, (N_GROUPS, per, tm), 1)
    m1 = jnp.max(g3, axis=1, keepdims=True)
    i1 = jnp.min(jnp.where(g3 == m1, sub, per), axis=1, keepdims=True)
    m2 = jnp.max(jnp.where(sub == i1, neg, g3), axis=1, keepdims=True)
    gs = (m1 + m2).reshape(N_GROUPS, tm)

    gi = lax.broadcasted_iota(jnp.int32, (N_GROUPS, tm), 0)
    gmask = jnp.zeros((N_GROUPS, tm), jnp.bool_)
    cur = gs
    for _ in range(TOPK_GROUPS):
        m = jnp.max(cur, axis=0, keepdims=True)
        idx = jnp.min(jnp.where(cur == m, gi, N_GROUPS), axis=0, keepdims=True)
        pick = gi == idx
        gmask = jnp.logical_or(gmask, pick)
        cur = jnp.where(pick, neg, cur)
    emask = jnp.broadcast_to(gmask.reshape(N_GROUPS, 1, tm), (N_GROUPS, per, tm)).reshape(e, tm)

    ei = lax.broadcasted_iota(jnp.int32, (e, tm), 0)
    picks, ids = [], []
    cur = jnp.where(emask, sel, neg)
    for _ in range(TOP_K):
        m = jnp.max(cur, axis=0, keepdims=True)
        idx = jnp.min(jnp.where(cur == m, ei, e), axis=0, keepdims=True)
        pick = ei == idx
        picks.append(pick)
        ids.append(idx)
        cur = jnp.where(pick, neg, cur)

    chosen = functools.reduce(jnp.logical_or, picks)
    cf = jnp.where(chosen, 1.0, 0.0)
    before = (lax.broadcasted_iota(jnp.int32, (tm, tm), 0)
              < lax.broadcasted_iota(jnp.int32, (tm, tm), 1)).astype(BF16)
    rank = carry_ref[:, 0:1] + jnp.dot(cf.astype(BF16), before, preferred_element_type=F32)
    carry_ref[...] = carry_ref[...] + jnp.sum(cf, axis=1, keepdims=True)
    cnt_ref[...] = carry_ref[...].astype(jnp.int32)

    w_rows = [jnp.sum(jnp.where(p, s, 0.0), axis=0, keepdims=True) for p in picks]
    r_rows = [jnp.sum(jnp.where(p, rank, 0.0), axis=0, keepdims=True) for p in picks]
    den = functools.reduce(lambda a, b: a + b, w_rows)
    w_rows = [w / den * ROUTED_SCALE for w in w_rows]
    eidx_ref[...] = _rows_to_tile(ids, K_PAD, jnp.int32)
    rank_ref[...] = _rows_to_tile(r_rows, K_PAD, F32).astype(jnp.int32)
    wt_ref[...] = _rows_to_tile(w_rows, 128, F32).T


def _route_io(tm, first_block):
    tok_major = lambda i: (i, 0)
    choice_major = lambda i: (0, i)
    in_specs = [
        _mod_spec(tm, 3, first_block),
        _mod_spec(tm, 4, first_block),
        pl.BlockSpec((N_EXPERTS, D_MODEL), lambda i: (0, 0)),
        pl.BlockSpec((N_EXPERTS, 1), lambda i: (0, 0)),
    ]
    out_specs = [
        pl.BlockSpec((tm, HALF), tok_major),
        pl.BlockSpec((K_PAD, tm), choice_major),
        pl.BlockSpec((K_PAD, tm), choice_major),
        pl.BlockSpec((tm, 128), tok_major),
        pl.BlockSpec((N_EXPERTS, 128), lambda i: (0, 0)),
    ]
    out_shapes = [
        jax.ShapeDtypeStruct((N_TRUNK, HALF), jnp.uint32),
        jax.ShapeDtypeStruct((K_PAD, N_TRUNK), jnp.int32),
        jax.ShapeDtypeStruct((K_PAD, N_TRUNK), jnp.int32),
        jax.ShapeDtypeStruct((N_TRUNK, 128), F32),
        jax.ShapeDtypeStruct((N_EXPERTS, 128), jnp.int32),
    ]
    return in_specs, out_specs, out_shapes, [pltpu.VMEM((N_EXPERTS, 128), F32)]


def _finalize_body(cnt_ref, eidx_ref, rank_ref, pos_ref):
    pos_ref[...] = rank_ref[...]

    def per_expert(e, off):
        n_tile = lax.div(cnt_ref[e] + (MOE_TILE - 1), MOE_TILE)
        pos_ref[...] = pos_ref[...] + jnp.where(eidx_ref[...] == e, off, 0)
        return off + n_tile * MOE_TILE

    lax.fori_loop(0, N_EXPERTS, per_expert, jnp.int32(0))


def _finalize(counts, eidx, rank):
    full = pl.BlockSpec((K_PAD, N_TRUNK), lambda: (0, 0))
    return pl.pallas_call(
        _finalize_body,
        in_specs=[pl.BlockSpec(memory_space=pltpu.SMEM), full, full],
        out_specs=full,
        out_shape=jax.ShapeDtypeStruct((K_PAD, N_TRUNK), jnp.int32),
        compiler_params=pltpu.CompilerParams(vmem_limit_bytes=VMEM_LIMIT),
        name="route_finalize",
    )(counts, eidx, rank)


def _ffn(h, wgu, wd):
    gu = jnp.dot(h, wgu, preferred_element_type=F32)
    hid = _silu(gu[:, :EXPERT_FF]) * gu[:, EXPERT_FF:]
    return jnp.dot(hid.astype(BF16), wd, preferred_element_type=F32)


def _cast_ffn_weights(wg_ref, wu_ref, wd_ref, wgub_ref, wdb_ref):
    wgub_ref[:, :EXPERT_FF] = wg_ref[...].astype(BF16)
    wgub_ref[:, EXPERT_FF:] = wu_ref[...].astype(BF16)
    wdb_ref[...] = wd_ref[...].astype(BF16)


_FFN_WEIGHT_SCRATCH = [pltpu.VMEM((D_MODEL, 2 * EXPERT_FF), BF16), pltpu.VMEM((EXPERT_FF, D_MODEL), BF16)]


def _sc_mesh():
    from jax.experimental.pallas import tpu_sc as plsc
    return plsc.VectorSubcoreMesh(core_axis_name="c", subcore_axis_name="s",
                                  num_cores=SC_CORES, num_subcores=SC_SUBCORES)


def _sc_worker_id():
    return lax.axis_index("s") * SC_CORES + lax.axis_index("c")


def _sc_dispatch(hp, pos_flat):
    win = SC_SCATTER_WIN
    per_worker = N_TRUNK // SC_WORKERS

    def body(rows_hbm, idx_hbm, out_hbm, *scratch):
        idx_v, rows_v, sem = scratch[:TOP_K], scratch[TOP_K], scratch[TOP_K + 1]
        base = _sc_worker_id() * per_worker

        @pl.loop(0, per_worker // win)
        def _(j):
            off = base + j * win
            pltpu.sync_copy(rows_hbm.at[pl.ds(off, win)], rows_v)
            for k in range(TOP_K):
                pltpu.sync_copy(idx_hbm.at[pl.ds(k * N_TRUNK + off, win)], idx_v[k])
            copies = [pltpu.async_copy(rows_v, out_hbm.at[idx_v[k]], sem) for k in range(TOP_K)]
            for cp in copies:
                cp.wait()

    return pl.kernel(
        body, mesh=_sc_mesh(),
        out_type=jax.ShapeDtypeStruct((N_SLOT, HALF), jnp.uint32),
        scratch_types=[pltpu.VMEM((win,), jnp.int32)] * TOP_K
        + [pltpu.VMEM((win, HALF), jnp.uint32), pltpu.SemaphoreType.DMA],
        name="sc_dispatch",
    )(hp, pos_flat)


RETURN_PARTS = 4
N_PART = N_TRUNK // RETURN_PARTS


def _sc_return(ys, pos_flat, part):
    win = SC_GATHER_WIN
    wins_per_choice = N_PART // win
    log2_wins = wins_per_choice.bit_length() - 1
    assert wins_per_choice == 1 << log2_wins
    wins_per_worker = TOP_K * wins_per_choice // SC_WORKERS
    assert wins_per_worker * SC_WORKERS == TOP_K * wins_per_choice

    def body(src_hbm, idx_hbm, out_hbm, idx_v, rows_v, sem):
        first = _sc_worker_id() * wins_per_worker

        @pl.loop(0, wins_per_worker)
        def _(j):
            g = first + j
            k = lax.shift_right_logical(g, log2_wins)
            src_off = k * N_TRUNK + part * N_PART + (g - k * wins_per_choice) * win
            pltpu.sync_copy(idx_hbm.at[pl.ds(src_off, win)], idx_v)
            pltpu.async_copy(src_hbm.at[idx_v], rows_v, sem).wait()
            pltpu.sync_copy(rows_v, out_hbm.at[pl.ds(g * win, win)])

    return pl.kernel(
        body, mesh=_sc_mesh(),
        out_type=jax.ShapeDtypeStruct((TOP_K * N_PART, HALF), jnp.uint32),
        scratch_types=[pltpu.VMEM((win,), jnp.int32), pltpu.VMEM((win, HALF), jnp.uint32),
                       pltpu.SemaphoreType.DMA],
        name="sc_return",
    )(ys, pos_flat)


N_PLAN = 2 * N_TILE


def _tile_plan_body(cnt0_ref, cnt1_ref, exp_ref, blk0_ref, blk1_ref, live_ref, n_ref):
    def per_expert(e, carry):
        s, b0, b1 = carry
        n0 = lax.div(cnt0_ref[e] + (MOE_TILE - 1), MOE_TILE)
        n1 = lax.div(cnt1_ref[e] + (MOE_TILE - 1), MOE_TILE)

        def fill(j, c):
            exp_ref[s + j] = e
            blk0_ref[s + j] = jnp.maximum(b0 + jnp.minimum(j, n0 - 1), 0)
            blk1_ref[s + j] = jnp.maximum(b1 + jnp.minimum(j, n1 - 1), 0)
            live_ref[s + j] = jnp.where(j < n0, 1, 0) + jnp.where(j < n1, 2, 0)
            return c
        n = jnp.maximum(n0, n1)
        lax.fori_loop(0, n, fill, 0)
        return s + n, b0 + n0, b1 + n1

    zero = jnp.int32(0)
    n_steps, _, _ = lax.fori_loop(0, N_EXPERTS, per_expert, (zero, zero, zero))
    n_ref[0] = n_steps

    def fill_tail(s, c):
        exp_ref[s] = 0
        blk0_ref[s] = 0
        blk1_ref[s] = 0
        live_ref[s] = 0
        return c
    lax.fori_loop(n_steps, N_PLAN, fill_tail, 0)


def _tile_plan(counts0, counts1):
    smem = pl.BlockSpec(memory_space=pltpu.SMEM)
    steps = jax.ShapeDtypeStruct((N_PLAN,), jnp.int32)
    return pl.pallas_call(
        _tile_plan_body,
        in_specs=[smem, smem],
        out_specs=[smem] * 5,
        out_shape=[steps, steps, steps, steps, jax.ShapeDtypeStruct((1,), jnp.int32)],
        name="tile_plan",
    )(counts0, counts1)


def _expert_body(exp_ref, blk0_ref, blk1_ref, live_ref, xs0_ref, xs1_ref, wg_ref, wu_ref, wd_ref,
                 ys0_ref, ys1_ref, wgub_ref, wdb_ref):
    i = pl.program_id(0)

    @pl.when(jnp.logical_or(i == 0, exp_ref[i] != exp_ref[jnp.maximum(i - 1, 0)]))
    def _():
        _cast_ffn_weights(wg_ref, wu_ref, wd_ref, wgub_ref, wdb_ref)

    for bit, xs_ref, ys_ref in ((1, xs0_ref, ys0_ref), (2, xs1_ref, ys1_ref)):
        @pl.when((live_ref[i] & bit) != 0)
        def _():
            lo, hi = _unpack_bf16_pair(xs_ref[...])
            h = jnp.concatenate([lo.astype(BF16), hi.astype(BF16)], axis=1)
            y = _ffn(h, wgub_ref[...], wdb_ref[...])
            ys_ref[...] = _pack_bf16_pair(y[:, :HALF], y[:, HALF:])


def _expert_ffn(plan, xs0, xs1, layer, w_gate, w_up, w_down):
    exp, blk0, blk1, live, n_steps = plan
    tile0 = lambda i, exp, blk0, blk1, live: (blk0[i], 0)
    tile1 = lambda i, exp, blk0, blk1, live: (blk1[i], 0)
    ew = lambda shape: pl.BlockSpec((None, None) + shape,
                                    lambda i, exp, blk0, blk1, live: (layer, exp[i], 0, 0))
    slots = jax.ShapeDtypeStruct((N_SLOT, HALF), jnp.uint32)
    return pl.pallas_call(
        _expert_body,
        grid_spec=pltpu.PrefetchScalarGridSpec(
            num_scalar_prefetch=4,
            grid=(n_steps[0],),
            in_specs=[
                pl.BlockSpec((MOE_TILE, HALF), tile0), pl.BlockSpec((MOE_TILE, HALF), tile1),
                ew((D_MODEL, EXPERT_FF)), ew((D_MODEL, EXPERT_FF)), ew((EXPERT_FF, D_MODEL)),
            ],
            out_specs=[pl.BlockSpec((MOE_TILE, HALF), tile0), pl.BlockSpec((MOE_TILE, HALF), tile1)],
            scratch_shapes=_FFN_WEIGHT_SCRATCH,
        ),
        out_shape=[slots, slots],
        compiler_params=pltpu.CompilerParams(
            dimension_semantics=("arbitrary",), vmem_limit_bytes=VMEM_LIMIT),
        name="expert_ffn",
    )(exp, blk0, blk1, live, xs0, xs1, w_gate, w_up, w_down)


def _combine_body(x_ref, sh_ref, sc_ref, g2_ref, yk_ref, wt_ref, sg_ref, su_ref, sd_ref, lng_ref, lnb_ref,
                  *rest):
    o_ref, sgub_ref, sdb_ref = rest[-3:]

    @pl.when(pl.program_id(0) == 0)
    def _():
        _cast_ffn_weights(sg_ref, su_ref, sd_ref, sgub_ref, sdb_ref)

    x = x_ref[...].astype(F32)
    h = (x * (1.0 + sc_ref[...]) + sh_ref[...]).astype(BF16)
    y = _ffn(h, sgub_ref[...], sdb_ref[...])
    wt = wt_ref[...]
    lo_acc = jnp.zeros((x.shape[0], HALF), F32)
    hi_acc = jnp.zeros((x.shape[0], HALF), F32)
    for k in range(TOP_K):
        lo, hi = _unpack_bf16_pair(yk_ref[k])
        w = wt[:, k:k + 1]
        lo_acc = lo_acc + w * lo
        hi_acc = hi_acc + w * hi
    y = y + jnp.concatenate([lo_acc, hi_acc], axis=1)
    z = DEEPNORM_ALPHA * x + g2_ref[...] * y
    o_ref[...] = _layer_norm(z, lng_ref[...], lnb_ref[...]).astype(o_ref.dtype)


def _combine(trunk, part, x, mod, yk, wt, layer, s_gate, s_up, s_down, ln_g, ln_b, out_dtype, earlier):
    tm = TOK_BLOCK
    part_block = part * (N_PART // tm)
    first_block = trunk * (N_TRUNK // tm) + part_block
    row = lambda i: (part_block + i, 0)
    vec = pl.BlockSpec((1, D_MODEL), lambda i: (0, 0))
    sw = lambda shape: pl.BlockSpec((None,) + shape, lambda i: (layer, 0, 0))
    mod_spec = lambda col: _mod_spec(tm, col, first_block)
    in_specs = [
        pl.BlockSpec((tm, D_MODEL), row),
        mod_spec(3), mod_spec(4), mod_spec(5),
        pl.BlockSpec((TOP_K, tm, HALF), lambda i: (0, i, 0)),
        pl.BlockSpec((tm, 128), row),
        sw((D_MODEL, EXPERT_FF)), sw((D_MODEL, EXPERT_FF)), sw((EXPERT_FF, D_MODEL)),
        vec, vec,
    ]
    args = (x, mod, mod, mod, yk, wt, s_gate, s_up, s_down, ln_g, ln_b)
    aliases = {}
    if earlier is not None:
        in_specs.append(pl.BlockSpec(memory_space=pl.ANY))
        aliases = {len(args): 0}
        args += (earlier,)
    return pl.pallas_call(
        _combine_body,
        grid=(N_PART // tm,),
        in_specs=in_specs,
        out_specs=pl.BlockSpec((tm, D_MODEL), row),
        out_shape=jax.ShapeDtypeStruct((N_TRUNK, D_MODEL), out_dtype),
        input_output_aliases=aliases,
        scratch_shapes=_FFN_WEIGHT_SCRATCH,
        compiler_params=pltpu.CompilerParams(
            dimension_semantics=("arbitrary",), vmem_limit_bytes=VMEM_LIMIT),
        name="moe_combine_ln",
    )(*args)


def _moe_dispatch(routing):
    hp, eidx, rank, wt, counts = routing
    counts = counts[:, 0]
    pos_flat = _finalize(counts, eidx, rank).reshape(K_PAD * N_TRUNK)
    return _sc_dispatch(hp, pos_flat), pos_flat, counts, wt


def _moe_experts(plans, layer, w_gate, w_up, w_down):
    (xs0, pos0, counts0, _), (xs1, pos1, counts1, _) = plans
    ys = _expert_ffn(_tile_plan(counts0, counts1), xs0, xs1, layer, w_gate, w_up, w_down)
    return [[_sc_return(y, pos, part).reshape(TOP_K, N_PART, HALF) for part in range(RETURN_PARTS)]
            for y, pos in zip(ys, (pos0, pos1))]


def kernel(x_prompt, x_sample, state_ret, c, c_ctx, ada_w, ada_b, ln_g, ln_b, ret_w_in, ret_w_out, ret_decay, conv_w_in, conv_w, conv_w_out, moe_router, moe_bias, moe_w_gate, moe_w_up, moe_w_down, shared_w_gate, shared_w_up, shared_w_down):
    x = (x_prompt.reshape(N_CTX, D_MODEL), x_sample.reshape(N_LAT, D_MODEL))
    cond = jnp.concatenate(
        [c_ctx[None, :], c, jnp.zeros((N_COND - 1 - DEC_BATCH, D_MODEL), F32)], axis=0)
    mods = _ada_table(cond, ada_w, ada_b).reshape(DEPTH, N_COND, 1, 6 * D_MODEL)
    cos, sin = _rope_tables()
    router_wt = jnp.swapaxes(moe_router, 1, 2)

    states = None
    for i in range(DEPTH):
        j = i // 2
        mod = mods[i]
        lng = ln_g[i].reshape(2, 1, D_MODEL)
        lnb = ln_b[i].reshape(2, 1, D_MODEL)
        bias_col = moe_bias[i].reshape(N_EXPERTS, 1)
        if i % 2 == 0:
            w_in_bf = _to_bf16(ret_w_in, j)
        mixed, plans = [], []
        for trunk in range(2):
            if i % 2 == 0:
                if trunk == 0:
                    a, states = _retention_ctx(x[0], mod, w_in_bf, ret_decay[j], j, states)
                else:
                    a = _retention_lat(x[1], mod, w_in_bf, ret_decay[j], state_ret, j, cos, sin)
                xm, *routing = _ret_out(trunk, a, x[trunk], mod, ret_w_out, j, lng[0], lnb[0],
                                        router_wt[i], bias_col)
            else:
                xm, *routing = _conv_layer(trunk, x[trunk], mod, conv_w_in, conv_w_out, j, conv_w[j],
                                           lng[0], lnb[0], router_wt[i], bias_col)
            mixed.append(xm)
            plans.append(_moe_dispatch(routing))
        yks = _moe_experts(plans, i, moe_w_gate, moe_w_up, moe_w_down)
        out_dtype = F32 if i == DEPTH - 1 else RES_DTYPE
        x = []
        for trunk in range(2):
            out = None
            for part in range(RETURN_PARTS):
                out = _combine(trunk, part, mixed[trunk], mod, yks[trunk][part], plans[trunk][3], i,
                               shared_w_gate, shared_w_up, shared_w_down, lng[1], lnb[1], out_dtype, out)
            x.append(out)

    y_prompt = x[0].reshape(BATCH, SEQ, D_MODEL)
    y_sample = x[1].reshape(DEC_BATCH, DEC_SEQ, D_MODEL)
    return y_prompt, y_sample, states
```

```python
import functools

import jax
import jax.numpy as jnp
from jax import lax
from jax.experimental import pallas as pl
from jax.experimental.pallas import tpu as pltpu

F32 = jnp.float32
BF16 = jnp.bfloat16

D_MODEL = 1024
BATCH = 32
SEQ = 256
DEPTH = 4
DEC_BATCH = 8
DEC_SEQ = 1024
GRID_W = 64
RET_HEADS = 4
RET_DK = D_MODEL // RET_HEADS
RET_DV = 2 * D_MODEL // RET_HEADS
ROPE_BASE = 10000.0
N_EXPERTS = 64
TOP_K = 6
N_GROUPS = 8
TOPK_GROUPS = 4
EXPERT_FF = 256
ROUTED_SCALE = 2.5
LN_EPS = 1e-5
DEEPNORM_ALPHA = (2.0 * DEPTH) ** 0.25

N_CTX = BATCH * SEQ
N_LAT = DEC_BATCH * DEC_SEQ
N_COND = 16
RET_CHUNK = 256
assert N_CTX == N_LAT
N_TRUNK = N_CTX
TOK_BLOCK = 512
MIX_BLOCK = 512
VMEM_LIMIT = 56 * 1024 * 1024
RES_DTYPE = BF16

HALF = D_MODEL // 2
MOE_TILE = 896
N_PAIR = N_TRUNK * TOP_K
N_TILE = -(-(N_PAIR + N_EXPERTS * (MOE_TILE - 1)) // MOE_TILE)
N_SLOT = N_TILE * MOE_TILE
K_PAD = 8

SC_CORES = 2
SC_SUBCORES = 16
SC_WORKERS = SC_CORES * SC_SUBCORES
SC_SCATTER_WIN = 64
SC_GATHER_WIN = 128


def _cond_row(tok_block_idx, tok_block):
    t0 = tok_block_idx * tok_block
    return jnp.where(t0 < N_CTX, 0, 1 + (t0 - N_CTX) // DEC_SEQ)


def _silu(x):
    return x * jax.nn.sigmoid(x)


def _layer_norm(z, g, b):
    mu = jnp.mean(z, axis=-1, keepdims=True)
    zc = z - mu
    var = jnp.mean(zc * zc, axis=-1, keepdims=True)
    return zc * lax.rsqrt(var + LN_EPS) * g + b


def _ada_body(cond_ref, w_ref, b_ref, o_ref):
    s = _silu(cond_ref[...]).astype(BF16)
    o_ref[...] = jnp.dot(s, w_ref[...].astype(BF16), preferred_element_type=F32) + b_ref[...]


def _ada_table(cond, ada_w, ada_b):
    tn = 2048
    return pl.pallas_call(
        _ada_body,
        grid=(DEPTH, 6 * D_MODEL // tn),
        in_specs=[
            pl.BlockSpec((N_COND, D_MODEL), lambda l, j: (0, 0)),
            pl.BlockSpec((None, D_MODEL, tn), lambda l, j: (l, 0, j)),
            pl.BlockSpec((None, 1, tn), lambda l, j: (l, 0, j)),
        ],
        out_specs=pl.BlockSpec((None, N_COND, tn), lambda l, j: (l, 0, j)),
        out_shape=jax.ShapeDtypeStruct((DEPTH, N_COND, 6 * D_MODEL), F32),
        compiler_params=pltpu.CompilerParams(
            dimension_semantics=("arbitrary", "arbitrary"), vmem_limit_bytes=VMEM_LIMIT),
        name="ada_table",
    )(cond, ada_w, ada_b.reshape(DEPTH, 1, 6 * D_MODEL))


def _mod_spec(tok_block, col, first_block=0):
    def index_map(*idx):
        return (_cond_row(first_block + idx[0], tok_block), 0, col)
    return pl.BlockSpec((None, 1, D_MODEL), index_map)


def _to_bf16_body(w_ref, o_ref):
    o_ref[...] = w_ref[...].astype(BF16)


def _to_bf16(w_all, w_idx):
    k_dim, n_out = w_all.shape[1:]
    tn = 1024
    return pl.pallas_call(
        _to_bf16_body,
        grid=(n_out // tn,),
        in_specs=[pl.BlockSpec((None, k_dim, tn), lambda j: (w_idx, 0, j))],
        out_specs=pl.BlockSpec((k_dim, tn), lambda j: (0, j)),
        out_shape=jax.ShapeDtypeStruct((k_dim, n_out), BF16),
        compiler_params=pltpu.CompilerParams(
            dimension_semantics=("arbitrary",), vmem_limit_bytes=VMEM_LIMIT),
        name="weight_to_bf16",
    )(w_all)


RET_GROUP = 1024


def _head_proj_specs():
    return [
        pl.BlockSpec((D_MODEL, RET_DK), lambda t, h: (0, h)),
        pl.BlockSpec((D_MODEL, RET_DK), lambda t, h: (0, RET_HEADS + h)),
        pl.BlockSpec((D_MODEL, RET_DV), lambda t, h: (0, RET_HEADS + h)),
        pl.BlockSpec((D_MODEL, RET_DV), lambda t, h: (0, 2 * RET_HEADS + h)),
    ]


def _head_proj(x_ref, sh_ref, sc_ref, wq_ref, wk_ref, wv_ref, wg_ref):
    h = (x_ref[...].astype(F32) * (1.0 + sc_ref[...]) + sh_ref[...]).astype(BF16)
    dot = lambda w_ref: jnp.dot(h, w_ref[...], preferred_element_type=F32)
    return dot(wq_ref), dot(wk_ref), dot(wv_ref).astype(BF16), dot(wg_ref)

def _log_sigmoid(v):
    return jnp.minimum(v, 0.0) - jnp.log1p(jnp.exp(-jnp.abs(v)))


def _decay_tables(dec_ref, head):
    c = RET_CHUNK
    lgf = _log_sigmoid(jnp.full((c, c), dec_ref[0, head], F32))
    lgb = _log_sigmoid(jnp.full((c, c), dec_ref[1, head], F32))
    row = lax.broadcasted_iota(jnp.int32, (c, c), 0).astype(F32)
    col = lax.broadcasted_iota(jnp.int32, (c, c), 1).astype(F32)
    diff = row - col
    kscale = RET_DK ** -0.5
    intra = jnp.where(diff > 0, jnp.exp(lgf * diff),
                      jnp.where(diff < 0, jnp.exp(-lgb * diff), 2.0)) * kscale
    qdec_f = jnp.exp(lgf * (row + 1.0))
    qdec_b = jnp.exp(lgb * (c - row))
    kdec_f = jnp.exp(lgf * (c - 1.0 - row)) * kscale
    kdec_b = jnp.exp(lgb * row) * kscale
    cdec_f = jnp.exp(lgf * c)
    cdec_b = jnp.exp(lgb * c)
    return intra, qdec_f, qdec_b, kdec_f, kdec_b, cdec_f, cdec_b


def _head_norm_gate(o, g):
    mu = jnp.mean(o, axis=-1, keepdims=True)
    oc = o - mu
    var = jnp.mean(oc * oc, axis=-1, keepdims=True)
    on = oc * lax.rsqrt(var + LN_EPS)
    return (_silu(g.astype(F32)) * on).astype(BF16)


_NT = (((1,), (1,)), ((), ()))
_TN = (((0,), (0,)), ((), ()))


def _ret_ctx_body(dec_ref, x_ref, sh_ref, sc_ref, wq_ref, wk_ref, wv_ref, wg_ref, *rest):
    a_ref, st_ref, tab_ref = rest[-3:]
    head = pl.program_id(1)

    @pl.when(pl.program_id(0) == 0)
    def _():
        intra, _, _, kdec_f, kdec_b, _, _ = _decay_tables(dec_ref, head)
        tab_ref[head, 0] = intra
        tab_ref[head, 1] = kdec_f
        tab_ref[head, 2] = kdec_b

    q, k, v, g = _head_proj(x_ref, sh_ref, sc_ref, wq_ref, wk_ref, wv_ref, wg_ref)
    for s in range(RET_GROUP // SEQ):
        rows = slice(s * SEQ, (s + 1) * SEQ)
        scores = lax.dot_general(q[rows].astype(BF16), k[rows].astype(BF16), _NT, preferred_element_type=F32)
        p = (scores * tab_ref[head, 0]).astype(BF16)
        o = jnp.dot(p, v[rows], preferred_element_type=F32)
        a_ref[rows, :] = _head_norm_gate(o, g[rows])
        st_ref[s, 0] = lax.dot_general((k[rows] * tab_ref[head, 1]).astype(BF16), v[rows], _TN,
                                       preferred_element_type=F32)
        st_ref[s, 1] = lax.dot_general((k[rows] * tab_ref[head, 2]).astype(BF16), v[rows], _TN,
                                       preferred_element_type=F32)


def _retention_ctx(x, mod, w_in_bf, decay, ret_idx, states):
    assert SEQ == RET_CHUNK
    seqs = RET_GROUP // SEQ
    n_ret = (DEPTH + 1) // 2
    st = jax.ShapeDtypeStruct((BATCH, n_ret, 2, RET_HEADS, RET_DK, RET_DV), F32)
    st_spec = pl.BlockSpec((seqs, None, 2, None, RET_DK, RET_DV), lambda t, h: (t, ret_idx, 0, h, 0, 0))
    in_specs = [
        pl.BlockSpec(memory_space=pltpu.SMEM),
        pl.BlockSpec((RET_GROUP, D_MODEL), lambda t, h: (t, 0)),
        _mod_spec(RET_GROUP, 0), _mod_spec(RET_GROUP, 1),
    ] + _head_proj_specs()
    args = (decay, x, mod, mod, w_in_bf, w_in_bf, w_in_bf, w_in_bf)
    aliases = {}
    if states is not None:
        in_specs.append(pl.BlockSpec(memory_space=pl.ANY))
        aliases = {len(args): 1}
        args += (states,)
    return pl.pallas_call(
        _ret_ctx_body,
        grid=(N_CTX // RET_GROUP, RET_HEADS),
        in_specs=in_specs,
        out_specs=[pl.BlockSpec((RET_GROUP, RET_DV), lambda t, h: (t, h)), st_spec],
        out_shape=[jax.ShapeDtypeStruct((N_CTX, RET_HEADS * RET_DV), BF16), st],
        input_output_aliases=aliases,
        scratch_shapes=[pltpu.VMEM((RET_HEADS, 3, RET_CHUNK, RET_CHUNK), F32)],
        compiler_params=pltpu.CompilerParams(
            dimension_semantics=("arbitrary", "arbitrary"), vmem_limit_bytes=VMEM_LIMIT),
        name="retention_ctx",
    )(*args)


def _rope(x, cos, sin):
    halves = [pltpu.roll(x[:, s:s + 128], 64, axis=1) for s in (0, 128)]
    return x * cos + jnp.concatenate(halves, axis=1) * sin


def _ret_lat_body(dec_ref, x_ref, sh_ref, sc_ref, wq_ref, wk_ref, wv_ref, wg_ref,
                  s0f_ref, s0b_ref, cos_ref, sin_ref,
                  a_ref, qr_ref, kr_ref, v_ref, g_ref, o_ref, st_ref):
    head = pl.program_id(1)
    c = RET_CHUNK
    nc = DEC_SEQ // c
    intra, qdec_f, qdec_b, kdec_f, kdec_b, cdec_f, cdec_b = _decay_tables(dec_ref, head)
    cdec_f = jnp.concatenate([cdec_f, cdec_f], axis=1)
    cdec_b = jnp.concatenate([cdec_b, cdec_b], axis=1)

    q, k, v, g = _head_proj(x_ref, sh_ref, sc_ref, wq_ref, wk_ref, wv_ref, wg_ref)
    qr_ref[...] = _rope(q, cos_ref[...], sin_ref[...])
    kr_ref[...] = _rope(k, cos_ref[...], sin_ref[...])
    v_ref[...] = v
    g_ref[...] = g

    st_ref[...] = s0f_ref[...]
    for ci in range(nc):
        rows = pl.ds(ci * c, c)
        q = qr_ref[rows, :]
        k = kr_ref[rows, :]
        v = v_ref[rows, :]
        scores = lax.dot_general(q.astype(BF16), k.astype(BF16), _NT, preferred_element_type=F32)
        p = (scores * intra).astype(BF16)
        o = jnp.dot(p, v, preferred_element_type=F32)
        qd = (q * qdec_f).astype(BF16)
        o = o + jnp.dot(qd, st_ref[...].astype(BF16), preferred_element_type=F32)
        o_ref[rows, :] = o
        if ci + 1 < nc:
            kd = (k * kdec_f).astype(BF16)
            st_ref[...] = cdec_f * st_ref[...] + lax.dot_general(kd, v, _TN, preferred_element_type=F32)

    st_ref[...] = s0b_ref[...]
    for ci in reversed(range(nc)):
        rows = pl.ds(ci * c, c)
        qd = (qr_ref[rows, :] * qdec_b).astype(BF16)
        o = o_ref[rows, :] + jnp.dot(qd, st_ref[...].astype(BF16), preferred_element_type=F32)
        a_ref[rows, :] = _head_norm_gate(o, g_ref[rows, :])
        if ci > 0:
            kd = (kr_ref[rows, :] * kdec_b).astype(BF16)
            st_ref[...] = cdec_b * st_ref[...] + lax.dot_general(kd, v_ref[rows, :], _TN,
                                                                 preferred_element_type=F32)


def _rope_tables():
    half = RET_DK // 4
    freqs = ROPE_BASE ** (-jnp.arange(half, dtype=F32) / half)
    t = jnp.arange(DEC_SEQ)
    row = (t // GRID_W).astype(F32)
    col = (t % GRID_W).astype(F32)
    ang_r = row[:, None] * freqs[None, :]
    ang_c = col[:, None] * freqs[None, :]
    cos = jnp.concatenate([jnp.cos(ang_r)] * 2 + [jnp.cos(ang_c)] * 2, axis=1)
    sin = jnp.concatenate([-jnp.sin(ang_r), jnp.sin(ang_r), -jnp.sin(ang_c), jnp.sin(ang_c)], axis=1)
    return cos, sin


def _retention_lat(x, mod, w_in_bf, decay, state_ret, ret_idx, cos, sin):
    n = DEC_SEQ
    assert n == RET_GROUP
    s0_spec = lambda d: pl.BlockSpec((None, None, None, None, RET_DK, RET_DV),
                                     lambda b, h: (b, ret_idx, d, h, 0, 0))
    tab_spec = pl.BlockSpec((n, RET_DK), lambda b, h: (0, 0))
    first_block = N_CTX // n
    return pl.pallas_call(
        _ret_lat_body,
        grid=(DEC_BATCH, RET_HEADS),
        in_specs=[
            pl.BlockSpec(memory_space=pltpu.SMEM),
            pl.BlockSpec((n, D_MODEL), lambda b, h: (b, 0)),
            _mod_spec(n, 0, first_block), _mod_spec(n, 1, first_block),
        ] + _head_proj_specs() + [s0_spec(0), s0_spec(1), tab_spec, tab_spec],
        out_specs=pl.BlockSpec((n, RET_DV), lambda b, h: (b, h)),
        out_shape=jax.ShapeDtypeStruct((N_LAT, RET_HEADS * RET_DV), BF16),
        scratch_shapes=[
            pltpu.VMEM((n, RET_DK), F32), pltpu.VMEM((n, RET_DK), F32),
            pltpu.VMEM((n, RET_DV), BF16), pltpu.VMEM((n, RET_DV), F32),
            pltpu.VMEM((n, RET_DV), F32), pltpu.VMEM((RET_DK, RET_DV), F32),
        ],
        compiler_params=pltpu.CompilerParams(
            dimension_semantics=("arbitrary", "arbitrary"), vmem_limit_bytes=VMEM_LIMIT),
        name="retention_lat",
    )(decay, x, mod, mod, w_in_bf, w_in_bf, w_in_bf, w_in_bf, state_ret, state_ret, cos, sin)


N_ROUTE_IN = 4
N_ROUTE_OUT = 5


def _split_route_refs(rest, n_scratch):
    route_in = rest[:N_ROUTE_IN]
    o_ref = rest[N_ROUTE_IN]
    route_out = rest[N_ROUTE_IN + 1:N_ROUTE_IN + 1 + N_ROUTE_OUT]
    return route_in, o_ref, route_out, rest[len(rest) - n_scratch:]


def _ret_out_body(a_ref, x_ref, g1_ref, w_ref, lng_ref, lnb_ref, *rest):
    route_in, o_ref, route_out, (wbf_ref, carry_ref) = _split_route_refs(rest, 2)

    @pl.when(pl.program_id(0) == 0)
    def _():
        wbf_ref[...] = w_ref[...].astype(BF16)
        carry_ref[...] = jnp.zeros_like(carry_ref)

    y = jnp.dot(a_ref[...], wbf_ref[...], preferred_element_type=F32)
    z = DEEPNORM_ALPHA * x_ref[...].astype(F32) + g1_ref[...] * y
    xn = _layer_norm(z, lng_ref[...], lnb_ref[...])
    o_ref[...] = xn.astype(o_ref.dtype)
    _route_block(xn, *route_in, *route_out, carry_ref)


def _ret_out(trunk, a, x, mod, w_all, w_idx, ln_g, ln_b, router_wt, bias_col):
    tm = MIX_BLOCK
    k_dim = w_all.shape[1]
    first_block = trunk * (N_TRUNK // tm)
    row = lambda i: (i, 0)
    vec = pl.BlockSpec((1, D_MODEL), lambda i: (0, 0))
    r_in, r_out, r_shapes, r_scratch = _route_io(tm, first_block)
    return pl.pallas_call(
        _ret_out_body,
        grid=(N_TRUNK // tm,),
        in_specs=[
            pl.BlockSpec((tm, k_dim), row),
            pl.BlockSpec((tm, D_MODEL), row),
            _mod_spec(tm, 2, first_block),
            pl.BlockSpec((None, k_dim, D_MODEL), lambda i: (w_idx, 0, 0)),
            vec, vec,
        ] + r_in,
        out_specs=[pl.BlockSpec((tm, D_MODEL), row)] + r_out,
        out_shape=[jax.ShapeDtypeStruct((N_TRUNK, D_MODEL), RES_DTYPE)] + r_shapes,
        scratch_shapes=[pltpu.VMEM((k_dim, D_MODEL), BF16)] + r_scratch,
        compiler_params=pltpu.CompilerParams(
            dimension_semantics=("arbitrary",), vmem_limit_bytes=VMEM_LIMIT),
        name="ret_out_ln_route",
    )(a, x, mod, w_all, ln_g, ln_b, mod, mod, router_wt, bias_col)


def _conv_layer_body(seg, x_ref, sh_ref, sc_ref, g1_ref, win_ref, wout_ref, cw_ref, lng_ref, lnb_ref, *rest):
    route_in, o_ref, route_out, (winb_ref, woutb_ref, carry_ref) = _split_route_refs(rest, 3)
    tm = o_ref.shape[0]

    @pl.when(pl.program_id(0) == 0)
    def _():
        winb_ref[...] = win_ref[...].astype(BF16)
        woutb_ref[...] = wout_ref[...].astype(BF16)
        carry_ref[...] = jnp.zeros_like(carry_ref)

    x = x_ref[...].astype(F32)
    h = (x * (1.0 + sc_ref[...]) + sh_ref[...]).astype(BF16)
    proj = jnp.dot(h, winb_ref[...], preferred_element_type=F32)
    bg, cg, xt = (proj[:, c * D_MODEL:(c + 1) * D_MODEL] for c in range(3))
    u = cg * xt
    pos = lax.broadcasted_iota(jnp.int32, (tm, D_MODEL), 0) & (seg - 1)
    u_prev = jnp.where(pos == 0, 0.0, pltpu.roll(u, 1, axis=0))
    u_next = jnp.where(pos == seg - 1, 0.0, pltpu.roll(u, tm - 1, axis=0))
    cu = u_prev * cw_ref[0:1, :] + u * cw_ref[1:2, :] + u_next * cw_ref[2:3, :]
    y = jnp.dot((bg * cu).astype(BF16), woutb_ref[...], preferred_element_type=F32)
    z = DEEPNORM_ALPHA * x + g1_ref[...] * y
    xn = _layer_norm(z, lng_ref[...], lnb_ref[...])
    o_ref[...] = xn.astype(o_ref.dtype)
    _route_block(xn, *route_in, *route_out, carry_ref)


def _conv_layer(trunk, x, mod, w_in_all, w_out_all, w_idx, conv_w, ln_g, ln_b, router_wt, bias_col):
    tm = MIX_BLOCK
    first_block = trunk * (N_TRUNK // tm)
    seg = SEQ if trunk == 0 else GRID_W
    assert tm % seg == 0 and seg & (seg - 1) == 0
    row = lambda i: (i, 0)
    vec = pl.BlockSpec((1, D_MODEL), lambda i: (0, 0))
    once = pl.Buffered(1)
    r_in, r_out, r_shapes, r_scratch = _route_io(tm, first_block)
    mod_spec = lambda col: _mod_spec(tm, col, first_block)
    return pl.pallas_call(
        functools.partial(_conv_layer_body, seg),
        grid=(N_TRUNK // tm,),
        in_specs=[
            pl.BlockSpec((tm, D_MODEL), row),
            mod_spec(0), mod_spec(1), mod_spec(2),
            pl.BlockSpec((None, D_MODEL, 3 * D_MODEL), lambda i: (w_idx, 0, 0), pipeline_mode=once),
            pl.BlockSpec((None, D_MODEL, D_MODEL), lambda i: (w_idx, 0, 0), pipeline_mode=once),
            pl.BlockSpec((3, D_MODEL), lambda i: (0, 0)),
            vec, vec,
        ] + r_in,
        out_specs=[pl.BlockSpec((tm, D_MODEL), row)] + r_out,
        out_shape=[jax.ShapeDtypeStruct((N_TRUNK, D_MODEL), RES_DTYPE)] + r_shapes,
        scratch_shapes=[pltpu.VMEM((D_MODEL, 3 * D_MODEL), BF16), pltpu.VMEM((D_MODEL, D_MODEL), BF16)]
        + r_scratch,
        compiler_params=pltpu.CompilerParams(
            dimension_semantics=("arbitrary",), vmem_limit_bytes=VMEM_LIMIT),
        name="conv_layer_route",
    )(x, mod, mod, mod, w_in_all, w_out_all, conv_w, ln_g, ln_b, mod, mod, router_wt, bias_col)


def _split_bf16(v):
    hi = v.astype(BF16)
    lo = (v - hi.astype(F32)).astype(BF16)
    return hi, lo


def _pack_bf16_pair(lo_f32, hi_f32):
    lo = lax.bitcast_convert_type(lo_f32.astype(BF16).astype(F32), jnp.uint32) >> 16
    hi = lax.bitcast_convert_type(hi_f32.astype(BF16).astype(F32), jnp.uint32) & jnp.uint32(0xFFFF0000)
    return hi | lo


def _unpack_bf16_pair(u):
    lo = lax.bitcast_convert_type(u << 16, F32)
    hi = lax.bitcast_convert_type(u & jnp.uint32(0xFFFF0000), F32)
    return lo, hi


def _rows_to_tile(rows, n_sub, dtype):
    tm = rows[0].shape[1]
    sub = lax.broadcasted_iota(jnp.int32, (n_sub, tm), 0)
    out = jnp.zeros((n_sub, tm), dtype)
    for k, r in enumerate(rows):
        out = jnp.where(sub == k, jnp.broadcast_to(r.astype(dtype), (n_sub, tm)), out)
    return out


def _route_block(xn, sh_ref, sc_ref, rwt_ref, bias_ref,
                 hp_ref, eidx_ref, rank_ref, wt_ref, cnt_ref, carry_ref):
    tm = xn.shape[0]
    e = N_EXPERTS
    per = e // N_GROUPS
    neg = -jnp.inf
    h = xn * (1.0 + sc_ref[...]) + sh_ref[...]
    hp_ref[...] = _pack_bf16_pair(h[:, :HALF], h[:, HALF:])
    h_hi, h_lo = _split_bf16(h)
    w_hi, w_lo = _split_bf16(rwt_ref[...])
    dot = lambda a, b: lax.dot_general(a, b, _NT, preferred_element_type=F32)
    logits = dot(w_hi, h_hi) + (dot(w_hi, h_lo) + dot(w_lo, h_hi))
    s = jax.nn.sigmoid(logits)
    sel = s + bias_ref[...]

    g3 = sel.reshape(N_GROUPS, per, tm)
    sub = lax.broadcasted_iota(jnp.int32, (N_GROUPS, per, tm), 1)
    m1 = jnp.max(g3, axis=1, keepdims=True)
    i1 = jnp.min(jnp.where(g3 == m1, sub, per), axis=1, keepdims=True)
    m2 = jnp.max(jnp.where(sub == i1, neg, g3), axis=1, keepdims=True)
    gs = (m1 + m2).reshape(N_GROUPS, tm)

    gi = lax.broadcasted_iota(jnp.int32, (N_GROUPS, tm), 0)
    gmask = jnp.zeros((N_GROUPS, tm), jnp.bool_)
    cur = gs
    for _ in range(TOPK_GROUPS):
        m = jnp.max(cur, axis=0, keepdims=True)
        idx = jnp.min(jnp.where(cur == m, gi, N_GROUPS), axis=0, keepdims=True)
        pick = gi == idx
        gmask = jnp.logical_or(gmask, pick)
        cur = jnp.where(pick, neg, cur)
    emask = jnp.broadcast_to(gmask.reshape(N_GROUPS, 1, tm), (N_GROUPS, per, tm)).reshape(e, tm)

    ei = lax.broadcasted_iota(jnp.int32, (e, tm), 0)
    picks, ids = [], []
    cur = jnp.where(emask, sel, neg)
    for _ in range(TOP_K):
        m = jnp.max(cur, axis=0, keepdims=True)
        idx = jnp.min(jnp.where(cur == m, ei, e), axis=0, keepdims=True)
        pick = ei == idx
        picks.append(pick)
        ids.append(idx)
        cur = jnp.where(pick, neg, cur)

    chosen = functools.reduce(jnp.logical_or, picks)
    cf = jnp.where(chosen, 1.0, 0.0)
    before = (lax.broadcasted_iota(jnp.int32, (tm, tm), 0)
              < lax.broadcasted_iota(jnp.int32, (tm, tm), 1)).astype(BF16)
    rank = carry_ref[:, 0:1] + jnp.dot(cf.astype(BF16), before, preferred_element_type=F32)
    carry_ref[...] = carry_ref[...] + jnp.sum(cf, axis=1, keepdims=True)
    cnt_ref[...] = carry_ref[...].astype(jnp.int32)

    w_rows = [jnp.sum(jnp.where(p, s, 0.0), axis=0, keepdims=True) for p in picks]
    r_rows = [jnp.sum(jnp.where(p, rank, 0.0), axis=0, keepdims=True) for p in picks]
    den = functools.reduce(lambda a, b: a + b, w_rows)
    w_rows = [w / den * ROUTED_SCALE for w in w_rows]
    eidx_ref[...] = _rows_to_tile(ids, K_PAD, jnp.int32)
    rank_ref[...] = _rows_to_tile(r_rows, K_PAD, F32).astype(jnp.int32)
    wt_ref[...] = _rows_to_tile(w_rows, 128, F32).T


def _route_io(tm, first_block):
    tok_major = lambda i: (i, 0)
    choice_major = lambda i: (0, i)
    in_specs = [
        _mod_spec(tm, 3, first_block),
        _mod_spec(tm, 4, first_block),
        pl.BlockSpec((N_EXPERTS, D_MODEL), lambda i: (0, 0)),
        pl.BlockSpec((N_EXPERTS, 1), lambda i: (0, 0)),
    ]
    out_specs = [
        pl.BlockSpec((tm, HALF), tok_major),
        pl.BlockSpec((K_PAD, tm), choice_major),
        pl.BlockSpec((K_PAD, tm), choice_major),
        pl.BlockSpec((tm, 128), tok_major),
        pl.BlockSpec((N_EXPERTS, 128), lambda i: (0, 0)),
    ]
    out_shapes = [
        jax.ShapeDtypeStruct((N_TRUNK, HALF), jnp.uint32),
        jax.ShapeDtypeStruct((K_PAD, N_TRUNK), jnp.int32),
        jax.ShapeDtypeStruct((K_PAD, N_TRUNK), jnp.int32),
        jax.ShapeDtypeStruct((N_TRUNK, 128), F32),
        jax.ShapeDtypeStruct((N_EXPERTS, 128), jnp.int32),
    ]
    return in_specs, out_specs, out_shapes, [pltpu.VMEM((N_EXPERTS, 128), F32)]


def _finalize_body(cnt_ref, eidx_ref, rank_ref, pos_ref):
    pos_ref[...] = rank_ref[...]

    def per_expert(e, off):
        n_tile = lax.div(cnt_ref[e] + (MOE_TILE - 1), MOE_TILE)
        pos_ref[...] = pos_ref[...] + jnp.where(eidx_ref[...] == e, off, 0)
        return off + n_tile * MOE_TILE

    lax.fori_loop(0, N_EXPERTS, per_expert, jnp.int32(0))


def _finalize(counts, eidx, rank):
    full = pl.BlockSpec((K_PAD, N_TRUNK), lambda: (0, 0))
    return pl.pallas_call(
        _finalize_body,
        in_specs=[pl.BlockSpec(memory_space=pltpu.SMEM), full, full],
        out_specs=full,
        out_shape=jax.ShapeDtypeStruct((K_PAD, N_TRUNK), jnp.int32),
        compiler_params=pltpu.CompilerParams(vmem_limit_bytes=VMEM_LIMIT),
        name="route_finalize",
    )(counts, eidx, rank)


def _ffn(h, wgu, wd):
    gu = jnp.dot(h, wgu, preferred_element_type=F32)
    hid = _silu(gu[:, :EXPERT_FF]) * gu[:, EXPERT_FF:]
    return jnp.dot(hid.astype(BF16), wd, preferred_element_type=F32)


def _cast_ffn_weights(wg_ref, wu_ref, wd_ref, wgub_ref, wdb_ref):
    wgub_ref[:, :EXPERT_FF] = wg_ref[...].astype(BF16)
    wgub_ref[:, EXPERT_FF:] = wu_ref[...].astype(BF16)
    wdb_ref[...] = wd_ref[...].astype(BF16)


_FFN_WEIGHT_SCRATCH = [pltpu.VMEM((D_MODEL, 2 * EXPERT_FF), BF16), pltpu.VMEM((EXPERT_FF, D_MODEL), BF16)]


def _sc_mesh():
    from jax.experimental.pallas import tpu_sc as plsc
    return plsc.VectorSubcoreMesh(core_axis_name="c", subcore_axis_name="s",
                                  num_cores=SC_CORES, num_subcores=SC_SUBCORES)


def _sc_worker_id():
    return lax.axis_index("s") * SC_CORES + lax.axis_index("c")


def _sc_dispatch(hp, pos_flat):
    win = SC_SCATTER_WIN
    per_worker = N_TRUNK // SC_WORKERS

    def body(rows_hbm, idx_hbm, out_hbm, *scratch):
        idx_v, rows_v, sem = scratch[:TOP_K], scratch[TOP_K], scratch[TOP_K + 1]
        base = _sc_worker_id() * per_worker

        @pl.loop(0, per_worker // win)
        def _(j):
            off = base + j * win
            pltpu.sync_copy(rows_hbm.at[pl.ds(off, win)], rows_v)
            for k in range(TOP_K):
                pltpu.sync_copy(idx_hbm.at[pl.ds(k * N_TRUNK + off, win)], idx_v[k])
            copies = [pltpu.async_copy(rows_v, out_hbm.at[idx_v[k]], sem) for k in range(TOP_K)]
            for cp in copies:
                cp.wait()

    return pl.kernel(
        body, mesh=_sc_mesh(),
        out_type=jax.ShapeDtypeStruct((N_SLOT, HALF), jnp.uint32),
        scratch_types=[pltpu.VMEM((win,), jnp.int32)] * TOP_K
        + [pltpu.VMEM((win, HALF), jnp.uint32), pltpu.SemaphoreType.DMA],
        name="sc_dispatch",
    )(hp, pos_flat)


RETURN_PARTS = (4, 1)


def _sc_return(ys, pos_flat, part, n_parts):
    win = SC_GATHER_WIN
    N_PART = N_TRUNK // n_parts
    wins_per_choice = N_PART // win
    log2_wins = wins_per_choice.bit_length() - 1
    assert wins_per_choice == 1 << log2_wins
    wins_per_worker = TOP_K * wins_per_choice // SC_WORKERS
    assert wins_per_worker * SC_WORKERS == TOP_K * wins_per_choice

    def body(src_hbm, idx_hbm, out_hbm, idx_v, rows_v, sem):
        first = _sc_worker_id() * wins_per_worker

        @pl.loop(0, wins_per_worker)
        def _(j):
            g = first + j
            k = lax.shift_right_logical(g, log2_wins)
            src_off = k * N_TRUNK + part * N_PART + (g - k * wins_per_choice) * win
            pltpu.sync_copy(idx_hbm.at[pl.ds(src_off, win)], idx_v)
            pltpu.async_copy(src_hbm.at[idx_v], rows_v, sem).wait()
            pltpu.sync_copy(rows_v, out_hbm.at[pl.ds(g * win, win)])

    return pl.kernel(
        body, mesh=_sc_mesh(),
        out_type=jax.ShapeDtypeStruct((TOP_K * N_PART, HALF), jnp.uint32),
        scratch_types=[pltpu.VMEM((win,), jnp.int32), pltpu.VMEM((win, HALF), jnp.uint32),
                       pltpu.SemaphoreType.DMA],
        name="sc_return",
    )(ys, pos_flat)


N_PLAN = 2 * N_TILE


def _tile_plan_body(cnt0_ref, cnt1_ref, exp_ref, blk0_ref, blk1_ref, live_ref, n_ref):
    def per_expert(e, carry):
        s, b0, b1 = carry
        n0 = lax.div(cnt0_ref[e] + (MOE_TILE - 1), MOE_TILE)
        n1 = lax.div(cnt1_ref[e] + (MOE_TILE - 1), MOE_TILE)

        def fill(j, c):
            exp_ref[s + j] = e
            blk0_ref[s + j] = jnp.maximum(b0 + jnp.minimum(j, n0 - 1), 0)
            blk1_ref[s + j] = jnp.maximum(b1 + jnp.minimum(j, n1 - 1), 0)
            live_ref[s + j] = jnp.where(j < n0, 1, 0) + jnp.where(j < n1, 2, 0)
            return c
        n = jnp.maximum(n0, n1)
        lax.fori_loop(0, n, fill, 0)
        return s + n, b0 + n0, b1 + n1

    zero = jnp.int32(0)
    n_steps, _, _ = lax.fori_loop(0, N_EXPERTS, per_expert, (zero, zero, zero))
    n_ref[0] = n_steps

    def fill_tail(s, c):
        exp_ref[s] = 0
        blk0_ref[s] = 0
        blk1_ref[s] = 0
        live_ref[s] = 0
        return c
    lax.fori_loop(n_steps, N_PLAN, fill_tail, 0)


def _tile_plan(counts0, counts1):
    smem = pl.BlockSpec(memory_space=pltpu.SMEM)
    steps = jax.ShapeDtypeStruct((N_PLAN,), jnp.int32)
    return pl.pallas_call(
        _tile_plan_body,
        in_specs=[smem, smem],
        out_specs=[smem] * 5,
        out_shape=[steps, steps, steps, steps, jax.ShapeDtypeStruct((1,), jnp.int32)],
        name="tile_plan",
    )(counts0, counts1)


def _expert_body(exp_ref, blk0_ref, blk1_ref, live_ref, xs0_ref, xs1_ref, wg_ref, wu_ref, wd_ref,
                 ys0_ref, ys1_ref, wgub_ref, wdb_ref):
    i = pl.program_id(0)

    @pl.when(jnp.logical_or(i == 0, exp_ref[i] != exp_ref[jnp.maximum(i - 1, 0)]))
    def _():
        _cast_ffn_weights(wg_ref, wu_ref, wd_ref, wgub_ref, wdb_ref)

    for bit, xs_ref, ys_ref in ((1, xs0_ref, ys0_ref), (2, xs1_ref, ys1_ref)):
        @pl.when((live_ref[i] & bit) != 0)
        def _():
            lo, hi = _unpack_bf16_pair(xs_ref[...])
            h = jnp.concatenate([lo.astype(BF16), hi.astype(BF16)], axis=1)
            y = _ffn(h, wgub_ref[...], wdb_ref[...])
            ys_ref[...] = _pack_bf16_pair(y[:, :HALF], y[:, HALF:])


def _expert_ffn(plan, xs0, xs1, layer, w_gate, w_up, w_down):
    exp, blk0, blk1, live, n_steps = plan
    tile0 = lambda i, exp, blk0, blk1, live: (blk0[i], 0)
    tile1 = lambda i, exp, blk0, blk1, live: (blk1[i], 0)
    ew = lambda shape: pl.BlockSpec((None, None) + shape,
                                    lambda i, exp, blk0, blk1, live: (layer, exp[i], 0, 0))
    slots = jax.ShapeDtypeStruct((N_SLOT, HALF), jnp.uint32)
    return pl.pallas_call(
        _expert_body,
        grid_spec=pltpu.PrefetchScalarGridSpec(
            num_scalar_prefetch=4,
            grid=(n_steps[0],),
            in_specs=[
                pl.BlockSpec((MOE_TILE, HALF), tile0), pl.BlockSpec((MOE_TILE, HALF), tile1),
                ew((D_MODEL, EXPERT_FF)), ew((D_MODEL, EXPERT_FF)), ew((EXPERT_FF, D_MODEL)),
            ],
            out_specs=[pl.BlockSpec((MOE_TILE, HALF), tile0), pl.BlockSpec((MOE_TILE, HALF), tile1)],
            scratch_shapes=_FFN_WEIGHT_SCRATCH,
        ),
        out_shape=[slots, slots],
        compiler_params=pltpu.CompilerParams(
            dimension_semantics=("arbitrary",), vmem_limit_bytes=VMEM_LIMIT),
        name="expert_ffn",
    )(exp, blk0, blk1, live, xs0, xs1, w_gate, w_up, w_down)


def _combine_body(x_ref, sh_ref, sc_ref, g2_ref, yk_ref, wt_ref, sg_ref, su_ref, sd_ref, lng_ref, lnb_ref,
                  *rest):
    o_ref, sgub_ref, sdb_ref = rest[-3:]

    @pl.when(pl.program_id(0) == 0)
    def _():
        _cast_ffn_weights(sg_ref, su_ref, sd_ref, sgub_ref, sdb_ref)

    x = x_ref[...].astype(F32)
    h = (x * (1.0 + sc_ref[...]) + sh_ref[...]).astype(BF16)
    y = _ffn(h, sgub_ref[...], sdb_ref[...])
    wt = wt_ref[...]
    lo_acc = jnp.zeros((x.shape[0], HALF), F32)
    hi_acc = jnp.zeros((x.shape[0], HALF), F32)
    for k in range(TOP_K):
        lo, hi = _unpack_bf16_pair(yk_ref[k])
        w = wt[:, k:k + 1]
        lo_acc = lo_acc + w * lo
        hi_acc = hi_acc + w * hi
    y = y + jnp.concatenate([lo_acc, hi_acc], axis=1)
    z = DEEPNORM_ALPHA * x + g2_ref[...] * y
    o_ref[...] = _layer_norm(z, lng_ref[...], lnb_ref[...]).astype(o_ref.dtype)


def _combine(trunk, part, x, mod, yk, wt, layer, s_gate, s_up, s_down, ln_g, ln_b, out_dtype, earlier):
    tm = TOK_BLOCK
    N_PART = yk.shape[1]
    part_block = part * (N_PART // tm)
    first_block = trunk * (N_TRUNK // tm) + part_block
    row = lambda i: (part_block + i, 0)
    vec = pl.BlockSpec((1, D_MODEL), lambda i: (0, 0))
    sw = lambda shape: pl.BlockSpec((None,) + shape, lambda i: (layer, 0, 0))
    mod_spec = lambda col: _mod_spec(tm, col, first_block)
    in_specs = [
        pl.BlockSpec((tm, D_MODEL), row),
        mod_spec(3), mod_spec(4), mod_spec(5),
        pl.BlockSpec((TOP_K, tm, HALF), lambda i: (0, i, 0)),
        pl.BlockSpec((tm, 128), row),
        sw((D_MODEL, EXPERT_FF)), sw((D_MODEL, EXPERT_FF)), sw((EXPERT_FF, D_MODEL)),
        vec, vec,
    ]
    args = (x, mod, mod, mod, yk, wt, s_gate, s_up, s_down, ln_g, ln_b)
    aliases = {}
    if earlier is not None:
        in_specs.append(pl.BlockSpec(memory_space=pl.ANY))
        aliases = {len(args): 0}
        args += (earlier,)
    return pl.pallas_call(
        _combine_body,
        grid=(N_PART // tm,),
        in_specs=in_specs,
        out_specs=pl.BlockSpec((tm, D_MODEL), row),
        out_shape=jax.ShapeDtypeStruct((N_TRUNK, D_MODEL), out_dtype),
        input_output_aliases=aliases,
        scratch_shapes=_FFN_WEIGHT_SCRATCH,
        compiler_params=pltpu.CompilerParams(
            dimension_semantics=("arbitrary",), vmem_limit_bytes=VMEM_LIMIT),
        name="moe_combine_ln",
    )(*args)


def _moe_dispatch(routing):
    hp, eidx, rank, wt, counts = routing
    counts = counts[:, 0]
    pos_flat = _finalize(counts, eidx, rank).reshape(K_PAD * N_TRUNK)
    return _sc_dispatch(hp, pos_flat), pos_flat, counts, wt


def _moe_experts(plans, layer, w_gate, w_up, w_down):
    (xs0, pos0, counts0, _), (xs1, pos1, counts1, _) = plans
    ys = _expert_ffn(_tile_plan(counts0, counts1), xs0, xs1, layer, w_gate, w_up, w_down)
    return [[_sc_return(y, pos, part, n).reshape(TOP_K, N_TRUNK // n, HALF) for part in range(n)]
            for y, pos, n in zip(ys, (pos0, pos1), RETURN_PARTS)]


def kernel(x_prompt, x_sample, state_ret, c, c_ctx, ada_w, ada_b, ln_g, ln_b, ret_w_in, ret_w_out, ret_decay, conv_w_in, conv_w, conv_w_out, moe_router, moe_bias, moe_w_gate, moe_w_up, moe_w_down, shared_w_gate, shared_w_up, shared_w_down):
    x = (x_prompt.reshape(N_CTX, D_MODEL), x_sample.reshape(N_LAT, D_MODEL))
    cond = jnp.concatenate(
        [c_ctx[None, :], c, jnp.zeros((N_COND - 1 - DEC_BATCH, D_MODEL), F32)], axis=0)
    mods = _ada_table(cond, ada_w, ada_b).reshape(DEPTH, N_COND, 1, 6 * D_MODEL)
    cos, sin = _rope_tables()
    router_wt = jnp.swapaxes(moe_router, 1, 2)

    states = None
    for i in range(DEPTH):
        j = i // 2
        mod = mods[i]
        lng = ln_g[i].reshape(2, 1, D_MODEL)
        lnb = ln_b[i].reshape(2, 1, D_MODEL)
        bias_col = moe_bias[i].reshape(N_EXPERTS, 1)
        if i % 2 == 0:
            w_in_bf = _to_bf16(ret_w_in, j)
        mixed, plans = [], []
        for trunk in range(2):
            if i % 2 == 0:
                if trunk == 0:
                    a, states = _retention_ctx(x[0], mod, w_in_bf, ret_decay[j], j, states)
                else:
                    a = _retention_lat(x[1], mod, w_in_bf, ret_decay[j], state_ret, j, cos, sin)
                xm, *routing = _ret_out(trunk, a, x[trunk], mod, ret_w_out, j, lng[0], lnb[0],
                                        router_wt[i], bias_col)
            else:
                xm, *routing = _conv_layer(trunk, x[trunk], mod, conv_w_in, conv_w_out, j, conv_w[j],
                                           lng[0], lnb[0], router_wt[i], bias_col)
            mixed.append(xm)
            plans.append(_moe_dispatch(routing))
        yks = _moe_experts(plans, i, moe_w_gate, moe_w_up, moe_w_down)
        out_dtype = F32 if i == DEPTH - 1 else RES_DTYPE
        x = []
        for trunk in range(2):
            out = None
            for part in range(RETURN_PARTS[trunk]):
                out = _combine(trunk, part, mixed[trunk], mod, yks[trunk][part], plans[trunk][3], i,
                               shared_w_gate, shared_w_up, shared_w_down, lng[1], lnb[1], out_dtype, out)
            x.append(out)

    y_prompt = x[0].reshape(BATCH, SEQ, D_MODEL)
    y_sample = x[1].reshape(DEC_BATCH, DEC_SEQ, D_MODEL)
    return y_prompt, y_sample, states
```

```python
import functools

import jax
import jax.numpy as jnp
from jax import lax
from jax.experimental import pallas as pl
from jax.experimental.pallas import tpu as pltpu

F32 = jnp.float32
BF16 = jnp.bfloat16

D_MODEL = 1024
BATCH = 32
SEQ = 256
DEPTH = 4
DEC_BATCH = 8
DEC_SEQ = 1024
GRID_W = 64
RET_HEADS = 4
RET_DK = D_MODEL // RET_HEADS
RET_DV = 2 * D_MODEL // RET_HEADS
ROPE_BASE = 10000.0
N_EXPERTS = 64
TOP_K = 6
N_GROUPS = 8
TOPK_GROUPS = 4
EXPERT_FF = 256
ROUTED_SCALE = 2.5
LN_EPS = 1e-5
DEEPNORM_ALPHA = (2.0 * DEPTH) ** 0.25

N_CTX = BATCH * SEQ
N_LAT = DEC_BATCH * DEC_SEQ
N_COND = 16
RET_CHUNK = 256
assert N_CTX == N_LAT
N_TRUNK = N_CTX
TOK_BLOCK = 512
MIX_BLOCK = 512
VMEM_LIMIT = 56 * 1024 * 1024
RES_DTYPE = BF16

HALF = D_MODEL // 2
MOE_TILE = 896
N_PAIR = N_TRUNK * TOP_K
N_TILE = -(-(N_PAIR + N_EXPERTS * (MOE_TILE - 1)) // MOE_TILE)
N_SLOT = N_TILE * MOE_TILE
K_PAD = 8

SC_CORES = 2
SC_SUBCORES = 16
SC_WORKERS = SC_CORES * SC_SUBCORES
SC_SCATTER_WIN = 64
SC_GATHER_WIN = 128


def _cond_row(tok_block_idx, tok_block):
    t0 = tok_block_idx * tok_block
    return jnp.where(t0 < N_CTX, 0, 1 + (t0 - N_CTX) // DEC_SEQ)


def _silu(x):
    return x * jax.nn.sigmoid(x)


def _layer_norm(z, g, b):
    mu = jnp.mean(z, axis=-1, keepdims=True)
    zc = z - mu
    var = jnp.mean(zc * zc, axis=-1, keepdims=True)
    return zc * lax.rsqrt(var + LN_EPS) * g + b


def _ada_body(cond_ref, w_ref, b_ref, o_ref):
    s = _silu(cond_ref[...]).astype(BF16)
    o_ref[...] = jnp.dot(s, w_ref[...].astype(BF16), preferred_element_type=F32) + b_ref[...]


def _ada_table(cond, ada_w, ada_b, layer):
    tn = 2048
    return pl.pallas_call(
        _ada_body,
        grid=(6 * D_MODEL // tn,),
        in_specs=[
            pl.BlockSpec((N_COND, D_MODEL), lambda j: (0, 0)),
            pl.BlockSpec((None, D_MODEL, tn), lambda j: (layer, 0, j)),
            pl.BlockSpec((None, 1, tn), lambda j: (layer, 0, j)),
        ],
        out_specs=pl.BlockSpec((N_COND, tn), lambda j: (0, j)),
        out_shape=jax.ShapeDtypeStruct((N_COND, 6 * D_MODEL), F32),
        compiler_params=pltpu.CompilerParams(
            dimension_semantics=("arbitrary",), vmem_limit_bytes=VMEM_LIMIT),
        name="ada_table",
    )(cond, ada_w, ada_b.reshape(DEPTH, 1, 6 * D_MODEL))


def _mod_spec(tok_block, col, first_block=0):
    def index_map(*idx):
        return (_cond_row(first_block + idx[0], tok_block), 0, col)
    return pl.BlockSpec((None, 1, D_MODEL), index_map)


def _to_bf16_body(w_ref, o_ref):
    o_ref[...] = w_ref[...].astype(BF16)


def _to_bf16(w_all, w_idx):
    k_dim, n_out = w_all.shape[1:]
    tn = 1024
    return pl.pallas_call(
        _to_bf16_body,
        grid=(n_out // tn,),
        in_specs=[pl.BlockSpec((None, k_dim, tn), lambda j: (w_idx, 0, j))],
        out_specs=pl.BlockSpec((k_dim, tn), lambda j: (0, j)),
        out_shape=jax.ShapeDtypeStruct((k_dim, n_out), BF16),
        compiler_params=pltpu.CompilerParams(
            dimension_semantics=("arbitrary",), vmem_limit_bytes=VMEM_LIMIT),
        name="weight_to_bf16",
    )(w_all)


RET_GROUP = 1024


def _head_proj_specs():
    return [
        pl.BlockSpec((D_MODEL, RET_DK), lambda t, h: (0, h)),
        pl.BlockSpec((D_MODEL, RET_DK), lambda t, h: (0, RET_HEADS + h)),
        pl.BlockSpec((D_MODEL, RET_DV), lambda t, h: (0, RET_HEADS + h)),
        pl.BlockSpec((D_MODEL, RET_DV), lambda t, h: (0, 2 * RET_HEADS + h)),
    ]


def _head_proj(x_ref, sh_ref, sc_ref, wq_ref, wk_ref, wv_ref, wg_ref):
    h = (x_ref[...].astype(F32) * (1.0 + sc_ref[...]) + sh_ref[...]).astype(BF16)
    dot = lambda w_ref: jnp.dot(h, w_ref[...], preferred_element_type=F32)
    return dot(wq_ref), dot(wk_ref), dot(wv_ref).astype(BF16), dot(wg_ref)

def _log_sigmoid(v):
    return jnp.minimum(v, 0.0) - jnp.log1p(jnp.exp(-jnp.abs(v)))


def _decay_tables(dec_ref, head):
    c = RET_CHUNK
    lgf = _log_sigmoid(jnp.full((c, c), dec_ref[0, head], F32))
    lgb = _log_sigmoid(jnp.full((c, c), dec_ref[1, head], F32))
    row = lax.broadcasted_iota(jnp.int32, (c, c), 0).astype(F32)
    col = lax.broadcasted_iota(jnp.int32, (c, c), 1).astype(F32)
    diff = row - col
    kscale = RET_DK ** -0.5
    intra = jnp.where(diff > 0, jnp.exp(lgf * diff),
                      jnp.where(diff < 0, jnp.exp(-lgb * diff), 2.0)) * kscale
    qdec_f = jnp.exp(lgf * (row + 1.0))
    qdec_b = jnp.exp(lgb * (c - row))
    kdec_f = jnp.exp(lgf * (c - 1.0 - row)) * kscale
    kdec_b = jnp.exp(lgb * row) * kscale
    cdec_f = jnp.exp(lgf * c)
    cdec_b = jnp.exp(lgb * c)
    return intra, qdec_f, qdec_b, kdec_f, kdec_b, cdec_f, cdec_b


def _head_norm_gate(o, g):
    mu = jnp.mean(o, axis=-1, keepdims=True)
    oc = o - mu
    var = jnp.mean(oc * oc, axis=-1, keepdims=True)
    on = oc * lax.rsqrt(var + LN_EPS)
    return (_silu(g.astype(F32)) * on).astype(BF16)


_NT = (((1,), (1,)), ((), ()))
_TN = (((0,), (0,)), ((), ()))


def _ret_ctx_body(dec_ref, x_ref, sh_ref, sc_ref, wq_ref, wk_ref, wv_ref, wg_ref, *rest):
    a_ref, st_ref, tab_ref = rest[-3:]
    head = pl.program_id(1)

    @pl.when(pl.program_id(0) == 0)
    def _():
        intra, _, _, kdec_f, kdec_b, _, _ = _decay_tables(dec_ref, head)
        tab_ref[head, 0] = intra
        tab_ref[head, 1] = kdec_f
        tab_ref[head, 2] = kdec_b

    q, k, v, g = _head_proj(x_ref, sh_ref, sc_ref, wq_ref, wk_ref, wv_ref, wg_ref)
    for s in range(RET_GROUP // SEQ):
        rows = slice(s * SEQ, (s + 1) * SEQ)
        scores = lax.dot_general(q[rows].astype(BF16), k[rows].astype(BF16), _NT, preferred_element_type=F32)
        p = (scores * tab_ref[head, 0]).astype(BF16)
        o = jnp.dot(p, v[rows], preferred_element_type=F32)
        a_ref[rows, :] = _head_norm_gate(o, g[rows])
        st_ref[s, 0] = lax.dot_general((k[rows] * tab_ref[head, 1]).astype(BF16), v[rows], _TN,
                                       preferred_element_type=F32)
        st_ref[s, 1] = lax.dot_general((k[rows] * tab_ref[head, 2]).astype(BF16), v[rows], _TN,
                                       preferred_element_type=F32)


def _retention_ctx(x, mod, w_in_bf, decay, ret_idx, states):
    assert SEQ == RET_CHUNK
    seqs = RET_GROUP // SEQ
    n_ret = (DEPTH + 1) // 2
    st = jax.ShapeDtypeStruct((BATCH, n_ret, 2, RET_HEADS, RET_DK, RET_DV), F32)
    st_spec = pl.BlockSpec((seqs, None, 2, None, RET_DK, RET_DV), lambda t, h: (t, ret_idx, 0, h, 0, 0))
    in_specs = [
        pl.BlockSpec(memory_space=pltpu.SMEM),
        pl.BlockSpec((RET_GROUP, D_MODEL), lambda t, h: (t, 0)),
        _mod_spec(RET_GROUP, 0), _mod_spec(RET_GROUP, 1),
    ] + _head_proj_specs()
    args = (decay, x, mod, mod, w_in_bf, w_in_bf, w_in_bf, w_in_bf)
    aliases = {}
    if states is not None:
        in_specs.append(pl.BlockSpec(memory_space=pl.ANY))
        aliases = {len(args): 1}
        args += (states,)
    return pl.pallas_call(
        _ret_ctx_body,
        grid=(N_CTX // RET_GROUP, RET_HEADS),
        in_specs=in_specs,
        out_specs=[pl.BlockSpec((RET_GROUP, RET_DV), lambda t, h: (t, h)), st_spec],
        out_shape=[jax.ShapeDtypeStruct((N_CTX, RET_HEADS * RET_DV), BF16), st],
        input_output_aliases=aliases,
        scratch_shapes=[pltpu.VMEM((RET_HEADS, 3, RET_CHUNK, RET_CHUNK), F32)],
        compiler_params=pltpu.CompilerParams(
            dimension_semantics=("arbitrary", "arbitrary"), vmem_limit_bytes=VMEM_LIMIT),
        name="retention_ctx",
    )(*args)


def _rope(x, cos, sin):
    halves = [pltpu.roll(x[:, s:s + 128], 64, axis=1) for s in (0, 128)]
    return x * cos + jnp.concatenate(halves, axis=1) * sin


def _ret_lat_body(dec_ref, x_ref, sh_ref, sc_ref, wq_ref, wk_ref, wv_ref, wg_ref,
                  s0f_ref, s0b_ref, cos_ref, sin_ref,
                  a_ref, qr_ref, kr_ref, v_ref, g_ref, o_ref, st_ref):
    head = pl.program_id(1)
    c = RET_CHUNK
    nc = DEC_SEQ // c
    intra, qdec_f, qdec_b, kdec_f, kdec_b, cdec_f, cdec_b = _decay_tables(dec_ref, head)
    cdec_f = jnp.concatenate([cdec_f, cdec_f], axis=1)
    cdec_b = jnp.concatenate([cdec_b, cdec_b], axis=1)

    q, k, v, g = _head_proj(x_ref, sh_ref, sc_ref, wq_ref, wk_ref, wv_ref, wg_ref)
    qr_ref[...] = _rope(q, cos_ref[...], sin_ref[...])
    kr_ref[...] = _rope(k, cos_ref[...], sin_ref[...])
    v_ref[...] = v
    g_ref[...] = g

    st_ref[...] = s0f_ref[...]
    for ci in range(nc):
        rows = pl.ds(ci * c, c)
        q = qr_ref[rows, :]
        k = kr_ref[rows, :]
        v = v_ref[rows, :]
        scores = lax.dot_general(q.astype(BF16), k.astype(BF16), _NT, preferred_element_type=F32)
        p = (scores * intra).astype(BF16)
        o = jnp.dot(p, v, preferred_element_type=F32)
        qd = (q * qdec_f).astype(BF16)
        o = o + jnp.dot(qd, st_ref[...].astype(BF16), preferred_element_type=F32)
        o_ref[rows, :] = o
        if ci + 1 < nc:
            kd = (k * kdec_f).astype(BF16)
            st_ref[...] = cdec_f * st_ref[...] + lax.dot_general(kd, v, _TN, preferred_element_type=F32)

    st_ref[...] = s0b_ref[...]
    for ci in reversed(range(nc)):
        rows = pl.ds(ci * c, c)
        qd = (qr_ref[rows, :] * qdec_b).astype(BF16)
        o = o_ref[rows, :] + jnp.dot(qd, st_ref[...].astype(BF16), preferred_element_type=F32)
        a_ref[rows, :] = _head_norm_gate(o, g_ref[rows, :])
        if ci > 0:
            kd = (kr_ref[rows, :] * kdec_b).astype(BF16)
            st_ref[...] = cdec_b * st_ref[...] + lax.dot_general(kd, v_ref[rows, :], _TN,
                                                                 preferred_element_type=F32)


def _rope_tables():
    half = RET_DK // 4
    freqs = ROPE_BASE ** (-jnp.arange(half, dtype=F32) / half)
    t = jnp.arange(DEC_SEQ)
    row = (t // GRID_W).astype(F32)
    col = (t % GRID_W).astype(F32)
    ang_r = row[:, None] * freqs[None, :]
    ang_c = col[:, None] * freqs[None, :]
    cos = jnp.concatenate([jnp.cos(ang_r)] * 2 + [jnp.cos(ang_c)] * 2, axis=1)
    sin = jnp.concatenate([-jnp.sin(ang_r), jnp.sin(ang_r), -jnp.sin(ang_c), jnp.sin(ang_c)], axis=1)
    return cos, sin


def _retention_lat(x, mod, w_in_bf, decay, state_ret, ret_idx, cos, sin):
    n = DEC_SEQ
    assert n == RET_GROUP
    s0_spec = lambda d: pl.BlockSpec((None, None, None, None, RET_DK, RET_DV),
                                     lambda b, h: (b, ret_idx, d, h, 0, 0))
    tab_spec = pl.BlockSpec((n, RET_DK), lambda b, h: (0, 0))
    first_block = N_CTX // n
    return pl.pallas_call(
        _ret_lat_body,
        grid=(DEC_BATCH, RET_HEADS),
        in_specs=[
            pl.BlockSpec(memory_space=pltpu.SMEM),
            pl.BlockSpec((n, D_MODEL), lambda b, h: (b, 0)),
            _mod_spec(n, 0, first_block), _mod_spec(n, 1, first_block),
        ] + _head_proj_specs() + [s0_spec(0), s0_spec(1), tab_spec, tab_spec],
        out_specs=pl.BlockSpec((n, RET_DV), lambda b, h: (b, h)),
        out_shape=jax.ShapeDtypeStruct((N_LAT, RET_HEADS * RET_DV), BF16),
        scratch_shapes=[
            pltpu.VMEM((n, RET_DK), F32), pltpu.VMEM((n, RET_DK), F32),
            pltpu.VMEM((n, RET_DV), BF16), pltpu.VMEM((n, RET_DV), F32),
            pltpu.VMEM((n, RET_DV), F32), pltpu.VMEM((RET_DK, RET_DV), F32),
        ],
        compiler_params=pltpu.CompilerParams(
            dimension_semantics=("arbitrary", "arbitrary"), vmem_limit_bytes=VMEM_LIMIT),
        name="retention_lat",
    )(decay, x, mod, mod, w_in_bf, w_in_bf, w_in_bf, w_in_bf, state_ret, state_ret, cos, sin)


N_ROUTE_IN = 4
N_ROUTE_OUT = 5


def _split_route_refs(rest, n_scratch):
    route_in = rest[:N_ROUTE_IN]
    o_ref = rest[N_ROUTE_IN]
    route_out = rest[N_ROUTE_IN + 1:N_ROUTE_IN + 1 + N_ROUTE_OUT]
    return route_in, o_ref, route_out, rest[len(rest) - n_scratch:]


def _ret_out_body(a_ref, x_ref, g1_ref, w_ref, lng_ref, lnb_ref, *rest):
    route_in, o_ref, route_out, (wbf_ref, carry_ref) = _split_route_refs(rest, 2)

    @pl.when(pl.program_id(0) == 0)
    def _():
        wbf_ref[...] = w_ref[...].astype(BF16)
        carry_ref[...] = jnp.zeros_like(carry_ref)

    y = jnp.dot(a_ref[...], wbf_ref[...], preferred_element_type=F32)
    z = DEEPNORM_ALPHA * x_ref[...].astype(F32) + g1_ref[...] * y
    xn = _layer_norm(z, lng_ref[...], lnb_ref[...])
    o_ref[...] = xn.astype(o_ref.dtype)
    _route_block(xn, *route_in, *route_out, carry_ref)


def _ret_out(trunk, a, x, mod, w_all, w_idx, ln_g, ln_b, router_wt, bias_col):
    tm = MIX_BLOCK
    k_dim = w_all.shape[1]
    first_block = trunk * (N_TRUNK // tm)
    row = lambda i: (i, 0)
    vec = pl.BlockSpec((1, D_MODEL), lambda i: (0, 0))
    r_in, r_out, r_shapes, r_scratch = _route_io(tm, first_block)
    return pl.pallas_call(
        _ret_out_body,
        grid=(N_TRUNK // tm,),
        in_specs=[
            pl.BlockSpec((tm, k_dim), row),
            pl.BlockSpec((tm, D_MODEL), row),
            _mod_spec(tm, 2, first_block),
            pl.BlockSpec((None, k_dim, D_MODEL), lambda i: (w_idx, 0, 0)),
            vec, vec,
        ] + r_in,
        out_specs=[pl.BlockSpec((tm, D_MODEL), row)] + r_out,
        out_shape=[jax.ShapeDtypeStruct((N_TRUNK, D_MODEL), RES_DTYPE)] + r_shapes,
        scratch_shapes=[pltpu.VMEM((k_dim, D_MODEL), BF16)] + r_scratch,
        compiler_params=pltpu.CompilerParams(
            dimension_semantics=("arbitrary",), vmem_limit_bytes=VMEM_LIMIT),
        name="ret_out_ln_route",
    )(a, x, mod, w_all, ln_g, ln_b, mod, mod, router_wt, bias_col)


def _conv_layer_body(seg, x_ref, sh_ref, sc_ref, g1_ref, win_ref, wout_ref, cw_ref, lng_ref, lnb_ref, *rest):
    route_in, o_ref, route_out, (winb_ref, woutb_ref, carry_ref) = _split_route_refs(rest, 3)
    tm = o_ref.shape[0]

    @pl.when(pl.program_id(0) == 0)
    def _():
        winb_ref[...] = win_ref[...].astype(BF16)
        woutb_ref[...] = wout_ref[...].astype(BF16)
        carry_ref[...] = jnp.zeros_like(carry_ref)

    x = x_ref[...].astype(F32)
    h = (x * (1.0 + sc_ref[...]) + sh_ref[...]).astype(BF16)
    proj = jnp.dot(h, winb_ref[...], preferred_element_type=F32)
    bg, cg, xt = (proj[:, c * D_MODEL:(c + 1) * D_MODEL] for c in range(3))
    u = cg * xt
    pos = lax.broadcasted_iota(jnp.int32, (tm, D_MODEL), 0) & (seg - 1)
    u_prev = jnp.where(pos == 0, 0.0, pltpu.roll(u, 1, axis=0))
    u_next = jnp.where(pos == seg - 1, 0.0, pltpu.roll(u, tm - 1, axis=0))
    cu = u_prev * cw_ref[0:1, :] + u * cw_ref[1:2, :] + u_next * cw_ref[2:3, :]
    y = jnp.dot((bg * cu).astype(BF16), woutb_ref[...], preferred_element_type=F32)
    z = DEEPNORM_ALPHA * x + g1_ref[...] * y
    xn = _layer_norm(z, lng_ref[...], lnb_ref[...])
    o_ref[...] = xn.astype(o_ref.dtype)
    _route_block(xn, *route_in, *route_out, carry_ref)


def _conv_layer(trunk, x, mod, w_in_all, w_out_all, w_idx, conv_w, ln_g, ln_b, router_wt, bias_col):
    tm = MIX_BLOCK
    first_block = trunk * (N_TRUNK // tm)
    seg = SEQ if trunk == 0 else GRID_W
    assert tm % seg == 0 and seg & (seg - 1) == 0
    row = lambda i: (i, 0)
    vec = pl.BlockSpec((1, D_MODEL), lambda i: (0, 0))
    once = pl.Buffered(1)
    r_in, r_out, r_shapes, r_scratch = _route_io(tm, first_block)
    mod_spec = lambda col: _mod_spec(tm, col, first_block)
    return pl.pallas_call(
        functools.partial(_conv_layer_body, seg),
        grid=(N_TRUNK // tm,),
        in_specs=[
            pl.BlockSpec((tm, D_MODEL), row),
            mod_spec(0), mod_spec(1), mod_spec(2),
            pl.BlockSpec((None, D_MODEL, 3 * D_MODEL), lambda i: (w_idx, 0, 0), pipeline_mode=once),
            pl.BlockSpec((None, D_MODEL, D_MODEL), lambda i: (w_idx, 0, 0), pipeline_mode=once),
            pl.BlockSpec((3, D_MODEL), lambda i: (0, 0)),
            vec, vec,
        ] + r_in,
        out_specs=[pl.BlockSpec((tm, D_MODEL), row)] + r_out,
        out_shape=[jax.ShapeDtypeStruct((N_TRUNK, D_MODEL), RES_DTYPE)] + r_shapes,
        scratch_shapes=[pltpu.VMEM((D_MODEL, 3 * D_MODEL), BF16), pltpu.VMEM((D_MODEL, D_MODEL), BF16)]
        + r_scratch,
        compiler_params=pltpu.CompilerParams(
            dimension_semantics=("arbitrary",), vmem_limit_bytes=VMEM_LIMIT),
        name="conv_layer_route",
    )(x, mod, mod, mod, w_in_all, w_out_all, conv_w, ln_g, ln_b, mod, mod, router_wt, bias_col)


def _split_bf16(v):
    hi = v.astype(BF16)
    lo = (v - hi.astype(F32)).astype(BF16)
    return hi, lo


def _pack_bf16_pair(lo_f32, hi_f32):
    lo = lax.bitcast_convert_type(lo_f32.astype(BF16).astype(F32), jnp.uint32) >> 16
    hi = lax.bitcast_convert_type(hi_f32.astype(BF16).astype(F32), jnp.uint32) & jnp.uint32(0xFFFF0000)
    return hi | lo


def _unpack_bf16_pair(u):
    lo = lax.bitcast_convert_type(u << 16, F32)
    hi = lax.bitcast_convert_type(u & jnp.uint32(0xFFFF0000), F32)
    return lo, hi


def _rows_to_tile(rows, n_sub, dtype):
    tm = rows[0].shape[1]
    sub = lax.broadcasted_iota(jnp.int32, (n_sub, tm), 0)
    out = jnp.zeros((n_sub, tm), dtype)
    for k, r in enumerate(rows):
        out = jnp.where(sub == k, jnp.broadcast_to(r.astype(dtype), (n_sub, tm)), out)
    return out


def _route_block(xn, sh_ref, sc_ref, rwt_ref, bias_ref,
                 hp_ref, eidx_ref, rank_ref, wt_ref, cnt_ref, carry_ref):
    tm = xn.shape[0]
    e = N_EXPERTS
    per = e // N_GROUPS
    neg = -jnp.inf
    h = xn * (1.0 + sc_ref[...]) + sh_ref[...]
    hp_ref[...] = _pack_bf16_pair(h[:, :HALF], h[:, HALF:])
    h_hi, h_lo = _split_bf16(h)
    w_hi, w_lo = _split_bf16(rwt_ref[...])
    dot = lambda a, b: lax.dot_general(a, b, _NT, preferred_element_type=F32)
    logits = dot(w_hi, h_hi) + (dot(w_hi, h_lo) + dot(w_lo, h_hi))
    s = jax.nn.sigmoid(logits)
    sel = s + bias_ref[...]

    g3 = sel.reshape(N_GROUPS, per, tm)
    sub = lax.broadcasted_iota(jnp.int32, (N_GROUPS, per, tm), 1)
    m1 = jnp.max(g3, axis=1, keepdims=True)
    i1 = jnp.min(jnp.where(g3 == m1, sub, per), axis=1, keepdims=True)
    m2 = jnp.max(jnp.where(sub == i1, neg, g3), axis=1, keepdims=True)
    gs = (m1 + m2).reshape(N_GROUPS, tm)

    gi = lax.broadcasted_iota(jnp.int32, (N_GROUPS, tm), 0)
    gmask = jnp.zeros((N_GROUPS, tm), jnp.bool_)
    cur = gs
    for _ in range(TOPK_GROUPS):
        m = jnp.max(cur, axis=0, keepdims=True)
        idx = jnp.min(jnp.where(cur == m, gi, N_GROUPS), axis=0, keepdims=True)
        pick = gi == idx
        gmask = jnp.logical_or(gmask, pick)
        cur = jnp.where(pick, neg, cur)
    emask = jnp.broadcast_to(gmask.reshape(N_GROUPS, 1, tm), (N_GROUPS, per, tm)).reshape(e, tm)

    ei = lax.broadcasted_iota(jnp.int32, (e, tm), 0)
    picks, ids = [], []
    cur = jnp.where(emask, sel, neg)
    for _ in range(TOP_K):
        m = jnp.max(cur, axis=0, keepdims=True)
        idx = jnp.min(jnp.where(cur == m, ei, e), axis=0, keepdims=True)
        pick = ei == idx
        picks.append(pick)
        ids.append(idx)
        cur = jnp.where(pick, neg, cur)

    chosen = functools.reduce(jnp.logical_or, picks)
    cf = jnp.where(chosen, 1.0, 0.0)
    before = (lax.broadcasted_iota(jnp.int32, (tm, tm), 0)
              < lax.broadcasted_iota(jnp.int32, (tm, tm), 1)).astype(BF16)
    rank = carry_ref[:, 0:1] + jnp.dot(cf.astype(BF16), before, preferred_element_type=F32)
    carry_ref[...] = carry_ref[...] + jnp.sum(cf, axis=1, keepdims=True)
    cnt_ref[...] = carry_ref[...].astype(jnp.int32)

    w_rows = [jnp.sum(jnp.where(p, s, 0.0), axis=0, keepdims=True) for p in picks]
    r_rows = [jnp.sum(jnp.where(p, rank, 0.0), axis=0, keepdims=True) for p in picks]
    den = functools.reduce(lambda a, b: a + b, w_rows)
    w_rows = [w / den * ROUTED_SCALE for w in w_rows]
    eidx_ref[...] = _rows_to_tile(ids, K_PAD, jnp.int32)
    rank_ref[...] = _rows_to_tile(r_rows, K_PAD, F32).astype(jnp.int32)
    wt_ref[...] = _rows_to_tile(w_rows, 128, F32).T


def _route_io(tm, first_block):
    tok_major = lambda i: (i, 0)
    choice_major = lambda i: (0, i)
    in_specs = [
        _mod_spec(tm, 3, first_block),
        _mod_spec(tm, 4, first_block),
        pl.BlockSpec((N_EXPERTS, D_MODEL), lambda i: (0, 0)),
        pl.BlockSpec((N_EXPERTS, 1), lambda i: (0, 0)),
    ]
    out_specs = [
        pl.BlockSpec((tm, HALF), tok_major),
        pl.BlockSpec((K_PAD, tm), choice_major),
        pl.BlockSpec((K_PAD, tm), choice_major),
        pl.BlockSpec((tm, 128), tok_major),
        pl.BlockSpec((N_EXPERTS, 128), lambda i: (0, 0)),
    ]
    out_shapes = [
        jax.ShapeDtypeStruct((N_TRUNK, HALF), jnp.uint32),
        jax.ShapeDtypeStruct((K_PAD, N_TRUNK), jnp.int32),
        jax.ShapeDtypeStruct((K_PAD, N_TRUNK), jnp.int32),
        jax.ShapeDtypeStruct((N_TRUNK, 128), F32),
        jax.ShapeDtypeStruct((N_EXPERTS, 128), jnp.int32),
    ]
    return in_specs, out_specs, out_shapes, [pltpu.VMEM((N_EXPERTS, 128), F32)]


def _finalize_body(cnt_ref, eidx_ref, rank_ref, pos_ref):
    pos_ref[...] = rank_ref[...]

    def per_expert(e, off):
        n_tile = lax.div(cnt_ref[e] + (MOE_TILE - 1), MOE_TILE)
        pos_ref[...] = pos_ref[...] + jnp.where(eidx_ref[...] == e, off, 0)
        return off + n_tile * MOE_TILE

    lax.fori_loop(0, N_EXPERTS, per_expert, jnp.int32(0))


def _finalize(counts, eidx, rank):
    full = pl.BlockSpec((K_PAD, N_TRUNK), lambda: (0, 0))
    return pl.pallas_call(
        _finalize_body,
        in_specs=[pl.BlockSpec(memory_space=pltpu.SMEM), full, full],
        out_specs=full,
        out_shape=jax.ShapeDtypeStruct((K_PAD, N_TRUNK), jnp.int32),
        compiler_params=pltpu.CompilerParams(vmem_limit_bytes=VMEM_LIMIT),
        name="route_finalize",
    )(counts, eidx, rank)


def _ffn(h, wgu, wd):
    gu = jnp.dot(h, wgu, preferred_element_type=F32)
    hid = _silu(gu[:, :EXPERT_FF]) * gu[:, EXPERT_FF:]
    return jnp.dot(hid.astype(BF16), wd, preferred_element_type=F32)


def _cast_ffn_weights(wg_ref, wu_ref, wd_ref, wgub_ref, wdb_ref):
    wgub_ref[:, :EXPERT_FF] = wg_ref[...].astype(BF16)
    wgub_ref[:, EXPERT_FF:] = wu_ref[...].astype(BF16)
    wdb_ref[...] = wd_ref[...].astype(BF16)


_FFN_WEIGHT_SCRATCH = [pltpu.VMEM((D_MODEL, 2 * EXPERT_FF), BF16), pltpu.VMEM((EXPERT_FF, D_MODEL), BF16)]


def _sc_mesh():
    from jax.experimental.pallas import tpu_sc as plsc
    return plsc.VectorSubcoreMesh(core_axis_name="c", subcore_axis_name="s",
                                  num_cores=SC_CORES, num_subcores=SC_SUBCORES)


def _sc_worker_id():
    return lax.axis_index("s") * SC_CORES + lax.axis_index("c")


def _sc_dispatch(hp, pos_flat):
    win = SC_SCATTER_WIN
    per_worker = N_TRUNK // SC_WORKERS

    def body(rows_hbm, idx_hbm, out_hbm, *scratch):
        idx_v, rows_v, sem = scratch[:TOP_K], scratch[TOP_K], scratch[TOP_K + 1]
        base = _sc_worker_id() * per_worker

        @pl.loop(0, per_worker // win)
        def _(j):
            off = base + j * win
            pltpu.sync_copy(rows_hbm.at[pl.ds(off, win)], rows_v)
            for k in range(TOP_K):
                pltpu.sync_copy(idx_hbm.at[pl.ds(k * N_TRUNK + off, win)], idx_v[k])
            copies = [pltpu.async_copy(rows_v, out_hbm.at[idx_v[k]], sem) for k in range(TOP_K)]
            for cp in copies:
                cp.wait()

    return pl.kernel(
        body, mesh=_sc_mesh(),
        out_type=jax.ShapeDtypeStruct((N_SLOT, HALF), jnp.uint32),
        scratch_types=[pltpu.VMEM((win,), jnp.int32)] * TOP_K
        + [pltpu.VMEM((win, HALF), jnp.uint32), pltpu.SemaphoreType.DMA],
        name="sc_dispatch",
    )(hp, pos_flat)


RETURN_PARTS = (4, 1)


def _sc_return(ys, pos_flat, part, n_parts):
    win = SC_GATHER_WIN
    N_PART = N_TRUNK // n_parts
    wins_per_choice = N_PART // win
    log2_wins = wins_per_choice.bit_length() - 1
    assert wins_per_choice == 1 << log2_wins
    wins_per_worker = TOP_K * wins_per_choice // SC_WORKERS
    assert wins_per_worker * SC_WORKERS == TOP_K * wins_per_choice

    def body(src_hbm, idx_hbm, out_hbm, idx_v, rows_v, sem):
        first = _sc_worker_id() * wins_per_worker

        @pl.loop(0, wins_per_worker)
        def _(j):
            g = first + j
            k = lax.shift_right_logical(g, log2_wins)
            src_off = k * N_TRUNK + part * N_PART + (g - k * wins_per_choice) * win
            pltpu.sync_copy(idx_hbm.at[pl.ds(src_off, win)], idx_v)
            pltpu.async_copy(src_hbm.at[idx_v], rows_v, sem).wait()
            pltpu.sync_copy(rows_v, out_hbm.at[pl.ds(g * win, win)])

    return pl.kernel(
        body, mesh=_sc_mesh(),
        out_type=jax.ShapeDtypeStruct((TOP_K * N_PART, HALF), jnp.uint32),
        scratch_types=[pltpu.VMEM((win,), jnp.int32), pltpu.VMEM((win, HALF), jnp.uint32),
                       pltpu.SemaphoreType.DMA],
        name="sc_return",
    )(ys, pos_flat)


N_PLAN = 2 * N_TILE


def _tile_plan_body(cnt0_ref, cnt1_ref, exp_ref, blk0_ref, blk1_ref, live_ref, n_ref):
    def per_expert(e, carry):
        s, b0, b1 = carry
        n0 = lax.div(cnt0_ref[e] + (MOE_TILE - 1), MOE_TILE)
        n1 = lax.div(cnt1_ref[e] + (MOE_TILE - 1), MOE_TILE)

        def fill(j, c):
            exp_ref[s + j] = e
            blk0_ref[s + j] = jnp.maximum(b0 + jnp.minimum(j, n0 - 1), 0)
            blk1_ref[s + j] = jnp.maximum(b1 + jnp.minimum(j, n1 - 1), 0)
            live_ref[s + j] = jnp.where(j < n0, 1, 0) + jnp.where(j < n1, 2, 0)
            return c
        n = jnp.maximum(n0, n1)
        lax.fori_loop(0, n, fill, 0)
        return s + n, b0 + n0, b1 + n1

    zero = jnp.int32(0)
    n_steps, _, _ = lax.fori_loop(0, N_EXPERTS, per_expert, (zero, zero, zero))
    n_ref[0] = n_steps

    def fill_tail(s, c):
        exp_ref[s] = 0
        blk0_ref[s] = 0
        blk1_ref[s] = 0
        live_ref[s] = 0
        return c
    lax.fori_loop(n_steps, N_PLAN, fill_tail, 0)


def _tile_plan(counts0, counts1):
    smem = pl.BlockSpec(memory_space=pltpu.SMEM)
    steps = jax.ShapeDtypeStruct((N_PLAN,), jnp.int32)
    return pl.pallas_call(
        _tile_plan_body,
        in_specs=[smem, smem],
        out_specs=[smem] * 5,
        out_shape=[steps, steps, steps, steps, jax.ShapeDtypeStruct((1,), jnp.int32)],
        name="tile_plan",
    )(counts0, counts1)


def _expert_body(exp_ref, blk0_ref, blk1_ref, live_ref, xs0_ref, xs1_ref, wg_ref, wu_ref, wd_ref,
                 ys0_ref, ys1_ref, wgub_ref, wdb_ref):
    i = pl.program_id(0)

    @pl.when(jnp.logical_or(i == 0, exp_ref[i] != exp_ref[jnp.maximum(i - 1, 0)]))
    def _():
        _cast_ffn_weights(wg_ref, wu_ref, wd_ref, wgub_ref, wdb_ref)

    for bit, xs_ref, ys_ref in ((1, xs0_ref, ys0_ref), (2, xs1_ref, ys1_ref)):
        @pl.when((live_ref[i] & bit) != 0)
        def _():
            lo, hi = _unpack_bf16_pair(xs_ref[...])
            h = jnp.concatenate([lo.astype(BF16), hi.astype(BF16)], axis=1)
            y = _ffn(h, wgub_ref[...], wdb_ref[...])
            ys_ref[...] = _pack_bf16_pair(y[:, :HALF], y[:, HALF:])


def _expert_ffn(plan, xs0, xs1, layer, w_gate, w_up, w_down):
    exp, blk0, blk1, live, n_steps = plan
    tile0 = lambda i, exp, blk0, blk1, live: (blk0[i], 0)
    tile1 = lambda i, exp, blk0, blk1, live: (blk1[i], 0)
    ew = lambda shape: pl.BlockSpec((None, None) + shape,
                                    lambda i, exp, blk0, blk1, live: (layer, exp[i], 0, 0))
    slots = jax.ShapeDtypeStruct((N_SLOT, HALF), jnp.uint32)
    return pl.pallas_call(
        _expert_body,
        grid_spec=pltpu.PrefetchScalarGridSpec(
            num_scalar_prefetch=4,
            grid=(n_steps[0],),
            in_specs=[
                pl.BlockSpec((MOE_TILE, HALF), tile0), pl.BlockSpec((MOE_TILE, HALF), tile1),
                ew((D_MODEL, EXPERT_FF)), ew((D_MODEL, EXPERT_FF)), ew((EXPERT_FF, D_MODEL)),
            ],
            out_specs=[pl.BlockSpec((MOE_TILE, HALF), tile0), pl.BlockSpec((MOE_TILE, HALF), tile1)],
            scratch_shapes=_FFN_WEIGHT_SCRATCH,
        ),
        out_shape=[slots, slots],
        compiler_params=pltpu.CompilerParams(
            dimension_semantics=("arbitrary",), vmem_limit_bytes=VMEM_LIMIT),
        name="expert_ffn",
    )(exp, blk0, blk1, live, xs0, xs1, w_gate, w_up, w_down)


def _combine_body(x_ref, sh_ref, sc_ref, g2_ref, yk_ref, wt_ref, sg_ref, su_ref, sd_ref, lng_ref, lnb_ref,
                  *rest):
    o_ref, sgub_ref, sdb_ref = rest[-3:]

    @pl.when(pl.program_id(0) == 0)
    def _():
        _cast_ffn_weights(sg_ref, su_ref, sd_ref, sgub_ref, sdb_ref)

    x = x_ref[...].astype(F32)
    h = (x * (1.0 + sc_ref[...]) + sh_ref[...]).astype(BF16)
    y = _ffn(h, sgub_ref[...], sdb_ref[...])
    wt = wt_ref[...]
    lo_acc = jnp.zeros((x.shape[0], HALF), F32)
    hi_acc = jnp.zeros((x.shape[0], HALF), F32)
    for k in range(TOP_K):
        lo, hi = _unpack_bf16_pair(yk_ref[k])
        w = wt[:, k:k + 1]
        lo_acc = lo_acc + w * lo
        hi_acc = hi_acc + w * hi
    y = y + jnp.concatenate([lo_acc, hi_acc], axis=1)
    z = DEEPNORM_ALPHA * x + g2_ref[...] * y
    o_ref[...] = _layer_norm(z, lng_ref[...], lnb_ref[...]).astype(o_ref.dtype)


def _combine(trunk, part, x, mod, yk, wt, layer, s_gate, s_up, s_down, ln_g, ln_b, out_dtype, earlier):
    tm = TOK_BLOCK
    N_PART = yk.shape[1]
    part_block = part * (N_PART // tm)
    first_block = trunk * (N_TRUNK // tm) + part_block
    row = lambda i: (part_block + i, 0)
    vec = pl.BlockSpec((1, D_MODEL), lambda i: (0, 0))
    sw = lambda shape: pl.BlockSpec((None,) + shape, lambda i: (layer, 0, 0))
    mod_spec = lambda col: _mod_spec(tm, col, first_block)
    in_specs = [
        pl.BlockSpec((tm, D_MODEL), row),
        mod_spec(3), mod_spec(4), mod_spec(5),
        pl.BlockSpec((TOP_K, tm, HALF), lambda i: (0, i, 0)),
        pl.BlockSpec((tm, 128), row),
        sw((D_MODEL, EXPERT_FF)), sw((D_MODEL, EXPERT_FF)), sw((EXPERT_FF, D_MODEL)),
        vec, vec,
    ]
    args = (x, mod, mod, mod, yk, wt, s_gate, s_up, s_down, ln_g, ln_b)
    aliases = {}
    if earlier is not None:
        in_specs.append(pl.BlockSpec(memory_space=pl.ANY))
        aliases = {len(args): 0}
        args += (earlier,)
    return pl.pallas_call(
        _combine_body,
        grid=(N_PART // tm,),
        in_specs=in_specs,
        out_specs=pl.BlockSpec((tm, D_MODEL), row),
        out_shape=jax.ShapeDtypeStruct((N_TRUNK, D_MODEL), out_dtype),
        input_output_aliases=aliases,
        scratch_shapes=_FFN_WEIGHT_SCRATCH,
        compiler_params=pltpu.CompilerParams(
            dimension_semantics=("arbitrary",), vmem_limit_bytes=VMEM_LIMIT),
        name="moe_combine_ln",
    )(*args)


def _moe_dispatch(routing):
    hp, eidx, rank, wt, counts = routing
    counts = counts[:, 0]
    pos_flat = _finalize(counts, eidx, rank).reshape(K_PAD * N_TRUNK)
    return _sc_dispatch(hp, pos_flat), pos_flat, counts, wt


def _moe_experts(plans, layer, w_gate, w_up, w_down):
    (xs0, pos0, counts0, _), (xs1, pos1, counts1, _) = plans
    ys = _expert_ffn(_tile_plan(counts0, counts1), xs0, xs1, layer, w_gate, w_up, w_down)
    return [[_sc_return(y, pos, part, n).reshape(TOP_K, N_TRUNK // n, HALF) for part in range(n)]
            for y, pos, n in zip(ys, (pos0, pos1), RETURN_PARTS)]


def kernel(x_prompt, x_sample, state_ret, c, c_ctx, ada_w, ada_b, ln_g, ln_b, ret_w_in, ret_w_out, ret_decay, conv_w_in, conv_w, conv_w_out, moe_router, moe_bias, moe_w_gate, moe_w_up, moe_w_down, shared_w_gate, shared_w_up, shared_w_down):
    x = (x_prompt.reshape(N_CTX, D_MODEL), x_sample.reshape(N_LAT, D_MODEL))
    cond = jnp.concatenate(
        [c_ctx[None, :], c, jnp.zeros((N_COND - 1 - DEC_BATCH, D_MODEL), F32)], axis=0)
    mods = [_ada_table(cond, ada_w, ada_b, i).reshape(N_COND, 1, 6 * D_MODEL) for i in range(DEPTH)]
    cos, sin = _rope_tables()
    router_wt = jnp.swapaxes(moe_router, 1, 2)

    states = None
    for i in range(DEPTH):
        j = i // 2
        mod = mods[i]
        lng = ln_g[i].reshape(2, 1, D_MODEL)
        lnb = ln_b[i].reshape(2, 1, D_MODEL)
        bias_col = moe_bias[i].reshape(N_EXPERTS, 1)
        if i % 2 == 0:
            w_in_bf = _to_bf16(ret_w_in, j)
        mixed, plans = [], []
        for trunk in range(2):
            if i % 2 == 0:
                if trunk == 0:
                    a, states = _retention_ctx(x[0], mod, w_in_bf, ret_decay[j], j, states)
                else:
                    a = _retention_lat(x[1], mod, w_in_bf, ret_decay[j], state_ret, j, cos, sin)
                xm, *routing = _ret_out(trunk, a, x[trunk], mod, ret_w_out, j, lng[0], lnb[0],
                                        router_wt[i], bias_col)
            else:
                xm, *routing = _conv_layer(trunk, x[trunk], mod, conv_w_in, conv_w_out, j, conv_w[j],
                                           lng[0], lnb[0], router_wt[i], bias_col)
            mixed.append(xm)
            plans.append(_moe_dispatch(routing))
        yks = _moe_experts(plans, i, moe_w_gate, moe_w_up, moe_w_down)
        out_dtype = F32 if i == DEPTH - 1 else RES_DTYPE
        x = []
        for trunk in range(2):
            out = None
            for part in range(RETURN_PARTS[trunk]):
                out = _combine(trunk, part, mixed[trunk], mod, yks[trunk][part], plans[trunk][3], i,
                               shared_w_gate, shared_w_up, shared_w_down, lng[1], lnb[1], out_dtype, out)
            x.append(out)

    y_prompt = x[0].reshape(BATCH, SEQ, D_MODEL)
    y_sample = x[1].reshape(DEC_BATCH, DEC_SEQ, D_MODEL)
    return y_prompt, y_sample, states
```
